```python
import math
import jax
import jax.numpy as jnp
from jax import lax
import numpy as np

D_MODEL = 2048
BATCH = 4
SEQ = 2048
DEPTH = 2
DEC_BATCH = 128
DEC_SEQ = 8
PAST_LEN = 16384
PAGE_SIZE = 128

N_META = 16
CHUNK = 128
CONV_W = 4
A_HEADS = 8
A_DK = 128
A_DV = 256
A_INNER = A_HEADS * A_DV
B_HEADS = 32
B_P = 64
B_N = 128
B_GROUPS = 4
B_HPG = B_HEADS // B_GROUPS
B_INNER = B_HEADS * B_P
B_CONV_DIM = B_INNER + 2 * B_GROUPS * B_N
IN0_SIZES = (A_HEADS * A_DK, A_HEADS * A_DK, A_INNER, A_INNER, A_HEADS, A_HEADS, B_INNER, B_CONV_DIM, B_HEADS)
IN0_DIM = 2 * A_HEADS * A_DK + 2 * A_INNER + 2 * A_HEADS + B_INNER + B_CONV_DIM + B_HEADS
MIX0 = A_INNER + B_INNER
D_RNN = 2560
LRU_BLOCKS = 16
LRU_BW = D_RNN // LRU_BLOCKS
LRU_C = 8.0
N_EXP_GROUPS = 4
EXP_PER_GROUP = 8
N_EXPERTS = N_EXP_GROUPS * EXP_PER_GROUP
TOP_K = 2
D_EXPERT = 512
ALPHA = (2.0 * DEPTH) ** 0.25
BETA = (8.0 * DEPTH) ** -0.25
N_EVEN = (DEPTH + 1) // 2
N_ODD = DEPTH // 2

kernel_name = 'hybrid_mlstm_ssd_rglru_hmoe_step'


def layer_norm(x, g, b, eps=1e-5):
    xf = x.astype(jnp.float32)
    mu = xf.mean(-1, keepdims=True)
    var = jnp.mean(jnp.square(xf - mu), -1, keepdims=True)
    return ((xf - mu) * lax.rsqrt(var + eps) * g + b).astype(x.dtype)


def rms_norm(x, g, eps=1e-6):
    xf = x.astype(jnp.float32)
    return xf * lax.rsqrt(jnp.mean(jnp.square(xf), -1, keepdims=True) + eps) * g


def causal_conv(x, buf, w, b):
    T = x.shape[1]
    xp = jnp.concatenate([buf.astype(x.dtype), x], axis=1)
    out = b + sum(w[j] * xp[:, j:j + T] for j in range(CONV_W))
    return out.astype(x.dtype), xp[:, T:]


def _chunk_len(T):
    return T if T <= CHUNK else CHUNK


def _to_chunks(a, L):
    Bsz, T = a.shape[:2]
    return jnp.moveaxis(a.reshape((Bsz, T // L, L) + a.shape[2:]), 1, 0)


def _from_chunks(a):
    nc, Bsz, L = a.shape[:3]
    return jnp.moveaxis(a, 0, 1).reshape((Bsz, nc * L) + a.shape[3:])


def mlstm_chunked(q, k, v, li, lf, state):
    T = q.shape[1]
    L = _chunk_len(T)
    causal = jnp.tril(jnp.ones((L, L), bool))

    def step(carry, inp):
        C, n, m = carry
        qc, kc, vc, ic, fc = inp
        b = jnp.cumsum(fc, axis=1)
        d = b[:, :, None, :] - b[:, None, :, :] + ic[:, None, :, :]
        d = jnp.where(causal[None, :, :, None], d, -jnp.inf)
        inter = b + m[:, None, :]
        m_t = jnp.maximum(inter, d.max(axis=2))
        w_intra = jnp.exp(d - m_t[:, :, None, :])
        w_inter = jnp.exp(inter - m_t)
        s = jnp.einsum('bthd,bshd->btsh', qc, kc) * w_intra
        num = jnp.einsum('btsh,bshv->bthv', s, vc) + w_inter[..., None] * jnp.einsum('bthd,bhdv->bthv', qc, C)
        nq = s.sum(2) + w_inter * jnp.einsum('bthd,bhd->bth', qc, n)
        h = num / jnp.maximum(jnp.abs(nq), jnp.exp(-m_t))[..., None]
        bL = b[:, -1]
        ws_log = bL[:, None, :] - b + ic
        m_new = jnp.maximum(bL + m, ws_log.max(1))
        ws = jnp.exp(ws_log - m_new[:, None, :])
        wc = jnp.exp(bL + m - m_new)
        C_new = wc[..., None, None] * C + jnp.einsum('bsh,bshd,bshv->bhdv', ws, kc, vc)
        n_new = wc[..., None] * n + jnp.einsum('bsh,bshd->bhd', ws, kc)
        return (C_new, n_new, m_new), h

    xs = tuple(_to_chunks(a, L) for a in (q, k, v, li, lf))
    state, hs = lax.scan(step, state, xs)
    return _from_chunks(hs), state


def ssd_chunked(xdt, a, Bm, Cm, S):
    Bsz, T = xdt.shape[:2]
    L = _chunk_len(T)
    causal = jnp.tril(jnp.ones((L, L), bool))
    xg = xdt.reshape(Bsz, T, B_GROUPS, B_HPG, B_P)
    ag = a.reshape(Bsz, T, B_GROUPS, B_HPG)
    Sg = S.reshape(Bsz, B_GROUPS, B_HPG, B_P, B_N)

    def step(Sc, inp):
        xc, ac, bc, cc = inp
        b = jnp.cumsum(ac, axis=1)
        seg = b[:, :, None] - b[:, None]
        decay = jnp.exp(jnp.where(causal[None, :, :, None, None], seg, -jnp.inf))
        cb = jnp.einsum('btgn,bsgn->btsg', cc, bc)
        y = jnp.einsum('btsg,btsgh,bsghp->btghp', cb, decay, xc)
        y = y + jnp.exp(b)[..., None] * jnp.einsum('btgn,bghpn->btghp', cc, Sc)
        bL = b[:, -1]
        w = jnp.exp(bL[:, None] - b)
        S_new = jnp.exp(bL)[..., None, None] * Sc + jnp.einsum('bsgh,bsgn,bsghp->bghpn', w, bc, xc)
        return S_new, y

    xs = tuple(_to_chunks(t, L) for t in (xg, ag, Bm, Cm))
    Sg, ys = lax.scan(step, Sg, xs)
    y = _from_chunks(ys).reshape(Bsz, T, B_HEADS, B_P)
    return y, Sg.reshape(Bsz, B_HEADS, B_P, B_N)


def linear_scan(a, u, h0):
    u = u.at[:, 0].add(a[:, 0] * h0)

    def comb(l, r):
        return (l[0] * r[0], r[0] * l[1] + r[1])

    _, h = lax.associative_scan(comb, (a, u), axis=1)
    return h


def even_mixer(x, segs, C0, n0, m0, S0, conv0, w_in, gate_b, dt_bias, A_log, D_skip, conv_w, conv_b,
               a_norm_g, b_norm_g, w_out):
    f32 = jnp.float32
    Bsz, T, _ = x.shape
    proj = x @ w_in
    parts, off = [], 0
    for size in IN0_SIZES:
        parts.append(proj[..., off:off + size])
        off += size
    q, k, v, o, ig, fg, z, xbc, dt = parts
    q = q.reshape(Bsz, T, A_HEADS, A_DK).astype(f32) * (A_DK ** -0.5)
    k = k.reshape(Bsz, T, A_HEADS, A_DK).astype(f32)
    v = v.reshape(Bsz, T, A_HEADS, A_DV).astype(f32)
    li = ig.astype(f32) + gate_b[0]
    lf = jax.nn.log_sigmoid(fg.astype(f32) + gate_b[1])
    xbc, conv_new = causal_conv(xbc, conv0, conv_w, conv_b)
    xbc = jax.nn.silu(xbc.astype(f32))
    xs = xbc[..., :B_INNER].reshape(Bsz, T, B_HEADS, B_P)
    Bm = xbc[..., B_INNER:B_INNER + B_GROUPS * B_N].reshape(Bsz, T, B_GROUPS, B_N)
    Cm = xbc[..., B_INNER + B_GROUPS * B_N:].reshape(Bsz, T, B_GROUPS, B_N)
    dt = jax.nn.softplus(dt.astype(f32) + dt_bias)
    a_log = -dt * jnp.exp(A_log.astype(f32))
    xdt = xs * dt[..., None]
    stA = (C0.astype(f32), n0.astype(f32), m0.astype(f32))
    S = S0.astype(f32)
    hs, ys, start = [], [], 0
    for L in segs:
        sl = slice(start, start + L)
        h_seg, stA = mlstm_chunked(q[:, sl], k[:, sl], v[:, sl], li[:, sl], lf[:, sl], stA)
        y_seg, S = ssd_chunked(xdt[:, sl], a_log[:, sl], Bm[:, sl], Cm[:, sl], S)
        hs.append(h_seg)
        ys.append(y_seg)
        start += L
    h = jnp.concatenate(hs, axis=1)
    y = jnp.concatenate(ys, axis=1) + D_skip[:, None] * xs
    h = rms_norm(h, a_norm_g.reshape(A_HEADS, A_DV)).reshape(Bsz, T, A_INNER) * jax.nn.sigmoid(o.astype(f32))
    y = y.reshape(Bsz, T, B_INNER) * jax.nn.silu(z.astype(f32))
    y = rms_norm(y.reshape(Bsz, T, B_GROUPS, B_INNER // B_GROUPS),
                 b_norm_g.reshape(B_GROUPS, B_INNER // B_GROUPS)).reshape(Bsz, T, B_INNER)
    mix = jnp.concatenate([h, y], axis=-1).astype(x.dtype) @ w_out
    return mix, stA, S, conv_new


def odd_mixer(x, h0, conv0, w_in, conv_w, conv_b, w_gate, b_gate, lam, w_out):
    f32 = jnp.float32
    Bsz, T, _ = x.shape
    proj = x @ w_in
    gate_in, xb = proj[..., :D_RNN], proj[..., D_RNN:]
    xb, conv_new = causal_conv(xb, conv0, conv_w, conv_b)
    xf = xb.astype(f32)
    gates = jnp.einsum('btki,gkij->gbtkj', xf.reshape(Bsz, T, LRU_BLOCKS, LRU_BW), w_gate)
    gates = gates.reshape(2, Bsz, T, D_RNN) + b_gate[:, None, None, :]
    r = jax.nn.sigmoid(gates[0])
    i = jax.nn.sigmoid(gates[1])
    log_a = -LRU_C * r * jax.nn.softplus(-lam.astype(f32))
    a = jnp.exp(log_a)
    u = jnp.sqrt(-jnp.expm1(2.0 * log_a)) * (i * xf)
    h = linear_scan(a, u, h0.astype(f32))
    out = (h.astype(x.dtype) * jax.nn.gelu(gate_in)) @ w_out
    return out, h[:, -1], conv_new


def hier_moe(x, w_group, b_group, w_expert, b_expert, w1, w3, w2):
    shp = x.shape
    t = x.reshape(-1, D_MODEL)
    M = t.shape[0]
    g_logits = (t @ w_group).astype(jnp.float32) + b_group
    g_prob = jax.nn.softmax(g_logits, axis=-1)
    g_idx = jnp.argmax(g_logits, axis=-1)
    g_w = jnp.take_along_axis(g_prob, g_idx[:, None], axis=1)[:, 0]
    e_logits = ((t @ w_expert).astype(jnp.float32) + b_expert).reshape(M, N_EXP_GROUPS, EXP_PER_GROUP)
    e_logits = jnp.take_along_axis(e_logits, g_idx[:, None, None], axis=1)[:, 0]
    top_w, top_i = lax.top_k(jax.nn.softmax(e_logits, axis=-1), TOP_K)
    top_w = top_w / top_w.sum(-1, keepdims=True)
    expert = (g_idx[:, None] * EXP_PER_GROUP + top_i).reshape(-1)
    gate = (g_w[:, None] * top_w).reshape(-1)
    order = jnp.argsort(expert)
    tok = order // TOP_K
    xs = t[tok]
    sizes = jnp.bincount(expert, length=N_EXPERTS).astype(jnp.int32)
    hid = jax.nn.silu(lax.ragged_dot(xs, w1, sizes)) * lax.ragged_dot(xs, w3, sizes)
    ys = lax.ragged_dot(hid, w2, sizes) * gate[order][:, None].astype(x.dtype)
    return jax.ops.segment_sum(ys, tok, num_segments=M).reshape(shp)


def trunk(x, segs, C0, n0, m0, S0, sc0, h0, lc0, P):
    nC, nn_, nm, nS, nsc, nh, nlc = [], [], [], [], [], [], []
    for l in range(DEPTH):
        if l % 2 == 0:
            e = l // 2
            mix, (C, n, m), S, sc = even_mixer(
                x, segs, C0[e], n0[e], m0[e], S0[e], sc0[e], P['w_in_even'][e], P['mlstm_gate_b'][e],
                P['ssd_dt_bias'][e], P['ssd_A_log'][e], P['ssd_D'][e], P['ssd_conv_w'][e], P['ssd_conv_b'][e],
                P['mlstm_norm_g'][e], P['ssd_norm_g'][e], P['w_out_even'][e])
            nC.append(C.astype(C0.dtype))
            nn_.append(n.astype(n0.dtype))
            nm.append(m.astype(m0.dtype))
            nS.append(S.astype(S0.dtype))
            nsc.append(sc.astype(sc0.dtype))
        else:
            o = l // 2
            mix, h, lc = odd_mixer(x, h0[o], lc0[o], P['w_in_odd'][o], P['lru_conv_w'][o], P['lru_conv_b'][o],
                                   P['lru_w_gate'][o], P['lru_b_gate'][o], P['lru_lambda'][o], P['w_out_odd'][o])
            nh.append(h.astype(h0.dtype))
            nlc.append(lc.astype(lc0.dtype))
        x = layer_norm(ALPHA * x + mix, P['ln_g'][l, 0], P['ln_b'][l, 0])
        ff = hier_moe(x, P['moe_w_group'][l], P['moe_b_group'][l], P['moe_w_expert'][l], P['moe_b_expert'][l],
                      P['moe_w1'][l], P['moe_w3'][l], P['moe_w2'][l])
        x = layer_norm(ALPHA * x + ff, P['ln_g'][l, 1], P['ln_b'][l, 1])
    return (x, jnp.stack(nC), jnp.stack(nn_), jnp.stack(nm), jnp.stack(nS), jnp.stack(nsc),
            jnp.stack(nh), jnp.stack(nlc))


def setup_inputs(seed: int = 0) -> dict:
    key = jax.random.key(seed)
    ks = iter(jax.random.split(key, 48))
    f32 = jnp.float32

    def nrm(shape, scale):
        return jax.random.normal(next(ks), shape, f32) * scale

    x_prompt = nrm((BATCH, SEQ, D_MODEL), 1.0)
    x_sample = nrm((DEC_BATCH, DEC_SEQ, D_MODEL), 1.0)
    state_mlstm_C = nrm((N_EVEN, DEC_BATCH, A_HEADS, A_DK, A_DV), 0.1)
    state_mlstm_n = nrm((N_EVEN, DEC_BATCH, A_HEADS, A_DK), 0.1)
    state_mlstm_m = nrm((N_EVEN, DEC_BATCH, A_HEADS), 0.5)
    state_ssd = nrm((N_EVEN, DEC_BATCH, B_HEADS, B_P, B_N), 0.1)
    state_ssd_conv = nrm((N_EVEN, DEC_BATCH, CONV_W - 1, B_CONV_DIM), 1.0)
    state_lru_h = nrm((N_ODD, DEC_BATCH, D_RNN), 0.5)
    state_lru_conv = nrm((N_ODD, DEC_BATCH, CONV_W - 1, D_RNN), 1.0)
    meta_tokens = nrm((N_META, D_MODEL), 1.0)
    w_in_even = nrm((N_EVEN, D_MODEL, IN0_DIM), D_MODEL ** -0.5)
    f_bias = jnp.linspace(3.0, 6.0, A_HEADS, dtype=f32)
    mlstm_gate_b = jnp.stack([nrm((N_EVEN, A_HEADS), 0.1), f_bias + nrm((N_EVEN, A_HEADS), 0.1)], axis=1)
    dt0 = jnp.exp(jax.random.uniform(next(ks), (N_EVEN, B_HEADS), f32, math.log(1e-3), math.log(1e-1)))
    ssd_dt_bias = dt0 + jnp.log(-jnp.expm1(-dt0))
    ssd_A_log = jnp.log(jax.random.uniform(next(ks), (N_EVEN, B_HEADS), f32, 1.0, 16.0))
    ssd_D = 1.0 + nrm((N_EVEN, B_HEADS), 0.02)
    ssd_conv_w = nrm((N_EVEN, CONV_W, B_CONV_DIM), CONV_W ** -0.5)
    ssd_conv_b = nrm((N_EVEN, B_CONV_DIM), 0.02)
    mlstm_norm_g = 1.0 + nrm((N_EVEN, A_INNER), 0.02)
    ssd_norm_g = 1.0 + nrm((N_EVEN, B_INNER), 0.02)
    w_out_even = nrm((N_EVEN, MIX0, D_MODEL), MIX0 ** -0.5 * BETA)
    w_in_odd = nrm((N_ODD, D_MODEL, 2 * D_RNN), D_MODEL ** -0.5)
    lru_conv_w = nrm((N_ODD, CONV_W, D_RNN), CONV_W ** -0.5)
    lru_conv_b = nrm((N_ODD, D_RNN), 0.02)
    lru_w_gate = nrm((N_ODD, 2, LRU_BLOCKS, LRU_BW, LRU_BW), LRU_BW ** -0.5)
    lru_b_gate = nrm((N_ODD, 2, D_RNN), 0.02)
    a_c = jax.random.uniform(next(ks), (N_ODD, D_RNN), f32, 0.9, 0.999)
    sig = a_c ** (1.0 / LRU_C)
    lru_lambda = jnp.log(sig) - jnp.log1p(-sig)
    w_out_odd = nrm((N_ODD, D_RNN, D_MODEL), D_RNN ** -0.5 * BETA)
    ln_g = 1.0 + nrm((DEPTH, 2, D_MODEL), 0.02)
    ln_b = nrm((DEPTH, 2, D_MODEL), 0.02)
    moe_w_group = nrm((DEPTH, D_MODEL, N_EXP_GROUPS), D_MODEL ** -0.5)
    moe_b_group = nrm((DEPTH, N_EXP_GROUPS), 0.01)
    moe_w_expert = nrm((DEPTH, D_MODEL, N_EXPERTS), D_MODEL ** -0.5)
    moe_b_expert = nrm((DEPTH, N_EXPERTS), 0.01)
    moe_w1 = nrm((DEPTH, N_EXPERTS, D_MODEL, D_EXPERT), D_MODEL ** -0.5)
    moe_w3 = nrm((DEPTH, N_EXPERTS, D_MODEL, D_EXPERT), D_MODEL ** -0.5)
    moe_w2 = nrm((DEPTH, N_EXPERTS, D_EXPERT, D_MODEL), D_EXPERT ** -0.5 * BETA)
    return {'x_prompt': x_prompt, 'x_sample': x_sample,
            'state_mlstm_C': state_mlstm_C, 'state_mlstm_n': state_mlstm_n, 'state_mlstm_m': state_mlstm_m,
            'state_ssd': state_ssd, 'state_ssd_conv': state_ssd_conv,
            'state_lru_h': state_lru_h, 'state_lru_conv': state_lru_conv,
            'meta_tokens': meta_tokens, 'w_in_even': w_in_even, 'mlstm_gate_b': mlstm_gate_b,
            'ssd_dt_bias': ssd_dt_bias, 'ssd_A_log': ssd_A_log, 'ssd_D': ssd_D,
            'ssd_conv_w': ssd_conv_w, 'ssd_conv_b': ssd_conv_b, 'mlstm_norm_g': mlstm_norm_g,
            'ssd_norm_g': ssd_norm_g, 'w_out_even': w_out_even, 'w_in_odd': w_in_odd,
            'lru_conv_w': lru_conv_w, 'lru_conv_b': lru_conv_b, 'lru_w_gate': lru_w_gate,
            'lru_b_gate': lru_b_gate, 'lru_lambda': lru_lambda, 'w_out_odd': w_out_odd,
            'ln_g': ln_g, 'ln_b': ln_b, 'moe_w_group': moe_w_group, 'moe_b_group': moe_b_group,
            'moe_w_expert': moe_w_expert, 'moe_b_expert': moe_b_expert,
            'moe_w1': moe_w1, 'moe_w3': moe_w3, 'moe_w2': moe_w2}


def reference(x_prompt, x_sample, state_mlstm_C, state_mlstm_n, state_mlstm_m, state_ssd, state_ssd_conv,
              state_lru_h, state_lru_conv, meta_tokens, w_in_even, mlstm_gate_b, ssd_dt_bias, ssd_A_log, ssd_D,
              ssd_conv_w, ssd_conv_b, mlstm_norm_g, ssd_norm_g, w_out_even, w_in_odd, lru_conv_w, lru_conv_b,
              lru_w_gate, lru_b_gate, lru_lambda, w_out_odd, ln_g, ln_b, moe_w_group, moe_b_group,
              moe_w_expert, moe_b_expert, moe_w1, moe_w3, moe_w2):
    P = {'w_in_even': w_in_even, 'mlstm_gate_b': mlstm_gate_b, 'ssd_dt_bias': ssd_dt_bias,
         'ssd_A_log': ssd_A_log, 'ssd_D': ssd_D, 'ssd_conv_w': ssd_conv_w, 'ssd_conv_b': ssd_conv_b,
         'mlstm_norm_g': mlstm_norm_g, 'ssd_norm_g': ssd_norm_g, 'w_out_even': w_out_even,
         'w_in_odd': w_in_odd, 'lru_conv_w': lru_conv_w, 'lru_conv_b': lru_conv_b, 'lru_w_gate': lru_w_gate,
         'lru_b_gate': lru_b_gate, 'lru_lambda': lru_lambda, 'w_out_odd': w_out_odd, 'ln_g': ln_g, 'ln_b': ln_b,
         'moe_w_group': moe_w_group, 'moe_b_group': moe_b_group, 'moe_w_expert': moe_w_expert,
         'moe_b_expert': moe_b_expert, 'moe_w1': moe_w1, 'moe_w3': moe_w3, 'moe_w2': moe_w2}
    Bp, T_p = x_prompt.shape[0], x_prompt.shape[1]
    meta = jnp.broadcast_to(meta_tokens[None].astype(x_prompt.dtype), (Bp, N_META, D_MODEL))
    xp = jnp.concatenate([meta, x_prompt], axis=1)

    def zeros_like_batch(s):
        return jnp.zeros((s.shape[0], Bp) + s.shape[2:], s.dtype)

    (yp, pC, pn, pm, pS, pSconv, pH, pHconv) = trunk(
        xp, (N_META, T_p), zeros_like_batch(state_mlstm_C), zeros_like_batch(state_mlstm_n),
        zeros_like_batch(state_mlstm_m), zeros_like_batch(state_ssd), zeros_like_batch(state_ssd_conv),
        zeros_like_batch(state_lru_h), zeros_like_batch(state_lru_conv), P)
    y_prompt = yp[:, N_META:]
    (y_sample, sC, sn, sm, sS, sSconv, sH, sHconv) = trunk(
        x_sample, (x_sample.shape[1],), state_mlstm_C, state_mlstm_n, state_mlstm_m, state_ssd, state_ssd_conv,
        state_lru_h, state_lru_conv, P)
    return (y_prompt, y_sample, pC, pn, pm, pS, pSconv, pH, pHconv, sC, sn, sm, sS, sSconv, sH, sHconv)
```

```python
import functools
import math

import jax
import jax.numpy as jnp
import numpy as np
from jax import lax
from jax.experimental import pallas as pl
from jax.experimental.pallas import tpu as pltpu

F32 = jnp.float32
BF16 = jnp.bfloat16

D_MODEL = 2048
N_META = 16
CHUNK = 128
CONV_W = 4
A_HEADS = 8
A_DK = 128
A_DV = 256
A_INNER = A_HEADS * A_DV
B_HEADS = 32
B_P = 64
B_N = 128
B_GROUPS = 4
B_HPG = B_HEADS // B_GROUPS
B_INNER = B_HEADS * B_P
B_GW = B_INNER // B_GROUPS
B_CONV_DIM = B_INNER + 2 * B_GROUPS * B_N
D_RNN = 2560
LRU_BLOCKS = 16
LRU_BW = D_RNN // LRU_BLOCKS
LRU_C = 8.0
LRU_SB = 640
N_LRU_SB = D_RNN // LRU_SB
N_EXP_GROUPS = 4
EXP_PER_GROUP = 8
N_EXPERTS = N_EXP_GROUPS * EXP_PER_GROUP
TOP_K = 2
D_EXPERT = 512
DEPTH = 2
ALPHA = (2.0 * DEPTH) ** 0.25

COL_Q = 0
COL_K = A_HEADS * A_DK
COL_V = 2 * A_HEADS * A_DK
COL_O = COL_V + A_INNER
COL_Z = COL_O + A_INNER
COL_XBC = COL_Z + B_INNER
COL_SMALL = COL_XBC + B_CONV_DIM
N_SMALL = 2 * A_HEADS + B_HEADS
IN0_PAD = 11520

NEG_BIG = -1e30
VMEM_LIMIT_BYTES = 48 * 1024 * 1024
TILE_ROWS = 256


def _cparams(n_axes):
    return pltpu.CompilerParams(dimension_semantics=("arbitrary",) * n_axes,
                                vmem_limit_bytes=VMEM_LIMIT_BYTES)


def _smem_spec():
    return pl.BlockSpec(memory_space=pltpu.SMEM)


def _mm_kernel(x_ref, w_ref, o_ref):
    o_ref[...] = jnp.dot(x_ref[...], w_ref[...], preferred_element_type=F32).astype(o_ref.dtype)


def _mm(x, w, bm, bn, name):
    M, K = x.shape
    N = w.shape[1]
    return pl.pallas_call(
        _mm_kernel,
        grid=(N // bn, M // bm),
        in_specs=[pl.BlockSpec((bm, K), lambda j, i: (i, 0)),
                  pl.BlockSpec((K, bn), lambda j, i: (0, j))],
        out_specs=pl.BlockSpec((bm, bn), lambda j, i: (i, j)),
        out_shape=jax.ShapeDtypeStruct((M, N), F32),
        compiler_params=_cparams(2),
        name=name,
    )(x, w)


def _layer_norm_rows(y, g, b):
    mu = jnp.mean(y, axis=-1, keepdims=True)
    yc = y - mu
    var = jnp.mean(yc * yc, axis=-1, keepdims=True)
    return yc * lax.rsqrt(var + 1e-5) * g + b


def _add_ln_kernel(x_ref, m_ref, g_ref, b_ref, o_ref, ob_ref):
    y = _layer_norm_rows(ALPHA * x_ref[...] + m_ref[...], g_ref[...], b_ref[...])
    o_ref[...] = y
    ob_ref[...] = y.astype(BF16)


def _add_ln(x, mix, g, b, name):
    M, D = x.shape
    tm = 256
    row = pl.BlockSpec((tm, D), lambda i: (i, 0))
    vec = pl.BlockSpec((1, D), lambda i: (0, 0))
    return pl.pallas_call(
        _add_ln_kernel,
        grid=(M // tm,),
        in_specs=[row, row, vec, vec],
        out_specs=[row, row],
        out_shape=[jax.ShapeDtypeStruct((M, D), F32), jax.ShapeDtypeStruct((M, D), BF16)],
        compiler_params=_cparams(1),
        name=name,
    )(x, mix, g.reshape(1, D), b.reshape(1, D))


def _combine_ln_kernel(x_ref, y0_ref, y1_ref, gate_ref, g_ref, b_ref, o_ref, ob_ref):
    gate = gate_ref[...]
    ff = gate[:, 0:1] * y0_ref[...] + gate[:, 1:2] * y1_ref[...]
    y = _layer_norm_rows(ALPHA * x_ref[...] + ff, g_ref[...], b_ref[...])
    o_ref[...] = y
    ob_ref[...] = y.astype(BF16)


def _combine_ln(x, y2, gates, g, b, name):
    M, D = x.shape
    tm = 256
    nblk = M // tm
    row = pl.BlockSpec((tm, D), lambda i: (i, 0))
    row_hi = pl.BlockSpec((tm, D), lambda i: (i + nblk, 0))
    vec = pl.BlockSpec((1, D), lambda i: (0, 0))
    return pl.pallas_call(
        _combine_ln_kernel,
        grid=(nblk,),
        in_specs=[row, row, row_hi, pl.BlockSpec((tm, 128), lambda i: (i, 0)), vec, vec],
        out_specs=[row, row],
        out_shape=[jax.ShapeDtypeStruct((M, D), F32), jax.ShapeDtypeStruct((M, D), BF16)],
        compiler_params=_cparams(1),
        name=name,
    )(x, y2, y2, gates, g.reshape(1, D), b.reshape(1, D))


def _softplus(x):
    return jnp.maximum(x, 0.0) + jnp.log1p(jnp.exp(-jnp.abs(x)))


def _sigmoid(x):
    return 1.0 / (1.0 + jnp.exp(-x))


def _silu(x):
    return x * _sigmoid(x)


def _row_to_col(row, eye):
    return jnp.sum(jnp.where(eye, row, 0.0), axis=1, keepdims=True)


def _causal_conv(x, carry_ref, buf_ref, w_ref, b_ref, L):
    buf_ref[0:8, :] = carry_ref[...]
    buf_ref[8:8 + L, :] = x
    acc = b_ref[...] + w_ref[0:1, :] * buf_ref[5:5 + L, :]
    for j in range(1, CONV_W):
        acc = acc + w_ref[j:j + 1, :] * buf_ref[5 + j:5 + j + L, :]
    carry_ref[...] = buf_ref[L:L + 8, :]
    return acc


def _mlstm_kernel(gb_ref, q_ref, k_ref, v_ref, o_ref, gt_ref, c0_ref, n0_ref, m0_ref, ng_ref,
                  h_ref, c_out_ref, n_out_ref, m_out_ref, c_s, n_s, m_s, *, L, n_pad):
    h = pl.program_id(1)
    c = pl.program_id(2)

    @pl.when(c == 0)
    def _():
        c_s[...] = c0_ref[0, 0]
        n_s[...] = n0_ref[0, 0]
        m_s[...] = m0_ref[0, 0]

    rows = lax.broadcasted_iota(jnp.int32, (L, L), 0)
    cols = lax.broadcasted_iota(jnp.int32, (L, L), 1)
    eye = rows == cols
    causal = cols <= rows

    li = gt_ref[0, pl.ds(h, 1), :] + gb_ref[0, h]
    fr = gt_ref[0, pl.ds(A_HEADS + h, 1), :] + gb_ref[1, h]
    lf = jnp.minimum(fr, 0.0) - jnp.log1p(jnp.exp(-jnp.abs(fr)))
    if n_pad:
        pad = (lax.broadcasted_iota(jnp.int32, (1, L), 1) < n_pad) & (c == 0)
        li = jnp.where(pad, NEG_BIG, li)
        lf = jnp.where(pad, 0.0, lf)

    lf_col = _row_to_col(lf, eye)
    b_col = jnp.sum(jnp.where(causal, lf, 0.0), axis=1, keepdims=True)
    b_row = jnp.sum(jnp.where(rows <= cols, lf_col, 0.0), axis=0, keepdims=True)
    m_prev = m_s[:, 0:1]

    d = jnp.where(causal, b_col - b_row + li, NEG_BIG)
    inter = b_col + m_prev
    m_t = jnp.maximum(inter, jnp.max(d, axis=1, keepdims=True))
    w_intra = jnp.exp(d - m_t)
    w_inter = jnp.exp(inter - m_t)

    q = q_ref[...] * (A_DK ** -0.5)
    k = k_ref[...]
    v = v_ref[...]
    qb = q.astype(BF16)
    kb = k.astype(BF16)
    vb = v.astype(BF16)
    s = lax.dot_general(qb, kb, (((1,), (1,)), ((), ())), preferred_element_type=F32) * w_intra
    c_prev = c_s[...]
    n_prev = n_s[...]
    num = jnp.dot(s.astype(BF16), vb, preferred_element_type=F32)
    num = num + w_inter * jnp.dot(qb, c_prev.astype(BF16), preferred_element_type=F32)
    qn = jnp.sum(qb.astype(F32) * n_prev.astype(BF16).astype(F32), axis=1, keepdims=True)
    nq = jnp.sum(s, axis=1, keepdims=True) + w_inter * qn
    hh = num / jnp.maximum(jnp.abs(nq), jnp.exp(-m_t))

    hn = hh * lax.rsqrt(jnp.mean(hh * hh, axis=-1, keepdims=True) + 1e-6) * ng_ref[...]
    h_ref[...] = (hn * _sigmoid(o_ref[...])).astype(h_ref.dtype)

    b_last = b_row[:, L - 1:L]
    ws_log = b_last - b_row + li
    m_new = jnp.maximum(b_last + m_prev, jnp.max(ws_log, axis=1, keepdims=True))
    ws_col = _row_to_col(jnp.exp(ws_log - m_new), eye)
    wc = jnp.exp(b_last + m_prev - m_new)
    kw = k * ws_col
    c_new = wc * c_prev + lax.dot_general(kw.astype(BF16), vb, (((0,), (0,)), ((), ())),
                                          preferred_element_type=F32)
    n_new = wc * n_prev + jnp.sum(ws_col.astype(BF16).astype(F32) * kb.astype(F32), axis=0, keepdims=True)
    c_s[...] = c_new
    n_s[...] = n_new
    m_s[...] = jnp.broadcast_to(m_new, m_s.shape)
    c_out_ref[0, 0] = c_new
    n_out_ref[0, 0] = n_new
    m_out_ref[0, 0] = jnp.broadcast_to(m_new, m_s.shape)


def _mlstm(proj, gates_t, gate_b, norm_g, c0, n0, m0, *, n_seq, n_chunks, L, row0, n_pad, name):
    rb0 = row0 // L

    def rblk(b, c):
        return rb0 + b * n_chunks + c

    out_dtype = BF16 if L % 16 == 0 else F32
    state_map = lambda b, h, c: (b, h, 0, 0)
    grid = (n_seq, A_HEADS, n_chunks)
    kern = functools.partial(_mlstm_kernel, L=L, n_pad=n_pad)
    return pl.pallas_call(
        kern,
        grid=grid,
        in_specs=[
            _smem_spec(),
            pl.BlockSpec((L, A_DK), lambda b, h, c: (rblk(b, c), COL_Q // A_DK + h)),
            pl.BlockSpec((L, A_DK), lambda b, h, c: (rblk(b, c), COL_K // A_DK + h)),
            pl.BlockSpec((L, A_DV), lambda b, h, c: (rblk(b, c), COL_V // A_DV + h)),
            pl.BlockSpec((L, A_DV), lambda b, h, c: (rblk(b, c), COL_O // A_DV + h)),
            pl.BlockSpec((1, N_SMALL, L), lambda b, h, c: (b, 0, c)),
            pl.BlockSpec((1, 1, A_DK, A_DV), state_map),
            pl.BlockSpec((1, 1, 1, A_DK), state_map),
            pl.BlockSpec((1, 1, 1, 128), state_map),
            pl.BlockSpec((1, A_DV), lambda b, h, c: (0, h)),
        ],
        out_specs=[
            pl.BlockSpec((L, A_DV), lambda b, h, c: (b * n_chunks + c, h)),
            pl.BlockSpec((1, 1, A_DK, A_DV), state_map),
            pl.BlockSpec((1, 1, 1, A_DK), state_map),
            pl.BlockSpec((1, 1, 1, 128), state_map),
        ],
        out_shape=[
            jax.ShapeDtypeStruct((n_seq * n_chunks * L, A_INNER), out_dtype),
            jax.ShapeDtypeStruct((n_seq, A_HEADS, A_DK, A_DV), F32),
            jax.ShapeDtypeStruct((n_seq, A_HEADS, 1, A_DK), F32),
            jax.ShapeDtypeStruct((n_seq, A_HEADS, 1, 128), F32),
        ],
        scratch_shapes=[pltpu.VMEM((A_DK, A_DV), F32), pltpu.VMEM((1, A_DK), F32), pltpu.VMEM((1, 128), F32)],
        compiler_params=_cparams(3),
        name=name,
    )(gate_b, proj, proj, proj, proj, gates_t, c0, n0, m0, norm_g.reshape(1, A_INNER))


def _ssd_kernel(dtb_ref, alog_ref, dsk_ref, xs_ref, bm_ref, cm_ref, z_ref, gt_ref, gc_ref, s0_ref,
                cx0_ref, cb0_ref, cc0_ref, wx_ref, wb_ref, wc_ref, bx_ref, bb_ref, bc_ref, ng_ref,
                y_ref, s_out_ref,
                s_s, cx_s, cb_s, cc_s, bufx, bufb, bufc, ybuf, *, L, n_pad):
    g = pl.program_id(1)
    c = pl.program_id(2)

    @pl.when(c == 0)
    def _():
        s_s[...] = s0_ref[0]
        cx_s[...] = cx0_ref[0]
        cb_s[...] = cb0_ref[0]
        cc_s[...] = cc0_ref[0]

    rows = lax.broadcasted_iota(jnp.int32, (L, L), 0)
    cols = lax.broadcasted_iota(jnp.int32, (L, L), 1)
    causal = cols <= rows
    if n_pad:
        pad_row = (lax.broadcasted_iota(jnp.int32, (1, L), 1) < n_pad) & (c == 0)
        pad_col = (lax.broadcasted_iota(jnp.int32, (L, 1), 0) < n_pad) & (c == 0)

    xs_raw = xs_ref[...]
    bm_raw = bm_ref[...]
    cm_raw = cm_ref[...]
    if n_pad:
        xs_raw = jnp.where(pad_col, 0.0, xs_raw)
        bm_raw = jnp.where(pad_col, 0.0, bm_raw)
        cm_raw = jnp.where(pad_col, 0.0, cm_raw)
    xs = _silu(_causal_conv(xs_raw, cx_s, bufx, wx_ref, bx_ref, L))
    bm = _silu(_causal_conv(bm_raw, cb_s, bufb, wb_ref, bb_ref, L))
    cm = _silu(_causal_conv(cm_raw, cc_s, bufc, wc_ref, bc_ref, L))
    bmb = bm.astype(BF16)
    cmb = cm.astype(BF16)
    cb = lax.dot_general(cmb, bmb, (((1,), (1,)), ((), ())), preferred_element_type=F32)

    gc = gc_ref[0, 0]
    for hh in range(B_HPG):
        head = g * B_HPG + hh
        dt_bias = dtb_ref[head]
        a_neg = -jnp.exp(alog_ref[head])
        dt_row = _softplus(gt_ref[0, pl.ds(2 * A_HEADS + head, 1), :] + dt_bias)
        dt_col = _softplus(gc[:, hh:hh + 1] + dt_bias)
        if n_pad:
            dt_row = jnp.where(pad_row, 0.0, dt_row)
            dt_col = jnp.where(pad_col, 0.0, dt_col)
        a_row = dt_row * a_neg
        a_col = dt_col * a_neg
        b_col = jnp.sum(jnp.where(causal, a_row, 0.0), axis=1, keepdims=True)
        b_row = jnp.sum(jnp.where(rows <= cols, a_col, 0.0), axis=0, keepdims=True)
        decay = jnp.exp(jnp.where(causal, b_col - b_row, NEG_BIG))
        x_h = xs[:, hh * B_P:(hh + 1) * B_P]
        xdt = x_h * dt_col
        s_prev = s_s[hh]
        y_h = jnp.dot((cb * decay).astype(BF16), xdt.astype(BF16), preferred_element_type=F32)
        y_h = y_h + jnp.exp(b_col) * lax.dot_general(cmb, s_prev.astype(BF16), (((1,), (1,)), ((), ())),
                                                     preferred_element_type=F32)
        b_last = b_row[:, L - 1:L]
        w_col = jnp.exp(b_last - b_col)
        s_new = jnp.exp(b_last) * s_prev + lax.dot_general((xdt * w_col).astype(BF16), bmb,
                                                           (((0,), (0,)), ((), ())),
                                                           preferred_element_type=F32)
        s_s[hh] = s_new
        s_out_ref[0, hh] = s_new
        ybuf[:, hh * B_P:(hh + 1) * B_P] = y_h + dsk_ref[head] * x_h

    y = ybuf[...] * _silu(z_ref[...])
    yn = y * lax.rsqrt(jnp.mean(y * y, axis=-1, keepdims=True) + 1e-6) * ng_ref[...]
    y_ref[...] = yn.astype(y_ref.dtype)


def _ssd(proj, gates_t, gates_c, dt_bias, a_log, d_skip, conv_w, conv_b, norm_g, s0, conv0, *,
         n_seq, n_chunks, L, row0, n_pad, name):
    rb0 = row0 // L

    def rblk(b, c):
        return rb0 + b * n_chunks + c

    out_dtype = BF16 if L % 16 == 0 else F32
    cw = conv_w
    cbias = conv_b.reshape(1, B_CONV_DIM)
    xoff = COL_XBC // B_GW
    boff = (COL_XBC + B_INNER) // B_N
    coff = (COL_XBC + B_INNER + B_GROUPS * B_N) // B_N
    kern = functools.partial(_ssd_kernel, L=L, n_pad=n_pad)
    return pl.pallas_call(
        kern,
        grid=(n_seq, B_GROUPS, n_chunks),
        in_specs=[
            _smem_spec(), _smem_spec(), _smem_spec(),
            pl.BlockSpec((L, B_GW), lambda b, g, c: (rblk(b, c), xoff + g)),
            pl.BlockSpec((L, B_N), lambda b, g, c: (rblk(b, c), boff + g)),
            pl.BlockSpec((L, B_N), lambda b, g, c: (rblk(b, c), coff + g)),
            pl.BlockSpec((L, B_GW), lambda b, g, c: (rblk(b, c), COL_Z // B_GW + g)),
            pl.BlockSpec((1, N_SMALL, L), lambda b, g, c: (b, 0, c)),
            pl.BlockSpec((1, 1, L, B_HPG), lambda b, g, c: (b, g, c, 0)),
            pl.BlockSpec((1, B_HPG, B_P, B_N), lambda b, g, c: (b, g, 0, 0)),
            pl.BlockSpec((1, 8, B_GW), lambda b, g, c: (b, 0, g)),
            pl.BlockSpec((1, 8, B_N), lambda b, g, c: (b, 0, B_INNER // B_N + g)),
            pl.BlockSpec((1, 8, B_N), lambda b, g, c: (b, 0, B_INNER // B_N + B_GROUPS + g)),
            pl.BlockSpec((CONV_W, B_GW), lambda b, g, c: (0, g)),
            pl.BlockSpec((CONV_W, B_N), lambda b, g, c: (0, B_INNER // B_N + g)),
            pl.BlockSpec((CONV_W, B_N), lambda b, g, c: (0, B_INNER // B_N + B_GROUPS + g)),
            pl.BlockSpec((1, B_GW), lambda b, g, c: (0, g)),
            pl.BlockSpec((1, B_N), lambda b, g, c: (0, B_INNER // B_N + g)),
            pl.BlockSpec((1, B_N), lambda b, g, c: (0, B_INNER // B_N + B_GROUPS + g)),
            pl.BlockSpec((1, B_GW), lambda b, g, c: (0, g)),
        ],
        out_specs=[
            pl.BlockSpec((L, B_GW), lambda b, g, c: (b * n_chunks + c, g)),
            pl.BlockSpec((1, B_HPG, B_P, B_N), lambda b, g, c: (b, g, 0, 0)),
        ],
        out_shape=[
            jax.ShapeDtypeStruct((n_seq * n_chunks * L, B_INNER), out_dtype),
            jax.ShapeDtypeStruct((n_seq, B_HEADS, B_P, B_N), F32),
        ],
        scratch_shapes=[
            pltpu.VMEM((B_HPG, B_P, B_N), F32),
            pltpu.VMEM((8, B_GW), F32), pltpu.VMEM((8, B_N), F32), pltpu.VMEM((8, B_N), F32),
            pltpu.VMEM((L + 8, B_GW), F32), pltpu.VMEM((L + 8, B_N), F32), pltpu.VMEM((L + 8, B_N), F32),
            pltpu.VMEM((L, B_GW), F32),
        ],
        compiler_params=_cparams(3),
        name=name,
    )(dt_bias, a_log, d_skip, proj, proj, proj, proj, gates_t, gates_c, s0, conv0, conv0, conv0,
      cw, cw, cw, cbias, cbias, cbias, norm_g.reshape(1, B_INNER))


def _gelu_tanh(x):
    return 0.5 * x * (1.0 + jnp.tanh(math.sqrt(2.0 / math.pi) * (x + 0.044715 * (x * x * x))))


def _expm1_nonpos(z):
    e = jnp.exp(z)
    safe = (e < 1.0) & (z > -1.0)
    return jnp.where(safe, (e - 1.0) * z / jnp.log(jnp.where(safe, e, 0.5)), jnp.where(e == 1.0, z, e - 1.0))


def _lru_kernel(gi_ref, xb_ref, h0_ref, cv0_ref, cw_ref, cb_ref, wg_ref, bg_ref, lam_ref,
                o_ref, h_out_ref, h_s, cv_s, buf, *, L, n_pad):
    c = pl.program_id(2)

    @pl.when(c == 0)
    def _():
        h_s[...] = h0_ref[0]
        cv_s[...] = cv0_ref[0]

    row_id = lax.broadcasted_iota(jnp.int32, (L, 1), 0)
    x_raw = xb_ref[...]
    if n_pad:
        pad_col = (row_id < n_pad) & (c == 0)
        x_raw = jnp.where(pad_col, 0.0, x_raw)
    xf = _causal_conv(x_raw, cv_s, buf, cw_ref, cb_ref, L)
    xfb = xf.astype(BF16)
    r = _sigmoid(jnp.dot(xfb, wg_ref[0, 0], preferred_element_type=F32) + bg_ref[0:1, :])
    i = _sigmoid(jnp.dot(xfb, wg_ref[1, 0], preferred_element_type=F32) + bg_ref[1:2, :])
    log_a = (-LRU_C) * r * _softplus(-lam_ref[...])
    a = jnp.exp(log_a)
    u = jnp.sqrt(-_expm1_nonpos(2.0 * log_a)) * (i * xf)
    if n_pad:
        a = jnp.where(pad_col, 1.0, a)
        u = jnp.where(pad_col, 0.0, u)
    k = 1
    while k < L:
        keep = row_id >= k
        a_sh = jnp.where(keep, pltpu.roll(a, k, 0), 1.0)
        u_sh = jnp.where(keep, pltpu.roll(u, k, 0), 0.0)
        u = a * u_sh + u
        a = a * a_sh
        k *= 2
    hs = a * h_s[...] + u
    h_last = hs[L - 1:L, :]
    h_s[...] = h_last
    h_out_ref[0] = h_last
    o_ref[...] = (hs * _gelu_tanh(gi_ref[...])).astype(o_ref.dtype)


def _lru(proj, conv_w, conv_b, wg_sb, b_gate, lam, h0, conv0, *, n_seq, n_chunks, L, row0, n_pad, name):
    rb0 = row0 // L

    def rblk(b, c):
        return rb0 + b * n_chunks + c

    out_dtype = BF16 if L % 16 == 0 else F32
    kern = functools.partial(_lru_kernel, L=L, n_pad=n_pad)
    return pl.pallas_call(
        kern,
        grid=(n_seq, N_LRU_SB, n_chunks),
        in_specs=[
            pl.BlockSpec((L, LRU_SB), lambda b, j, c: (rblk(b, c), j)),
            pl.BlockSpec((L, LRU_SB), lambda b, j, c: (rblk(b, c), N_LRU_SB + j)),
            pl.BlockSpec((1, 1, LRU_SB), lambda b, j, c: (b, 0, j)),
            pl.BlockSpec((1, 8, LRU_SB), lambda b, j, c: (b, 0, j)),
            pl.BlockSpec((CONV_W, LRU_SB), lambda b, j, c: (0, j)),
            pl.BlockSpec((1, LRU_SB), lambda b, j, c: (0, j)),
            pl.BlockSpec((2, 1, LRU_SB, LRU_SB), lambda b, j, c: (0, j, 0, 0)),
            pl.BlockSpec((2, LRU_SB), lambda b, j, c: (0, j)),
            pl.BlockSpec((1, LRU_SB), lambda b, j, c: (0, j)),
        ],
        out_specs=[
            pl.BlockSpec((L, LRU_SB), lambda b, j, c: (b * n_chunks + c, j)),
            pl.BlockSpec((1, 1, LRU_SB), lambda b, j, c: (b, 0, j)),
        ],
        out_shape=[
            jax.ShapeDtypeStruct((n_seq * n_chunks * L, D_RNN), out_dtype),
            jax.ShapeDtypeStruct((n_seq, 1, D_RNN), F32),
        ],
        scratch_shapes=[pltpu.VMEM((1, LRU_SB), F32), pltpu.VMEM((8, LRU_SB), F32),
                        pltpu.VMEM((L + 8, LRU_SB), F32)],
        compiler_params=_cparams(3),
        name=name,
    )(proj, proj, h0, conv0, conv_w, conv_b.reshape(1, D_RNN), wg_sb, b_gate, lam.reshape(1, D_RNN))


def _router_kernel(x_ref, w_ref, b_ref, gate_ref, id_ref):
    logits = jnp.dot(x_ref[...], w_ref[...], preferred_element_type=F32) + b_ref[...]
    tm = logits.shape[0]
    lane = lax.broadcasted_iota(jnp.int32, (tm, 128), 1)
    lane_f = lane.astype(F32)
    is_group = (lane >= N_EXPERTS) & (lane < N_EXPERTS + N_EXP_GROUPS)
    gl = jnp.where(is_group, logits, -jnp.inf)
    g_max = jnp.max(gl, axis=1, keepdims=True)
    g_lane = jnp.min(jnp.where(gl == g_max, lane_f, 1e9), axis=1, keepdims=True)
    g_w = 1.0 / jnp.sum(jnp.exp(gl - g_max), axis=1, keepdims=True)
    g_idx = g_lane - float(N_EXPERTS)
    lo = g_idx * float(EXP_PER_GROUP)
    in_group = (lane_f >= lo) & (lane_f < lo + float(EXP_PER_GROUP))
    el = jnp.where(in_group, logits, -jnp.inf)
    e_max = jnp.max(el, axis=1, keepdims=True)
    i1 = jnp.min(jnp.where(el == e_max, lane_f, 1e9), axis=1, keepdims=True)
    e_sum = jnp.sum(jnp.exp(el - e_max), axis=1, keepdims=True)
    el2 = jnp.where(lane_f == i1, -jnp.inf, el)
    e2_max = jnp.max(el2, axis=1, keepdims=True)
    i2 = jnp.min(jnp.where(el2 == e2_max, lane_f, 1e9), axis=1, keepdims=True)
    p1 = 1.0 / e_sum
    p2 = jnp.exp(e2_max - e_max) / e_sum
    tot = p1 + p2
    gate1 = g_w * (p1 / tot)
    gate2 = g_w * (p2 / tot)
    gate_ref[...] = jnp.where(lane == 0, gate1, jnp.where(lane == 1, gate2, 0.0))
    id_ref[...] = jnp.where(lane == 0, i1, jnp.where(lane == 1, i2, 0.0)).astype(jnp.int32)


def _router(x, w_r, b_r, name):
    M, D = x.shape
    tm = 256
    return pl.pallas_call(
        _router_kernel,
        grid=(M // tm,),
        in_specs=[pl.BlockSpec((tm, D), lambda i: (i, 0)),
                  pl.BlockSpec((D, 128), lambda i: (0, 0)),
                  pl.BlockSpec((1, 128), lambda i: (0, 0))],
        out_specs=[pl.BlockSpec((tm, 128), lambda i: (i, 0)), pl.BlockSpec((tm, 128), lambda i: (i, 0))],
        out_shape=[jax.ShapeDtypeStruct((M, 128), F32), jax.ShapeDtypeStruct((M, 128), jnp.int32)],
        compiler_params=_cparams(1),
        name=name,
    )(x, w_r, b_r)


GATHER_ROWS = 256


def _row_copy(src_hbm, dst_hbm, sem, src_row, dst_row):
    return pltpu.make_async_copy(src_hbm.at[pl.ds(src_row, 1), :], dst_hbm.at[pl.ds(dst_row, 1), :], sem)


def _gather_kernel(idx_ref, src_hbm, dst_hbm, sems):
    i = pl.program_id(0)
    n = pl.num_programs(0)

    def issue(r, carry):
        row = i * GATHER_ROWS + r
        _row_copy(src_hbm, dst_hbm, sems.at[i % 2], idx_ref[row], row).start()
        return carry

    lax.fori_loop(0, GATHER_ROWS, issue, 0)

    def wait_batch(step):
        def body(r, carry):
            _row_copy(src_hbm, dst_hbm, sems.at[step % 2], 0, 0).wait()
            return carry
        lax.fori_loop(0, GATHER_ROWS, body, 0)

    @pl.when(i > 0)
    def _():
        wait_batch(i - 1)

    @pl.when(i == n - 1)
    def _():
        wait_batch(i)


def _gather_rows(src, idx, name):
    R = idx.shape[0]
    D = src.shape[1]
    return pl.pallas_call(
        _gather_kernel,
        grid_spec=pltpu.PrefetchScalarGridSpec(
            num_scalar_prefetch=1,
            grid=(R // GATHER_ROWS,),
            in_specs=[pl.BlockSpec(memory_space=pl.ANY)],
            out_specs=pl.BlockSpec(memory_space=pl.ANY),
            scratch_shapes=[pltpu.SemaphoreType.DMA((2,))],
        ),
        out_shape=jax.ShapeDtypeStruct((R, D), src.dtype),
        compiler_params=_cparams(1),
        name=name,
    )(idx, src)


def _ffn_kernel(te_ref, nv_ref, x_ref, w1_ref, w3_ref, w2_ref, o_ref, w1b, w3b, w2b):
    i = pl.program_id(0)
    prev = te_ref[jnp.maximum(i - 1, 0)]

    @pl.when((i == 0) | (te_ref[i] != prev))
    def _():
        w1b[...] = w1_ref[0].astype(BF16)
        w3b[...] = w3_ref[0].astype(BF16)
        w2b[...] = w2_ref[0].astype(BF16)

    @pl.when(i < nv_ref[0])
    def _():
        xb = x_ref[...].astype(BF16)
        h1 = jnp.dot(xb, w1b[...], preferred_element_type=F32)
        h3 = jnp.dot(xb, w3b[...], preferred_element_type=F32)
        hid = (_silu(h1) * h3).astype(BF16)
        o_ref[...] = jnp.dot(hid, w2b[...], preferred_element_type=F32)

    @pl.when(i >= nv_ref[0])
    def _():
        o_ref[...] = jnp.zeros_like(o_ref)


def _expert_ffn(xs, tile_expert, n_valid, w1, w3, w2, name):
    R, D = xs.shape
    tm = TILE_ROWS
    return pl.pallas_call(
        _ffn_kernel,
        grid_spec=pltpu.PrefetchScalarGridSpec(
            num_scalar_prefetch=2,
            grid=(R // tm,),
            in_specs=[
                pl.BlockSpec((tm, D), lambda i, te, nv: (i, 0)),
                pl.BlockSpec((1, D, D_EXPERT), lambda i, te, nv: (te[i], 0, 0)),
                pl.BlockSpec((1, D, D_EXPERT), lambda i, te, nv: (te[i], 0, 0)),
                pl.BlockSpec((1, D_EXPERT, D), lambda i, te, nv: (te[i], 0, 0)),
            ],
            out_specs=pl.BlockSpec((tm, D), lambda i, te, nv: (i, 0)),
            scratch_shapes=[pltpu.VMEM((D, D_EXPERT), BF16), pltpu.VMEM((D, D_EXPERT), BF16),
                            pltpu.VMEM((D_EXPERT, D), BF16)],
        ),
        out_shape=jax.ShapeDtypeStruct((R, D), F32),
        compiler_params=_cparams(1),
        name=name,
    )(tile_expert, n_valid, xs, w1, w3, w2)


def _route_plan(ids, n_tok):
    tm = TILE_ROWS
    n_pairs = TOP_K * n_tok
    n_rows = n_pairs + N_EXPERTS * tm
    n_tiles = n_rows // tm
    e_flat = ids.T.reshape(-1)
    order = jnp.argsort(e_flat, stable=True).astype(jnp.int32)
    e_sorted = e_flat[order]
    sizes = jnp.zeros((N_EXPERTS,), jnp.int32).at[e_flat].add(1)
    start = jnp.cumsum(sizes) - sizes
    psz = ((sizes + tm - 1) // tm) * tm
    pend = jnp.cumsum(psz)
    pstart = pend - psz
    dest = pstart[e_sorted] + (jnp.arange(n_pairs, dtype=jnp.int32) - start[e_sorted])
    row_src = jnp.zeros((n_rows,), jnp.int32).at[dest].set(order % n_tok)
    pos = jnp.zeros((n_pairs,), jnp.int32).at[order].set(dest)
    tile_start = jnp.arange(n_tiles, dtype=jnp.int32) * tm
    n_valid = (pend[-1] // tm).astype(jnp.int32)
    te = jnp.searchsorted(pend, tile_start, side='right').astype(jnp.int32)
    last_e = jnp.max(jnp.where(sizes > 0, jnp.arange(N_EXPERTS, dtype=jnp.int32), 0))
    te = jnp.where(tile_start < pend[-1], te, last_e)
    return row_src, pos, te, n_valid.reshape(1)


def _hier_moe_ln(x, xb, w_r, b_r, w1, w3, w2, ln_g, ln_b, tag):
    n_tok = x.shape[0]
    gates, ids = _router(xb, w_r.astype(BF16), b_r, name=f"router_{tag}")
    row_src, pos, te, n_valid = _route_plan(ids[:, :TOP_K], n_tok)
    xs = _gather_rows(x, row_src, name=f"dispatch_{tag}")
    ys = _expert_ffn(xs, te, n_valid, w1, w3, w2, name=f"experts_{tag}")
    y2 = _gather_rows(ys, pos, name=f"collect_{tag}")
    return _combine_ln(x, y2, gates, ln_g, ln_b, name=f"combine_ln_{tag}")


def _router_weights(w_group, b_group, w_expert, b_expert):
    w = jnp.concatenate([w_expert, w_group], axis=1)
    w = jnp.pad(w, ((0, 0), (0, 128 - w.shape[1])))
    b = jnp.pad(jnp.concatenate([b_expert, b_group]), (0, 128 - N_EXPERTS - N_EXP_GROUPS))
    return w, b.reshape(1, 128)


def _pad_conv_state(conv):
    return jnp.pad(conv, ((0, 0), (8 - (CONV_W - 1), 0), (0, 0)))


def kernel(x_prompt, x_sample, state_mlstm_C, state_mlstm_n, state_mlstm_m, state_ssd, state_ssd_conv,
           state_lru_h, state_lru_conv, meta_tokens, w_in_even, mlstm_gate_b, ssd_dt_bias, ssd_A_log, ssd_D,
           ssd_conv_w, ssd_conv_b, mlstm_norm_g, ssd_norm_g, w_out_even, w_in_odd, lru_conv_w, lru_conv_b,
           lru_w_gate, lru_b_gate, lru_lambda, w_out_odd, ln_g, ln_b, moe_w_group, moe_b_group,
           moe_w_expert, moe_b_expert, moe_w1, moe_w3, moe_w2):
    Bp, Tp, D = x_prompt.shape
    Bs, Ts, _ = x_sample.shape
    n_chunks_p = (N_META + Tp + CHUNK - 1) // CHUNK
    Tpp = n_chunks_p * CHUNK
    n_pad = Tpp - N_META - Tp
    n_p = Bp * Tpp
    n_s = Bs * Ts
    n_tok = n_p + n_s

    meta = jnp.broadcast_to(meta_tokens[None], (Bp, N_META, D))
    xp = jnp.concatenate([jnp.zeros((Bp, n_pad, D), F32), meta, x_prompt], axis=1).reshape(n_p, D)
    x0 = jnp.concatenate([xp, x_sample.reshape(n_s, D)], axis=0)
    x0b = x0.astype(BF16)

    grp = [dict(n_seq=Bp, n_chunks=n_chunks_p, L=CHUNK, row0=0, n_pad=n_pad),
           dict(n_seq=Bs, n_chunks=1, L=Ts, row0=n_p, n_pad=0)]

    def seq_view(a, gi):
        if gi == 0:
            return a[:n_p].reshape(Bp, Tpp, a.shape[1])
        return a[n_p:].reshape(Bs, Ts, a.shape[1])

    def tail_rows(a, gi, col0, ncol):
        nb, T, base = (Bp, Tpp, 0) if gi == 0 else (Bs, Ts, n_p)
        idx = (base + np.arange(nb)[:, None] * T + np.arange(T - (CONV_W - 1), T)[None, :]).reshape(-1)
        rows = jnp.take(a, jnp.asarray(idx, jnp.int32), axis=0)
        return rows[:, col0:col0 + ncol].reshape(nb, CONV_W - 1, ncol)

    e = 0
    w = w_in_even[e]
    src_small = 2 * A_HEADS * A_DK + 2 * A_INNER
    src_z = src_small + 2 * A_HEADS
    src_dt = src_z + B_INNER + B_CONV_DIM
    w_all = jnp.concatenate([w[:, :src_small], w[:, src_z:src_dt], w[:, src_small:src_z], w[:, src_dt:],
                             jnp.zeros((D, IN0_PAD - COL_SMALL - N_SMALL), F32)], axis=1).astype(BF16)
    proj = _mm(x0b, w_all, 512, 1280, name="in_proj_even")

    small = proj[:, COL_SMALL:COL_SMALL + N_SMALL]
    h_parts, y_parts, st = [], [], []
    for gi, g in enumerate(grp):
        sv = seq_view(small, gi)
        gates_t = jnp.swapaxes(sv, 1, 2)
        nb, T = sv.shape[0], sv.shape[1]
        gates_c = sv[:, :, 2 * A_HEADS:].reshape(nb, T, B_GROUPS, B_HPG).transpose(0, 2, 1, 3)
        if gi == 0:
            c0 = jnp.zeros((Bp, A_HEADS, A_DK, A_DV), F32)
            n0 = jnp.zeros((Bp, A_HEADS, 1, A_DK), F32)
            m0 = jnp.zeros((Bp, A_HEADS, 1, 128), F32)
            s0 = jnp.zeros((Bp, B_HEADS, B_P, B_N), F32)
            cv0 = jnp.zeros((Bp, 8, B_CONV_DIM), F32)
        else:
            c0 = state_mlstm_C[e]
            n0 = state_mlstm_n[e][:, :, None, :]
            m0 = jnp.broadcast_to(state_mlstm_m[e][:, :, None, None], (Bs, A_HEADS, 1, 128))
            s0 = state_ssd[e]
            cv0 = _pad_conv_state(state_ssd_conv[e])
        h_g, c_g, n_g, m_g = _mlstm(proj, gates_t, mlstm_gate_b[e], mlstm_norm_g[e], c0, n0, m0,
                                    name=f"mlstm_{gi}", **g)
        y_g, s_g = _ssd(proj, gates_t, gates_c, ssd_dt_bias[e], ssd_A_log[e], ssd_D[e], ssd_conv_w[e],
                        ssd_conv_b[e], ssd_norm_g[e], s0, cv0, name=f"ssd_{gi}", **g)
        h_parts.append(h_g.astype(BF16))
        y_parts.append(y_g.astype(BF16))
        sconv = tail_rows(proj, gi, COL_XBC, B_CONV_DIM)
        st.append((c_g[None], n_g[:, :, 0, :][None], m_g[:, :, 0, 0][None], s_g[None], sconv[None]))
    hy = jnp.concatenate([jnp.concatenate(h_parts, axis=0), jnp.concatenate(y_parts, axis=0)], axis=1)
    mix = _mm(hy, w_out_even[e].astype(BF16), 512, 1024, name="out_proj_even")
    x1, x1b = _add_ln(x0, mix, ln_g[0, 0], ln_b[0, 0], name="add_ln_0")
    w_r, b_r = _router_weights(moe_w_group[0], moe_b_group[0], moe_w_expert[0], moe_b_expert[0])
    x2, x2b = _hier_moe_ln(x1, x1b, w_r, b_r, moe_w1[0], moe_w3[0], moe_w2[0], ln_g[0, 1], ln_b[0, 1], "0")

    o = 0
    proj1 = _mm(x2b, w_in_odd[o].astype(BF16), 512, 1280, name="in_proj_odd")
    wg = lru_w_gate[o].reshape(2, N_LRU_SB, LRU_SB // LRU_BW, LRU_BW, LRU_BW)
    eye4 = jnp.eye(LRU_SB // LRU_BW, dtype=F32)
    wg_sb = jnp.einsum('gjaik,ab->gjaibk', wg, eye4).reshape(2, N_LRU_SB, LRU_SB, LRU_SB).astype(BF16)
    o_parts, st1 = [], []
    for gi, g in enumerate(grp):
        if gi == 0:
            h0 = jnp.zeros((Bp, 1, D_RNN), F32)
            cv0 = jnp.zeros((Bp, 8, D_RNN), F32)
        else:
            h0 = state_lru_h[o][:, None, :]
            cv0 = _pad_conv_state(state_lru_conv[o])
        o_g, hN = _lru(proj1, lru_conv_w[o], lru_conv_b[o], wg_sb, lru_b_gate[o], lru_lambda[o], h0, cv0,
                       name=f"lru_{gi}", **g)
        o_parts.append(o_g.astype(BF16))
        hconv = tail_rows(proj1, gi, D_RNN, D_RNN)
        st1.append((hN[:, 0, :][None], hconv[None]))
    mix1 = _mm(jnp.concatenate(o_parts, axis=0), w_out_odd[o].astype(BF16), 512, 1024, name="out_proj_odd")
    x3, x3b = _add_ln(x2, mix1, ln_g[1, 0], ln_b[1, 0], name="add_ln_1")
    w_r, b_r = _router_weights(moe_w_group[1], moe_b_group[1], moe_w_expert[1], moe_b_expert[1])
    x4, _ = _hier_moe_ln(x3, x3b, w_r, b_r, moe_w1[1], moe_w3[1], moe_w2[1], ln_g[1, 1], ln_b[1, 1], "1")

    y_prompt = x4[:n_p].reshape(Bp, Tpp, D)[:, Tpp - Tp:]
    y_sample = x4[n_p:].reshape(Bs, Ts, D)
    (pC, pn, pm, pS, pSc), (sC, sn, sm, sS, sSc) = st
    (pH, pHc), (sH, sHc) = st1
    return (y_prompt, y_sample, pC, pn, pm, pS, pSc, pH, pHc, sC, sn, sm, sS, sSc, sH, sHc)
```

```python
import functools
import math

import jax
import jax.numpy as jnp
import numpy as np
from jax import lax
from jax.experimental import pallas as pl
from jax.experimental.pallas import tpu as pltpu

F32 = jnp.float32
BF16 = jnp.bfloat16

D_MODEL = 2048
N_META = 16
CHUNK = 128
CONV_W = 4
A_HEADS = 8
A_DK = 128
A_DV = 256
A_INNER = A_HEADS * A_DV
B_HEADS = 32
B_P = 64
B_N = 128
B_GROUPS = 4
B_HPG = B_HEADS // B_GROUPS
B_INNER = B_HEADS * B_P
B_GW = B_INNER // B_GROUPS
B_CONV_DIM = B_INNER + 2 * B_GROUPS * B_N
D_RNN = 2560
LRU_BLOCKS = 16
LRU_BW = D_RNN // LRU_BLOCKS
LRU_C = 8.0
LRU_SB = 640
N_LRU_SB = D_RNN // LRU_SB
N_EXP_GROUPS = 4
EXP_PER_GROUP = 8
N_EXPERTS = N_EXP_GROUPS * EXP_PER_GROUP
TOP_K = 2
D_EXPERT = 512
DEPTH = 2
ALPHA = (2.0 * DEPTH) ** 0.25

COL_Q = 0
COL_K = A_HEADS * A_DK
COL_V = 2 * A_HEADS * A_DK
COL_O = COL_V + A_INNER
COL_Z = COL_O + A_INNER
COL_XBC = COL_Z + B_INNER
COL_SMALL = COL_XBC + B_CONV_DIM
N_SMALL = 2 * A_HEADS + B_HEADS
IN0_PAD = 11520

NEG_BIG = -1e30
VMEM_LIMIT_BYTES = 48 * 1024 * 1024
TILE_ROWS = 256
LN_ROWS = 256
SLAB = D_MODEL // 128


def _cparams(n_axes):
    return pltpu.CompilerParams(dimension_semantics=("arbitrary",) * n_axes,
                                vmem_limit_bytes=VMEM_LIMIT_BYTES)


def _smem_spec():
    return pl.BlockSpec(memory_space=pltpu.SMEM)


def _mm_kernel(x_ref, w_ref, o_ref):
    o_ref[...] = jnp.dot(x_ref[...], w_ref[...], preferred_element_type=F32).astype(o_ref.dtype)


def _mm(x, w, bm, bn, name):
    M, K = x.shape
    N = w.shape[1]
    return pl.pallas_call(
        _mm_kernel,
        grid=(N // bn, M // bm),
        in_specs=[pl.BlockSpec((bm, K), lambda j, i: (i, 0)),
                  pl.BlockSpec((K, bn), lambda j, i: (0, j))],
        out_specs=pl.BlockSpec((bm, bn), lambda j, i: (i, j)),
        out_shape=jax.ShapeDtypeStruct((M, N), F32),
        compiler_params=_cparams(2),
        name=name,
    )(x, w)


def _layer_norm_rows(y, g, b):
    mu = jnp.mean(y, axis=-1, keepdims=True)
    yc = y - mu
    var = jnp.mean(yc * yc, axis=-1, keepdims=True)
    return yc * lax.rsqrt(var + 1e-5) * g + b


def _slab_store(ref, val):
    tm = val.shape[0]
    for j in range(SLAB):
        ref[pl.ds(j, tm, stride=SLAB), :] = val[:, j * 128:(j + 1) * 128]


def _slab_piece(ref, j, tm):
    return ref[pl.ds(j, tm, stride=SLAB), :]


def _add_ln_kernel(x_ref, m_ref, g_ref, b_ref, os_ref, ob_ref):
    y = _layer_norm_rows(ALPHA * x_ref[...] + m_ref[...], g_ref[...], b_ref[...])
    _slab_store(os_ref, y)
    ob_ref[...] = y.astype(BF16)


def _add_ln(x, mix, g, b, name):
    M, D = x.shape
    tm = LN_ROWS
    row = pl.BlockSpec((tm, D), lambda i: (i, 0))
    vec = pl.BlockSpec((1, D), lambda i: (0, 0))
    return pl.pallas_call(
        _add_ln_kernel,
        grid=(M // tm,),
        in_specs=[row, row, vec, vec],
        out_specs=[pl.BlockSpec((tm * SLAB, 128), lambda i: (i, 0)), row],
        out_shape=[jax.ShapeDtypeStruct((M * SLAB, 128), F32), jax.ShapeDtypeStruct((M, D), BF16)],
        compiler_params=_cparams(1),
        name=name,
    )(x, mix, g.reshape(1, D), b.reshape(1, D))


def _combine_ln_kernel(x_ref, y0_ref, y1_ref, gate_ref, g_ref, b_ref, o_ref, ob_ref, v_s):
    tm = gate_ref.shape[0]
    gate = gate_ref[...]
    g0 = gate[:, 0:1]
    g1 = gate[:, 1:2]
    for j in range(SLAB):
        v_s[:, j * 128:(j + 1) * 128] = (ALPHA * _slab_piece(x_ref, j, tm) + g0 * _slab_piece(y0_ref, j, tm)
                                         + g1 * _slab_piece(y1_ref, j, tm))
    y = _layer_norm_rows(v_s[...], g_ref[...], b_ref[...])
    o_ref[...] = y
    ob_ref[...] = y.astype(BF16)


def _combine_ln(xs, y2, gates, g, b, name):
    M = gates.shape[0]
    D = D_MODEL
    tm = LN_ROWS
    nblk = M // tm
    row = pl.BlockSpec((tm, D), lambda i: (i, 0))
    slab = pl.BlockSpec((tm * SLAB, 128), lambda i: (i, 0))
    slab_hi = pl.BlockSpec((tm * SLAB, 128), lambda i: (i + nblk, 0))
    vec = pl.BlockSpec((1, D), lambda i: (0, 0))
    return pl.pallas_call(
        _combine_ln_kernel,
        grid=(nblk,),
        in_specs=[slab, slab, slab_hi, pl.BlockSpec((tm, 128), lambda i: (i, 0)), vec, vec],
        out_specs=[row, row],
        out_shape=[jax.ShapeDtypeStruct((M, D), F32), jax.ShapeDtypeStruct((M, D), BF16)],
        scratch_shapes=[pltpu.VMEM((tm, D), F32)],
        compiler_params=_cparams(1),
        name=name,
    )(xs, y2, y2, gates, g.reshape(1, D), b.reshape(1, D))


def _softplus(x):
    return jnp.maximum(x, 0.0) + jnp.log1p(jnp.exp(-jnp.abs(x)))


def _sigmoid(x):
    return 1.0 / (1.0 + jnp.exp(-x))


def _silu(x):
    return x * _sigmoid(x)


def _row_to_col(row, eye):
    return jnp.sum(jnp.where(eye, row, 0.0), axis=1, keepdims=True)


def _causal_conv(x, carry_ref, buf_ref, w_ref, b_ref, L):
    buf_ref[0:8, :] = carry_ref[...]
    buf_ref[8:8 + L, :] = x
    acc = b_ref[...] + w_ref[0:1, :] * buf_ref[5:5 + L, :]
    for j in range(1, CONV_W):
        acc = acc + w_ref[j:j + 1, :] * buf_ref[5 + j:5 + j + L, :]
    carry_ref[...] = buf_ref[L:L + 8, :]
    return acc


def _mlstm_kernel(gb_ref, q_ref, k_ref, v_ref, o_ref, gt_ref, c0_ref, n0_ref, m0_ref, ng_ref,
                  h_ref, c_out_ref, n_out_ref, m_out_ref, c_s, n_s, m_s, *, L, n_pad):
    c = pl.program_id(1)

    @pl.when(c == 0)
    def _():
        c_s[...] = c0_ref[0]
        n_s[...] = n0_ref[0]
        m_s[...] = m0_ref[0]

    rows = lax.broadcasted_iota(jnp.int32, (L, L), 0)
    cols = lax.broadcasted_iota(jnp.int32, (L, L), 1)
    eye = rows == cols
    causal = cols <= rows

    if n_pad:
        pad = (lax.broadcasted_iota(jnp.int32, (1, L), 1) < n_pad) & (c == 0)

    for hd in range(A_HEADS):
        li = gt_ref[0, hd:hd + 1, :] + gb_ref[0, hd]
        fr = gt_ref[0, A_HEADS + hd:A_HEADS + hd + 1, :] + gb_ref[1, hd]
        lf = jnp.minimum(fr, 0.0) - jnp.log1p(jnp.exp(-jnp.abs(fr)))
        if n_pad:
            li = jnp.where(pad, NEG_BIG, li)
            lf = jnp.where(pad, 0.0, lf)

        lf_col = _row_to_col(lf, eye)
        b_col = jnp.sum(jnp.where(causal, lf, 0.0), axis=1, keepdims=True)
        b_row = jnp.sum(jnp.where(rows <= cols, lf_col, 0.0), axis=0, keepdims=True)
        m_prev = m_s[hd][:, 0:1]

        d = jnp.where(causal, b_col - b_row + li, NEG_BIG)
        inter = b_col + m_prev
        m_t = jnp.maximum(inter, jnp.max(d, axis=1, keepdims=True))
        w_intra = jnp.exp(d - m_t)
        w_inter = jnp.exp(inter - m_t)

        q = q_ref[:, hd * A_DK:(hd + 1) * A_DK] * (A_DK ** -0.5)
        k = k_ref[:, hd * A_DK:(hd + 1) * A_DK]
        v = v_ref[:, hd * A_DV:(hd + 1) * A_DV]
        qb = q.astype(BF16)
        kb = k.astype(BF16)
        vb = v.astype(BF16)
        s = lax.dot_general(qb, kb, (((1,), (1,)), ((), ())), preferred_element_type=F32) * w_intra
        c_prev = c_s[hd]
        n_prev = n_s[hd]
        num = jnp.dot(s.astype(BF16), vb, preferred_element_type=F32)
        num = num + w_inter * jnp.dot(qb, c_prev.astype(BF16), preferred_element_type=F32)
        qn = jnp.sum(qb.astype(F32) * n_prev.astype(BF16).astype(F32), axis=1, keepdims=True)
        nq = jnp.sum(s, axis=1, keepdims=True) + w_inter * qn
        hh = num / jnp.maximum(jnp.abs(nq), jnp.exp(-m_t))

        hn = hh * lax.rsqrt(jnp.mean(hh * hh, axis=-1, keepdims=True) + 1e-6)
        hn = hn * ng_ref[:, hd * A_DV:(hd + 1) * A_DV]
        h_ref[:, hd * A_DV:(hd + 1) * A_DV] = (
            hn * _sigmoid(o_ref[:, hd * A_DV:(hd + 1) * A_DV])).astype(h_ref.dtype)

        b_last = b_row[:, L - 1:L]
        ws_log = b_last - b_row + li
        m_new = jnp.maximum(b_last + m_prev, jnp.max(ws_log, axis=1, keepdims=True))
        ws_col = _row_to_col(jnp.exp(ws_log - m_new), eye)
        wc = jnp.exp(b_last + m_prev - m_new)
        kw = k * ws_col
        c_new = wc * c_prev + lax.dot_general(kw.astype(BF16), vb, (((0,), (0,)), ((), ())),
                                              preferred_element_type=F32)
        n_new = wc * n_prev + jnp.sum(ws_col.astype(BF16).astype(F32) * kb.astype(F32), axis=0, keepdims=True)
        m_new_b = jnp.broadcast_to(m_new, (1, 128))
        c_s[hd] = c_new
        n_s[hd] = n_new
        m_s[hd] = m_new_b
        c_out_ref[0, hd] = c_new
        n_out_ref[0, hd] = n_new
        m_out_ref[0, hd] = m_new_b


def _mlstm(proj, gates_t, gate_b, norm_g, c0, n0, m0, *, n_seq, n_chunks, L, row0, n_pad, name):
    rb0 = row0 // L

    def rblk(b, c):
        return rb0 + b * n_chunks + c

    out_dtype = BF16 if L % 16 == 0 else F32
    qk_w = A_HEADS * A_DK
    state_map = lambda b, c: (b, 0, 0, 0)
    kern = functools.partial(_mlstm_kernel, L=L, n_pad=n_pad)
    return pl.pallas_call(
        kern,
        grid=(n_seq, n_chunks),
        in_specs=[
            _smem_spec(),
            pl.BlockSpec((L, qk_w), lambda b, c: (rblk(b, c), COL_Q // qk_w)),
            pl.BlockSpec((L, qk_w), lambda b, c: (rblk(b, c), COL_K // qk_w)),
            pl.BlockSpec((L, A_INNER), lambda b, c: (rblk(b, c), COL_V // A_INNER)),
            pl.BlockSpec((L, A_INNER), lambda b, c: (rblk(b, c), COL_O // A_INNER)),
            pl.BlockSpec((1, N_SMALL, L), lambda b, c: (b, 0, c)),
            pl.BlockSpec((1, A_HEADS, A_DK, A_DV), state_map),
            pl.BlockSpec((1, A_HEADS, 1, A_DK), state_map),
            pl.BlockSpec((1, A_HEADS, 1, 128), state_map),
            pl.BlockSpec((1, A_INNER), lambda b, c: (0, 0)),
        ],
        out_specs=[
            pl.BlockSpec((L, A_INNER), lambda b, c: (b * n_chunks + c, 0)),
            pl.BlockSpec((1, A_HEADS, A_DK, A_DV), state_map),
            pl.BlockSpec((1, A_HEADS, 1, A_DK), state_map),
            pl.BlockSpec((1, A_HEADS, 1, 128), state_map),
        ],
        out_shape=[
            jax.ShapeDtypeStruct((n_seq * n_chunks * L, A_INNER), out_dtype),
            jax.ShapeDtypeStruct((n_seq, A_HEADS, A_DK, A_DV), F32),
            jax.ShapeDtypeStruct((n_seq, A_HEADS, 1, A_DK), F32),
            jax.ShapeDtypeStruct((n_seq, A_HEADS, 1, 128), F32),
        ],
        scratch_shapes=[pltpu.VMEM((A_HEADS, A_DK, A_DV), F32), pltpu.VMEM((A_HEADS, 1, A_DK), F32),
                        pltpu.VMEM((A_HEADS, 1, 128), F32)],
        compiler_params=_cparams(2),
        name=name,
    )(gate_b, proj, proj, proj, proj, gates_t, c0, n0, m0, norm_g.reshape(1, A_INNER))


def _ssd_kernel(dtb_ref, alog_ref, dsk_ref, xs_ref, bm_ref, cm_ref, z_ref, gt_ref, gc_ref, s0_ref,
                cx0_ref, cb0_ref, cc0_ref, wx_ref, wb_ref, wc_ref, bx_ref, bb_ref, bc_ref, ng_ref,
                y_ref, s_out_ref,
                s_s, cx_s, cb_s, cc_s, bufx, bufb, bufc, ybuf, *, L, n_pad):
    g = pl.program_id(1)
    c = pl.program_id(2)

    @pl.when(c == 0)
    def _():
        s_s[...] = s0_ref[0]
        cx_s[...] = cx0_ref[0]
        cb_s[...] = cb0_ref[0]
        cc_s[...] = cc0_ref[0]

    rows = lax.broadcasted_iota(jnp.int32, (L, L), 0)
    cols = lax.broadcasted_iota(jnp.int32, (L, L), 1)
    causal = cols <= rows
    if n_pad:
        pad_row = (lax.broadcasted_iota(jnp.int32, (1, L), 1) < n_pad) & (c == 0)
        pad_col = (lax.broadcasted_iota(jnp.int32, (L, 1), 0) < n_pad) & (c == 0)

    xs_raw = xs_ref[...]
    bm_raw = bm_ref[...]
    cm_raw = cm_ref[...]
    if n_pad:
        xs_raw = jnp.where(pad_col, 0.0, xs_raw)
        bm_raw = jnp.where(pad_col, 0.0, bm_raw)
        cm_raw = jnp.where(pad_col, 0.0, cm_raw)
    xs = _silu(_causal_conv(xs_raw, cx_s, bufx, wx_ref, bx_ref, L))
    bm = _silu(_causal_conv(bm_raw, cb_s, bufb, wb_ref, bb_ref, L))
    cm = _silu(_causal_conv(cm_raw, cc_s, bufc, wc_ref, bc_ref, L))
    bmb = bm.astype(BF16)
    cmb = cm.astype(BF16)
    cb = lax.dot_general(cmb, bmb, (((1,), (1,)), ((), ())), preferred_element_type=F32)

    gc = gc_ref[0, 0]
    for hh in range(B_HPG):
        head = g * B_HPG + hh
        dt_bias = dtb_ref[head]
        a_neg = -jnp.exp(alog_ref[head])
        dt_row = _softplus(gt_ref[0, pl.ds(2 * A_HEADS + head, 1), :] + dt_bias)
        dt_col = _softplus(gc[:, hh:hh + 1] + dt_bias)
        if n_pad:
            dt_row = jnp.where(pad_row, 0.0, dt_row)
            dt_col = jnp.where(pad_col, 0.0, dt_col)
        a_row = dt_row * a_neg
        a_col = dt_col * a_neg
        b_col = jnp.sum(jnp.where(causal, a_row, 0.0), axis=1, keepdims=True)
        b_row = jnp.sum(jnp.where(rows <= cols, a_col, 0.0), axis=0, keepdims=True)
        decay = jnp.exp(jnp.where(causal, b_col - b_row, NEG_BIG))
        x_h = xs[:, hh * B_P:(hh + 1) * B_P]
        xdt = x_h * dt_col
        s_prev = s_s[hh]
        y_h = jnp.dot((cb * decay).astype(BF16), xdt.astype(BF16), preferred_element_type=F32)
        y_h = y_h + jnp.exp(b_col) * lax.dot_general(cmb, s_prev.astype(BF16), (((1,), (1,)), ((), ())),
                                                     preferred_element_type=F32)
        b_last = b_row[:, L - 1:L]
        w_col = jnp.exp(b_last - b_col)
        s_new = jnp.exp(b_last) * s_prev + lax.dot_general((xdt * w_col).astype(BF16), bmb,
                                                           (((0,), (0,)), ((), ())),
                                                           preferred_element_type=F32)
        s_s[hh] = s_new
        s_out_ref[0, hh] = s_new
        ybuf[:, hh * B_P:(hh + 1) * B_P] = y_h + dsk_ref[head] * x_h

    y = ybuf[...] * _silu(z_ref[...])
    yn = y * lax.rsqrt(jnp.mean(y * y, axis=-1, keepdims=True) + 1e-6) * ng_ref[...]
    y_ref[...] = yn.astype(y_ref.dtype)


def _ssd(proj, gates_t, gates_c, dt_bias, a_log, d_skip, conv_w, conv_b, norm_g, s0, conv0, *,
         n_seq, n_chunks, L, row0, n_pad, name):
    rb0 = row0 // L

    def rblk(b, c):
        return rb0 + b * n_chunks + c

    out_dtype = BF16 if L % 16 == 0 else F32
    cw = conv_w
    cbias = conv_b.reshape(1, B_CONV_DIM)
    xoff = COL_XBC // B_GW
    boff = (COL_XBC + B_INNER) // B_N
    coff = (COL_XBC + B_INNER + B_GROUPS * B_N) // B_N
    kern = functools.partial(_ssd_kernel, L=L, n_pad=n_pad)
    return pl.pallas_call(
        kern,
        grid=(n_seq, B_GROUPS, n_chunks),
        in_specs=[
            _smem_spec(), _smem_spec(), _smem_spec(),
            pl.BlockSpec((L, B_GW), lambda b, g, c: (rblk(b, c), xoff + g)),
            pl.BlockSpec((L, B_N), lambda b, g, c: (rblk(b, c), boff + g)),
            pl.BlockSpec((L, B_N), lambda b, g, c: (rblk(b, c), coff + g)),
            pl.BlockSpec((L, B_GW), lambda b, g, c: (rblk(b, c), COL_Z // B_GW + g)),
            pl.BlockSpec((1, N_SMALL, L), lambda b, g, c: (b, 0, c)),
            pl.BlockSpec((1, 1, L, B_HPG), lambda b, g, c: (b, g, c, 0)),
            pl.BlockSpec((1, B_HPG, B_P, B_N), lambda b, g, c: (b, g, 0, 0)),
            pl.BlockSpec((1, 8, B_GW), lambda b, g, c: (b, 0, g)),
            pl.BlockSpec((1, 8, B_N), lambda b, g, c: (b, 0, B_INNER // B_N + g)),
            pl.BlockSpec((1, 8, B_N), lambda b, g, c: (b, 0, B_INNER // B_N + B_GROUPS + g)),
            pl.BlockSpec((CONV_W, B_GW), lambda b, g, c: (0, g)),
            pl.BlockSpec((CONV_W, B_N), lambda b, g, c: (0, B_INNER // B_N + g)),
            pl.BlockSpec((CONV_W, B_N), lambda b, g, c: (0, B_INNER // B_N + B_GROUPS + g)),
            pl.BlockSpec((1, B_GW), lambda b, g, c: (0, g)),
            pl.BlockSpec((1, B_N), lambda b, g, c: (0, B_INNER // B_N + g)),
            pl.BlockSpec((1, B_N), lambda b, g, c: (0, B_INNER // B_N + B_GROUPS + g)),
            pl.BlockSpec((1, B_GW), lambda b, g, c: (0, g)),
        ],
        out_specs=[
            pl.BlockSpec((L, B_GW), lambda b, g, c: (b * n_chunks + c, g)),
            pl.BlockSpec((1, B_HPG, B_P, B_N), lambda b, g, c: (b, g, 0, 0)),
        ],
        out_shape=[
            jax.ShapeDtypeStruct((n_seq * n_chunks * L, B_INNER), out_dtype),
            jax.ShapeDtypeStruct((n_seq, B_HEADS, B_P, B_N), F32),
        ],
        scratch_shapes=[
            pltpu.VMEM((B_HPG, B_P, B_N), F32),
            pltpu.VMEM((8, B_GW), F32), pltpu.VMEM((8, B_N), F32), pltpu.VMEM((8, B_N), F32),
            pltpu.VMEM((L + 8, B_GW), F32), pltpu.VMEM((L + 8, B_N), F32), pltpu.VMEM((L + 8, B_N), F32),
            pltpu.VMEM((L, B_GW), F32),
        ],
        compiler_params=_cparams(3),
        name=name,
    )(dt_bias, a_log, d_skip, proj, proj, proj, proj, gates_t, gates_c, s0, conv0, conv0, conv0,
      cw, cw, cw, cbias, cbias, cbias, norm_g.reshape(1, B_INNER))


def _gelu_tanh(x):
    return 0.5 * x * (1.0 + jnp.tanh(math.sqrt(2.0 / math.pi) * (x + 0.044715 * (x * x * x))))


def _expm1_nonpos(z):
    e = jnp.exp(z)
    safe = (e < 1.0) & (z > -1.0)
    return jnp.where(safe, (e - 1.0) * z / jnp.log(jnp.where(safe, e, 0.5)), jnp.where(e == 1.0, z, e - 1.0))


def _lru_kernel(gi_ref, xb_ref, h0_ref, cv0_ref, cw_ref, cb_ref, wg_ref, bg_ref, lam_ref,
                o_ref, h_out_ref, h_s, cv_s, buf, *, L, n_pad):
    c = pl.program_id(2)

    @pl.when(c == 0)
    def _():
        h_s[...] = h0_ref[0]
        cv_s[...] = cv0_ref[0]

    row_id = lax.broadcasted_iota(jnp.int32, (L, 1), 0)
    x_raw = xb_ref[...]
    if n_pad:
        pad_col = (row_id < n_pad) & (c == 0)
        x_raw = jnp.where(pad_col, 0.0, x_raw)
    xf = _causal_conv(x_raw, cv_s, buf, cw_ref, cb_ref, L)
    xfb = xf.astype(BF16)
    r = _sigmoid(jnp.dot(xfb, wg_ref[0, 0], preferred_element_type=F32) + bg_ref[0:1, :])
    i = _sigmoid(jnp.dot(xfb, wg_ref[1, 0], preferred_element_type=F32) + bg_ref[1:2, :])
    log_a = (-LRU_C) * r * _softplus(-lam_ref[...])
    a = jnp.exp(log_a)
    u = jnp.sqrt(-_expm1_nonpos(2.0 * log_a)) * (i * xf)
    if n_pad:
        a = jnp.where(pad_col, 1.0, a)
        u = jnp.where(pad_col, 0.0, u)
    k = 1
    while k < L:
        keep = row_id >= k
        a_sh = jnp.where(keep, pltpu.roll(a, k, 0), 1.0)
        u_sh = jnp.where(keep, pltpu.roll(u, k, 0), 0.0)
        u = a * u_sh + u
        a = a * a_sh
        k *= 2
    hs = a * h_s[...] + u
    h_last = hs[L - 1:L, :]
    h_s[...] = h_last
    h_out_ref[0] = h_last
    o_ref[...] = (hs * _gelu_tanh(gi_ref[...])).astype(o_ref.dtype)


def _lru(proj, conv_w, conv_b, wg_sb, b_gate, lam, h0, conv0, *, n_seq, n_chunks, L, row0, n_pad, name):
    rb0 = row0 // L

    def rblk(b, c):
        return rb0 + b * n_chunks + c

    out_dtype = BF16 if L % 16 == 0 else F32
    kern = functools.partial(_lru_kernel, L=L, n_pad=n_pad)
    return pl.pallas_call(
        kern,
        grid=(n_seq, N_LRU_SB, n_chunks),
        in_specs=[
            pl.BlockSpec((L, LRU_SB), lambda b, j, c: (rblk(b, c), j)),
            pl.BlockSpec((L, LRU_SB), lambda b, j, c: (rblk(b, c), N_LRU_SB + j)),
            pl.BlockSpec((1, 1, LRU_SB), lambda b, j, c: (b, 0, j)),
            pl.BlockSpec((1, 8, LRU_SB), lambda b, j, c: (b, 0, j)),
            pl.BlockSpec((CONV_W, LRU_SB), lambda b, j, c: (0, j)),
            pl.BlockSpec((1, LRU_SB), lambda b, j, c: (0, j)),
            pl.BlockSpec((2, 1, LRU_SB, LRU_SB), lambda b, j, c: (0, j, 0, 0)),
            pl.BlockSpec((2, LRU_SB), lambda b, j, c: (0, j)),
            pl.BlockSpec((1, LRU_SB), lambda b, j, c: (0, j)),
        ],
        out_specs=[
            pl.BlockSpec((L, LRU_SB), lambda b, j, c: (b * n_chunks + c, j)),
            pl.BlockSpec((1, 1, LRU_SB), lambda b, j, c: (b, 0, j)),
        ],
        out_shape=[
            jax.ShapeDtypeStruct((n_seq * n_chunks * L, D_RNN), out_dtype),
            jax.ShapeDtypeStruct((n_seq, 1, D_RNN), F32),
        ],
        scratch_shapes=[pltpu.VMEM((1, LRU_SB), F32), pltpu.VMEM((8, LRU_SB), F32),
                        pltpu.VMEM((L + 8, LRU_SB), F32)],
        compiler_params=_cparams(3),
        name=name,
    )(proj, proj, h0, conv0, conv_w, conv_b.reshape(1, D_RNN), wg_sb, b_gate, lam.reshape(1, D_RNN))


def _router_kernel(x_ref, w_ref, b_ref, gate_ref, id_ref):
    logits = jnp.dot(x_ref[...], w_ref[...], preferred_element_type=F32) + b_ref[...]
    tm = logits.shape[0]
    lane = lax.broadcasted_iota(jnp.int32, (tm, 128), 1)
    lane_f = lane.astype(F32)
    is_group = (lane >= N_EXPERTS) & (lane < N_EXPERTS + N_EXP_GROUPS)
    gl = jnp.where(is_group, logits, -jnp.inf)
    g_max = jnp.max(gl, axis=1, keepdims=True)
    g_lane = jnp.min(jnp.where(gl == g_max, lane_f, 1e9), axis=1, keepdims=True)
    g_w = 1.0 / jnp.sum(jnp.exp(gl - g_max), axis=1, keepdims=True)
    g_idx = g_lane - float(N_EXPERTS)
    lo = g_idx * float(EXP_PER_GROUP)
    in_group = (lane_f >= lo) & (lane_f < lo + float(EXP_PER_GROUP))
    el = jnp.where(in_group, logits, -jnp.inf)
    e_max = jnp.max(el, axis=1, keepdims=True)
    i1 = jnp.min(jnp.where(el == e_max, lane_f, 1e9), axis=1, keepdims=True)
    e_sum = jnp.sum(jnp.exp(el - e_max), axis=1, keepdims=True)
    el2 = jnp.where(lane_f == i1, -jnp.inf, el)
    e2_max = jnp.max(el2, axis=1, keepdims=True)
    i2 = jnp.min(jnp.where(el2 == e2_max, lane_f, 1e9), axis=1, keepdims=True)
    p1 = 1.0 / e_sum
    p2 = jnp.exp(e2_max - e_max) / e_sum
    tot = p1 + p2
    gate1 = g_w * (p1 / tot)
    gate2 = g_w * (p2 / tot)
    gate_ref[...] = jnp.where(lane == 0, gate1, jnp.where(lane == 1, gate2, 0.0))
    id_ref[...] = jnp.where(lane == 0, i1, jnp.where(lane == 1, i2, 0.0)).astype(jnp.int32)


def _router(x, w_r, b_r, name):
    M, D = x.shape
    tm = 256
    return pl.pallas_call(
        _router_kernel,
        grid=(M // tm,),
        in_specs=[pl.BlockSpec((tm, D), lambda i: (i, 0)),
                  pl.BlockSpec((D, 128), lambda i: (0, 0)),
                  pl.BlockSpec((1, 128), lambda i: (0, 0))],
        out_specs=[pl.BlockSpec((tm, 128), lambda i: (i, 0)), pl.BlockSpec((tm, 128), lambda i: (i, 0))],
        out_shape=[jax.ShapeDtypeStruct((M, 128), F32), jax.ShapeDtypeStruct((M, 128), jnp.int32)],
        compiler_params=_cparams(1),
        name=name,
    )(x, w_r, b_r)


GATHER_ROWS = 256


def _row_copy(src_hbm, dst_hbm, sem, src_row, dst_row):
    return pltpu.make_async_copy(src_hbm.at[src_row], dst_hbm.at[dst_row], sem)


def _gather_kernel(idx_ref, src_hbm, dst_hbm, sems):
    i = pl.program_id(0)
    n = pl.num_programs(0)

    def issue(r, carry):
        row = i * GATHER_ROWS + r
        _row_copy(src_hbm, dst_hbm, sems.at[i % 2], idx_ref[row], row).start()
        return carry

    lax.fori_loop(0, GATHER_ROWS, issue, 0)

    def wait_batch(step):
        def body(r, carry):
            _row_copy(src_hbm, dst_hbm, sems.at[step % 2], 0, 0).wait()
            return carry
        lax.fori_loop(0, GATHER_ROWS, body, 0)

    @pl.when(i > 0)
    def _():
        wait_batch(i - 1)

    @pl.when(i == n - 1)
    def _():
        wait_batch(i)


def _gather_rows(src_slab, idx, name):
    R = idx.shape[0]
    src = src_slab.reshape(src_slab.shape[0] // SLAB, SLAB, 128)
    out = pl.pallas_call(
        _gather_kernel,
        grid_spec=pltpu.PrefetchScalarGridSpec(
            num_scalar_prefetch=1,
            grid=(R // GATHER_ROWS,),
            in_specs=[pl.BlockSpec(memory_space=pl.ANY)],
            out_specs=pl.BlockSpec(memory_space=pl.ANY),
            scratch_shapes=[pltpu.SemaphoreType.DMA((2,))],
        ),
        out_shape=jax.ShapeDtypeStruct((R, SLAB, 128), src.dtype),
        compiler_params=_cparams(1),
        name=name,
    )(idx, src)
    return out.reshape(R * SLAB, 128)


def _ffn_kernel(te_ref, nv_ref, x_ref, w1_ref, w3_ref, w2_ref, o_ref, w1b, w3b, w2b, xb_s):
    i = pl.program_id(0)
    tm = xb_s.shape[0]
    prev = te_ref[jnp.maximum(i - 1, 0)]

    @pl.when((i == 0) | (te_ref[i] != prev))
    def _():
        w1b[...] = w1_ref[0].astype(BF16)
        w3b[...] = w3_ref[0].astype(BF16)
        w2b[...] = w2_ref[0].astype(BF16)

    @pl.when(i < nv_ref[0])
    def _():
        for j in range(SLAB):
            xb_s[:, j * 128:(j + 1) * 128] = _slab_piece(x_ref, j, tm).astype(BF16)
        xb = xb_s[...]
        h1 = jnp.dot(xb, w1b[...], preferred_element_type=F32)
        h3 = jnp.dot(xb, w3b[...], preferred_element_type=F32)
        hid = (_silu(h1) * h3).astype(BF16)
        _slab_store(o_ref, jnp.dot(hid, w2b[...], preferred_element_type=F32))

    @pl.when(i >= nv_ref[0])
    def _():
        o_ref[...] = jnp.zeros_like(o_ref)


def _expert_ffn(xs, tile_expert, n_valid, w1, w3, w2, name):
    R = xs.shape[0] // SLAB
    D = D_MODEL
    tm = TILE_ROWS
    return pl.pallas_call(
        _ffn_kernel,
        grid_spec=pltpu.PrefetchScalarGridSpec(
            num_scalar_prefetch=2,
            grid=(R // tm,),
            in_specs=[
                pl.BlockSpec((tm * SLAB, 128), lambda i, te, nv: (i, 0)),
                pl.BlockSpec((1, D, D_EXPERT), lambda i, te, nv: (te[i], 0, 0)),
                pl.BlockSpec((1, D, D_EXPERT), lambda i, te, nv: (te[i], 0, 0)),
                pl.BlockSpec((1, D_EXPERT, D), lambda i, te, nv: (te[i], 0, 0)),
            ],
            out_specs=pl.BlockSpec((tm * SLAB, 128), lambda i, te, nv: (i, 0)),
            scratch_shapes=[pltpu.VMEM((D, D_EXPERT), BF16), pltpu.VMEM((D, D_EXPERT), BF16),
                            pltpu.VMEM((D_EXPERT, D), BF16), pltpu.VMEM((tm, D), BF16)],
        ),
        out_shape=jax.ShapeDtypeStruct((R * SLAB, 128), F32),
        compiler_params=_cparams(1),
        name=name,
    )(tile_expert, n_valid, xs, w1, w3, w2)


def _route_plan(ids, n_tok):
    tm = TILE_ROWS
    n_pairs = TOP_K * n_tok
    n_rows = n_pairs + N_EXPERTS * tm
    n_tiles = n_rows // tm
    e_flat = ids.T.reshape(-1)
    order = jnp.argsort(e_flat, stable=True).astype(jnp.int32)
    e_sorted = e_flat[order]
    sizes = jnp.zeros((N_EXPERTS,), jnp.int32).at[e_flat].add(1)
    start = jnp.cumsum(sizes) - sizes
    psz = ((sizes + tm - 1) // tm) * tm
    pend = jnp.cumsum(psz)
    pstart = pend - psz
    dest = pstart[e_sorted] + (jnp.arange(n_pairs, dtype=jnp.int32) - start[e_sorted])
    row_src = jnp.zeros((n_rows,), jnp.int32).at[dest].set(order % n_tok)
    pos = jnp.zeros((n_pairs,), jnp.int32).at[order].set(dest)
    tile_start = jnp.arange(n_tiles, dtype=jnp.int32) * tm
    n_valid = (pend[-1] // tm).astype(jnp.int32)
    te = jnp.searchsorted(pend, tile_start, side='right').astype(jnp.int32)
    last_e = jnp.max(jnp.where(sizes > 0, jnp.arange(N_EXPERTS, dtype=jnp.int32), 0))
    te = jnp.where(tile_start < pend[-1], te, last_e)
    return row_src, pos, te, n_valid.reshape(1)


def _hier_moe_ln(x_slab, xb, w_r, b_r, w1, w3, w2, ln_g, ln_b, tag):
    n_tok = xb.shape[0]
    gates, ids = _router(xb, w_r.astype(BF16), b_r, name=f"router_{tag}")
    row_src, pos, te, n_valid = _route_plan(ids[:, :TOP_K], n_tok)
    xs = _gather_rows(x_slab, row_src, name=f"dispatch_{tag}")
    ys = _expert_ffn(xs, te, n_valid, w1, w3, w2, name=f"experts_{tag}")
    y2 = _gather_rows(ys, pos, name=f"collect_{tag}")
    return _combine_ln(x_slab, y2, gates, ln_g, ln_b, name=f"combine_ln_{tag}")


def _router_weights(w_group, b_group, w_expert, b_expert):
    w = jnp.concatenate([w_expert, w_group], axis=1)
    w = jnp.pad(w, ((0, 0), (0, 128 - w.shape[1])))
    b = jnp.pad(jnp.concatenate([b_expert, b_group]), (0, 128 - N_EXPERTS - N_EXP_GROUPS))
    return w, b.reshape(1, 128)


def _pad_conv_state(conv):
    return jnp.pad(conv, ((0, 0), (8 - (CONV_W - 1), 0), (0, 0)))


def kernel(x_prompt, x_sample, state_mlstm_C, state_mlstm_n, state_mlstm_m, state_ssd, state_ssd_conv,
           state_lru_h, state_lru_conv, meta_tokens, w_in_even, mlstm_gate_b, ssd_dt_bias, ssd_A_log, ssd_D,
           ssd_conv_w, ssd_conv_b, mlstm_norm_g, ssd_norm_g, w_out_even, w_in_odd, lru_conv_w, lru_conv_b,
           lru_w_gate, lru_b_gate, lru_lambda, w_out_odd, ln_g, ln_b, moe_w_group, moe_b_group,
           moe_w_expert, moe_b_expert, moe_w1, moe_w3, moe_w2):
    Bp, Tp, D = x_prompt.shape
    Bs, Ts, _ = x_sample.shape
    n_chunks_p = (N_META + Tp + CHUNK - 1) // CHUNK
    Tpp = n_chunks_p * CHUNK
    n_pad = Tpp - N_META - Tp
    n_p = Bp * Tpp
    n_s = Bs * Ts
    n_tok = n_p + n_s

    meta = jnp.broadcast_to(meta_tokens[None], (Bp, N_META, D))
    xp = jnp.concatenate([jnp.zeros((Bp, n_pad, D), F32), meta, x_prompt], axis=1).reshape(n_p, D)
    x0 = jnp.concatenate([xp, x_sample.reshape(n_s, D)], axis=0)
    x0b = x0.astype(BF16)

    grp = [dict(n_seq=Bp, n_chunks=n_chunks_p, L=CHUNK, row0=0, n_pad=n_pad),
           dict(n_seq=Bs, n_chunks=1, L=Ts, row0=n_p, n_pad=0)]

    def seq_view(a, gi):
        if gi == 0:
            return a[:n_p].reshape(Bp, Tpp, a.shape[1])
        return a[n_p:].reshape(Bs, Ts, a.shape[1])

    def tail_rows(a, gi, col0, ncol):
        nb, T, base = (Bp, Tpp, 0) if gi == 0 else (Bs, Ts, n_p)
        idx = (base + np.arange(nb)[:, None] * T + np.arange(T - (CONV_W - 1), T)[None, :]).reshape(-1)
        rows = jnp.take(a, jnp.asarray(idx, jnp.int32), axis=0)
        return rows[:, col0:col0 + ncol].reshape(nb, CONV_W - 1, ncol)

    e = 0
    w = w_in_even[e]
    src_small = 2 * A_HEADS * A_DK + 2 * A_INNER
    src_z = src_small + 2 * A_HEADS
    src_dt = src_z + B_INNER + B_CONV_DIM
    w_all = jnp.concatenate([w[:, :src_small], w[:, src_z:src_dt], w[:, src_small:src_z], w[:, src_dt:],
                             jnp.zeros((D, IN0_PAD - COL_SMALL - N_SMALL), F32)], axis=1).astype(BF16)
    proj = _mm(x0b, w_all, 512, 1280, name="in_proj_even")

    small = proj[:, COL_SMALL:COL_SMALL + N_SMALL]
    h_parts, y_parts, st = [], [], []
    for gi, g in enumerate(grp):
        sv = seq_view(small, gi)
        gates_t = jnp.swapaxes(sv, 1, 2)
        nb, T = sv.shape[0], sv.shape[1]
        gates_c = sv[:, :, 2 * A_HEADS:].reshape(nb, T, B_GROUPS, B_HPG).transpose(0, 2, 1, 3)
        if gi == 0:
            c0 = jnp.zeros((Bp, A_HEADS, A_DK, A_DV), F32)
            n0 = jnp.zeros((Bp, A_HEADS, 1, A_DK), F32)
            m0 = jnp.zeros((Bp, A_HEADS, 1, 128), F32)
            s0 = jnp.zeros((Bp, B_HEADS, B_P, B_N), F32)
            cv0 = jnp.zeros((Bp, 8, B_CONV_DIM), F32)
        else:
            c0 = state_mlstm_C[e]
            n0 = state_mlstm_n[e][:, :, None, :]
            m0 = jnp.broadcast_to(state_mlstm_m[e][:, :, None, None], (Bs, A_HEADS, 1, 128))
            s0 = state_ssd[e]
            cv0 = _pad_conv_state(state_ssd_conv[e])
        h_g, c_g, n_g, m_g = _mlstm(proj, gates_t, mlstm_gate_b[e], mlstm_norm_g[e], c0, n0, m0,
                                    name=f"mlstm_{gi}", **g)
        y_g, s_g = _ssd(proj, gates_t, gates_c, ssd_dt_bias[e], ssd_A_log[e], ssd_D[e], ssd_conv_w[e],
                        ssd_conv_b[e], ssd_norm_g[e], s0, cv0, name=f"ssd_{gi}", **g)
        h_parts.append(h_g.astype(BF16))
        y_parts.append(y_g.astype(BF16))
        sconv = tail_rows(proj, gi, COL_XBC, B_CONV_DIM)
        st.append((c_g[None], n_g[:, :, 0, :][None], m_g[:, :, 0, 0][None], s_g[None], sconv[None]))
    hy = jnp.concatenate([jnp.concatenate(h_parts, axis=0), jnp.concatenate(y_parts, axis=0)], axis=1)
    mix = _mm(hy, w_out_even[e].astype(BF16), 512, 1024, name="out_proj_even")
    x1s, x1b = _add_ln(x0, mix, ln_g[0, 0], ln_b[0, 0], name="add_ln_0")
    w_r, b_r = _router_weights(moe_w_group[0], moe_b_group[0], moe_w_expert[0], moe_b_expert[0])
    x2, x2b = _hier_moe_ln(x1s, x1b, w_r, b_r, moe_w1[0], moe_w3[0], moe_w2[0], ln_g[0, 1], ln_b[0, 1], "0")

    o = 0
    proj1 = _mm(x2b, w_in_odd[o].astype(BF16), 512, 1280, name="in_proj_odd")
    wg = lru_w_gate[o].reshape(2, N_LRU_SB, LRU_SB // LRU_BW, LRU_BW, LRU_BW)
    eye4 = jnp.eye(LRU_SB // LRU_BW, dtype=F32)
    wg_sb = jnp.einsum('gjaik,ab->gjaibk', wg, eye4).reshape(2, N_LRU_SB, LRU_SB, LRU_SB).astype(BF16)
    o_parts, st1 = [], []
    for gi, g in enumerate(grp):
        if gi == 0:
            h0 = jnp.zeros((Bp, 1, D_RNN), F32)
            cv0 = jnp.zeros((Bp, 8, D_RNN), F32)
        else:
            h0 = state_lru_h[o][:, None, :]
            cv0 = _pad_conv_state(state_lru_conv[o])
        o_g, hN = _lru(proj1, lru_conv_w[o], lru_conv_b[o], wg_sb, lru_b_gate[o], lru_lambda[o], h0, cv0,
                       name=f"lru_{gi}", **g)
        o_parts.append(o_g.astype(BF16))
        hconv = tail_rows(proj1, gi, D_RNN, D_RNN)
        st1.append((hN[:, 0, :][None], hconv[None]))
    mix1 = _mm(jnp.concatenate(o_parts, axis=0), w_out_odd[o].astype(BF16), 512, 1024, name="out_proj_odd")
    x3s, x3b = _add_ln(x2, mix1, ln_g[1, 0], ln_b[1, 0], name="add_ln_1")
    w_r, b_r = _router_weights(moe_w_group[1], moe_b_group[1], moe_w_expert[1], moe_b_expert[1])
    x4, _ = _hier_moe_ln(x3s, x3b, w_r, b_r, moe_w1[1], moe_w3[1], moe_w2[1], ln_g[1, 1], ln_b[1, 1], "1")

    y_prompt = x4[:n_p].reshape(Bp, Tpp, D)[:, Tpp - Tp:]
    y_sample = x4[n_p:].reshape(Bs, Ts, D)
    (pC, pn, pm, pS, pSc), (sC, sn, sm, sS, sSc) = st
    (pH, pHc), (sH, sHc) = st1
    return (y_prompt, y_sample, pC, pn, pm, pS, pSc, pH, pHc, sC, sn, sm, sS, sSc, sH, sHc)
```

```python
import functools
import math

import jax
import jax.numpy as jnp
import numpy as np
from jax import lax
from jax.experimental import pallas as pl
from jax.experimental.pallas import tpu as pltpu

F32 = jnp.float32
BF16 = jnp.bfloat16

D_MODEL = 2048
N_META = 16
CHUNK = 128
CONV_W = 4
A_HEADS = 8
A_DK = 128
A_DV = 256
A_INNER = A_HEADS * A_DV
B_HEADS = 32
B_P = 64
B_N = 128
B_GROUPS = 4
B_HPG = B_HEADS // B_GROUPS
B_INNER = B_HEADS * B_P
B_GW = B_INNER // B_GROUPS
B_CONV_DIM = B_INNER + 2 * B_GROUPS * B_N
D_RNN = 2560
LRU_BLOCKS = 16
LRU_BW = D_RNN // LRU_BLOCKS
LRU_C = 8.0
LRU_SB = 640
N_LRU_SB = D_RNN // LRU_SB
N_EXP_GROUPS = 4
EXP_PER_GROUP = 8
N_EXPERTS = N_EXP_GROUPS * EXP_PER_GROUP
TOP_K = 2
D_EXPERT = 512
DEPTH = 2
ALPHA = (2.0 * DEPTH) ** 0.25

COL_Q = 0
COL_K = A_HEADS * A_DK
COL_V = 2 * A_HEADS * A_DK
COL_O = COL_V + A_INNER
COL_Z = COL_O + A_INNER
COL_XBC = COL_Z + B_INNER
COL_SMALL = COL_XBC + B_CONV_DIM
N_SMALL = 2 * A_HEADS + B_HEADS
IN0_PAD = 11520

NEG_BIG = -1e30
VMEM_LIMIT_BYTES = 48 * 1024 * 1024
TILE_ROWS = 256
LN_ROWS = 256
SLAB = D_MODEL // 128


def _cparams(n_axes):
    return pltpu.CompilerParams(dimension_semantics=("arbitrary",) * n_axes,
                                vmem_limit_bytes=VMEM_LIMIT_BYTES)


def _smem_spec():
    return pl.BlockSpec(memory_space=pltpu.SMEM)


def _mm_kernel(x_ref, w_ref, o_ref):
    o_ref[...] = jnp.dot(x_ref[...], w_ref[...], preferred_element_type=F32).astype(o_ref.dtype)


def _mm(x, w, bm, bn, name):
    M, K = x.shape
    N = w.shape[1]
    return pl.pallas_call(
        _mm_kernel,
        grid=(N // bn, M // bm),
        in_specs=[pl.BlockSpec((bm, K), lambda j, i: (i, 0)),
                  pl.BlockSpec((K, bn), lambda j, i: (0, j))],
        out_specs=pl.BlockSpec((bm, bn), lambda j, i: (i, j)),
        out_shape=jax.ShapeDtypeStruct((M, N), F32),
        compiler_params=_cparams(2),
        name=name,
    )(x, w)


def _layer_norm_rows(y, g, b):
    mu = jnp.mean(y, axis=-1, keepdims=True)
    yc = y - mu
    var = jnp.mean(yc * yc, axis=-1, keepdims=True)
    return yc * lax.rsqrt(var + 1e-5) * g + b


def _slab_store(ref, val):
    tm = val.shape[0]
    for j in range(SLAB):
        ref[pl.ds(j, tm, stride=SLAB), :] = val[:, j * 128:(j + 1) * 128]


def _slab_piece(ref, j, tm):
    return ref[pl.ds(j, tm, stride=SLAB), :]


def _add_ln_kernel(x_ref, m_ref, g_ref, b_ref, os_ref, ob_ref):
    y = _layer_norm_rows(ALPHA * x_ref[...] + m_ref[...], g_ref[...], b_ref[...])
    _slab_store(os_ref, y)
    ob_ref[...] = y.astype(BF16)


def _add_ln(x, mix, g, b, name):
    M, D = x.shape
    tm = LN_ROWS
    row = pl.BlockSpec((tm, D), lambda i: (i, 0))
    vec = pl.BlockSpec((1, D), lambda i: (0, 0))
    return pl.pallas_call(
        _add_ln_kernel,
        grid=(M // tm,),
        in_specs=[row, row, vec, vec],
        out_specs=[pl.BlockSpec((tm * SLAB, 128), lambda i: (i, 0)), row],
        out_shape=[jax.ShapeDtypeStruct((M * SLAB, 128), F32), jax.ShapeDtypeStruct((M, D), BF16)],
        compiler_params=_cparams(1),
        name=name,
    )(x, mix, g.reshape(1, D), b.reshape(1, D))


def _slab_copy(src_hbm, dst_vmem, sem, src_tok, dst_tok):
    return pltpu.make_async_copy(src_hbm.at[pl.ds(pl.multiple_of(src_tok * SLAB, SLAB), SLAB), :],
                                 dst_vmem.at[pl.ds(pl.multiple_of(dst_tok * SLAB, SLAB), SLAB), :], sem)


def _issue_gather(idx_ref, idx0, src_hbm, dst_vmem, dst0, sem, n):
    def body(r, carry):
        _slab_copy(src_hbm, dst_vmem, sem, idx_ref[idx0 + r], dst0 + r).start()
        return carry
    lax.fori_loop(0, n, body, 0)


def _wait_gather(src_hbm, dst_vmem, sem, n):
    def body(r, carry):
        _slab_copy(src_hbm, dst_vmem, sem, 0, 0).wait()
        return carry
    lax.fori_loop(0, n, body, 0)


def _combine_ln_kernel(pos_ref, x_ref, y_hbm, gate_ref, g_ref, b_ref, o_ref, ob_ref, v_s, ybuf, sems, *, n_tok):
    i = pl.program_id(0)
    n = pl.num_programs(0)
    tm = LN_ROWS
    slot = i % 2

    def fetch(tile, sl):
        for kk in range(TOP_K):
            _issue_gather(pos_ref, kk * n_tok + tile * tm, y_hbm, ybuf, (sl * TOP_K + kk) * tm, sems.at[sl], tm)

    @pl.when(i == 0)
    def _():
        fetch(0, 0)

    @pl.when(i + 1 < n)
    def _():
        fetch(i + 1, 1 - slot)

    _wait_gather(y_hbm, ybuf, sems.at[slot], TOP_K * tm)
    gate = gate_ref[...]
    g0 = gate[:, 0:1]
    g1 = gate[:, 1:2]
    base0 = slot * (TOP_K * tm * SLAB)
    base1 = base0 + tm * SLAB
    for j in range(SLAB):
        v_s[:, j * 128:(j + 1) * 128] = (ALPHA * _slab_piece(x_ref, j, tm)
                                         + g0 * ybuf[pl.ds(base0 + j, tm, stride=SLAB), :]
                                         + g1 * ybuf[pl.ds(base1 + j, tm, stride=SLAB), :])
    y = _layer_norm_rows(v_s[...], g_ref[...], b_ref[...])
    o_ref[...] = y
    ob_ref[...] = y.astype(BF16)


def _combine_ln(x_slab, y_slab, pos, gates, g, b, name):
    M = gates.shape[0]
    D = D_MODEL
    tm = LN_ROWS
    row = pl.BlockSpec((tm, D), lambda i, pos: (i, 0))
    vec = pl.BlockSpec((1, D), lambda i, pos: (0, 0))
    return pl.pallas_call(
        functools.partial(_combine_ln_kernel, n_tok=M),
        grid_spec=pltpu.PrefetchScalarGridSpec(
            num_scalar_prefetch=1,
            grid=(M // tm,),
            in_specs=[pl.BlockSpec((tm * SLAB, 128), lambda i, pos: (i, 0)),
                      pl.BlockSpec(memory_space=pl.ANY),
                      pl.BlockSpec((tm, 128), lambda i, pos: (i, 0)), vec, vec],
            out_specs=[row, row],
            scratch_shapes=[pltpu.VMEM((tm, D), F32), pltpu.VMEM((2 * TOP_K * tm * SLAB, 128), F32),
                            pltpu.SemaphoreType.DMA((2,))],
        ),
        out_shape=[jax.ShapeDtypeStruct((M, D), F32), jax.ShapeDtypeStruct((M, D), BF16)],
        compiler_params=_cparams(1),
        name=name,
    )(pos, x_slab, y_slab, gates, g.reshape(1, D), b.reshape(1, D))


def _softplus(x):
    return jnp.maximum(x, 0.0) + jnp.log1p(jnp.exp(-jnp.abs(x)))


def _sigmoid(x):
    return 1.0 / (1.0 + jnp.exp(-x))


def _silu(x):
    return x * _sigmoid(x)


def _row_to_col(row, eye):
    return jnp.sum(jnp.where(eye, row, 0.0), axis=1, keepdims=True)


def _causal_conv(x, carry_ref, buf_ref, w_ref, b_ref, L):
    buf_ref[0:8, :] = carry_ref[...]
    buf_ref[8:8 + L, :] = x
    acc = b_ref[...] + w_ref[0:1, :] * buf_ref[5:5 + L, :]
    for j in range(1, CONV_W):
        acc = acc + w_ref[j:j + 1, :] * buf_ref[5 + j:5 + j + L, :]
    carry_ref[...] = buf_ref[L:L + 8, :]
    return acc


def _mlstm_kernel(gb_ref, q_ref, k_ref, v_ref, o_ref, gt_ref, c0_ref, n0_ref, m0_ref, ng_ref,
                  h_ref, c_out_ref, n_out_ref, m_out_ref, c_s, n_s, m_s, *, L, n_pad):
    c = pl.program_id(1)

    @pl.when(c == 0)
    def _():
        c_s[...] = c0_ref[0]
        n_s[...] = n0_ref[0]
        m_s[...] = m0_ref[0]

    rows = lax.broadcasted_iota(jnp.int32, (L, L), 0)
    cols = lax.broadcasted_iota(jnp.int32, (L, L), 1)
    eye = rows == cols
    causal = cols <= rows

    if n_pad:
        pad = (lax.broadcasted_iota(jnp.int32, (1, L), 1) < n_pad) & (c == 0)

    for hd in range(A_HEADS):
        li = gt_ref[0, hd:hd + 1, :] + gb_ref[0, hd]
        fr = gt_ref[0, A_HEADS + hd:A_HEADS + hd + 1, :] + gb_ref[1, hd]
        lf = jnp.minimum(fr, 0.0) - jnp.log1p(jnp.exp(-jnp.abs(fr)))
        if n_pad:
            li = jnp.where(pad, NEG_BIG, li)
            lf = jnp.where(pad, 0.0, lf)

        lf_col = _row_to_col(lf, eye)
        b_col = jnp.sum(jnp.where(causal, lf, 0.0), axis=1, keepdims=True)
        b_row = jnp.sum(jnp.where(rows <= cols, lf_col, 0.0), axis=0, keepdims=True)
        m_prev = m_s[hd][:, 0:1]

        d = jnp.where(causal, b_col - b_row + li, NEG_BIG)
        inter = b_col + m_prev
        m_t = jnp.maximum(inter, jnp.max(d, axis=1, keepdims=True))
        w_intra = jnp.exp(d - m_t)
        w_inter = jnp.exp(inter - m_t)

        q = q_ref[:, hd * A_DK:(hd + 1) * A_DK] * (A_DK ** -0.5)
        k = k_ref[:, hd * A_DK:(hd + 1) * A_DK]
        v = v_ref[:, hd * A_DV:(hd + 1) * A_DV]
        qb = q.astype(BF16)
        kb = k.astype(BF16)
        vb = v.astype(BF16)
        s = lax.dot_general(qb, kb, (((1,), (1,)), ((), ())), preferred_element_type=F32) * w_intra
        c_prev = c_s[hd]
        n_prev = n_s[hd]
        num = jnp.dot(s.astype(BF16), vb, preferred_element_type=F32)
        num = num + w_inter * jnp.dot(qb, c_prev.astype(BF16), preferred_element_type=F32)
        qn = jnp.sum(qb.astype(F32) * n_prev.astype(BF16).astype(F32), axis=1, keepdims=True)
        nq = jnp.sum(s, axis=1, keepdims=True) + w_inter * qn
        hh = num / jnp.maximum(jnp.abs(nq), jnp.exp(-m_t))

        hn = hh * lax.rsqrt(jnp.mean(hh * hh, axis=-1, keepdims=True) + 1e-6)
        hn = hn * ng_ref[:, hd * A_DV:(hd + 1) * A_DV]
        h_ref[:, hd * A_DV:(hd + 1) * A_DV] = (
            hn * _sigmoid(o_ref[:, hd * A_DV:(hd + 1) * A_DV])).astype(h_ref.dtype)

        b_last = b_row[:, L - 1:L]
        ws_log = b_last - b_row + li
        m_new = jnp.maximum(b_last + m_prev, jnp.max(ws_log, axis=1, keepdims=True))
        ws_col = _row_to_col(jnp.exp(ws_log - m_new), eye)
        wc = jnp.exp(b_last + m_prev - m_new)
        kw = k * ws_col
        c_new = wc * c_prev + lax.dot_general(kw.astype(BF16), vb, (((0,), (0,)), ((), ())),
                                              preferred_element_type=F32)
        n_new = wc * n_prev + jnp.sum(ws_col.astype(BF16).astype(F32) * kb.astype(F32), axis=0, keepdims=True)
        m_new_b = jnp.broadcast_to(m_new, (1, 128))
        c_s[hd] = c_new
        n_s[hd] = n_new
        m_s[hd] = m_new_b
        c_out_ref[0, hd] = c_new
        n_out_ref[0, hd] = n_new
        m_out_ref[0, hd] = m_new_b


def _mlstm(proj, gates_t, gate_b, norm_g, c0, n0, m0, *, n_seq, n_chunks, L, row0, n_pad, name):
    rb0 = row0 // L

    def rblk(b, c):
        return rb0 + b * n_chunks + c

    out_dtype = BF16 if L % 16 == 0 else F32
    qk_w = A_HEADS * A_DK
    state_map = lambda b, c: (b, 0, 0, 0)
    kern = functools.partial(_mlstm_kernel, L=L, n_pad=n_pad)
    return pl.pallas_call(
        kern,
        grid=(n_seq, n_chunks),
        in_specs=[
            _smem_spec(),
            pl.BlockSpec((L, qk_w), lambda b, c: (rblk(b, c), COL_Q // qk_w)),
            pl.BlockSpec((L, qk_w), lambda b, c: (rblk(b, c), COL_K // qk_w)),
            pl.BlockSpec((L, A_INNER), lambda b, c: (rblk(b, c), COL_V // A_INNER)),
            pl.BlockSpec((L, A_INNER), lambda b, c: (rblk(b, c), COL_O // A_INNER)),
            pl.BlockSpec((1, N_SMALL, L), lambda b, c: (b, 0, c)),
            pl.BlockSpec((1, A_HEADS, A_DK, A_DV), state_map),
            pl.BlockSpec((1, A_HEADS, 1, A_DK), state_map),
            pl.BlockSpec((1, A_HEADS, 1, 128), state_map),
            pl.BlockSpec((1, A_INNER), lambda b, c: (0, 0)),
        ],
        out_specs=[
            pl.BlockSpec((L, A_INNER), lambda b, c: (b * n_chunks + c, 0)),
            pl.BlockSpec((1, A_HEADS, A_DK, A_DV), state_map),
            pl.BlockSpec((1, A_HEADS, 1, A_DK), state_map),
            pl.BlockSpec((1, A_HEADS, 1, 128), state_map),
        ],
        out_shape=[
            jax.ShapeDtypeStruct((n_seq * n_chunks * L, A_INNER), out_dtype),
            jax.ShapeDtypeStruct((n_seq, A_HEADS, A_DK, A_DV), F32),
            jax.ShapeDtypeStruct((n_seq, A_HEADS, 1, A_DK), F32),
            jax.ShapeDtypeStruct((n_seq, A_HEADS, 1, 128), F32),
        ],
        scratch_shapes=[pltpu.VMEM((A_HEADS, A_DK, A_DV), F32), pltpu.VMEM((A_HEADS, 1, A_DK), F32),
                        pltpu.VMEM((A_HEADS, 1, 128), F32)],
        compiler_params=_cparams(2),
        name=name,
    )(gate_b, proj, proj, proj, proj, gates_t, c0, n0, m0, norm_g.reshape(1, A_INNER))


def _ssd_kernel(dtb_ref, alog_ref, dsk_ref, xs_ref, bm_ref, cm_ref, z_ref, gt_ref, gc_ref, s0_ref,
                cx0_ref, cb0_ref, cc0_ref, wx_ref, wb_ref, wc_ref, bx_ref, bb_ref, bc_ref, ng_ref,
                y_ref, s_out_ref,
                s_s, cx_s, cb_s, cc_s, bufx, bufb, bufc, ybuf, *, L, n_pad):
    g = pl.program_id(1)
    c = pl.program_id(2)

    @pl.when(c == 0)
    def _():
        s_s[...] = s0_ref[0]
        cx_s[...] = cx0_ref[0]
        cb_s[...] = cb0_ref[0]
        cc_s[...] = cc0_ref[0]

    rows = lax.broadcasted_iota(jnp.int32, (L, L), 0)
    cols = lax.broadcasted_iota(jnp.int32, (L, L), 1)
    causal = cols <= rows
    if n_pad:
        pad_row = (lax.broadcasted_iota(jnp.int32, (1, L), 1) < n_pad) & (c == 0)
        pad_col = (lax.broadcasted_iota(jnp.int32, (L, 1), 0) < n_pad) & (c == 0)

    xs_raw = xs_ref[...]
    bm_raw = bm_ref[...]
    cm_raw = cm_ref[...]
    if n_pad:
        xs_raw = jnp.where(pad_col, 0.0, xs_raw)
        bm_raw = jnp.where(pad_col, 0.0, bm_raw)
        cm_raw = jnp.where(pad_col, 0.0, cm_raw)
    xs = _silu(_causal_conv(xs_raw, cx_s, bufx, wx_ref, bx_ref, L))
    bm = _silu(_causal_conv(bm_raw, cb_s, bufb, wb_ref, bb_ref, L))
    cm = _silu(_causal_conv(cm_raw, cc_s, bufc, wc_ref, bc_ref, L))
    bmb = bm.astype(BF16)
    cmb = cm.astype(BF16)
    cb = lax.dot_general(cmb, bmb, (((1,), (1,)), ((), ())), preferred_element_type=F32)

    gc = gc_ref[0, 0]
    for hh in range(B_HPG):
        head = g * B_HPG + hh
        dt_bias = dtb_ref[head]
        a_neg = -jnp.exp(alog_ref[head])
        dt_row = _softplus(gt_ref[0, pl.ds(2 * A_HEADS + head, 1), :] + dt_bias)
        dt_col = _softplus(gc[:, hh:hh + 1] + dt_bias)
        if n_pad:
            dt_row = jnp.where(pad_row, 0.0, dt_row)
            dt_col = jnp.where(pad_col, 0.0, dt_col)
        a_row = dt_row * a_neg
        a_col = dt_col * a_neg
        b_col = jnp.sum(jnp.where(causal, a_row, 0.0), axis=1, keepdims=True)
        b_row = jnp.sum(jnp.where(rows <= cols, a_col, 0.0), axis=0, keepdims=True)
        decay = jnp.exp(jnp.where(causal, b_col - b_row, NEG_BIG))
        x_h = xs[:, hh * B_P:(hh + 1) * B_P]
        xdt = x_h * dt_col
        s_prev = s_s[hh]
        y_h = jnp.dot((cb * decay).astype(BF16), xdt.astype(BF16), preferred_element_type=F32)
        y_h = y_h + jnp.exp(b_col) * lax.dot_general(cmb, s_prev.astype(BF16), (((1,), (1,)), ((), ())),
                                                     preferred_element_type=F32)
        b_last = b_row[:, L - 1:L]
        w_col = jnp.exp(b_last - b_col)
        s_new = jnp.exp(b_last) * s_prev + lax.dot_general((xdt * w_col).astype(BF16), bmb,
                                                           (((0,), (0,)), ((), ())),
                                                           preferred_element_type=F32)
        s_s[hh] = s_new
        s_out_ref[0, hh] = s_new
        ybuf[:, hh * B_P:(hh + 1) * B_P] = y_h + dsk_ref[head] * x_h

    y = ybuf[...] * _silu(z_ref[...])
    yn = y * lax.rsqrt(jnp.mean(y * y, axis=-1, keepdims=True) + 1e-6) * ng_ref[...]
    y_ref[...] = yn.astype(y_ref.dtype)


def _ssd(proj, gates_t, gates_c, dt_bias, a_log, d_skip, conv_w, conv_b, norm_g, s0, conv0, *,
         n_seq, n_chunks, L, row0, n_pad, name):
    rb0 = row0 // L

    def rblk(b, c):
        return rb0 + b * n_chunks + c

    out_dtype = BF16 if L % 16 == 0 else F32
    cw = conv_w
    cbias = conv_b.reshape(1, B_CONV_DIM)
    xoff = COL_XBC // B_GW
    boff = (COL_XBC + B_INNER) // B_N
    coff = (COL_XBC + B_INNER + B_GROUPS * B_N) // B_N
    kern = functools.partial(_ssd_kernel, L=L, n_pad=n_pad)
    return pl.pallas_call(
        kern,
        grid=(n_seq, B_GROUPS, n_chunks),
        in_specs=[
            _smem_spec(), _smem_spec(), _smem_spec(),
            pl.BlockSpec((L, B_GW), lambda b, g, c: (rblk(b, c), xoff + g)),
            pl.BlockSpec((L, B_N), lambda b, g, c: (rblk(b, c), boff + g)),
            pl.BlockSpec((L, B_N), lambda b, g, c: (rblk(b, c), coff + g)),
            pl.BlockSpec((L, B_GW), lambda b, g, c: (rblk(b, c), COL_Z // B_GW + g)),
            pl.BlockSpec((1, N_SMALL, L), lambda b, g, c: (b, 0, c)),
            pl.BlockSpec((1, 1, L, B_HPG), lambda b, g, c: (b, g, c, 0)),
            pl.BlockSpec((1, B_HPG, B_P, B_N), lambda b, g, c: (b, g, 0, 0)),
            pl.BlockSpec((1, 8, B_GW), lambda b, g, c: (b, 0, g)),
            pl.BlockSpec((1, 8, B_N), lambda b, g, c: (b, 0, B_INNER // B_N + g)),
            pl.BlockSpec((1, 8, B_N), lambda b, g, c: (b, 0, B_INNER // B_N + B_GROUPS + g)),
            pl.BlockSpec((CONV_W, B_GW), lambda b, g, c: (0, g)),
            pl.BlockSpec((CONV_W, B_N), lambda b, g, c: (0, B_INNER // B_N + g)),
            pl.BlockSpec((CONV_W, B_N), lambda b, g, c: (0, B_INNER // B_N + B_GROUPS + g)),
            pl.BlockSpec((1, B_GW), lambda b, g, c: (0, g)),
            pl.BlockSpec((1, B_N), lambda b, g, c: (0, B_INNER // B_N + g)),
            pl.BlockSpec((1, B_N), lambda b, g, c: (0, B_INNER // B_N + B_GROUPS + g)),
            pl.BlockSpec((1, B_GW), lambda b, g, c: (0, g)),
        ],
        out_specs=[
            pl.BlockSpec((L, B_GW), lambda b, g, c: (b * n_chunks + c, g)),
            pl.BlockSpec((1, B_HPG, B_P, B_N), lambda b, g, c: (b, g, 0, 0)),
        ],
        out_shape=[
            jax.ShapeDtypeStruct((n_seq * n_chunks * L, B_INNER), out_dtype),
            jax.ShapeDtypeStruct((n_seq, B_HEADS, B_P, B_N), F32),
        ],
        scratch_shapes=[
            pltpu.VMEM((B_HPG, B_P, B_N), F32),
            pltpu.VMEM((8, B_GW), F32), pltpu.VMEM((8, B_N), F32), pltpu.VMEM((8, B_N), F32),
            pltpu.VMEM((L + 8, B_GW), F32), pltpu.VMEM((L + 8, B_N), F32), pltpu.VMEM((L + 8, B_N), F32),
            pltpu.VMEM((L, B_GW), F32),
        ],
        compiler_params=_cparams(3),
        name=name,
    )(dt_bias, a_log, d_skip, proj, proj, proj, proj, gates_t, gates_c, s0, conv0, conv0, conv0,
      cw, cw, cw, cbias, cbias, cbias, norm_g.reshape(1, B_INNER))


def _gelu_tanh(x):
    return 0.5 * x * (1.0 + jnp.tanh(math.sqrt(2.0 / math.pi) * (x + 0.044715 * (x * x * x))))


def _expm1_nonpos(z):
    e = jnp.exp(z)
    safe = (e < 1.0) & (z > -1.0)
    return jnp.where(safe, (e - 1.0) * z / jnp.log(jnp.where(safe, e, 0.5)), jnp.where(e == 1.0, z, e - 1.0))


def _lru_kernel(gi_ref, xb_ref, h0_ref, cv0_ref, cw_ref, cb_ref, wg_ref, bg_ref, lam_ref,
                o_ref, h_out_ref, h_s, cv_s, buf, *, L, n_pad):
    c = pl.program_id(2)

    @pl.when(c == 0)
    def _():
        h_s[...] = h0_ref[0]
        cv_s[...] = cv0_ref[0]

    row_id = lax.broadcasted_iota(jnp.int32, (L, 1), 0)
    x_raw = xb_ref[...]
    if n_pad:
        pad_col = (row_id < n_pad) & (c == 0)
        x_raw = jnp.where(pad_col, 0.0, x_raw)
    xf = _causal_conv(x_raw, cv_s, buf, cw_ref, cb_ref, L)
    xfb = xf.astype(BF16)
    r = _sigmoid(jnp.dot(xfb, wg_ref[0, 0], preferred_element_type=F32) + bg_ref[0:1, :])
    i = _sigmoid(jnp.dot(xfb, wg_ref[1, 0], preferred_element_type=F32) + bg_ref[1:2, :])
    log_a = (-LRU_C) * r * _softplus(-lam_ref[...])
    a = jnp.exp(log_a)
    u = jnp.sqrt(-_expm1_nonpos(2.0 * log_a)) * (i * xf)
    if n_pad:
        a = jnp.where(pad_col, 1.0, a)
        u = jnp.where(pad_col, 0.0, u)
    k = 1
    while k < L:
        keep = row_id >= k
        a_sh = jnp.where(keep, pltpu.roll(a, k, 0), 1.0)
        u_sh = jnp.where(keep, pltpu.roll(u, k, 0), 0.0)
        u = a * u_sh + u
        a = a * a_sh
        k *= 2
    hs = a * h_s[...] + u
    h_last = hs[L - 1:L, :]
    h_s[...] = h_last
    h_out_ref[0] = h_last
    o_ref[...] = (hs * _gelu_tanh(gi_ref[...])).astype(o_ref.dtype)


def _lru(proj, conv_w, conv_b, wg_sb, b_gate, lam, h0, conv0, *, n_seq, n_chunks, L, row0, n_pad, name):
    rb0 = row0 // L

    def rblk(b, c):
        return rb0 + b * n_chunks + c

    out_dtype = BF16 if L % 16 == 0 else F32
    kern = functools.partial(_lru_kernel, L=L, n_pad=n_pad)
    return pl.pallas_call(
        kern,
        grid=(n_seq, N_LRU_SB, n_chunks),
        in_specs=[
            pl.BlockSpec((L, LRU_SB), lambda b, j, c: (rblk(b, c), j)),
            pl.BlockSpec((L, LRU_SB), lambda b, j, c: (rblk(b, c), N_LRU_SB + j)),
            pl.BlockSpec((1, 1, LRU_SB), lambda b, j, c: (b, 0, j)),
            pl.BlockSpec((1, 8, LRU_SB), lambda b, j, c: (b, 0, j)),
            pl.BlockSpec((CONV_W, LRU_SB), lambda b, j, c: (0, j)),
            pl.BlockSpec((1, LRU_SB), lambda b, j, c: (0, j)),
            pl.BlockSpec((2, 1, LRU_SB, LRU_SB), lambda b, j, c: (0, j, 0, 0)),
            pl.BlockSpec((2, LRU_SB), lambda b, j, c: (0, j)),
            pl.BlockSpec((1, LRU_SB), lambda b, j, c: (0, j)),
        ],
        out_specs=[
            pl.BlockSpec((L, LRU_SB), lambda b, j, c: (b * n_chunks + c, j)),
            pl.BlockSpec((1, 1, LRU_SB), lambda b, j, c: (b, 0, j)),
        ],
        out_shape=[
            jax.ShapeDtypeStruct((n_seq * n_chunks * L, D_RNN), out_dtype),
            jax.ShapeDtypeStruct((n_seq, 1, D_RNN), F32),
        ],
        scratch_shapes=[pltpu.VMEM((1, LRU_SB), F32), pltpu.VMEM((8, LRU_SB), F32),
                        pltpu.VMEM((L + 8, LRU_SB), F32)],
        compiler_params=_cparams(3),
        name=name,
    )(proj, proj, h0, conv0, conv_w, conv_b.reshape(1, D_RNN), wg_sb, b_gate, lam.reshape(1, D_RNN))


def _router_kernel(x_ref, w_ref, b_ref, gate_ref, id_ref):
    logits = jnp.dot(x_ref[...], w_ref[...], preferred_element_type=F32) + b_ref[...]
    tm = logits.shape[0]
    lane = lax.broadcasted_iota(jnp.int32, (tm, 128), 1)
    lane_f = lane.astype(F32)
    is_group = (lane >= N_EXPERTS) & (lane < N_EXPERTS + N_EXP_GROUPS)
    gl = jnp.where(is_group, logits, -jnp.inf)
    g_max = jnp.max(gl, axis=1, keepdims=True)
    g_lane = jnp.min(jnp.where(gl == g_max, lane_f, 1e9), axis=1, keepdims=True)
    g_w = 1.0 / jnp.sum(jnp.exp(gl - g_max), axis=1, keepdims=True)
    g_idx = g_lane - float(N_EXPERTS)
    lo = g_idx * float(EXP_PER_GROUP)
    in_group = (lane_f >= lo) & (lane_f < lo + float(EXP_PER_GROUP))
    el = jnp.where(in_group, logits, -jnp.inf)
    e_max = jnp.max(el, axis=1, keepdims=True)
    i1 = jnp.min(jnp.where(el == e_max, lane_f, 1e9), axis=1, keepdims=True)
    e_sum = jnp.sum(jnp.exp(el - e_max), axis=1, keepdims=True)
    el2 = jnp.where(lane_f == i1, -jnp.inf, el)
    e2_max = jnp.max(el2, axis=1, keepdims=True)
    i2 = jnp.min(jnp.where(el2 == e2_max, lane_f, 1e9), axis=1, keepdims=True)
    p1 = 1.0 / e_sum
    p2 = jnp.exp(e2_max - e_max) / e_sum
    tot = p1 + p2
    gate1 = g_w * (p1 / tot)
    gate2 = g_w * (p2 / tot)
    gate_ref[...] = jnp.where(lane == 0, gate1, jnp.where(lane == 1, gate2, 0.0))
    id_ref[...] = jnp.where(lane == 0, i1, jnp.where(lane == 1, i2, 0.0)).astype(jnp.int32)


def _router(x, w_r, b_r, name):
    M, D = x.shape
    tm = 256
    return pl.pallas_call(
        _router_kernel,
        grid=(M // tm,),
        in_specs=[pl.BlockSpec((tm, D), lambda i: (i, 0)),
                  pl.BlockSpec((D, 128), lambda i: (0, 0)),
                  pl.BlockSpec((1, 128), lambda i: (0, 0))],
        out_specs=[pl.BlockSpec((tm, 128), lambda i: (i, 0)), pl.BlockSpec((tm, 128), lambda i: (i, 0))],
        out_shape=[jax.ShapeDtypeStruct((M, 128), F32), jax.ShapeDtypeStruct((M, 128), jnp.int32)],
        compiler_params=_cparams(1),
        name=name,
    )(x, w_r, b_r)


def _ffn_kernel(te_ref, nv_ref, idx_ref, x_hbm, w1_ref, w3_ref, w2_ref, o_ref,
                w1b, w3b, w2b, xb_s, xbuf, sems):
    i = pl.program_id(0)
    tm = TILE_ROWS
    nv = nv_ref[0]
    slot = i % 2

    @pl.when(i == 0)
    def _():
        _issue_gather(idx_ref, 0, x_hbm, xbuf, 0, sems.at[0], tm)

    @pl.when(i + 1 < nv)
    def _():
        _issue_gather(idx_ref, (i + 1) * tm, x_hbm, xbuf, (1 - slot) * tm, sems.at[1 - slot], tm)

    prev = te_ref[jnp.maximum(i - 1, 0)]

    @pl.when((i == 0) | (te_ref[i] != prev))
    def _():
        w1b[...] = w1_ref[0, 0].astype(BF16)
        w3b[...] = w3_ref[0, 0].astype(BF16)
        w2b[...] = w2_ref[0, 0].astype(BF16)

    @pl.when(i < nv)
    def _():
        _wait_gather(x_hbm, xbuf, sems.at[slot], tm)
        base = slot * (tm * SLAB)
        for j in range(SLAB):
            xb_s[:, j * 128:(j + 1) * 128] = xbuf[pl.ds(base + j, tm, stride=SLAB), :].astype(BF16)
        xb = xb_s[...]
        h1 = jnp.dot(xb, w1b[...], preferred_element_type=F32)
        h3 = jnp.dot(xb, w3b[...], preferred_element_type=F32)
        hid = (_silu(h1) * h3).astype(BF16)
        _slab_store(o_ref, jnp.dot(hid, w2b[...], preferred_element_type=F32))

    @pl.when(i >= nv)
    def _():
        o_ref[...] = jnp.zeros_like(o_ref)


def _expert_ffn(x_slab, row_src, tile_expert, n_valid, w1, w3, w2, layer, name):
    R = row_src.shape[0]
    D = D_MODEL
    tm = TILE_ROWS
    wmap = lambda i, te, nv, idx: (layer, te[i], 0, 0)
    return pl.pallas_call(
        _ffn_kernel,
        grid_spec=pltpu.PrefetchScalarGridSpec(
            num_scalar_prefetch=3,
            grid=(R // tm,),
            in_specs=[
                pl.BlockSpec(memory_space=pl.ANY),
                pl.BlockSpec((1, 1, D, D_EXPERT), wmap),
                pl.BlockSpec((1, 1, D, D_EXPERT), wmap),
                pl.BlockSpec((1, 1, D_EXPERT, D), wmap),
            ],
            out_specs=pl.BlockSpec((tm * SLAB, 128), lambda i, te, nv, idx: (i, 0)),
            scratch_shapes=[pltpu.VMEM((D, D_EXPERT), BF16), pltpu.VMEM((D, D_EXPERT), BF16),
                            pltpu.VMEM((D_EXPERT, D), BF16), pltpu.VMEM((tm, D), BF16),
                            pltpu.VMEM((2 * tm * SLAB, 128), F32), pltpu.SemaphoreType.DMA((2,))],
        ),
        out_shape=jax.ShapeDtypeStruct((R * SLAB, 128), F32),
        compiler_params=_cparams(1),
        name=name,
    )(tile_expert, n_valid, row_src, x_slab, w1, w3, w2)


def _route_plan(ids, n_tok):
    tm = TILE_ROWS
    n_pairs = TOP_K * n_tok
    n_rows = n_pairs + N_EXPERTS * tm
    experts = jnp.arange(N_EXPERTS, dtype=jnp.int32)
    e_flat = ids.T.reshape(-1)
    order = jnp.argsort(e_flat, stable=True).astype(jnp.int32)
    rank_sorted = jnp.argsort(order).astype(jnp.int32)
    sizes = jnp.sum((e_flat[:, None] == experts[None, :]).astype(jnp.int32), axis=0)
    start = jnp.cumsum(sizes) - sizes
    psz = ((sizes + tm - 1) // tm) * tm
    pend = jnp.cumsum(psz)
    pstart = pend - psz
    pos = pstart[e_flat] + rank_sorted - start[e_flat]
    rows = jnp.arange(n_rows, dtype=jnp.int32)
    e_row = jnp.minimum(jnp.sum((rows[:, None] >= pend[None, :]).astype(jnp.int32), axis=1), N_EXPERTS - 1)
    rank = rows - pstart[e_row]
    valid = (rank < sizes[e_row]) & (rows < pend[-1])
    pair = order[jnp.clip(start[e_row] + rank, 0, n_pairs - 1)]
    row_src = jnp.where(valid, pair % n_tok, 0)
    last_e = jnp.max(jnp.where(sizes > 0, experts, 0))
    te = jnp.where(rows[::tm] < pend[-1], e_row[::tm], last_e)
    n_valid = (pend[-1] // tm).astype(jnp.int32)
    return row_src, pos, te, n_valid.reshape(1)


def _hier_moe_ln(x_slab, xb, w_r, b_r, w1, w3, w2, layer, ln_g, ln_b, tag):
    n_tok = xb.shape[0]
    gates, ids = _router(xb, w_r.astype(BF16), b_r, name=f"router_{tag}")
    row_src, pos, te, n_valid = _route_plan(ids[:, :TOP_K], n_tok)
    ys = _expert_ffn(x_slab, row_src, te, n_valid, w1, w3, w2, layer, name=f"experts_{tag}")
    return _combine_ln(x_slab, ys, pos, gates, ln_g, ln_b, name=f"combine_ln_{tag}")


def _router_weights(w_group, b_group, w_expert, b_expert):
    w = jnp.concatenate([w_expert, w_group], axis=1)
    w = jnp.pad(w, ((0, 0), (0, 128 - w.shape[1])))
    b = jnp.pad(jnp.concatenate([b_expert, b_group]), (0, 128 - N_EXPERTS - N_EXP_GROUPS))
    return w, b.reshape(1, 128)


def _pad_conv_state(conv):
    return jnp.pad(conv, ((0, 0), (8 - (CONV_W - 1), 0), (0, 0)))


def kernel(x_prompt, x_sample, state_mlstm_C, state_mlstm_n, state_mlstm_m, state_ssd, state_ssd_conv,
           state_lru_h, state_lru_conv, meta_tokens, w_in_even, mlstm_gate_b, ssd_dt_bias, ssd_A_log, ssd_D,
           ssd_conv_w, ssd_conv_b, mlstm_norm_g, ssd_norm_g, w_out_even, w_in_odd, lru_conv_w, lru_conv_b,
           lru_w_gate, lru_b_gate, lru_lambda, w_out_odd, ln_g, ln_b, moe_w_group, moe_b_group,
           moe_w_expert, moe_b_expert, moe_w1, moe_w3, moe_w2):
    Bp, Tp, D = x_prompt.shape
    Bs, Ts, _ = x_sample.shape
    n_chunks_p = (N_META + Tp + CHUNK - 1) // CHUNK
    Tpp = n_chunks_p * CHUNK
    n_pad = Tpp - N_META - Tp
    n_p = Bp * Tpp
    n_s = Bs * Ts
    n_tok = n_p + n_s

    meta = jnp.broadcast_to(meta_tokens[None], (Bp, N_META, D))
    xp = jnp.concatenate([jnp.zeros((Bp, n_pad, D), F32), meta, x_prompt], axis=1).reshape(n_p, D)
    x0 = jnp.concatenate([xp, x_sample.reshape(n_s, D)], axis=0)
    x0b = x0.astype(BF16)

    grp = [dict(n_seq=Bp, n_chunks=n_chunks_p, L=CHUNK, row0=0, n_pad=n_pad),
           dict(n_seq=Bs, n_chunks=1, L=Ts, row0=n_p, n_pad=0)]

    def seq_view(a, gi):
        if gi == 0:
            return a[:n_p].reshape(Bp, Tpp, a.shape[1])
        return a[n_p:].reshape(Bs, Ts, a.shape[1])

    def tail_rows(a, gi, col0, ncol):
        nb, T, base = (Bp, Tpp, 0) if gi == 0 else (Bs, Ts, n_p)
        idx = (base + np.arange(nb)[:, None] * T + np.arange(T - (CONV_W - 1), T)[None, :]).reshape(-1)
        rows = jnp.take(a, jnp.asarray(idx, jnp.int32), axis=0)
        return rows[:, col0:col0 + ncol].reshape(nb, CONV_W - 1, ncol)

    e = 0
    w = w_in_even[e]
    src_small = 2 * A_HEADS * A_DK + 2 * A_INNER
    src_z = src_small + 2 * A_HEADS
    src_dt = src_z + B_INNER + B_CONV_DIM
    w_all = jnp.concatenate([w[:, :src_small], w[:, src_z:src_dt], w[:, src_small:src_z], w[:, src_dt:],
                             jnp.zeros((D, IN0_PAD - COL_SMALL - N_SMALL), F32)], axis=1).astype(BF16)
    proj = _mm(x0b, w_all, 512, 1280, name="in_proj_even")

    small = proj[:, COL_SMALL:COL_SMALL + N_SMALL]
    h_parts, y_parts, st = [], [], []
    for gi, g in enumerate(grp):
        sv = seq_view(small, gi)
        gates_t = jnp.swapaxes(sv, 1, 2)
        nb, T = sv.shape[0], sv.shape[1]
        gates_c = sv[:, :, 2 * A_HEADS:].reshape(nb, T, B_GROUPS, B_HPG).transpose(0, 2, 1, 3)
        if gi == 0:
            c0 = jnp.zeros((Bp, A_HEADS, A_DK, A_DV), F32)
            n0 = jnp.zeros((Bp, A_HEADS, 1, A_DK), F32)
            m0 = jnp.zeros((Bp, A_HEADS, 1, 128), F32)
            s0 = jnp.zeros((Bp, B_HEADS, B_P, B_N), F32)
            cv0 = jnp.zeros((Bp, 8, B_CONV_DIM), F32)
        else:
            c0 = state_mlstm_C[e]
            n0 = state_mlstm_n[e][:, :, None, :]
            m0 = jnp.broadcast_to(state_mlstm_m[e][:, :, None, None], (Bs, A_HEADS, 1, 128))
            s0 = state_ssd[e]
            cv0 = _pad_conv_state(state_ssd_conv[e])
        h_g, c_g, n_g, m_g = _mlstm(proj, gates_t, mlstm_gate_b[e], mlstm_norm_g[e], c0, n0, m0,
                                    name=f"mlstm_{gi}", **g)
        y_g, s_g = _ssd(proj, gates_t, gates_c, ssd_dt_bias[e], ssd_A_log[e], ssd_D[e], ssd_conv_w[e],
                        ssd_conv_b[e], ssd_norm_g[e], s0, cv0, name=f"ssd_{gi}", **g)
        h_parts.append(h_g.astype(BF16))
        y_parts.append(y_g.astype(BF16))
        sconv = tail_rows(proj, gi, COL_XBC, B_CONV_DIM)
        st.append((c_g[None], n_g[:, :, 0, :][None], m_g[:, :, 0, 0][None], s_g[None], sconv[None]))
    hy = jnp.concatenate([jnp.concatenate(h_parts, axis=0), jnp.concatenate(y_parts, axis=0)], axis=1)
    mix = _mm(hy, w_out_even[e].astype(BF16), 512, 1024, name="out_proj_even")
    x1s, x1b = _add_ln(x0, mix, ln_g[0, 0], ln_b[0, 0], name="add_ln_0")
    w_r, b_r = _router_weights(moe_w_group[0], moe_b_group[0], moe_w_expert[0], moe_b_expert[0])
    x2, x2b = _hier_moe_ln(x1s, x1b, w_r, b_r, moe_w1, moe_w3, moe_w2, 0, ln_g[0, 1], ln_b[0, 1], "0")

    o = 0
    proj1 = _mm(x2b, w_in_odd[o].astype(BF16), 512, 1280, name="in_proj_odd")
    wg = lru_w_gate[o].reshape(2, N_LRU_SB, LRU_SB // LRU_BW, LRU_BW, LRU_BW)
    eye4 = jnp.eye(LRU_SB // LRU_BW, dtype=F32)
    wg_sb = jnp.einsum('gjaik,ab->gjaibk', wg, eye4).reshape(2, N_LRU_SB, LRU_SB, LRU_SB).astype(BF16)
    o_parts, st1 = [], []
    for gi, g in enumerate(grp):
        if gi == 0:
            h0 = jnp.zeros((Bp, 1, D_RNN), F32)
            cv0 = jnp.zeros((Bp, 8, D_RNN), F32)
        else:
            h0 = state_lru_h[o][:, None, :]
            cv0 = _pad_conv_state(state_lru_conv[o])
        o_g, hN = _lru(proj1, lru_conv_w[o], lru_conv_b[o], wg_sb, lru_b_gate[o], lru_lambda[o], h0, cv0,
                       name=f"lru_{gi}", **g)
        o_parts.append(o_g.astype(BF16))
        hconv = tail_rows(proj1, gi, D_RNN, D_RNN)
        st1.append((hN[:, 0, :][None], hconv[None]))
    mix1 = _mm(jnp.concatenate(o_parts, axis=0), w_out_odd[o].astype(BF16), 512, 1024, name="out_proj_odd")
    x3s, x3b = _add_ln(x2, mix1, ln_g[1, 0], ln_b[1, 0], name="add_ln_1")
    w_r, b_r = _router_weights(moe_w_group[1], moe_b_group[1], moe_w_expert[1], moe_b_expert[1])
    x4, _ = _hier_moe_ln(x3s, x3b, w_r, b_r, moe_w1, moe_w3, moe_w2, 1, ln_g[1, 1], ln_b[1, 1], "1")

    y_prompt = x4[:n_p].reshape(Bp, Tpp, D)[:, Tpp - Tp:]
    y_sample = x4[n_p:].reshape(Bs, Ts, D)
    (pC, pn, pm, pS, pSc), (sC, sn, sm, sS, sSc) = st
    (pH, pHc), (sH, sHc) = st1
    return (y_prompt, y_sample, pC, pn, pm, pS, pSc, pH, pHc, sC, sn, sm, sS, sSc, sH, sHc)
```

```python
import functools
import math

import jax
import jax.numpy as jnp
import numpy as np
from jax import lax
from jax.experimental import pallas as pl
from jax.experimental.pallas import tpu as pltpu

F32 = jnp.float32
BF16 = jnp.bfloat16

D_MODEL = 2048
N_META = 16
CHUNK = 128
CONV_W = 4
A_HEADS = 8
A_DK = 128
A_DV = 256
A_INNER = A_HEADS * A_DV
B_HEADS = 32
B_P = 64
B_N = 128
B_GROUPS = 4
B_HPG = B_HEADS // B_GROUPS
B_INNER = B_HEADS * B_P
B_GW = B_INNER // B_GROUPS
B_CONV_DIM = B_INNER + 2 * B_GROUPS * B_N
D_RNN = 2560
LRU_BLOCKS = 16
LRU_BW = D_RNN // LRU_BLOCKS
LRU_C = 8.0
LRU_SB = 640
N_LRU_SB = D_RNN // LRU_SB
N_EXP_GROUPS = 4
EXP_PER_GROUP = 8
N_EXPERTS = N_EXP_GROUPS * EXP_PER_GROUP
TOP_K = 2
D_EXPERT = 512
DEPTH = 2
ALPHA = (2.0 * DEPTH) ** 0.25

COL_Q = 0
COL_K = A_HEADS * A_DK
COL_V = 2 * A_HEADS * A_DK
COL_O = COL_V + A_INNER
COL_Z = COL_O + A_INNER
COL_XBC = COL_Z + B_INNER
COL_SMALL = COL_XBC + B_CONV_DIM
N_SMALL = 2 * A_HEADS + B_HEADS
IN0_PAD = 11520

NEG_BIG = -1e30
VMEM_LIMIT_BYTES = 48 * 1024 * 1024
TILE_ROWS = 256
LN_ROWS = 256
SLAB = D_MODEL // 128


def _cparams(n_axes):
    return pltpu.CompilerParams(dimension_semantics=("arbitrary",) * n_axes,
                                vmem_limit_bytes=VMEM_LIMIT_BYTES)


def _smem_spec():
    return pl.BlockSpec(memory_space=pltpu.SMEM)


def _mm_kernel(x_ref, w_ref, o_ref):
    o_ref[...] = jnp.dot(x_ref[...], w_ref[...], preferred_element_type=F32).astype(o_ref.dtype)


def _mm(x, w, bm, bn, name):
    M, K = x.shape
    N = w.shape[1]
    return pl.pallas_call(
        _mm_kernel,
        grid=(N // bn, M // bm),
        in_specs=[pl.BlockSpec((bm, K), lambda j, i: (i, 0)),
                  pl.BlockSpec((K, bn), lambda j, i: (0, j))],
        out_specs=pl.BlockSpec((bm, bn), lambda j, i: (i, j)),
        out_shape=jax.ShapeDtypeStruct((M, N), F32),
        compiler_params=_cparams(2),
        name=name,
    )(x, w)


def _layer_norm_rows(y, g, b):
    mu = jnp.mean(y, axis=-1, keepdims=True)
    yc = y - mu
    var = jnp.mean(yc * yc, axis=-1, keepdims=True)
    return yc * lax.rsqrt(var + 1e-5) * g + b


def _slab_store(ref, val):
    tm = val.shape[0]
    for j in range(SLAB):
        ref[pl.ds(j, tm, stride=SLAB), :] = val[:, j * 128:(j + 1) * 128]


def _slab_piece(ref, j, tm):
    return ref[pl.ds(j, tm, stride=SLAB), :]


def _add_ln_kernel(x_ref, m_ref, g_ref, b_ref, os_ref, ob_ref):
    y = _layer_norm_rows(ALPHA * x_ref[...] + m_ref[...], g_ref[...], b_ref[...])
    _slab_store(os_ref, y)
    ob_ref[...] = y.astype(BF16)


def _add_ln(x, mix, g, b, name):
    M, D = x.shape
    tm = LN_ROWS
    row = pl.BlockSpec((tm, D), lambda i: (i, 0))
    vec = pl.BlockSpec((1, D), lambda i: (0, 0))
    return pl.pallas_call(
        _add_ln_kernel,
        grid=(M // tm,),
        in_specs=[row, row, vec, vec],
        out_specs=[pl.BlockSpec((tm * SLAB, 128), lambda i: (i, 0)), row],
        out_shape=[jax.ShapeDtypeStruct((M * SLAB, 128), F32), jax.ShapeDtypeStruct((M, D), BF16)],
        compiler_params=_cparams(1),
        name=name,
    )(x, mix, g.reshape(1, D), b.reshape(1, D))


def _slab_copy(src_hbm, dst_vmem, sem, src_tok, dst_tok):
    return pltpu.make_async_copy(src_hbm.at[pl.ds(pl.multiple_of(src_tok * SLAB, SLAB), SLAB), :],
                                 dst_vmem.at[pl.ds(pl.multiple_of(dst_tok * SLAB, SLAB), SLAB), :], sem)


def _issue_gather(idx_ref, idx0, src_hbm, dst_vmem, dst0, sem, n):
    def body(r, carry):
        _slab_copy(src_hbm, dst_vmem, sem, idx_ref[idx0 + r], dst0 + r).start()
        return carry
    lax.fori_loop(0, n, body, 0, unroll=8)


def _wait_gather(src_hbm, dst_vmem, sem, n):
    def body(r, carry):
        _slab_copy(src_hbm, dst_vmem, sem, 0, 0).wait()
        return carry
    lax.fori_loop(0, n, body, 0, unroll=8)


def _combine_ln_kernel(pos_ref, x_ref, y_hbm, gate_ref, g_ref, b_ref, o_ref, ob_ref, v_s, ybuf, sems, *, n_tok):
    i = pl.program_id(0)
    n = pl.num_programs(0)
    tm = LN_ROWS
    slot = i % 2

    def fetch(tile, sl):
        for kk in range(TOP_K):
            _issue_gather(pos_ref, kk * n_tok + tile * tm, y_hbm, ybuf, (sl * TOP_K + kk) * tm, sems.at[sl], tm)

    @pl.when(i == 0)
    def _():
        fetch(0, 0)

    @pl.when(i + 1 < n)
    def _():
        fetch(i + 1, 1 - slot)

    _wait_gather(y_hbm, ybuf, sems.at[slot], TOP_K * tm)
    gate = gate_ref[...]
    g0 = gate[:, 0:1]
    g1 = gate[:, 1:2]
    base0 = slot * (TOP_K * tm * SLAB)
    base1 = base0 + tm * SLAB
    for j in range(SLAB):
        v_s[:, j * 128:(j + 1) * 128] = (ALPHA * _slab_piece(x_ref, j, tm)
                                         + g0 * ybuf[pl.ds(base0 + j, tm, stride=SLAB), :]
                                         + g1 * ybuf[pl.ds(base1 + j, tm, stride=SLAB), :])
    y = _layer_norm_rows(v_s[...], g_ref[...], b_ref[...])
    o_ref[...] = y
    ob_ref[...] = y.astype(BF16)


def _combine_ln(x_slab, y_slab, pos, gates, g, b, name):
    M = gates.shape[0]
    D = D_MODEL
    tm = LN_ROWS
    row = pl.BlockSpec((tm, D), lambda i, pos: (i, 0))
    vec = pl.BlockSpec((1, D), lambda i, pos: (0, 0))
    return pl.pallas_call(
        functools.partial(_combine_ln_kernel, n_tok=M),
        grid_spec=pltpu.PrefetchScalarGridSpec(
            num_scalar_prefetch=1,
            grid=(M // tm,),
            in_specs=[pl.BlockSpec((tm * SLAB, 128), lambda i, pos: (i, 0)),
                      pl.BlockSpec(memory_space=pl.ANY),
                      pl.BlockSpec((tm, 128), lambda i, pos: (i, 0)), vec, vec],
            out_specs=[row, row],
            scratch_shapes=[pltpu.VMEM((tm, D), F32), pltpu.VMEM((2 * TOP_K * tm * SLAB, 128), F32),
                            pltpu.SemaphoreType.DMA((2,))],
        ),
        out_shape=[jax.ShapeDtypeStruct((M, D), F32), jax.ShapeDtypeStruct((M, D), BF16)],
        compiler_params=_cparams(1),
        name=name,
    )(pos, x_slab, y_slab, gates, g.reshape(1, D), b.reshape(1, D))


def _softplus(x):
    return jnp.maximum(x, 0.0) + jnp.log1p(jnp.exp(-jnp.abs(x)))


def _sigmoid(x):
    return 1.0 / (1.0 + jnp.exp(-x))


def _silu(x):
    return x * _sigmoid(x)


def _row_to_col(row, eye):
    return jnp.sum(jnp.where(eye, row, 0.0), axis=1, keepdims=True)


def _causal_conv(x, carry_ref, buf_ref, w_ref, b_ref, L):
    buf_ref[0:8, :] = carry_ref[...]
    buf_ref[8:8 + L, :] = x
    acc = b_ref[...] + w_ref[0:1, :] * buf_ref[5:5 + L, :]
    for j in range(1, CONV_W):
        acc = acc + w_ref[j:j + 1, :] * buf_ref[5 + j:5 + j + L, :]
    carry_ref[...] = buf_ref[L:L + 8, :]
    return acc


def _mlstm_kernel(gb_ref, q_ref, k_ref, v_ref, o_ref, gt_ref, c0_ref, n0_ref, m0_ref, ng_ref,
                  h_ref, c_out_ref, n_out_ref, m_out_ref, c_s, n_s, m_s, *, L, n_pad):
    c = pl.program_id(1)

    @pl.when(c == 0)
    def _():
        c_s[...] = c0_ref[0]
        n_s[...] = n0_ref[0]
        m_s[...] = m0_ref[0]

    rows = lax.broadcasted_iota(jnp.int32, (L, L), 0)
    cols = lax.broadcasted_iota(jnp.int32, (L, L), 1)
    eye = rows == cols
    causal = cols <= rows

    if n_pad:
        pad = (lax.broadcasted_iota(jnp.int32, (1, L), 1) < n_pad) & (c == 0)

    for hd in range(A_HEADS):
        li = gt_ref[0, hd:hd + 1, :] + gb_ref[0, hd]
        fr = gt_ref[0, A_HEADS + hd:A_HEADS + hd + 1, :] + gb_ref[1, hd]
        lf = jnp.minimum(fr, 0.0) - jnp.log1p(jnp.exp(-jnp.abs(fr)))
        if n_pad:
            li = jnp.where(pad, NEG_BIG, li)
            lf = jnp.where(pad, 0.0, lf)

        lf_col = _row_to_col(lf, eye)
        b_col = jnp.sum(jnp.where(causal, lf, 0.0), axis=1, keepdims=True)
        b_row = jnp.sum(jnp.where(rows <= cols, lf_col, 0.0), axis=0, keepdims=True)
        m_prev = m_s[hd][:, 0:1]

        d = jnp.where(causal, b_col - b_row + li, NEG_BIG)
        inter = b_col + m_prev
        m_t = jnp.maximum(inter, jnp.max(d, axis=1, keepdims=True))
        w_intra = jnp.exp(d - m_t)
        w_inter = jnp.exp(inter - m_t)

        q = q_ref[:, hd * A_DK:(hd + 1) * A_DK] * (A_DK ** -0.5)
        k = k_ref[:, hd * A_DK:(hd + 1) * A_DK]
        v = v_ref[:, hd * A_DV:(hd + 1) * A_DV]
        qb = q.astype(BF16)
        kb = k.astype(BF16)
        vb = v.astype(BF16)
        s = lax.dot_general(qb, kb, (((1,), (1,)), ((), ())), preferred_element_type=F32) * w_intra
        c_prev = c_s[hd]
        n_prev = n_s[hd]
        num = jnp.dot(s.astype(BF16), vb, preferred_element_type=F32)
        num = num + w_inter * jnp.dot(qb, c_prev.astype(BF16), preferred_element_type=F32)
        qn = jnp.sum(qb.astype(F32) * n_prev.astype(BF16).astype(F32), axis=1, keepdims=True)
        nq = jnp.sum(s, axis=1, keepdims=True) + w_inter * qn
        hh = num / jnp.maximum(jnp.abs(nq), jnp.exp(-m_t))

        hn = hh * lax.rsqrt(jnp.mean(hh * hh, axis=-1, keepdims=True) + 1e-6)
        hn = hn * ng_ref[:, hd * A_DV:(hd + 1) * A_DV]
        h_ref[:, hd * A_DV:(hd + 1) * A_DV] = (
            hn * _sigmoid(o_ref[:, hd * A_DV:(hd + 1) * A_DV])).astype(h_ref.dtype)

        b_last = b_row[:, L - 1:L]
        ws_log = b_last - b_row + li
        m_new = jnp.maximum(b_last + m_prev, jnp.max(ws_log, axis=1, keepdims=True))
        ws_col = _row_to_col(jnp.exp(ws_log - m_new), eye)
        wc = jnp.exp(b_last + m_prev - m_new)
        kw = k * ws_col
        c_new = wc * c_prev + lax.dot_general(kw.astype(BF16), vb, (((0,), (0,)), ((), ())),
                                              preferred_element_type=F32)
        n_new = wc * n_prev + jnp.sum(ws_col.astype(BF16).astype(F32) * kb.astype(F32), axis=0, keepdims=True)
        m_new_b = jnp.broadcast_to(m_new, (1, 128))
        c_s[hd] = c_new
        n_s[hd] = n_new
        m_s[hd] = m_new_b
        c_out_ref[0, hd] = c_new
        n_out_ref[0, hd] = n_new
        m_out_ref[0, hd] = m_new_b


def _mlstm(proj, gates_t, gate_b, norm_g, c0, n0, m0, *, n_seq, n_chunks, L, row0, n_pad, name):
    rb0 = row0 // L

    def rblk(b, c):
        return rb0 + b * n_chunks + c

    out_dtype = BF16 if L % 16 == 0 else F32
    qk_w = A_HEADS * A_DK
    state_map = lambda b, c: (b, 0, 0, 0)
    kern = functools.partial(_mlstm_kernel, L=L, n_pad=n_pad)
    return pl.pallas_call(
        kern,
        grid=(n_seq, n_chunks),
        in_specs=[
            _smem_spec(),
            pl.BlockSpec((L, qk_w), lambda b, c: (rblk(b, c), COL_Q // qk_w)),
            pl.BlockSpec((L, qk_w), lambda b, c: (rblk(b, c), COL_K // qk_w)),
            pl.BlockSpec((L, A_INNER), lambda b, c: (rblk(b, c), COL_V // A_INNER)),
            pl.BlockSpec((L, A_INNER), lambda b, c: (rblk(b, c), COL_O // A_INNER)),
            pl.BlockSpec((1, N_SMALL, L), lambda b, c: (b, 0, c)),
            pl.BlockSpec((1, A_HEADS, A_DK, A_DV), state_map),
            pl.BlockSpec((1, A_HEADS, 1, A_DK), state_map),
            pl.BlockSpec((1, A_HEADS, 1, 128), state_map),
            pl.BlockSpec((1, A_INNER), lambda b, c: (0, 0)),
        ],
        out_specs=[
            pl.BlockSpec((L, A_INNER), lambda b, c: (b * n_chunks + c, 0)),
            pl.BlockSpec((1, A_HEADS, A_DK, A_DV), state_map),
            pl.BlockSpec((1, A_HEADS, 1, A_DK), state_map),
            pl.BlockSpec((1, A_HEADS, 1, 128), state_map),
        ],
        out_shape=[
            jax.ShapeDtypeStruct((n_seq * n_chunks * L, A_INNER), out_dtype),
            jax.ShapeDtypeStruct((n_seq, A_HEADS, A_DK, A_DV), F32),
            jax.ShapeDtypeStruct((n_seq, A_HEADS, 1, A_DK), F32),
            jax.ShapeDtypeStruct((n_seq, A_HEADS, 1, 128), F32),
        ],
        scratch_shapes=[pltpu.VMEM((A_HEADS, A_DK, A_DV), F32), pltpu.VMEM((A_HEADS, 1, A_DK), F32),
                        pltpu.VMEM((A_HEADS, 1, 128), F32)],
        compiler_params=_cparams(2),
        name=name,
    )(gate_b, proj, proj, proj, proj, gates_t, c0, n0, m0, norm_g.reshape(1, A_INNER))


def _ssd_kernel(dtr_ref, dtc_ref, alr_ref, alc_ref, xs_ref, bm_ref, cm_ref, z_ref, gt_ref, gc_ref, s0_ref,
                cx0_ref, cb0_ref, cc0_ref, wx_ref, wb_ref, wc_ref, bx_ref, bb_ref, bc_ref, ng_ref, dsk_ref,
                y_ref, s_out_ref,
                s_s, cx_s, cb_s, cc_s, bufx, bufb, bufc, ybuf, *, L, n_pad, gps):
    grp0 = pl.program_id(1) * gps
    c = pl.program_id(2)

    @pl.when(c == 0)
    def _():
        s_s[...] = s0_ref[0]
        cx_s[...] = cx0_ref[0]
        cb_s[...] = cb0_ref[0]
        cc_s[...] = cc0_ref[0]

    rows = lax.broadcasted_iota(jnp.int32, (L, L), 0)
    cols = lax.broadcasted_iota(jnp.int32, (L, L), 1)
    causal = cols <= rows
    tril = causal.astype(F32)
    triu = (rows <= cols).astype(F32)
    lane_lo = lax.broadcasted_iota(jnp.int32, (L, 2 * B_P), 1) < B_P
    row_lo = lax.broadcasted_iota(jnp.int32, (2 * B_P, 1), 0) < B_P
    if n_pad:
        pad_row = (lax.broadcasted_iota(jnp.int32, (1, L), 1) < n_pad) & (c == 0)
        pad_col = (lax.broadcasted_iota(jnp.int32, (L, 1), 0) < n_pad) & (c == 0)

    xs_raw = xs_ref[...]
    bm_raw = bm_ref[...]
    cm_raw = cm_ref[...]
    if n_pad:
        xs_raw = jnp.where(pad_col, 0.0, xs_raw)
        bm_raw = jnp.where(pad_col, 0.0, bm_raw)
        cm_raw = jnp.where(pad_col, 0.0, cm_raw)
    xs = _silu(_causal_conv(xs_raw, cx_s, bufx, wx_ref, bx_ref, L))
    bm = _silu(_causal_conv(bm_raw, cb_s, bufb, wb_ref, bb_ref, L))
    cm = _silu(_causal_conv(cm_raw, cc_s, bufc, wc_ref, bc_ref, L))

    def two(col, h0):
        return jnp.where(lane_lo, col[:, h0:h0 + 1], col[:, h0 + 1:h0 + 2])

    for gg in range(gps):
        bmb = bm[:, gg * B_N:(gg + 1) * B_N].astype(BF16)
        cmb = cm[:, gg * B_N:(gg + 1) * B_N].astype(BF16)
        cb = lax.dot_general(cmb, bmb, (((1,), (1,)), ((), ())), preferred_element_type=F32)

        row0 = pl.multiple_of(2 * A_HEADS + (grp0 + gg) * B_HPG, B_HPG)
        dt_c = _softplus(gc_ref[0, gg] + dtr_ref[gg])
        dt_r = _softplus(gt_ref[0, pl.ds(row0, B_HPG), :] + dtc_ref[gg])
        if n_pad:
            dt_c = jnp.where(pad_col, 0.0, dt_c)
            dt_r = jnp.where(pad_row, 0.0, dt_r)
        a_c = dt_c * (-jnp.exp(alr_ref[gg]))
        a_r = dt_r * (-jnp.exp(alc_ref[gg]))
        b_c = jnp.dot(tril, a_c, preferred_element_type=F32, precision=lax.Precision.HIGHEST)
        b_r = jnp.dot(a_r, triu, preferred_element_type=F32, precision=lax.Precision.HIGHEST)
        eb_c = jnp.exp(b_c)
        b_last = b_c[L - 1:L, :]
        w_c = jnp.exp(b_last - b_c)
        e_last = jnp.exp(b_last)

        for p in range(B_HPG // 2):
            h0 = 2 * p
            pair = gg * (B_HPG // 2) + p
            x_pair = xs[:, pair * 2 * B_P:(pair + 1) * 2 * B_P]
            xdt = x_pair * two(dt_c, h0)
            y = None
            for hh, keep in ((h0, lane_lo), (h0 + 1, ~lane_lo)):
                decay = jnp.exp(jnp.where(causal, b_c[:, hh:hh + 1] - b_r[hh:hh + 1, :], NEG_BIG))
                part = jnp.dot((cb * decay).astype(BF16), jnp.where(keep, xdt, 0.0).astype(BF16),
                               preferred_element_type=F32)
                y = part if y is None else y + part
            s_prev = s_s[pair]
            y = y + two(eb_c, h0) * lax.dot_general(cmb, s_prev.astype(BF16), (((1,), (1,)), ((), ())),
                                                    preferred_element_type=F32)
            upd = lax.dot_general((xdt * two(w_c, h0)).astype(BF16), bmb, (((0,), (0,)), ((), ())),
                                  preferred_element_type=F32)
            s_new = jnp.where(row_lo, e_last[:, h0:h0 + 1], e_last[:, h0 + 1:h0 + 2]) * s_prev + upd
            s_s[pair] = s_new
            s_out_ref[0, pair] = s_new
            ybuf[:, pair * 2 * B_P:(pair + 1) * 2 * B_P] = y

    y = (ybuf[...] + dsk_ref[...] * xs) * _silu(z_ref[...])
    for gg in range(gps):
        yg = y[:, gg * B_GW:(gg + 1) * B_GW]
        yn = yg * lax.rsqrt(jnp.mean(yg * yg, axis=-1, keepdims=True) + 1e-6) * ng_ref[:, gg * B_GW:(gg + 1) * B_GW]
        y_ref[:, gg * B_GW:(gg + 1) * B_GW] = yn.astype(y_ref.dtype)


def _ssd(proj, gates_t, gates_c, dt_bias, a_log, d_skip, conv_w, conv_b, norm_g, s0, conv0, *,
         n_seq, n_chunks, L, row0, n_pad, name):
    rb0 = row0 // L

    def rblk(b, c):
        return rb0 + b * n_chunks + c

    out_dtype = BF16 if L % 16 == 0 else F32
    gps = B_GROUPS if n_chunks == 1 else 1
    xw, nw = gps * B_GW, gps * B_N
    n_pairs = gps * B_HPG // 2
    cw = conv_w
    cbias = conv_b.reshape(1, B_CONV_DIM)
    xoff = COL_XBC // xw
    boff = (COL_XBC + B_INNER) // nw
    coff = (COL_XBC + B_INNER + B_GROUPS * B_N) // nw
    cvb = B_INNER // nw
    cvc = (B_INNER + B_GROUPS * B_N) // nw
    dt_r = dt_bias.reshape(B_GROUPS, 1, B_HPG)
    dt_c = dt_bias.reshape(B_GROUPS, B_HPG, 1)
    al_r = a_log.reshape(B_GROUPS, 1, B_HPG)
    al_c = a_log.reshape(B_GROUPS, B_HPG, 1)
    d_row = jnp.repeat(d_skip, B_P).reshape(1, B_INNER)
    s0p = s0.reshape(n_seq, B_HEADS // 2, 2 * B_P, B_N)
    prow = pl.BlockSpec((gps, 1, B_HPG), lambda b, g, c: (g, 0, 0))
    pcol = pl.BlockSpec((gps, B_HPG, 1), lambda b, g, c: (g, 0, 0))
    kern = functools.partial(_ssd_kernel, L=L, n_pad=n_pad, gps=gps)
    y, s_new = pl.pallas_call(
        kern,
        grid=(n_seq, B_GROUPS // gps, n_chunks),
        in_specs=[
            prow, pcol, prow, pcol,
            pl.BlockSpec((L, xw), lambda b, g, c: (rblk(b, c), xoff + g)),
            pl.BlockSpec((L, nw), lambda b, g, c: (rblk(b, c), boff + g)),
            pl.BlockSpec((L, nw), lambda b, g, c: (rblk(b, c), coff + g)),
            pl.BlockSpec((L, xw), lambda b, g, c: (rblk(b, c), COL_Z // xw + g)),
            pl.BlockSpec((1, N_SMALL, L), lambda b, g, c: (b, 0, c)),
            pl.BlockSpec((1, gps, L, B_HPG), lambda b, g, c: (b, g, c, 0)),
            pl.BlockSpec((1, n_pairs, 2 * B_P, B_N), lambda b, g, c: (b, g, 0, 0)),
            pl.BlockSpec((1, 8, xw), lambda b, g, c: (b, 0, g)),
            pl.BlockSpec((1, 8, nw), lambda b, g, c: (b, 0, cvb + g)),
            pl.BlockSpec((1, 8, nw), lambda b, g, c: (b, 0, cvc + g)),
            pl.BlockSpec((CONV_W, xw), lambda b, g, c: (0, g)),
            pl.BlockSpec((CONV_W, nw), lambda b, g, c: (0, cvb + g)),
            pl.BlockSpec((CONV_W, nw), lambda b, g, c: (0, cvc + g)),
            pl.BlockSpec((1, xw), lambda b, g, c: (0, g)),
            pl.BlockSpec((1, nw), lambda b, g, c: (0, cvb + g)),
            pl.BlockSpec((1, nw), lambda b, g, c: (0, cvc + g)),
            pl.BlockSpec((1, xw), lambda b, g, c: (0, g)),
            pl.BlockSpec((1, xw), lambda b, g, c: (0, g)),
        ],
        out_specs=[
            pl.BlockSpec((L, xw), lambda b, g, c: (b * n_chunks + c, g)),
            pl.BlockSpec((1, n_pairs, 2 * B_P, B_N), lambda b, g, c: (b, g, 0, 0)),
        ],
        out_shape=[
            jax.ShapeDtypeStruct((n_seq * n_chunks * L, B_INNER), out_dtype),
            jax.ShapeDtypeStruct((n_seq, B_HEADS // 2, 2 * B_P, B_N), F32),
        ],
        scratch_shapes=[
            pltpu.VMEM((n_pairs, 2 * B_P, B_N), F32),
            pltpu.VMEM((8, xw), F32), pltpu.VMEM((8, nw), F32), pltpu.VMEM((8, nw), F32),
            pltpu.VMEM((L + 8, xw), F32), pltpu.VMEM((L + 8, nw), F32), pltpu.VMEM((L + 8, nw), F32),
            pltpu.VMEM((L, xw), F32),
        ],
        compiler_params=_cparams(3),
        name=name,
    )(dt_r, dt_c, al_r, al_c, proj, proj, proj, proj, gates_t, gates_c, s0p, conv0, conv0, conv0,
      cw, cw, cw, cbias, cbias, cbias, norm_g.reshape(1, B_INNER), d_row)
    return y, s_new.reshape(n_seq, B_HEADS, B_P, B_N)


def _gelu_tanh(x):
    return 0.5 * x * (1.0 + jnp.tanh(math.sqrt(2.0 / math.pi) * (x + 0.044715 * (x * x * x))))


def _expm1_nonpos(z):
    e = jnp.exp(z)
    safe = (e < 1.0) & (z > -1.0)
    return jnp.where(safe, (e - 1.0) * z / jnp.log(jnp.where(safe, e, 0.5)), jnp.where(e == 1.0, z, e - 1.0))


def _lru_kernel(gi_ref, xb_ref, h0_ref, cv0_ref, cw_ref, cb_ref, wg_ref, bg_ref, lam_ref,
                o_ref, h_out_ref, h_s, cv_s, buf, *, L, n_pad):
    c = pl.program_id(2)

    @pl.when(c == 0)
    def _():
        h_s[...] = h0_ref[0]
        cv_s[...] = cv0_ref[0]

    row_id = lax.broadcasted_iota(jnp.int32, (L, 1), 0)
    x_raw = xb_ref[...]
    if n_pad:
        pad_col = (row_id < n_pad) & (c == 0)
        x_raw = jnp.where(pad_col, 0.0, x_raw)
    xf = _causal_conv(x_raw, cv_s, buf, cw_ref, cb_ref, L)
    xfb = xf.astype(BF16)
    r = _sigmoid(jnp.dot(xfb, wg_ref[0, 0], preferred_element_type=F32) + bg_ref[0:1, :])
    i = _sigmoid(jnp.dot(xfb, wg_ref[1, 0], preferred_element_type=F32) + bg_ref[1:2, :])
    log_a = (-LRU_C) * r * _softplus(-lam_ref[...])
    a = jnp.exp(log_a)
    u = jnp.sqrt(-_expm1_nonpos(2.0 * log_a)) * (i * xf)
    if n_pad:
        a = jnp.where(pad_col, 1.0, a)
        u = jnp.where(pad_col, 0.0, u)
    k = 1
    while k < L:
        keep = row_id >= k
        a_sh = jnp.where(keep, pltpu.roll(a, k, 0), 1.0)
        u_sh = jnp.where(keep, pltpu.roll(u, k, 0), 0.0)
        u = a * u_sh + u
        a = a * a_sh
        k *= 2
    hs = a * h_s[...] + u
    h_last = hs[L - 1:L, :]
    h_s[...] = h_last
    h_out_ref[0] = h_last
    o_ref[...] = (hs * _gelu_tanh(gi_ref[...])).astype(o_ref.dtype)


def _lru(proj, conv_w, conv_b, wg_sb, b_gate, lam, h0, conv0, *, n_seq, n_chunks, L, row0, n_pad, name):
    rb0 = row0 // L

    def rblk(b, c):
        return rb0 + b * n_chunks + c

    out_dtype = BF16 if L % 16 == 0 else F32
    kern = functools.partial(_lru_kernel, L=L, n_pad=n_pad)
    return pl.pallas_call(
        kern,
        grid=(n_seq, N_LRU_SB, n_chunks),
        in_specs=[
            pl.BlockSpec((L, LRU_SB), lambda b, j, c: (rblk(b, c), j)),
            pl.BlockSpec((L, LRU_SB), lambda b, j, c: (rblk(b, c), N_LRU_SB + j)),
            pl.BlockSpec((1, 1, LRU_SB), lambda b, j, c: (b, 0, j)),
            pl.BlockSpec((1, 8, LRU_SB), lambda b, j, c: (b, 0, j)),
            pl.BlockSpec((CONV_W, LRU_SB), lambda b, j, c: (0, j)),
            pl.BlockSpec((1, LRU_SB), lambda b, j, c: (0, j)),
            pl.BlockSpec((2, 1, LRU_SB, LRU_SB), lambda b, j, c: (0, j, 0, 0)),
            pl.BlockSpec((2, LRU_SB), lambda b, j, c: (0, j)),
            pl.BlockSpec((1, LRU_SB), lambda b, j, c: (0, j)),
        ],
        out_specs=[
            pl.BlockSpec((L, LRU_SB), lambda b, j, c: (b * n_chunks + c, j)),
            pl.BlockSpec((1, 1, LRU_SB), lambda b, j, c: (b, 0, j)),
        ],
        out_shape=[
            jax.ShapeDtypeStruct((n_seq * n_chunks * L, D_RNN), out_dtype),
            jax.ShapeDtypeStruct((n_seq, 1, D_RNN), F32),
        ],
        scratch_shapes=[pltpu.VMEM((1, LRU_SB), F32), pltpu.VMEM((8, LRU_SB), F32),
                        pltpu.VMEM((L + 8, LRU_SB), F32)],
        compiler_params=_cparams(3),
        name=name,
    )(proj, proj, h0, conv0, conv_w, conv_b.reshape(1, D_RNN), wg_sb, b_gate, lam.reshape(1, D_RNN))


def _router_kernel(x_ref, w_ref, b_ref, gate_ref, id_ref):
    logits = jnp.dot(x_ref[...], w_ref[...], preferred_element_type=F32) + b_ref[...]
    tm = logits.shape[0]
    lane = lax.broadcasted_iota(jnp.int32, (tm, 128), 1)
    lane_f = lane.astype(F32)
    is_group = (lane >= N_EXPERTS) & (lane < N_EXPERTS + N_EXP_GROUPS)
    gl = jnp.where(is_group, logits, -jnp.inf)
    g_max = jnp.max(gl, axis=1, keepdims=True)
    g_lane = jnp.min(jnp.where(gl == g_max, lane_f, 1e9), axis=1, keepdims=True)
    g_w = 1.0 / jnp.sum(jnp.exp(gl - g_max), axis=1, keepdims=True)
    g_idx = g_lane - float(N_EXPERTS)
    lo = g_idx * float(EXP_PER_GROUP)
    in_group = (lane_f >= lo) & (lane_f < lo + float(EXP_PER_GROUP))
    el = jnp.where(in_group, logits, -jnp.inf)
    e_max = jnp.max(el, axis=1, keepdims=True)
    i1 = jnp.min(jnp.where(el == e_max, lane_f, 1e9), axis=1, keepdims=True)
    e_sum = jnp.sum(jnp.exp(el - e_max), axis=1, keepdims=True)
    el2 = jnp.where(lane_f == i1, -jnp.inf, el)
    e2_max = jnp.max(el2, axis=1, keepdims=True)
    i2 = jnp.min(jnp.where(el2 == e2_max, lane_f, 1e9), axis=1, keepdims=True)
    p1 = 1.0 / e_sum
    p2 = jnp.exp(e2_max - e_max) / e_sum
    tot = p1 + p2
    gate1 = g_w * (p1 / tot)
    gate2 = g_w * (p2 / tot)
    gate_ref[...] = jnp.where(lane == 0, gate1, jnp.where(lane == 1, gate2, 0.0))
    id_ref[...] = jnp.where(lane == 0, i1, jnp.where(lane == 1, i2, 0.0)).astype(jnp.int32)


def _router(x, w_r, b_r, name):
    M, D = x.shape
    tm = 256
    return pl.pallas_call(
        _router_kernel,
        grid=(M // tm,),
        in_specs=[pl.BlockSpec((tm, D), lambda i: (i, 0)),
                  pl.BlockSpec((D, 128), lambda i: (0, 0)),
                  pl.BlockSpec((1, 128), lambda i: (0, 0))],
        out_specs=[pl.BlockSpec((tm, 128), lambda i: (i, 0)), pl.BlockSpec((tm, 128), lambda i: (i, 0))],
        out_shape=[jax.ShapeDtypeStruct((M, 128), F32), jax.ShapeDtypeStruct((M, 128), jnp.int32)],
        compiler_params=_cparams(1),
        name=name,
    )(x, w_r, b_r)


def _ffn_kernel(te_ref, nv_ref, idx_ref, x_hbm, w1_ref, w3_ref, w2_ref, o_ref,
                w1b, w3b, w2b, xb_s, xbuf, sems):
    i = pl.program_id(0)
    tm = TILE_ROWS
    nv = nv_ref[0]
    slot = i % 2

    @pl.when(i == 0)
    def _():
        _issue_gather(idx_ref, 0, x_hbm, xbuf, 0, sems.at[0], tm)

    @pl.when(i + 1 < nv)
    def _():
        _issue_gather(idx_ref, (i + 1) * tm, x_hbm, xbuf, (1 - slot) * tm, sems.at[1 - slot], tm)

    prev = te_ref[jnp.maximum(i - 1, 0)]

    @pl.when((i == 0) | (te_ref[i] != prev))
    def _():
        w1b[...] = w1_ref[0, 0].astype(BF16)
        w3b[...] = w3_ref[0, 0].astype(BF16)
        w2b[...] = w2_ref[0, 0].astype(BF16)

    @pl.when(i < nv)
    def _():
        _wait_gather(x_hbm, xbuf, sems.at[slot], tm)
        base = slot * (tm * SLAB)
        for j in range(SLAB):
            xb_s[:, j * 128:(j + 1) * 128] = xbuf[pl.ds(base + j, tm, stride=SLAB), :].astype(BF16)
        xb = xb_s[...]
        h1 = jnp.dot(xb, w1b[...], preferred_element_type=F32)
        h3 = jnp.dot(xb, w3b[...], preferred_element_type=F32)
        hid = (_silu(h1) * h3).astype(BF16)
        _slab_store(o_ref, jnp.dot(hid, w2b[...], preferred_element_type=F32))

    @pl.when(i >= nv)
    def _():
        o_ref[...] = jnp.zeros_like(o_ref)


def _expert_ffn(x_slab, row_src, tile_expert, n_valid, w1, w3, w2, layer, name):
    R = row_src.shape[0]
    D = D_MODEL
    tm = TILE_ROWS
    wmap = lambda i, te, nv, idx: (layer, te[i], 0, 0)
    return pl.pallas_call(
        _ffn_kernel,
        grid_spec=pltpu.PrefetchScalarGridSpec(
            num_scalar_prefetch=3,
            grid=(R // tm,),
            in_specs=[
                pl.BlockSpec(memory_space=pl.ANY),
                pl.BlockSpec((1, 1, D, D_EXPERT), wmap),
                pl.BlockSpec((1, 1, D, D_EXPERT), wmap),
                pl.BlockSpec((1, 1, D_EXPERT, D), wmap),
            ],
            out_specs=pl.BlockSpec((tm * SLAB, 128), lambda i, te, nv, idx: (i, 0)),
            scratch_shapes=[pltpu.VMEM((D, D_EXPERT), BF16), pltpu.VMEM((D, D_EXPERT), BF16),
                            pltpu.VMEM((D_EXPERT, D), BF16), pltpu.VMEM((tm, D), BF16),
                            pltpu.VMEM((2 * tm * SLAB, 128), F32), pltpu.SemaphoreType.DMA((2,))],
        ),
        out_shape=jax.ShapeDtypeStruct((R * SLAB, 128), F32),
        compiler_params=_cparams(1),
        name=name,
    )(tile_expert, n_valid, row_src, x_slab, w1, w3, w2)


def _route_plan(ids, n_tok):
    tm = TILE_ROWS
    n_pairs = TOP_K * n_tok
    n_rows = n_pairs + N_EXPERTS * tm
    experts = jnp.arange(N_EXPERTS, dtype=jnp.int32)
    e_flat = ids.T.reshape(-1)
    order = jnp.argsort(e_flat, stable=True).astype(jnp.int32)
    rank_sorted = jnp.argsort(order).astype(jnp.int32)
    sizes = jnp.sum((e_flat[:, None] == experts[None, :]).astype(jnp.int32), axis=0)
    start = jnp.cumsum(sizes) - sizes
    psz = ((sizes + tm - 1) // tm) * tm
    pend = jnp.cumsum(psz)
    pstart = pend - psz
    pos = pstart[e_flat] + rank_sorted - start[e_flat]
    rows = jnp.arange(n_rows, dtype=jnp.int32)
    e_row = jnp.minimum(jnp.sum((rows[:, None] >= pend[None, :]).astype(jnp.int32), axis=1), N_EXPERTS - 1)
    rank = rows - pstart[e_row]
    valid = (rank < sizes[e_row]) & (rows < pend[-1])
    pair = order[jnp.clip(start[e_row] + rank, 0, n_pairs - 1)]
    row_src = jnp.where(valid, pair % n_tok, 0)
    last_e = jnp.max(jnp.where(sizes > 0, experts, 0))
    te = jnp.where(rows[::tm] < pend[-1], e_row[::tm], last_e)
    n_valid = (pend[-1] // tm).astype(jnp.int32)
    return row_src, pos, te, n_valid.reshape(1)


def _hier_moe_ln(x_slab, xb, w_r, b_r, w1, w3, w2, layer, ln_g, ln_b, tag):
    n_tok = xb.shape[0]
    gates, ids = _router(xb, w_r.astype(BF16), b_r, name=f"router_{tag}")
    row_src, pos, te, n_valid = _route_plan(ids[:, :TOP_K], n_tok)
    ys = _expert_ffn(x_slab, row_src, te, n_valid, w1, w3, w2, layer, name=f"experts_{tag}")
    return _combine_ln(x_slab, ys, pos, gates, ln_g, ln_b, name=f"combine_ln_{tag}")


def _router_weights(w_group, b_group, w_expert, b_expert):
    w = jnp.concatenate([w_expert, w_group], axis=1)
    w = jnp.pad(w, ((0, 0), (0, 128 - w.shape[1])))
    b = jnp.pad(jnp.concatenate([b_expert, b_group]), (0, 128 - N_EXPERTS - N_EXP_GROUPS))
    return w, b.reshape(1, 128)


def _pad_conv_state(conv):
    return jnp.pad(conv, ((0, 0), (8 - (CONV_W - 1), 0), (0, 0)))


def kernel(x_prompt, x_sample, state_mlstm_C, state_mlstm_n, state_mlstm_m, state_ssd, state_ssd_conv,
           state_lru_h, state_lru_conv, meta_tokens, w_in_even, mlstm_gate_b, ssd_dt_bias, ssd_A_log, ssd_D,
           ssd_conv_w, ssd_conv_b, mlstm_norm_g, ssd_norm_g, w_out_even, w_in_odd, lru_conv_w, lru_conv_b,
           lru_w_gate, lru_b_gate, lru_lambda, w_out_odd, ln_g, ln_b, moe_w_group, moe_b_group,
           moe_w_expert, moe_b_expert, moe_w1, moe_w3, moe_w2):
    Bp, Tp, D = x_prompt.shape
    Bs, Ts, _ = x_sample.shape
    n_chunks_p = (N_META + Tp + CHUNK - 1) // CHUNK
    Tpp = n_chunks_p * CHUNK
    n_pad = Tpp - N_META - Tp
    n_p = Bp * Tpp
    n_s = Bs * Ts
    n_tok = n_p + n_s

    meta = jnp.broadcast_to(meta_tokens[None], (Bp, N_META, D))
    xp = jnp.concatenate([jnp.zeros((Bp, n_pad, D), F32), meta, x_prompt], axis=1).reshape(n_p, D)
    x0 = jnp.concatenate([xp, x_sample.reshape(n_s, D)], axis=0)
    x0b = x0.astype(BF16)

    grp = [dict(n_seq=Bp, n_chunks=n_chunks_p, L=CHUNK, row0=0, n_pad=n_pad),
           dict(n_seq=Bs, n_chunks=1, L=Ts, row0=n_p, n_pad=0)]

    def seq_view(a, gi):
        if gi == 0:
            return a[:n_p].reshape(Bp, Tpp, a.shape[1])
        return a[n_p:].reshape(Bs, Ts, a.shape[1])

    def tail_rows(a, gi, col0, ncol):
        nb, T, base = (Bp, Tpp, 0) if gi == 0 else (Bs, Ts, n_p)
        idx = (base + np.arange(nb)[:, None] * T + np.arange(T - (CONV_W - 1), T)[None, :]).reshape(-1)
        rows = jnp.take(a, jnp.asarray(idx, jnp.int32), axis=0)
        return rows[:, col0:col0 + ncol].reshape(nb, CONV_W - 1, ncol)

    e = 0
    w = w_in_even[e]
    src_small = 2 * A_HEADS * A_DK + 2 * A_INNER
    src_z = src_small + 2 * A_HEADS
    src_dt = src_z + B_INNER + B_CONV_DIM
    w_all = jnp.concatenate([w[:, :src_small], w[:, src_z:src_dt], w[:, src_small:src_z], w[:, src_dt:],
                             jnp.zeros((D, IN0_PAD - COL_SMALL - N_SMALL), F32)], axis=1).astype(BF16)
    proj = _mm(x0b, w_all, 512, 1280, name="in_proj_even")

    small = proj[:, COL_SMALL:COL_SMALL + N_SMALL]
    h_parts, y_parts, st = [], [], []
    for gi, g in enumerate(grp):
        sv = seq_view(small, gi)
        gates_t = jnp.swapaxes(sv, 1, 2)
        nb, T = sv.shape[0], sv.shape[1]
        gates_c = sv[:, :, 2 * A_HEADS:].reshape(nb, T, B_GROUPS, B_HPG).transpose(0, 2, 1, 3)
        if gi == 0:
            c0 = jnp.zeros((Bp, A_HEADS, A_DK, A_DV), F32)
            n0 = jnp.zeros((Bp, A_HEADS, 1, A_DK), F32)
            m0 = jnp.zeros((Bp, A_HEADS, 1, 128), F32)
            s0 = jnp.zeros((Bp, B_HEADS, B_P, B_N), F32)
            cv0 = jnp.zeros((Bp, 8, B_CONV_DIM), F32)
        else:
            c0 = state_mlstm_C[e]
            n0 = state_mlstm_n[e][:, :, None, :]
            m0 = jnp.broadcast_to(state_mlstm_m[e][:, :, None, None], (Bs, A_HEADS, 1, 128))
            s0 = state_ssd[e]
            cv0 = _pad_conv_state(state_ssd_conv[e])
        h_g, c_g, n_g, m_g = _mlstm(proj, gates_t, mlstm_gate_b[e], mlstm_norm_g[e], c0, n0, m0,
                                    name=f"mlstm_{gi}", **g)
        y_g, s_g = _ssd(proj, gates_t, gates_c, ssd_dt_bias[e], ssd_A_log[e], ssd_D[e], ssd_conv_w[e],
                        ssd_conv_b[e], ssd_norm_g[e], s0, cv0, name=f"ssd_{gi}", **g)
        h_parts.append(h_g.astype(BF16))
        y_parts.append(y_g.astype(BF16))
        sconv = tail_rows(proj, gi, COL_XBC, B_CONV_DIM)
        st.append((c_g[None], n_g[:, :, 0, :][None], m_g[:, :, 0, 0][None], s_g[None], sconv[None]))
    hy = jnp.concatenate([jnp.concatenate(h_parts, axis=0), jnp.concatenate(y_parts, axis=0)], axis=1)
    mix = _mm(hy, w_out_even[e].astype(BF16), 512, 1024, name="out_proj_even")
    x1s, x1b = _add_ln(x0, mix, ln_g[0, 0], ln_b[0, 0], name="add_ln_0")
    w_r, b_r = _router_weights(moe_w_group[0], moe_b_group[0], moe_w_expert[0], moe_b_expert[0])
    x2, x2b = _hier_moe_ln(x1s, x1b, w_r, b_r, moe_w1, moe_w3, moe_w2, 0, ln_g[0, 1], ln_b[0, 1], "0")

    o = 0
    proj1 = _mm(x2b, w_in_odd[o].astype(BF16), 512, 1280, name="in_proj_odd")
    wg = lru_w_gate[o].reshape(2, N_LRU_SB, LRU_SB // LRU_BW, LRU_BW, LRU_BW)
    eye4 = jnp.eye(LRU_SB // LRU_BW, dtype=F32)
    wg_sb = jnp.einsum('gjaik,ab->gjaibk', wg, eye4).reshape(2, N_LRU_SB, LRU_SB, LRU_SB).astype(BF16)
    o_parts, st1 = [], []
    for gi, g in enumerate(grp):
        if gi == 0:
            h0 = jnp.zeros((Bp, 1, D_RNN), F32)
            cv0 = jnp.zeros((Bp, 8, D_RNN), F32)
        else:
            h0 = state_lru_h[o][:, None, :]
            cv0 = _pad_conv_state(state_lru_conv[o])
        o_g, hN = _lru(proj1, lru_conv_w[o], lru_conv_b[o], wg_sb, lru_b_gate[o], lru_lambda[o], h0, cv0,
                       name=f"lru_{gi}", **g)
        o_parts.append(o_g.astype(BF16))
        hconv = tail_rows(proj1, gi, D_RNN, D_RNN)
        st1.append((hN[:, 0, :][None], hconv[None]))
    mix1 = _mm(jnp.concatenate(o_parts, axis=0), w_out_odd[o].astype(BF16), 512, 1024, name="out_proj_odd")
    x3s, x3b = _add_ln(x2, mix1, ln_g[1, 0], ln_b[1, 0], name="add_ln_1")
    w_r, b_r = _router_weights(moe_w_group[1], moe_b_group[1], moe_w_expert[1], moe_b_expert[1])
    x4, _ = _hier_moe_ln(x3s, x3b, w_r, b_r, moe_w1, moe_w3, moe_w2, 1, ln_g[1, 1], ln_b[1, 1], "1")

    y_prompt = x4[:n_p].reshape(Bp, Tpp, D)[:, Tpp - Tp:]
    y_sample = x4[n_p:].reshape(Bs, Ts, D)
    (pC, pn, pm, pS, pSc), (sC, sn, sm, sS, sSc) = st
    (pH, pHc), (sH, sHc) = st1
    return (y_prompt, y_sample, pC, pn, pm, pS, pSc, pH, pHc, sC, sn, sm, sS, sSc, sH, sHc)
```

```python
import functools
import math

import jax
import jax.numpy as jnp
import numpy as np
from jax import lax
from jax.experimental import pallas as pl
from jax.experimental.pallas import tpu as pltpu

F32 = jnp.float32
BF16 = jnp.bfloat16

D_MODEL = 2048
N_META = 16
CHUNK = 128
CONV_W = 4
A_HEADS = 8
A_DK = 128
A_DV = 256
A_INNER = A_HEADS * A_DV
B_HEADS = 32
B_P = 64
B_N = 128
B_GROUPS = 4
B_HPG = B_HEADS // B_GROUPS
B_INNER = B_HEADS * B_P
B_GW = B_INNER // B_GROUPS
B_CONV_DIM = B_INNER + 2 * B_GROUPS * B_N
D_RNN = 2560
LRU_BLOCKS = 16
LRU_BW = D_RNN // LRU_BLOCKS
LRU_C = 8.0
LRU_SB = 640
N_LRU_SB = D_RNN // LRU_SB
N_EXP_GROUPS = 4
EXP_PER_GROUP = 8
N_EXPERTS = N_EXP_GROUPS * EXP_PER_GROUP
TOP_K = 2
D_EXPERT = 512
DEPTH = 2
ALPHA = (2.0 * DEPTH) ** 0.25

COL_Q = 0
COL_K = A_HEADS * A_DK
COL_V = 2 * A_HEADS * A_DK
COL_O = COL_V + A_INNER
COL_Z = COL_O + A_INNER
COL_XBC = COL_Z + B_INNER
COL_SMALL = COL_XBC + B_CONV_DIM
N_SMALL = 2 * A_HEADS + B_HEADS
IN0_PAD = 11520

NEG_BIG = -1e30
VMEM_LIMIT_BYTES = 48 * 1024 * 1024
TILE_ROWS = 256
LN_ROWS = 256
SLAB = D_MODEL // 128


def _cparams(n_axes):
    return pltpu.CompilerParams(dimension_semantics=("arbitrary",) * n_axes,
                                vmem_limit_bytes=VMEM_LIMIT_BYTES)


def _smem_spec():
    return pl.BlockSpec(memory_space=pltpu.SMEM)


def _mm_kernel(x_ref, w_ref, o_ref):
    o_ref[...] = jnp.dot(x_ref[...], w_ref[...], preferred_element_type=F32).astype(o_ref.dtype)


def _mm(x, w, bm, bn, name):
    M, K = x.shape
    N = w.shape[1]
    return pl.pallas_call(
        _mm_kernel,
        grid=(N // bn, M // bm),
        in_specs=[pl.BlockSpec((bm, K), lambda j, i: (i, 0)),
                  pl.BlockSpec((K, bn), lambda j, i: (0, j))],
        out_specs=pl.BlockSpec((bm, bn), lambda j, i: (i, j)),
        out_shape=jax.ShapeDtypeStruct((M, N), F32),
        compiler_params=_cparams(2),
        name=name,
    )(x, w)


def _mm_parts_kernel(*refs, k_sizes, n_blk_a):
    n = len(k_sizes)
    a_refs, b_refs, w_ref, o_ref = refs[:n], refs[n:2 * n], refs[2 * n], refs[2 * n + 1]
    i = pl.program_id(1)

    def run(lhs_refs):
        acc, k0 = None, 0
        for r, ks in zip(lhs_refs, k_sizes):
            part = jnp.dot(r[...], w_ref[k0:k0 + ks, :], preferred_element_type=F32)
            acc = part if acc is None else acc + part
            k0 += ks
        o_ref[...] = acc

    @pl.when(i < n_blk_a)
    def _():
        run(a_refs)

    @pl.when(i >= n_blk_a)
    def _():
        run(b_refs)


def _mm_parts(parts_a, parts_b, w, bm, bn, name):
    k_sizes = tuple(p.shape[1] for p in parts_a)
    n_blk_a = parts_a[0].shape[0] // bm
    n_blk_b = parts_b[0].shape[0] // bm
    K, N = w.shape
    a_specs = [pl.BlockSpec((bm, ks), lambda j, i: (jnp.minimum(i, n_blk_a - 1), 0)) for ks in k_sizes]
    b_specs = [pl.BlockSpec((bm, ks), lambda j, i: (jnp.maximum(i - n_blk_a, 0), 0)) for ks in k_sizes]
    return pl.pallas_call(
        functools.partial(_mm_parts_kernel, k_sizes=k_sizes, n_blk_a=n_blk_a),
        grid=(N // bn, n_blk_a + n_blk_b),
        in_specs=a_specs + b_specs + [pl.BlockSpec((K, bn), lambda j, i: (0, j))],
        out_specs=pl.BlockSpec((bm, bn), lambda j, i: (i, j)),
        out_shape=jax.ShapeDtypeStruct(((n_blk_a + n_blk_b) * bm, N), F32),
        compiler_params=_cparams(2),
        name=name,
    )(*parts_a, *parts_b, w)


def _layer_norm_rows(y, g, b):
    mu = jnp.mean(y, axis=-1, keepdims=True)
    yc = y - mu
    var = jnp.mean(yc * yc, axis=-1, keepdims=True)
    return yc * lax.rsqrt(var + 1e-5) * g + b


def _slab_store(ref, val):
    tm = val.shape[0]
    for j in range(SLAB):
        ref[pl.ds(j, tm, stride=SLAB), :] = val[:, j * 128:(j + 1) * 128]


def _slab_piece(ref, j, tm):
    return ref[pl.ds(j, tm, stride=SLAB), :]


def _add_ln_kernel(x_ref, m_ref, g_ref, b_ref, os_ref, ob_ref):
    y = _layer_norm_rows(ALPHA * x_ref[...] + m_ref[...], g_ref[...], b_ref[...])
    _slab_store(os_ref, y)
    ob_ref[...] = y.astype(BF16)


def _add_ln(x, mix, g, b, name):
    M, D = x.shape
    tm = LN_ROWS
    row = pl.BlockSpec((tm, D), lambda i: (i, 0))
    vec = pl.BlockSpec((1, D), lambda i: (0, 0))
    return pl.pallas_call(
        _add_ln_kernel,
        grid=(M // tm,),
        in_specs=[row, row, vec, vec],
        out_specs=[pl.BlockSpec((tm * SLAB, 128), lambda i: (i, 0)), row],
        out_shape=[jax.ShapeDtypeStruct((M * SLAB, 128), F32), jax.ShapeDtypeStruct((M, D), BF16)],
        compiler_params=_cparams(1),
        name=name,
    )(x, mix, g.reshape(1, D), b.reshape(1, D))


def _slab_copy(src_hbm, dst_vmem, sem, src_tok, dst_tok):
    return pltpu.make_async_copy(src_hbm.at[pl.ds(pl.multiple_of(src_tok * SLAB, SLAB), SLAB), :],
                                 dst_vmem.at[pl.ds(pl.multiple_of(dst_tok * SLAB, SLAB), SLAB), :], sem)


def _issue_gather(idx_ref, idx0, src_hbm, dst_vmem, dst0, sem, n):
    def body(r, carry):
        _slab_copy(src_hbm, dst_vmem, sem, idx_ref[idx0 + r], dst0 + r).start()
        return carry
    lax.fori_loop(0, n, body, 0, unroll=8)


def _wait_gather(src_hbm, dst_vmem, sem, n):
    def body(r, carry):
        _slab_copy(src_hbm, dst_vmem, sem, 0, 0).wait()
        return carry
    lax.fori_loop(0, n, body, 0, unroll=8)


def _combine_rows(pos_ref, x_ref, y_hbm, gate_ref, g_ref, b_ref, v_s, ybuf, sems, n_tok, tm):
    i = pl.program_id(0)
    n = pl.num_programs(0)
    slot = i % 2

    def fetch(tile, sl):
        for kk in range(TOP_K):
            _issue_gather(pos_ref, kk * n_tok + tile * tm, y_hbm, ybuf, (sl * TOP_K + kk) * tm, sems.at[sl], tm)

    @pl.when(i == 0)
    def _():
        fetch(0, 0)

    @pl.when(i + 1 < n)
    def _():
        fetch(i + 1, 1 - slot)

    _wait_gather(y_hbm, ybuf, sems.at[slot], TOP_K * tm)
    gate = gate_ref[...]
    g0 = gate[:, 0:1]
    g1 = gate[:, 1:2]
    base0 = slot * (TOP_K * tm * SLAB)
    base1 = base0 + tm * SLAB
    for j in range(SLAB):
        v_s[:, j * 128:(j + 1) * 128] = (ALPHA * _slab_piece(x_ref, j, tm)
                                         + g0 * ybuf[pl.ds(base0 + j, tm, stride=SLAB), :]
                                         + g1 * ybuf[pl.ds(base1 + j, tm, stride=SLAB), :])
    return _layer_norm_rows(v_s[...], g_ref[...], b_ref[...])


def _combine_ln_kernel(pos_ref, x_ref, y_hbm, gate_ref, g_ref, b_ref, o_ref, ob_ref, v_s, ybuf, sems, *, n_tok):
    y = _combine_rows(pos_ref, x_ref, y_hbm, gate_ref, g_ref, b_ref, v_s, ybuf, sems, n_tok, LN_ROWS)
    o_ref[...] = y
    ob_ref[...] = y.astype(BF16)


def _combine_out_kernel(pos_ref, x_ref, y_hbm, gate_ref, g_ref, b_ref, yp_ref, ys_ref, v_s, ybuf, sems, *,
                        n_tok, n_blk_p, blk_per_seq):
    y = _combine_rows(pos_ref, x_ref, y_hbm, gate_ref, g_ref, b_ref, v_s, ybuf, sems, n_tok, CHUNK)
    i = pl.program_id(0)

    @pl.when((i < n_blk_p) & (i % blk_per_seq > 0))
    def _():
        yp_ref[...] = y

    @pl.when(i >= n_blk_p)
    def _():
        ys_ref[...] = y


def _combine_ln(x_slab, y_slab, pos, gates, g, b, name, split=None):
    M = gates.shape[0]
    D = D_MODEL
    tm = LN_ROWS if split is None else CHUNK
    vec = pl.BlockSpec((1, D), lambda i, pos: (0, 0))
    if split is None:
        kern = functools.partial(_combine_ln_kernel, n_tok=M)
        row = pl.BlockSpec((tm, D), lambda i, pos: (i, 0))
        out_specs = [row, row]
        out_shape = [jax.ShapeDtypeStruct((M, D), F32), jax.ShapeDtypeStruct((M, D), BF16)]
    else:
        n_p, bps = split
        n_blk_p = n_p // tm
        kern = functools.partial(_combine_out_kernel, n_tok=M, n_blk_p=n_blk_p, blk_per_seq=bps)

        def yp_map(i, pos):
            ip = jnp.minimum(i, n_blk_p - 1)
            return ((ip // bps) * (bps - 1) + jnp.maximum(ip % bps - 1, 0), 0)

        out_specs = [pl.BlockSpec((tm, D), yp_map),
                     pl.BlockSpec((tm, D), lambda i, pos: (jnp.maximum(i - n_blk_p, 0), 0))]
        out_shape = [jax.ShapeDtypeStruct((n_blk_p // bps * (bps - 1) * tm, D), F32),
                     jax.ShapeDtypeStruct((M - n_p, D), F32)]
    return pl.pallas_call(
        kern,
        grid_spec=pltpu.PrefetchScalarGridSpec(
            num_scalar_prefetch=1,
            grid=(M // tm,),
            in_specs=[pl.BlockSpec((tm * SLAB, 128), lambda i, pos: (i, 0)),
                      pl.BlockSpec(memory_space=pl.ANY),
                      pl.BlockSpec((tm, 128), lambda i, pos: (i, 0)), vec, vec],
            out_specs=out_specs,
            scratch_shapes=[pltpu.VMEM((tm, D), F32), pltpu.VMEM((2 * TOP_K * tm * SLAB, 128), F32),
                            pltpu.SemaphoreType.DMA((2,))],
        ),
        out_shape=out_shape,
        compiler_params=_cparams(1),
        name=name,
    )(pos, x_slab, y_slab, gates, g.reshape(1, D), b.reshape(1, D))


def _softplus(x):
    return jnp.maximum(x, 0.0) + jnp.log1p(jnp.exp(-jnp.abs(x)))


def _sigmoid(x):
    return 1.0 / (1.0 + jnp.exp(-x))


def _silu(x):
    return x * _sigmoid(x)


def _row_to_col(row, eye):
    return jnp.sum(jnp.where(eye, row, 0.0), axis=1, keepdims=True)


def _causal_conv(x, carry_ref, buf_ref, w_ref, b_ref, L):
    buf_ref[0:8, :] = carry_ref[...]
    buf_ref[8:8 + L, :] = x
    acc = b_ref[...] + w_ref[0:1, :] * buf_ref[5:5 + L, :]
    for j in range(1, CONV_W):
        acc = acc + w_ref[j:j + 1, :] * buf_ref[5 + j:5 + j + L, :]
    carry_ref[...] = buf_ref[L:L + 8, :]
    return acc


def _mlstm_kernel(gb_ref, q_ref, k_ref, v_ref, o_ref, gt_ref, c0_ref, n0_ref, m0_ref, ng_ref,
                  h_ref, c_out_ref, n_out_ref, m_out_ref, c_s, n_s, m_s, *, L, n_pad):
    c = pl.program_id(1)

    @pl.when(c == 0)
    def _():
        c_s[...] = c0_ref[0]
        n_s[...] = n0_ref[0]
        m_s[...] = m0_ref[0]

    rows = lax.broadcasted_iota(jnp.int32, (L, L), 0)
    cols = lax.broadcasted_iota(jnp.int32, (L, L), 1)
    eye = rows == cols
    causal = cols <= rows

    if n_pad:
        pad = (lax.broadcasted_iota(jnp.int32, (1, L), 1) < n_pad) & (c == 0)

    for hd in range(A_HEADS):
        li = gt_ref[0, hd:hd + 1, :] + gb_ref[0, hd]
        fr = gt_ref[0, A_HEADS + hd:A_HEADS + hd + 1, :] + gb_ref[1, hd]
        lf = jnp.minimum(fr, 0.0) - jnp.log1p(jnp.exp(-jnp.abs(fr)))
        if n_pad:
            li = jnp.where(pad, NEG_BIG, li)
            lf = jnp.where(pad, 0.0, lf)

        lf_col = _row_to_col(lf, eye)
        b_col = jnp.sum(jnp.where(causal, lf, 0.0), axis=1, keepdims=True)
        b_row = jnp.sum(jnp.where(rows <= cols, lf_col, 0.0), axis=0, keepdims=True)
        m_prev = m_s[hd][:, 0:1]

        d = jnp.where(causal, b_col - b_row + li, NEG_BIG)
        inter = b_col + m_prev
        m_t = jnp.maximum(inter, jnp.max(d, axis=1, keepdims=True))
        w_intra = jnp.exp(d - m_t)
        w_inter = jnp.exp(inter - m_t)

        q = q_ref[:, hd * A_DK:(hd + 1) * A_DK] * (A_DK ** -0.5)
        k = k_ref[:, hd * A_DK:(hd + 1) * A_DK]
        v = v_ref[:, hd * A_DV:(hd + 1) * A_DV]
        qb = q.astype(BF16)
        kb = k.astype(BF16)
        vb = v.astype(BF16)
        s = lax.dot_general(qb, kb, (((1,), (1,)), ((), ())), preferred_element_type=F32) * w_intra
        c_prev = c_s[hd]
        n_prev = n_s[hd]
        num = jnp.dot(s.astype(BF16), vb, preferred_element_type=F32)
        num = num + w_inter * jnp.dot(qb, c_prev.astype(BF16), preferred_element_type=F32)
        qn = jnp.sum(qb.astype(F32) * n_prev.astype(BF16).astype(F32), axis=1, keepdims=True)
        nq = jnp.sum(s, axis=1, keepdims=True) + w_inter * qn
        hh = num / jnp.maximum(jnp.abs(nq), jnp.exp(-m_t))

        hn = hh * lax.rsqrt(jnp.mean(hh * hh, axis=-1, keepdims=True) + 1e-6)
        hn = hn * ng_ref[:, hd * A_DV:(hd + 1) * A_DV]
        h_ref[:, hd * A_DV:(hd + 1) * A_DV] = (
            hn * _sigmoid(o_ref[:, hd * A_DV:(hd + 1) * A_DV])).astype(h_ref.dtype)

        b_last = b_row[:, L - 1:L]
        ws_log = b_last - b_row + li
        m_new = jnp.maximum(b_last + m_prev, jnp.max(ws_log, axis=1, keepdims=True))
        ws_col = _row_to_col(jnp.exp(ws_log - m_new), eye)
        wc = jnp.exp(b_last + m_prev - m_new)
        kw = k * ws_col
        c_new = wc * c_prev + lax.dot_general(kw.astype(BF16), vb, (((0,), (0,)), ((), ())),
                                              preferred_element_type=F32)
        n_new = wc * n_prev + jnp.sum(ws_col.astype(BF16).astype(F32) * kb.astype(F32), axis=0, keepdims=True)
        m_new_b = jnp.broadcast_to(m_new, (1, 128))
        c_s[hd] = c_new
        n_s[hd] = n_new
        m_s[hd] = m_new_b
        c_out_ref[0, hd] = c_new
        n_out_ref[0, hd] = n_new
        m_out_ref[0, hd] = m_new_b


def _mlstm(proj, gates_t, gate_b, norm_g, c0, n0, m0, *, n_seq, n_chunks, L, row0, n_pad, name):
    rb0 = row0 // L

    def rblk(b, c):
        return rb0 + b * n_chunks + c

    out_dtype = BF16 if L % 16 == 0 else F32
    qk_w = A_HEADS * A_DK
    state_map = lambda b, c: (b, 0, 0, 0)
    kern = functools.partial(_mlstm_kernel, L=L, n_pad=n_pad)
    return pl.pallas_call(
        kern,
        grid=(n_seq, n_chunks),
        in_specs=[
            _smem_spec(),
            pl.BlockSpec((L, qk_w), lambda b, c: (rblk(b, c), COL_Q // qk_w)),
            pl.BlockSpec((L, qk_w), lambda b, c: (rblk(b, c), COL_K // qk_w)),
            pl.BlockSpec((L, A_INNER), lambda b, c: (rblk(b, c), COL_V // A_INNER)),
            pl.BlockSpec((L, A_INNER), lambda b, c: (rblk(b, c), COL_O // A_INNER)),
            pl.BlockSpec((1, N_SMALL, L), lambda b, c: (b, 0, c)),
            pl.BlockSpec((1, A_HEADS, A_DK, A_DV), state_map),
            pl.BlockSpec((1, A_HEADS, 1, A_DK), state_map),
            pl.BlockSpec((1, A_HEADS, 1, 128), state_map),
            pl.BlockSpec((1, A_INNER), lambda b, c: (0, 0)),
        ],
        out_specs=[
            pl.BlockSpec((L, A_INNER), lambda b, c: (b * n_chunks + c, 0)),
            pl.BlockSpec((1, A_HEADS, A_DK, A_DV), state_map),
            pl.BlockSpec((1, A_HEADS, 1, A_DK), state_map),
            pl.BlockSpec((1, A_HEADS, 1, 128), state_map),
        ],
        out_shape=[
            jax.ShapeDtypeStruct((n_seq * n_chunks * L, A_INNER), out_dtype),
            jax.ShapeDtypeStruct((n_seq, A_HEADS, A_DK, A_DV), F32),
            jax.ShapeDtypeStruct((n_seq, A_HEADS, 1, A_DK), F32),
            jax.ShapeDtypeStruct((n_seq, A_HEADS, 1, 128), F32),
        ],
        scratch_shapes=[pltpu.VMEM((A_HEADS, A_DK, A_DV), F32), pltpu.VMEM((A_HEADS, 1, A_DK), F32),
                        pltpu.VMEM((A_HEADS, 1, 128), F32)],
        compiler_params=_cparams(2),
        name=name,
    )(gate_b, proj, proj, proj, proj, gates_t, c0, n0, m0, norm_g.reshape(1, A_INNER))


def _ssd_kernel(dtr_ref, dtc_ref, alr_ref, alc_ref, xs_ref, bm_ref, cm_ref, z_ref, gt_ref, gc_ref, s0_ref,
                cx0_ref, cb0_ref, cc0_ref, wx_ref, wb_ref, wc_ref, bx_ref, bb_ref, bc_ref, ng_ref, dsk_ref,
                y_ref, s_out_ref,
                s_s, cx_s, cb_s, cc_s, bufx, bufb, bufc, ybuf, *, L, n_pad, gps):
    grp0 = pl.program_id(1) * gps
    c = pl.program_id(2)

    @pl.when(c == 0)
    def _():
        s_s[...] = s0_ref[0]
        cx_s[...] = cx0_ref[0]
        cb_s[...] = cb0_ref[0]
        cc_s[...] = cc0_ref[0]

    rows = lax.broadcasted_iota(jnp.int32, (L, L), 0)
    cols = lax.broadcasted_iota(jnp.int32, (L, L), 1)
    causal = cols <= rows
    tril = causal.astype(F32)
    triu = (rows <= cols).astype(F32)
    lane_lo = lax.broadcasted_iota(jnp.int32, (L, 2 * B_P), 1) < B_P
    row_lo = lax.broadcasted_iota(jnp.int32, (2 * B_P, 1), 0) < B_P
    if n_pad:
        pad_row = (lax.broadcasted_iota(jnp.int32, (1, L), 1) < n_pad) & (c == 0)
        pad_col = (lax.broadcasted_iota(jnp.int32, (L, 1), 0) < n_pad) & (c == 0)

    xs_raw = xs_ref[...]
    bm_raw = bm_ref[...]
    cm_raw = cm_ref[...]
    if n_pad:
        xs_raw = jnp.where(pad_col, 0.0, xs_raw)
        bm_raw = jnp.where(pad_col, 0.0, bm_raw)
        cm_raw = jnp.where(pad_col, 0.0, cm_raw)
    xs = _silu(_causal_conv(xs_raw, cx_s, bufx, wx_ref, bx_ref, L))
    bm = _silu(_causal_conv(bm_raw, cb_s, bufb, wb_ref, bb_ref, L))
    cm = _silu(_causal_conv(cm_raw, cc_s, bufc, wc_ref, bc_ref, L))

    def two(col, h0):
        return jnp.where(lane_lo, col[:, h0:h0 + 1], col[:, h0 + 1:h0 + 2])

    for gg in range(gps):
        bmb = bm[:, gg * B_N:(gg + 1) * B_N].astype(BF16)
        cmb = cm[:, gg * B_N:(gg + 1) * B_N].astype(BF16)
        cb = lax.dot_general(cmb, bmb, (((1,), (1,)), ((), ())), preferred_element_type=F32)

        row0 = pl.multiple_of(2 * A_HEADS + (grp0 + gg) * B_HPG, B_HPG)
        dt_c = _softplus(gc_ref[0, gg] + dtr_ref[gg])
        dt_r = _softplus(gt_ref[0, pl.ds(row0, B_HPG), :] + dtc_ref[gg])
        if n_pad:
            dt_c = jnp.where(pad_col, 0.0, dt_c)
            dt_r = jnp.where(pad_row, 0.0, dt_r)
        a_c = dt_c * (-jnp.exp(alr_ref[gg]))
        a_r = dt_r * (-jnp.exp(alc_ref[gg]))
        b_c = jnp.dot(tril, a_c, preferred_element_type=F32, precision=lax.Precision.HIGHEST)
        b_r = jnp.dot(a_r, triu, preferred_element_type=F32, precision=lax.Precision.HIGHEST)
        eb_c = jnp.exp(b_c)
        b_last = b_c[L - 1:L, :]
        w_c = jnp.exp(b_last - b_c)
        e_last = jnp.exp(b_last)

        for p in range(B_HPG // 2):
            h0 = 2 * p
            pair = gg * (B_HPG // 2) + p
            x_pair = xs[:, pair * 2 * B_P:(pair + 1) * 2 * B_P]
            xdt = x_pair * two(dt_c, h0)
            y = None
            for hh, keep in ((h0, lane_lo), (h0 + 1, ~lane_lo)):
                decay = jnp.exp(jnp.where(causal, b_c[:, hh:hh + 1] - b_r[hh:hh + 1, :], NEG_BIG))
                part = jnp.dot((cb * decay).astype(BF16), jnp.where(keep, xdt, 0.0).astype(BF16),
                               preferred_element_type=F32)
                y = part if y is None else y + part
            s_prev = s_s[pair]
            y = y + two(eb_c, h0) * lax.dot_general(cmb, s_prev.astype(BF16), (((1,), (1,)), ((), ())),
                                                    preferred_element_type=F32)
            upd = lax.dot_general((xdt * two(w_c, h0)).astype(BF16), bmb, (((0,), (0,)), ((), ())),
                                  preferred_element_type=F32)
            s_new = jnp.where(row_lo, e_last[:, h0:h0 + 1], e_last[:, h0 + 1:h0 + 2]) * s_prev + upd
            s_s[pair] = s_new
            s_out_ref[0, pair] = s_new
            ybuf[:, pair * 2 * B_P:(pair + 1) * 2 * B_P] = y

    y = (ybuf[...] + dsk_ref[...] * xs) * _silu(z_ref[...])
    for gg in range(gps):
        yg = y[:, gg * B_GW:(gg + 1) * B_GW]
        yn = yg * lax.rsqrt(jnp.mean(yg * yg, axis=-1, keepdims=True) + 1e-6) * ng_ref[:, gg * B_GW:(gg + 1) * B_GW]
        y_ref[:, gg * B_GW:(gg + 1) * B_GW] = yn.astype(y_ref.dtype)


def _ssd(proj, gates_t, gates_c, dt_bias, a_log, d_skip, conv_w, conv_b, norm_g, s0, conv0, *,
         n_seq, n_chunks, L, row0, n_pad, name):
    rb0 = row0 // L

    def rblk(b, c):
        return rb0 + b * n_chunks + c

    out_dtype = BF16 if L % 16 == 0 else F32
    gps = B_GROUPS if n_chunks == 1 else 1
    xw, nw = gps * B_GW, gps * B_N
    n_pairs = gps * B_HPG // 2
    cw = conv_w
    cbias = conv_b.reshape(1, B_CONV_DIM)
    xoff = COL_XBC // xw
    boff = (COL_XBC + B_INNER) // nw
    coff = (COL_XBC + B_INNER + B_GROUPS * B_N) // nw
    cvb = B_INNER // nw
    cvc = (B_INNER + B_GROUPS * B_N) // nw
    dt_r = dt_bias.reshape(B_GROUPS, 1, B_HPG)
    dt_c = dt_bias.reshape(B_GROUPS, B_HPG, 1)
    al_r = a_log.reshape(B_GROUPS, 1, B_HPG)
    al_c = a_log.reshape(B_GROUPS, B_HPG, 1)
    d_row = jnp.repeat(d_skip, B_P).reshape(1, B_INNER)
    s0p = s0.reshape(n_seq, B_HEADS // 2, 2 * B_P, B_N)
    prow = pl.BlockSpec((gps, 1, B_HPG), lambda b, g, c: (g, 0, 0))
    pcol = pl.BlockSpec((gps, B_HPG, 1), lambda b, g, c: (g, 0, 0))
    kern = functools.partial(_ssd_kernel, L=L, n_pad=n_pad, gps=gps)
    y, s_new = pl.pallas_call(
        kern,
        grid=(n_seq, B_GROUPS // gps, n_chunks),
        in_specs=[
            prow, pcol, prow, pcol,
            pl.BlockSpec((L, xw), lambda b, g, c: (rblk(b, c), xoff + g)),
            pl.BlockSpec((L, nw), lambda b, g, c: (rblk(b, c), boff + g)),
            pl.BlockSpec((L, nw), lambda b, g, c: (rblk(b, c), coff + g)),
            pl.BlockSpec((L, xw), lambda b, g, c: (rblk(b, c), COL_Z // xw + g)),
            pl.BlockSpec((1, N_SMALL, L), lambda b, g, c: (b, 0, c)),
            pl.BlockSpec((1, gps, L, B_HPG), lambda b, g, c: (b, g, c, 0)),
            pl.BlockSpec((1, n_pairs, 2 * B_P, B_N), lambda b, g, c: (b, g, 0, 0)),
            pl.BlockSpec((1, 8, xw), lambda b, g, c: (b, 0, g)),
            pl.BlockSpec((1, 8, nw), lambda b, g, c: (b, 0, cvb + g)),
            pl.BlockSpec((1, 8, nw), lambda b, g, c: (b, 0, cvc + g)),
            pl.BlockSpec((CONV_W, xw), lambda b, g, c: (0, g)),
            pl.BlockSpec((CONV_W, nw), lambda b, g, c: (0, cvb + g)),
            pl.BlockSpec((CONV_W, nw), lambda b, g, c: (0, cvc + g)),
            pl.BlockSpec((1, xw), lambda b, g, c: (0, g)),
            pl.BlockSpec((1, nw), lambda b, g, c: (0, cvb + g)),
            pl.BlockSpec((1, nw), lambda b, g, c: (0, cvc + g)),
            pl.BlockSpec((1, xw), lambda b, g, c: (0, g)),
            pl.BlockSpec((1, xw), lambda b, g, c: (0, g)),
        ],
        out_specs=[
            pl.BlockSpec((L, xw), lambda b, g, c: (b * n_chunks + c, g)),
            pl.BlockSpec((1, n_pairs, 2 * B_P, B_N), lambda b, g, c: (b, g, 0, 0)),
        ],
        out_shape=[
            jax.ShapeDtypeStruct((n_seq * n_chunks * L, B_INNER), out_dtype),
            jax.ShapeDtypeStruct((n_seq, B_HEADS // 2, 2 * B_P, B_N), F32),
        ],
        scratch_shapes=[
            pltpu.VMEM((n_pairs, 2 * B_P, B_N), F32),
            pltpu.VMEM((8, xw), F32), pltpu.VMEM((8, nw), F32), pltpu.VMEM((8, nw), F32),
            pltpu.VMEM((L + 8, xw), F32), pltpu.VMEM((L + 8, nw), F32), pltpu.VMEM((L + 8, nw), F32),
            pltpu.VMEM((L, xw), F32),
        ],
        compiler_params=_cparams(3),
        name=name,
    )(dt_r, dt_c, al_r, al_c, proj, proj, proj, proj, gates_t, gates_c, s0p, conv0, conv0, conv0,
      cw, cw, cw, cbias, cbias, cbias, norm_g.reshape(1, B_INNER), d_row)
    return y, s_new.reshape(n_seq, B_HEADS, B_P, B_N)


def _gelu_tanh(x):
    return 0.5 * x * (1.0 + jnp.tanh(math.sqrt(2.0 / math.pi) * (x + 0.044715 * (x * x * x))))


def _expm1_nonpos(z):
    e = jnp.exp(z)
    safe = (e < 1.0) & (z > -1.0)
    return jnp.where(safe, (e - 1.0) * z / jnp.log(jnp.where(safe, e, 0.5)), jnp.where(e == 1.0, z, e - 1.0))


def _lru_gates(xf, wg_ref, bg_ref, lam_ref):
    xfb = xf.astype(BF16)
    r = _sigmoid(jnp.dot(xfb, wg_ref[0, 0], preferred_element_type=F32) + bg_ref[0:1, :])
    i = _sigmoid(jnp.dot(xfb, wg_ref[1, 0], preferred_element_type=F32) + bg_ref[1:2, :])
    log_a = (-LRU_C) * r * _softplus(-lam_ref[...])
    a = jnp.exp(log_a)
    u = jnp.sqrt(-_expm1_nonpos(2.0 * log_a)) * (i * xf)
    return a, u


def _scan_rows8(a, u, t8):
    for k in (1, 2, 4):
        keep = t8 >= k
        a_sh = jnp.where(keep, pltpu.roll(a, k, 0), 1.0)
        u_sh = jnp.where(keep, pltpu.roll(u, k, 0), 0.0)
        u = a * u_sh + u
        a = a * a_sh
    return a, u


def _lru_kernel(gi_ref, xb_ref, h0_ref, cv0_ref, cw_ref, cb_ref, wg_ref, bg_ref, lam_ref,
                o_ref, h_out_ref, h_s, cv_s, buf, *, L, n_pad):
    c = pl.program_id(2)

    @pl.when(c == 0)
    def _():
        h_s[...] = h0_ref[0]
        cv_s[...] = cv0_ref[0]

    row_id = lax.broadcasted_iota(jnp.int32, (L, 1), 0)
    x_raw = xb_ref[...]
    if n_pad:
        pad_col = (row_id < n_pad) & (c == 0)
        x_raw = jnp.where(pad_col, 0.0, x_raw)
    xf = _causal_conv(x_raw, cv_s, buf, cw_ref, cb_ref, L)
    a, u = _lru_gates(xf, wg_ref, bg_ref, lam_ref)
    if n_pad:
        a = jnp.where(pad_col, 1.0, a)
        u = jnp.where(pad_col, 0.0, u)
    a, u = _scan_rows8(a, u, row_id % 8)
    h_in = h_s[...]
    pieces = []
    for t in range(L // 8):
        piece = a[t * 8:(t + 1) * 8, :] * h_in + u[t * 8:(t + 1) * 8, :]
        pieces.append(piece)
        h_in = piece[7:8, :]
    hs = jnp.concatenate(pieces, axis=0)
    h_s[...] = h_in
    h_out_ref[0] = h_in
    o_ref[...] = (hs * _gelu_tanh(gi_ref[...])).astype(o_ref.dtype)


def _lru_rows8_kernel(gi_ref, xb_ref, h0_ref, hist_ref, cw_ref, cb_ref, wg_ref, bg_ref, lam_ref,
                      o_ref, hs_ref):
    R = xb_ref.shape[0]
    t8 = lax.broadcasted_iota(jnp.int32, (R, 1), 0) % 8
    x = xb_ref[...]
    hist = hist_ref[...]
    acc = cb_ref[...] + cw_ref[CONV_W - 1:CONV_W, :] * x
    for j in range(CONV_W - 1):
        sh = CONV_W - 1 - j
        tap = jnp.where(t8 >= sh, pltpu.roll(x, sh, 0), pltpu.roll(hist, (sh - 8) % R, 0))
        acc = acc + cw_ref[j:j + 1, :] * tap
    a, u = _lru_gates(acc, wg_ref, bg_ref, lam_ref)
    a, u = _scan_rows8(a, u, t8)
    hs = a * h0_ref[...] + u
    hs_ref[...] = hs
    o_ref[...] = (hs * _gelu_tanh(gi_ref[...])).astype(o_ref.dtype)


def _lru_rows8(proj, conv_w, conv_b, wg_sb, b_gate, lam, h0, conv0, *, n_seq, row0, name):
    R = 128
    n_rows = n_seq * 8
    rb0 = row0 // R
    h0x = jnp.repeat(h0.reshape(n_seq, D_RNN), 8, axis=0)
    hist = conv0.reshape(n_rows, D_RNN)
    blk = lambda j, i: (i, j)
    return pl.pallas_call(
        _lru_rows8_kernel,
        grid=(N_LRU_SB, n_rows // R),
        in_specs=[
            pl.BlockSpec((R, LRU_SB), lambda j, i: (rb0 + i, j)),
            pl.BlockSpec((R, LRU_SB), lambda j, i: (rb0 + i, N_LRU_SB + j)),
            pl.BlockSpec((R, LRU_SB), blk),
            pl.BlockSpec((R, LRU_SB), blk),
            pl.BlockSpec((CONV_W, LRU_SB), lambda j, i: (0, j)),
            pl.BlockSpec((1, LRU_SB), lambda j, i: (0, j)),
            pl.BlockSpec((2, 1, LRU_SB, LRU_SB), lambda j, i: (0, j, 0, 0)),
            pl.BlockSpec((2, LRU_SB), lambda j, i: (0, j)),
            pl.BlockSpec((1, LRU_SB), lambda j, i: (0, j)),
        ],
        out_specs=[pl.BlockSpec((R, LRU_SB), blk), pl.BlockSpec((R, LRU_SB), blk)],
        out_shape=[jax.ShapeDtypeStruct((n_rows, D_RNN), BF16), jax.ShapeDtypeStruct((n_rows, D_RNN), F32)],
        compiler_params=_cparams(2),
        name=name,
    )(proj, proj, h0x, hist, conv_w, conv_b.reshape(1, D_RNN), wg_sb, b_gate, lam.reshape(1, D_RNN))


def _lru(proj, conv_w, conv_b, wg_sb, b_gate, lam, h0, conv0, *, n_seq, n_chunks, L, row0, n_pad, name):
    rb0 = row0 // L

    def rblk(b, c):
        return rb0 + b * n_chunks + c

    out_dtype = BF16 if L % 16 == 0 else F32
    kern = functools.partial(_lru_kernel, L=L, n_pad=n_pad)
    return pl.pallas_call(
        kern,
        grid=(n_seq, N_LRU_SB, n_chunks),
        in_specs=[
            pl.BlockSpec((L, LRU_SB), lambda b, j, c: (rblk(b, c), j)),
            pl.BlockSpec((L, LRU_SB), lambda b, j, c: (rblk(b, c), N_LRU_SB + j)),
            pl.BlockSpec((1, 1, LRU_SB), lambda b, j, c: (b, 0, j)),
            pl.BlockSpec((1, 8, LRU_SB), lambda b, j, c: (b, 0, j)),
            pl.BlockSpec((CONV_W, LRU_SB), lambda b, j, c: (0, j)),
            pl.BlockSpec((1, LRU_SB), lambda b, j, c: (0, j)),
            pl.BlockSpec((2, 1, LRU_SB, LRU_SB), lambda b, j, c: (0, j, 0, 0)),
            pl.BlockSpec((2, LRU_SB), lambda b, j, c: (0, j)),
            pl.BlockSpec((1, LRU_SB), lambda b, j, c: (0, j)),
        ],
        out_specs=[
            pl.BlockSpec((L, LRU_SB), lambda b, j, c: (b * n_chunks + c, j)),
            pl.BlockSpec((1, 1, LRU_SB), lambda b, j, c: (b, 0, j)),
        ],
        out_shape=[
            jax.ShapeDtypeStruct((n_seq * n_chunks * L, D_RNN), out_dtype),
            jax.ShapeDtypeStruct((n_seq, 1, D_RNN), F32),
        ],
        scratch_shapes=[pltpu.VMEM((1, LRU_SB), F32), pltpu.VMEM((8, LRU_SB), F32),
                        pltpu.VMEM((L + 8, LRU_SB), F32)],
        compiler_params=_cparams(3),
        name=name,
    )(proj, proj, h0, conv0, conv_w, conv_b.reshape(1, D_RNN), wg_sb, b_gate, lam.reshape(1, D_RNN))


def _router_kernel(x_ref, w_ref, b_ref, gate_ref, id_ref):
    logits = jnp.dot(x_ref[...], w_ref[...], preferred_element_type=F32) + b_ref[...]
    tm = logits.shape[0]
    lane = lax.broadcasted_iota(jnp.int32, (tm, 128), 1)
    lane_f = lane.astype(F32)
    is_group = (lane >= N_EXPERTS) & (lane < N_EXPERTS + N_EXP_GROUPS)
    gl = jnp.where(is_group, logits, -jnp.inf)
    g_max = jnp.max(gl, axis=1, keepdims=True)
    g_lane = jnp.min(jnp.where(gl == g_max, lane_f, 1e9), axis=1, keepdims=True)
    g_w = 1.0 / jnp.sum(jnp.exp(gl - g_max), axis=1, keepdims=True)
    g_idx = g_lane - float(N_EXPERTS)
    lo = g_idx * float(EXP_PER_GROUP)
    in_group = (lane_f >= lo) & (lane_f < lo + float(EXP_PER_GROUP))
    el = jnp.where(in_group, logits, -jnp.inf)
    e_max = jnp.max(el, axis=1, keepdims=True)
    i1 = jnp.min(jnp.where(el == e_max, lane_f, 1e9), axis=1, keepdims=True)
    e_sum = jnp.sum(jnp.exp(el - e_max), axis=1, keepdims=True)
    el2 = jnp.where(lane_f == i1, -jnp.inf, el)
    e2_max = jnp.max(el2, axis=1, keepdims=True)
    i2 = jnp.min(jnp.where(el2 == e2_max, lane_f, 1e9), axis=1, keepdims=True)
    p1 = 1.0 / e_sum
    p2 = jnp.exp(e2_max - e_max) / e_sum
    tot = p1 + p2
    gate1 = g_w * (p1 / tot)
    gate2 = g_w * (p2 / tot)
    gate_ref[...] = jnp.where(lane == 0, gate1, jnp.where(lane == 1, gate2, 0.0))
    id_ref[...] = jnp.where(lane == 0, i1, jnp.where(lane == 1, i2, 0.0)).astype(jnp.int32)


def _router(x, w_r, b_r, name):
    M, D = x.shape
    tm = 256
    return pl.pallas_call(
        _router_kernel,
        grid=(M // tm,),
        in_specs=[pl.BlockSpec((tm, D), lambda i: (i, 0)),
                  pl.BlockSpec((D, 128), lambda i: (0, 0)),
                  pl.BlockSpec((1, 128), lambda i: (0, 0))],
        out_specs=[pl.BlockSpec((tm, 128), lambda i: (i, 0)), pl.BlockSpec((tm, 128), lambda i: (i, 0))],
        out_shape=[jax.ShapeDtypeStruct((M, 128), F32), jax.ShapeDtypeStruct((M, 128), jnp.int32)],
        compiler_params=_cparams(1),
        name=name,
    )(x, w_r, b_r)


def _ffn_kernel(te_ref, nv_ref, idx_ref, x_hbm, w1_ref, w3_ref, w2_ref, o_ref,
                w1b, w3b, w2b, xb_s, xbuf, sems):
    i = pl.program_id(0)
    tm = TILE_ROWS
    nv = nv_ref[0]
    slot = i % 2

    @pl.when(i == 0)
    def _():
        _issue_gather(idx_ref, 0, x_hbm, xbuf, 0, sems.at[0], tm)

    @pl.when(i + 1 < nv)
    def _():
        _issue_gather(idx_ref, (i + 1) * tm, x_hbm, xbuf, (1 - slot) * tm, sems.at[1 - slot], tm)

    prev = te_ref[jnp.maximum(i - 1, 0)]

    @pl.when((i == 0) | (te_ref[i] != prev))
    def _():
        w1b[...] = w1_ref[0, 0].astype(BF16)
        w3b[...] = w3_ref[0, 0].astype(BF16)
        w2b[...] = w2_ref[0, 0].astype(BF16)

    @pl.when(i < nv)
    def _():
        _wait_gather(x_hbm, xbuf, sems.at[slot], tm)
        base = slot * (tm * SLAB)
        for j in range(SLAB):
            xb_s[:, j * 128:(j + 1) * 128] = xbuf[pl.ds(base + j, tm, stride=SLAB), :].astype(BF16)
        xb = xb_s[...]
        h1 = jnp.dot(xb, w1b[...], preferred_element_type=F32)
        h3 = jnp.dot(xb, w3b[...], preferred_element_type=F32)
        hid = (_silu(h1) * h3).astype(BF16)
        _slab_store(o_ref, jnp.dot(hid, w2b[...], preferred_element_type=F32))

    @pl.when(i >= nv)
    def _():
        o_ref[...] = jnp.zeros_like(o_ref)


def _expert_ffn(x_slab, row_src, tile_expert, n_valid, w1, w3, w2, layer, name):
    R = row_src.shape[0]
    D = D_MODEL
    tm = TILE_ROWS
    wmap = lambda i, te, nv, idx: (layer, te[i], 0, 0)
    return pl.pallas_call(
        _ffn_kernel,
        grid_spec=pltpu.PrefetchScalarGridSpec(
            num_scalar_prefetch=3,
            grid=(R // tm,),
            in_specs=[
                pl.BlockSpec(memory_space=pl.ANY),
                pl.BlockSpec((1, 1, D, D_EXPERT), wmap),
                pl.BlockSpec((1, 1, D, D_EXPERT), wmap),
                pl.BlockSpec((1, 1, D_EXPERT, D), wmap),
            ],
            out_specs=pl.BlockSpec((tm * SLAB, 128), lambda i, te, nv, idx: (i, 0)),
            scratch_shapes=[pltpu.VMEM((D, D_EXPERT), BF16), pltpu.VMEM((D, D_EXPERT), BF16),
                            pltpu.VMEM((D_EXPERT, D), BF16), pltpu.VMEM((tm, D), BF16),
                            pltpu.VMEM((2 * tm * SLAB, 128), F32), pltpu.SemaphoreType.DMA((2,))],
        ),
        out_shape=jax.ShapeDtypeStruct((R * SLAB, 128), F32),
        compiler_params=_cparams(1),
        name=name,
    )(tile_expert, n_valid, row_src, x_slab, w1, w3, w2)


def _route_plan(ids, n_tok):
    tm = TILE_ROWS
    n_pairs = TOP_K * n_tok
    n_rows = n_pairs + N_EXPERTS * tm
    experts = jnp.arange(N_EXPERTS, dtype=jnp.int32)
    e_flat = ids.T.reshape(-1)
    order = jnp.argsort(e_flat, stable=True).astype(jnp.int32)
    rank_sorted = jnp.argsort(order).astype(jnp.int32)
    sizes = jnp.sum((e_flat[:, None] == experts[None, :]).astype(jnp.int32), axis=0)
    start = jnp.cumsum(sizes) - sizes
    psz = ((sizes + tm - 1) // tm) * tm
    pend = jnp.cumsum(psz)
    pstart = pend - psz
    pos = pstart[e_flat] + rank_sorted - start[e_flat]
    rows = jnp.arange(n_rows, dtype=jnp.int32)
    e_row = jnp.minimum(jnp.sum((rows[:, None] >= pend[None, :]).astype(jnp.int32), axis=1), N_EXPERTS - 1)
    rank = rows - pstart[e_row]
    valid = (rank < sizes[e_row]) & (rows < pend[-1])
    pair = order[jnp.clip(start[e_row] + rank, 0, n_pairs - 1)]
    row_src = jnp.where(valid, pair % n_tok, 0)
    last_e = jnp.max(jnp.where(sizes > 0, experts, 0))
    te = jnp.where(rows[::tm] < pend[-1], e_row[::tm], last_e)
    n_valid = (pend[-1] // tm).astype(jnp.int32)
    return row_src, pos, te, n_valid.reshape(1)


def _hier_moe_ln(x_slab, xb, w_r, b_r, w1, w3, w2, layer, ln_g, ln_b, tag, split=None):
    n_tok = xb.shape[0]
    gates, ids = _router(xb, w_r.astype(BF16), b_r, name=f"router_{tag}")
    row_src, pos, te, n_valid = _route_plan(ids[:, :TOP_K], n_tok)
    ys = _expert_ffn(x_slab, row_src, te, n_valid, w1, w3, w2, layer, name=f"experts_{tag}")
    return _combine_ln(x_slab, ys, pos, gates, ln_g, ln_b, name=f"combine_ln_{tag}", split=split)


def _router_weights(w_group, b_group, w_expert, b_expert):
    w = jnp.concatenate([w_expert, w_group], axis=1)
    w = jnp.pad(w, ((0, 0), (0, 128 - w.shape[1])))
    b = jnp.pad(jnp.concatenate([b_expert, b_group]), (0, 128 - N_EXPERTS - N_EXP_GROUPS))
    return w, b.reshape(1, 128)


def _pad_conv_state(conv):
    return jnp.pad(conv, ((0, 0), (8 - (CONV_W - 1), 0), (0, 0)))


def kernel(x_prompt, x_sample, state_mlstm_C, state_mlstm_n, state_mlstm_m, state_ssd, state_ssd_conv,
           state_lru_h, state_lru_conv, meta_tokens, w_in_even, mlstm_gate_b, ssd_dt_bias, ssd_A_log, ssd_D,
           ssd_conv_w, ssd_conv_b, mlstm_norm_g, ssd_norm_g, w_out_even, w_in_odd, lru_conv_w, lru_conv_b,
           lru_w_gate, lru_b_gate, lru_lambda, w_out_odd, ln_g, ln_b, moe_w_group, moe_b_group,
           moe_w_expert, moe_b_expert, moe_w1, moe_w3, moe_w2):
    Bp, Tp, D = x_prompt.shape
    Bs, Ts, _ = x_sample.shape
    n_chunks_p = (N_META + Tp + CHUNK - 1) // CHUNK
    Tpp = n_chunks_p * CHUNK
    n_pad = Tpp - N_META - Tp
    n_p = Bp * Tpp
    n_s = Bs * Ts
    n_tok = n_p + n_s

    meta = jnp.broadcast_to(meta_tokens[None], (Bp, N_META, D))
    xp = jnp.concatenate([jnp.zeros((Bp, n_pad, D), F32), meta, x_prompt], axis=1).reshape(n_p, D)
    x0 = jnp.concatenate([xp, x_sample.reshape(n_s, D)], axis=0)
    x0b = x0.astype(BF16)

    grp = [dict(n_seq=Bp, n_chunks=n_chunks_p, L=CHUNK, row0=0, n_pad=n_pad),
           dict(n_seq=Bs, n_chunks=1, L=Ts, row0=n_p, n_pad=0)]

    def seq_view(a, gi):
        if gi == 0:
            return a[:n_p].reshape(Bp, Tpp, a.shape[1])
        return a[n_p:].reshape(Bs, Ts, a.shape[1])

    def tail_rows(a, gi, col0, ncol):
        nb, T, base = (Bp, Tpp, 0) if gi == 0 else (Bs, Ts, n_p)
        idx = (base + np.arange(nb)[:, None] * T + np.arange(T - (CONV_W - 1), T)[None, :]).reshape(-1)
        rows = jnp.take(a, jnp.asarray(idx, jnp.int32), axis=0)
        return rows[:, col0:col0 + ncol].reshape(nb, CONV_W - 1, ncol)

    e = 0
    w = w_in_even[e]
    src_small = 2 * A_HEADS * A_DK + 2 * A_INNER
    src_z = src_small + 2 * A_HEADS
    src_dt = src_z + B_INNER + B_CONV_DIM
    w_all = jnp.concatenate([w[:, :src_small], w[:, src_z:src_dt], w[:, src_small:src_z], w[:, src_dt:],
                             jnp.zeros((D, IN0_PAD - COL_SMALL - N_SMALL), F32)], axis=1).astype(BF16)
    proj = _mm(x0b, w_all, 512, 1280, name="in_proj_even")

    small = proj[:, COL_SMALL:COL_SMALL + N_SMALL]
    h_parts, y_parts, st = [], [], []
    for gi, g in enumerate(grp):
        sv = seq_view(small, gi)
        gates_t = jnp.swapaxes(sv, 1, 2)
        nb, T = sv.shape[0], sv.shape[1]
        gates_c = sv[:, :, 2 * A_HEADS:].reshape(nb, T, B_GROUPS, B_HPG).transpose(0, 2, 1, 3)
        if gi == 0:
            c0 = jnp.zeros((Bp, A_HEADS, A_DK, A_DV), F32)
            n0 = jnp.zeros((Bp, A_HEADS, 1, A_DK), F32)
            m0 = jnp.zeros((Bp, A_HEADS, 1, 128), F32)
            s0 = jnp.zeros((Bp, B_HEADS, B_P, B_N), F32)
            cv0 = jnp.zeros((Bp, 8, B_CONV_DIM), F32)
        else:
            c0 = state_mlstm_C[e]
            n0 = state_mlstm_n[e][:, :, None, :]
            m0 = jnp.broadcast_to(state_mlstm_m[e][:, :, None, None], (Bs, A_HEADS, 1, 128))
            s0 = state_ssd[e]
            cv0 = _pad_conv_state(state_ssd_conv[e])
        h_g, c_g, n_g, m_g = _mlstm(proj, gates_t, mlstm_gate_b[e], mlstm_norm_g[e], c0, n0, m0,
                                    name=f"mlstm_{gi}", **g)
        y_g, s_g = _ssd(proj, gates_t, gates_c, ssd_dt_bias[e], ssd_A_log[e], ssd_D[e], ssd_conv_w[e],
                        ssd_conv_b[e], ssd_norm_g[e], s0, cv0, name=f"ssd_{gi}", **g)
        h_parts.append(h_g.astype(BF16))
        y_parts.append(y_g.astype(BF16))
        sconv = tail_rows(proj, gi, COL_XBC, B_CONV_DIM)
        st.append((c_g[None], n_g[:, :, 0, :][None], m_g[:, :, 0, 0][None], s_g[None], sconv[None]))
    mix = _mm_parts([h_parts[0], y_parts[0]], [h_parts[1], y_parts[1]], w_out_even[e].astype(BF16), 512, 1024,
                    name="out_proj_even")
    x1s, x1b = _add_ln(x0, mix, ln_g[0, 0], ln_b[0, 0], name="add_ln_0")
    w_r, b_r = _router_weights(moe_w_group[0], moe_b_group[0], moe_w_expert[0], moe_b_expert[0])
    x2, x2b = _hier_moe_ln(x1s, x1b, w_r, b_r, moe_w1, moe_w3, moe_w2, 0, ln_g[0, 1], ln_b[0, 1], "0")

    o = 0
    proj1 = _mm(x2b, w_in_odd[o].astype(BF16), 512, 1280, name="in_proj_odd")
    wg = lru_w_gate[o].reshape(2, N_LRU_SB, LRU_SB // LRU_BW, LRU_BW, LRU_BW)
    eye4 = jnp.eye(LRU_SB // LRU_BW, dtype=F32)
    wg_sb = jnp.einsum('gjaik,ab->gjaibk', wg, eye4).reshape(2, N_LRU_SB, LRU_SB, LRU_SB).astype(BF16)
    o_parts, st1 = [], []
    for gi, g in enumerate(grp):
        if gi == 0:
            h0 = jnp.zeros((Bp, 1, D_RNN), F32)
            cv0 = jnp.zeros((Bp, 8, D_RNN), F32)
        else:
            h0 = state_lru_h[o][:, None, :]
            cv0 = _pad_conv_state(state_lru_conv[o])
        if g["L"] == 8 and g["n_chunks"] == 1:
            o_g, hs = _lru_rows8(proj1, lru_conv_w[o], lru_conv_b[o], wg_sb, lru_b_gate[o], lru_lambda[o], h0, cv0,
                                 n_seq=g["n_seq"], row0=g["row0"], name=f"lru_{gi}")
            hN = hs.reshape(g["n_seq"], 8, D_RNN)[:, 7, :]
        else:
            o_g, hN = _lru(proj1, lru_conv_w[o], lru_conv_b[o], wg_sb, lru_b_gate[o], lru_lambda[o], h0, cv0,
                           name=f"lru_{gi}", **g)
            hN = hN[:, 0, :]
        o_parts.append(o_g.astype(BF16))
        hconv = tail_rows(proj1, gi, D_RNN, D_RNN)
        st1.append((hN[None], hconv[None]))
    mix1 = _mm_parts([o_parts[0]], [o_parts[1]], w_out_odd[o].astype(BF16), 512, 1024, name="out_proj_odd")
    x3s, x3b = _add_ln(x2, mix1, ln_g[1, 0], ln_b[1, 0], name="add_ln_1")
    w_r, b_r = _router_weights(moe_w_group[1], moe_b_group[1], moe_w_expert[1], moe_b_expert[1])
    assert Tpp - Tp == CHUNK
    yp, ys = _hier_moe_ln(x3s, x3b, w_r, b_r, moe_w1, moe_w3, moe_w2, 1, ln_g[1, 1], ln_b[1, 1], "1",
                          split=(n_p, n_chunks_p))
    y_prompt = yp.reshape(Bp, Tp, D)
    y_sample = ys.reshape(Bs, Ts, D)
    (pC, pn, pm, pS, pSc), (sC, sn, sm, sS, sSc) = st
    (pH, pHc), (sH, sHc) = st1
    return (y_prompt, y_sample, pC, pn, pm, pS, pSc, pH, pHc, sC, sn, sm, sS, sSc, sH, sHc)
```

```python
import functools
import math

import jax
import jax.numpy as jnp
import numpy as np
from jax import lax
from jax.experimental import pallas as pl
from jax.experimental.pallas import tpu as pltpu

F32 = jnp.float32
BF16 = jnp.bfloat16

D_MODEL = 2048
N_META = 16
CHUNK = 128
CONV_W = 4
A_HEADS = 8
A_DK = 128
A_DV = 256
A_INNER = A_HEADS * A_DV
B_HEADS = 32
B_P = 64
B_N = 128
B_GROUPS = 4
B_HPG = B_HEADS // B_GROUPS
B_INNER = B_HEADS * B_P
B_GW = B_INNER // B_GROUPS
B_CONV_DIM = B_INNER + 2 * B_GROUPS * B_N
D_RNN = 2560
LRU_BLOCKS = 16
LRU_BW = D_RNN // LRU_BLOCKS
LRU_C = 8.0
LRU_SB = 640
N_LRU_SB = D_RNN // LRU_SB
N_EXP_GROUPS = 4
EXP_PER_GROUP = 8
N_EXPERTS = N_EXP_GROUPS * EXP_PER_GROUP
TOP_K = 2
D_EXPERT = 512
DEPTH = 2
ALPHA = (2.0 * DEPTH) ** 0.25

COL_Q = 0
COL_K = A_HEADS * A_DK
COL_V = 2 * A_HEADS * A_DK
COL_O = COL_V + A_INNER
COL_Z = COL_O + A_INNER
COL_XBC = COL_Z + B_INNER
COL_SMALL = COL_XBC + B_CONV_DIM
N_SMALL = 2 * A_HEADS + B_HEADS
IN0_PAD = 11520

NEG_BIG = -1e30
VMEM_LIMIT_BYTES = 48 * 1024 * 1024
TILE_ROWS = 256
FFN_AHEAD = 2
LN_ROWS = 256
SLAB = D_MODEL // 128


def _cparams(n_axes):
    return pltpu.CompilerParams(dimension_semantics=("arbitrary",) * n_axes,
                                vmem_limit_bytes=VMEM_LIMIT_BYTES)


def _smem_spec():
    return pl.BlockSpec(memory_space=pltpu.SMEM)


def _mm_kernel(x_ref, w_ref, o_ref):
    o_ref[...] = jnp.dot(x_ref[...].astype(BF16), w_ref[...], preferred_element_type=F32).astype(o_ref.dtype)


def _mm(x, w, bm, bn, name):
    M, K = x.shape
    N = w.shape[1]
    return pl.pallas_call(
        _mm_kernel,
        grid=(N // bn, M // bm),
        in_specs=[pl.BlockSpec((bm, K), lambda j, i: (i, 0)),
                  pl.BlockSpec((K, bn), lambda j, i: (0, j))],
        out_specs=pl.BlockSpec((bm, bn), lambda j, i: (i, j)),
        out_shape=jax.ShapeDtypeStruct((M, N), F32),
        compiler_params=_cparams(2),
        name=name,
    )(x, w)


def _mm_parts_kernel(*refs, k_sizes, n_blk_a):
    n = len(k_sizes)
    a_refs, b_refs, w_ref, o_ref = refs[:n], refs[n:2 * n], refs[2 * n], refs[2 * n + 1]
    i = pl.program_id(1)

    def run(lhs_refs):
        acc, k0 = None, 0
        for r, ks in zip(lhs_refs, k_sizes):
            part = jnp.dot(r[...], w_ref[k0:k0 + ks, :], preferred_element_type=F32)
            acc = part if acc is None else acc + part
            k0 += ks
        o_ref[...] = acc

    @pl.when(i < n_blk_a)
    def _():
        run(a_refs)

    @pl.when(i >= n_blk_a)
    def _():
        run(b_refs)


def _mm_parts(parts_a, parts_b, w, bm, bn, name):
    k_sizes = tuple(p.shape[1] for p in parts_a)
    n_blk_a = parts_a[0].shape[0] // bm
    n_blk_b = parts_b[0].shape[0] // bm
    K, N = w.shape
    a_specs = [pl.BlockSpec((bm, ks), lambda j, i: (jnp.minimum(i, n_blk_a - 1), 0)) for ks in k_sizes]
    b_specs = [pl.BlockSpec((bm, ks), lambda j, i: (jnp.maximum(i - n_blk_a, 0), 0)) for ks in k_sizes]
    return pl.pallas_call(
        functools.partial(_mm_parts_kernel, k_sizes=k_sizes, n_blk_a=n_blk_a),
        grid=(N // bn, n_blk_a + n_blk_b),
        in_specs=a_specs + b_specs + [pl.BlockSpec((K, bn), lambda j, i: (0, j))],
        out_specs=pl.BlockSpec((bm, bn), lambda j, i: (i, j)),
        out_shape=jax.ShapeDtypeStruct(((n_blk_a + n_blk_b) * bm, N), F32),
        compiler_params=_cparams(2),
        name=name,
    )(*parts_a, *parts_b, w)


def _layer_norm_rows(y, g, b):
    mu = jnp.mean(y, axis=-1, keepdims=True)
    yc = y - mu
    var = jnp.mean(yc * yc, axis=-1, keepdims=True)
    return yc * lax.rsqrt(var + 1e-5) * g + b


def _slab_store(ref, val):
    tm = val.shape[0]
    for j in range(SLAB):
        ref[pl.ds(j, tm, stride=SLAB), :] = val[:, j * 128:(j + 1) * 128]


def _slab_piece(ref, j, tm):
    return ref[pl.ds(j, tm, stride=SLAB), :]


def _add_ln_kernel(x_ref, m_ref, g_ref, b_ref, os_ref, ob_ref):
    y = _layer_norm_rows(ALPHA * x_ref[...] + m_ref[...], g_ref[...], b_ref[...])
    _slab_store(os_ref, y)
    ob_ref[...] = y.astype(BF16)


def _add_ln(x, mix, g, b, name):
    M, D = x.shape
    tm = LN_ROWS
    row = pl.BlockSpec((tm, D), lambda i: (i, 0))
    vec = pl.BlockSpec((1, D), lambda i: (0, 0))
    return pl.pallas_call(
        _add_ln_kernel,
        grid=(M // tm,),
        in_specs=[row, row, vec, vec],
        out_specs=[pl.BlockSpec((tm * SLAB, 128), lambda i: (i, 0)), row],
        out_shape=[jax.ShapeDtypeStruct((M * SLAB, 128), F32), jax.ShapeDtypeStruct((M, D), BF16)],
        compiler_params=_cparams(1),
        name=name,
    )(x, mix, g.reshape(1, D), b.reshape(1, D))


def _slab_copy(src_hbm, dst_vmem, sem, src_tok, dst_tok):
    return pltpu.make_async_copy(src_hbm.at[pl.ds(pl.multiple_of(src_tok * SLAB, SLAB), SLAB), :],
                                 dst_vmem.at[pl.ds(pl.multiple_of(dst_tok * SLAB, SLAB), SLAB), :], sem)


def _issue_gather(idx_ref, idx0, src_hbm, dst_vmem, dst0, sem, n):
    def body(r, carry):
        _slab_copy(src_hbm, dst_vmem, sem, idx_ref[idx0 + r], dst0 + r).start()
        return carry
    lax.fori_loop(0, n, body, 0, unroll=8)


def _wait_gather(src_hbm, dst_vmem, sem, n):
    def body(r, carry):
        _slab_copy(src_hbm, dst_vmem, sem, 0, 0).wait()
        return carry
    lax.fori_loop(0, n, body, 0, unroll=8)


def _combine_rows(pos_ref, x_ref, y_hbm, gate_ref, g_ref, b_ref, v_s, ybuf, sems, n_tok, tm):
    i = pl.program_id(0)
    n = pl.num_programs(0)
    slot = i % 2

    def fetch(tile, sl):
        for kk in range(TOP_K):
            _issue_gather(pos_ref, kk * n_tok + tile * tm, y_hbm, ybuf, (sl * TOP_K + kk) * tm, sems.at[sl], tm)

    @pl.when(i == 0)
    def _():
        fetch(0, 0)

    @pl.when(i + 1 < n)
    def _():
        fetch(i + 1, 1 - slot)

    _wait_gather(y_hbm, ybuf, sems.at[slot], TOP_K * tm)
    gate = gate_ref[...]
    g0 = gate[:, 0:1]
    g1 = gate[:, 1:2]
    base0 = slot * (TOP_K * tm * SLAB)
    base1 = base0 + tm * SLAB
    for j in range(SLAB):
        v_s[:, j * 128:(j + 1) * 128] = (ALPHA * _slab_piece(x_ref, j, tm)
                                         + g0 * ybuf[pl.ds(base0 + j, tm, stride=SLAB), :]
                                         + g1 * ybuf[pl.ds(base1 + j, tm, stride=SLAB), :])
    return _layer_norm_rows(v_s[...], g_ref[...], b_ref[...])


def _combine_ln_kernel(pos_ref, x_ref, y_hbm, gate_ref, g_ref, b_ref, o_ref, ob_ref, v_s, ybuf, sems, *, n_tok):
    y = _combine_rows(pos_ref, x_ref, y_hbm, gate_ref, g_ref, b_ref, v_s, ybuf, sems, n_tok, LN_ROWS)
    o_ref[...] = y
    ob_ref[...] = y.astype(BF16)


def _combine_out_kernel(pos_ref, x_ref, y_hbm, gate_ref, g_ref, b_ref, yp_ref, ys_ref, v_s, ybuf, sems, *,
                        n_tok, n_blk_p, blk_per_seq):
    y = _combine_rows(pos_ref, x_ref, y_hbm, gate_ref, g_ref, b_ref, v_s, ybuf, sems, n_tok, CHUNK)
    i = pl.program_id(0)

    @pl.when((i < n_blk_p) & (i % blk_per_seq > 0))
    def _():
        yp_ref[...] = y

    @pl.when(i >= n_blk_p)
    def _():
        ys_ref[...] = y


def _combine_ln(x_slab, y_slab, pos, gates, g, b, name, split=None):
    M = gates.shape[0]
    D = D_MODEL
    tm = LN_ROWS if split is None else CHUNK
    vec = pl.BlockSpec((1, D), lambda i, pos: (0, 0))
    if split is None:
        kern = functools.partial(_combine_ln_kernel, n_tok=M)
        row = pl.BlockSpec((tm, D), lambda i, pos: (i, 0))
        out_specs = [row, row]
        out_shape = [jax.ShapeDtypeStruct((M, D), F32), jax.ShapeDtypeStruct((M, D), BF16)]
    else:
        n_p, bps = split
        n_blk_p = n_p // tm
        kern = functools.partial(_combine_out_kernel, n_tok=M, n_blk_p=n_blk_p, blk_per_seq=bps)

        def yp_map(i, pos):
            ip = jnp.minimum(i, n_blk_p - 1)
            return ((ip // bps) * (bps - 1) + jnp.maximum(ip % bps - 1, 0), 0)

        out_specs = [pl.BlockSpec((tm, D), yp_map),
                     pl.BlockSpec((tm, D), lambda i, pos: (jnp.maximum(i - n_blk_p, 0), 0))]
        out_shape = [jax.ShapeDtypeStruct((n_blk_p // bps * (bps - 1) * tm, D), F32),
                     jax.ShapeDtypeStruct((M - n_p, D), F32)]
    return pl.pallas_call(
        kern,
        grid_spec=pltpu.PrefetchScalarGridSpec(
            num_scalar_prefetch=1,
            grid=(M // tm,),
            in_specs=[pl.BlockSpec((tm * SLAB, 128), lambda i, pos: (i, 0)),
                      pl.BlockSpec(memory_space=pl.ANY),
                      pl.BlockSpec((tm, 128), lambda i, pos: (i, 0)), vec, vec],
            out_specs=out_specs,
            scratch_shapes=[pltpu.VMEM((tm, D), F32), pltpu.VMEM((2 * TOP_K * tm * SLAB, 128), F32),
                            pltpu.SemaphoreType.DMA((2,))],
        ),
        out_shape=out_shape,
        compiler_params=_cparams(1),
        name=name,
    )(pos, x_slab, y_slab, gates, g.reshape(1, D), b.reshape(1, D))


def _softplus(x):
    return jnp.maximum(x, 0.0) + jnp.log1p(jnp.exp(-jnp.abs(x)))


def _sigmoid(x):
    return 1.0 / (1.0 + jnp.exp(-x))


def _silu(x):
    return x * _sigmoid(x)


def _dot_nn(m, x):
    return jnp.dot(m, x, preferred_element_type=F32)


def _dot_nt(a, b):
    return lax.dot_general(a, b, (((1,), (1,)), ((), ())), preferred_element_type=F32)


def _row_to_col(row, eye):
    return jnp.sum(jnp.where(eye, row, 0.0), axis=1, keepdims=True)


def _causal_conv(x, carry_ref, buf_ref, w_ref, b_ref, L):
    buf_ref[0:8, :] = carry_ref[...]
    buf_ref[8:8 + L, :] = x
    acc = b_ref[...] + w_ref[0:1, :] * buf_ref[5:5 + L, :]
    for j in range(1, CONV_W):
        acc = acc + w_ref[j:j + 1, :] * buf_ref[5 + j:5 + j + L, :]
    carry_ref[...] = buf_ref[L:L + 8, :]
    return acc


def _mlstm_kernel(gb_ref, q_ref, k_ref, v_ref, o_ref, gt_ref, c0_ref, n0_ref, m0_ref, ng_ref,
                  h_ref, c_out_ref, n_out_ref, m_out_ref, *state, L, n_pad):
    c_s, n_s, m_s = state[:A_HEADS], state[A_HEADS:2 * A_HEADS], state[2 * A_HEADS:]
    c = pl.program_id(1)

    @pl.when(c == 0)
    def _():
        for hd in range(A_HEADS):
            c_s[hd][...] = c0_ref[0, hd]
            n_s[hd][...] = n0_ref[0, hd]
            m_s[hd][...] = m0_ref[0, hd]

    rows = lax.broadcasted_iota(jnp.int32, (L, L), 0)
    cols = lax.broadcasted_iota(jnp.int32, (L, L), 1)
    eye = rows == cols
    causal = cols <= rows

    if n_pad:
        pad = (lax.broadcasted_iota(jnp.int32, (1, L), 1) < n_pad) & (c == 0)

    for hd in range(A_HEADS):
        li = gt_ref[0, hd:hd + 1, :] + gb_ref[0, hd]
        fr = gt_ref[0, A_HEADS + hd:A_HEADS + hd + 1, :] + gb_ref[1, hd]
        lf = jnp.minimum(fr, 0.0) - jnp.log1p(jnp.exp(-jnp.abs(fr)))
        if n_pad:
            li = jnp.where(pad, NEG_BIG, li)
            lf = jnp.where(pad, 0.0, lf)

        lf_col = _row_to_col(lf, eye)
        b_col = jnp.sum(jnp.where(causal, lf, 0.0), axis=1, keepdims=True)
        b_row = jnp.sum(jnp.where(rows <= cols, lf_col, 0.0), axis=0, keepdims=True)
        m_prev = m_s[hd][:, 0:1]

        d = jnp.where(causal, b_col - b_row + li, NEG_BIG)
        inter = b_col + m_prev
        m_t = jnp.maximum(inter, jnp.max(d, axis=1, keepdims=True))
        w_intra = jnp.exp(d - m_t)
        w_inter = jnp.exp(inter - m_t)

        q = q_ref[:, hd * A_DK:(hd + 1) * A_DK] * (A_DK ** -0.5)
        k = k_ref[:, hd * A_DK:(hd + 1) * A_DK]
        v = v_ref[:, hd * A_DV:(hd + 1) * A_DV]
        qb = q.astype(BF16)
        kb = k.astype(BF16)
        vb = v.astype(BF16)
        s = _dot_nt(qb, kb) * w_intra
        c_prev = c_s[hd][...]
        n_prev = n_s[hd][...]
        num = _dot_nn(s.astype(BF16), vb)
        num = num + w_inter * jnp.dot(qb, c_prev.astype(BF16), preferred_element_type=F32)
        qn = jnp.sum(qb.astype(F32) * n_prev.astype(BF16).astype(F32), axis=1, keepdims=True)
        nq = jnp.sum(s, axis=1, keepdims=True) + w_inter * qn
        hh = num / jnp.maximum(jnp.abs(nq), jnp.exp(-m_t))

        hn = hh * lax.rsqrt(jnp.mean(hh * hh, axis=-1, keepdims=True) + 1e-6)
        hn = hn * ng_ref[:, hd * A_DV:(hd + 1) * A_DV]
        h_ref[:, hd * A_DV:(hd + 1) * A_DV] = (
            hn * _sigmoid(o_ref[:, hd * A_DV:(hd + 1) * A_DV])).astype(h_ref.dtype)

        b_last = b_row[:, L - 1:L]
        ws_log = b_last - b_row + li
        m_new = jnp.maximum(b_last + m_prev, jnp.max(ws_log, axis=1, keepdims=True))
        ws_col = _row_to_col(jnp.exp(ws_log - m_new), eye)
        wc = jnp.exp(b_last + m_prev - m_new)
        kw = k * ws_col
        c_new = wc * c_prev + lax.dot_general(kw.astype(BF16), vb, (((0,), (0,)), ((), ())),
                                              preferred_element_type=F32)
        n_new = wc * n_prev + jnp.sum(ws_col.astype(BF16).astype(F32) * kb.astype(F32), axis=0, keepdims=True)
        m_new_b = jnp.broadcast_to(m_new, (1, 128))
        c_s[hd][...] = c_new
        n_s[hd][...] = n_new
        m_s[hd][...] = m_new_b
        c_out_ref[0, hd] = c_new
        n_out_ref[0, hd] = n_new
        m_out_ref[0, hd] = m_new_b


def _mlstm(proj, gates_t, gate_b, norm_g, c0, n0, m0, *, n_seq, n_chunks, L, row0, n_pad, name):
    rb0 = row0 // L

    def rblk(b, c):
        return rb0 + b * n_chunks + c

    out_dtype = BF16 if L % 16 == 0 else F32
    qk_w = A_HEADS * A_DK
    state_map = lambda b, c: (b, 0, 0, 0)
    kern = functools.partial(_mlstm_kernel, L=L, n_pad=n_pad)
    return pl.pallas_call(
        kern,
        grid=(n_seq, n_chunks),
        in_specs=[
            _smem_spec(),
            pl.BlockSpec((L, qk_w), lambda b, c: (rblk(b, c), COL_Q // qk_w)),
            pl.BlockSpec((L, qk_w), lambda b, c: (rblk(b, c), COL_K // qk_w)),
            pl.BlockSpec((L, A_INNER), lambda b, c: (rblk(b, c), COL_V // A_INNER)),
            pl.BlockSpec((L, A_INNER), lambda b, c: (rblk(b, c), COL_O // A_INNER)),
            pl.BlockSpec((1, N_SMALL, L), lambda b, c: (b, 0, c)),
            pl.BlockSpec((1, A_HEADS, A_DK, A_DV), state_map),
            pl.BlockSpec((1, A_HEADS, 1, A_DK), state_map),
            pl.BlockSpec((1, A_HEADS, 1, 128), state_map),
            pl.BlockSpec((1, A_INNER), lambda b, c: (0, 0)),
        ],
        out_specs=[
            pl.BlockSpec((L, A_INNER), lambda b, c: (b * n_chunks + c, 0)),
            pl.BlockSpec((1, A_HEADS, A_DK, A_DV), state_map),
            pl.BlockSpec((1, A_HEADS, 1, A_DK), state_map),
            pl.BlockSpec((1, A_HEADS, 1, 128), state_map),
        ],
        out_shape=[
            jax.ShapeDtypeStruct((n_seq * n_chunks * L, A_INNER), out_dtype),
            jax.ShapeDtypeStruct((n_seq, A_HEADS, A_DK, A_DV), F32),
            jax.ShapeDtypeStruct((n_seq, A_HEADS, 1, A_DK), F32),
            jax.ShapeDtypeStruct((n_seq, A_HEADS, 1, 128), F32),
        ],
        scratch_shapes=([pltpu.VMEM((A_DK, A_DV), F32)] * A_HEADS + [pltpu.VMEM((1, A_DK), F32)] * A_HEADS
                        + [pltpu.VMEM((1, 128), F32)] * A_HEADS),
        compiler_params=_cparams(2),
        name=name,
    )(gate_b, proj, proj, proj, proj, gates_t, c0, n0, m0, norm_g.reshape(1, A_INNER))


def _ssd_kernel(dtr_ref, dtc_ref, alr_ref, alc_ref, xs_ref, bm_ref, cm_ref, z_ref, gt_ref, gc_ref, s0_ref,
                cx0_ref, cb0_ref, cc0_ref, wx_ref, wb_ref, wc_ref, bx_ref, bb_ref, bc_ref, ng_ref, dsk_ref,
                y_ref, s_out_ref,
                cx_s, cb_s, cc_s, bufx, bufb, bufc, ybuf, *s_s, L, n_pad, gps):
    grp0 = pl.program_id(1) * gps
    c = pl.program_id(2)

    @pl.when(c == 0)
    def _():
        for pair in range(len(s_s)):
            s_s[pair][...] = s0_ref[0, pair]
        cx_s[...] = cx0_ref[0]
        cb_s[...] = cb0_ref[0]
        cc_s[...] = cc0_ref[0]

    rows = lax.broadcasted_iota(jnp.int32, (L, L), 0)
    cols = lax.broadcasted_iota(jnp.int32, (L, L), 1)
    causal = cols <= rows
    tril = causal.astype(F32)
    triu = (rows <= cols).astype(F32)
    lane_lo = lax.broadcasted_iota(jnp.int32, (L, 2 * B_P), 1) < B_P
    row_lo = lax.broadcasted_iota(jnp.int32, (2 * B_P, 1), 0) < B_P
    if n_pad:
        pad_row = (lax.broadcasted_iota(jnp.int32, (1, L), 1) < n_pad) & (c == 0)
        pad_col = (lax.broadcasted_iota(jnp.int32, (L, 1), 0) < n_pad) & (c == 0)

    xs_raw = xs_ref[...]
    bm_raw = bm_ref[...]
    cm_raw = cm_ref[...]
    if n_pad:
        xs_raw = jnp.where(pad_col, 0.0, xs_raw)
        bm_raw = jnp.where(pad_col, 0.0, bm_raw)
        cm_raw = jnp.where(pad_col, 0.0, cm_raw)
    xs = _silu(_causal_conv(xs_raw, cx_s, bufx, wx_ref, bx_ref, L))
    bm = _silu(_causal_conv(bm_raw, cb_s, bufb, wb_ref, bb_ref, L))
    cm = _silu(_causal_conv(cm_raw, cc_s, bufc, wc_ref, bc_ref, L))

    def two(col, h0):
        return jnp.where(lane_lo, col[:, h0:h0 + 1], col[:, h0 + 1:h0 + 2])

    for gg in range(gps):
        bmb = bm[:, gg * B_N:(gg + 1) * B_N].astype(BF16)
        cmb = cm[:, gg * B_N:(gg + 1) * B_N].astype(BF16)
        cb = lax.dot_general(cmb, bmb, (((1,), (1,)), ((), ())), preferred_element_type=F32)

        row0 = pl.multiple_of(2 * A_HEADS + (grp0 + gg) * B_HPG, B_HPG)
        dt_c = _softplus(gc_ref[0, gg] + dtr_ref[gg])
        dt_r = _softplus(gt_ref[0, pl.ds(row0, B_HPG), :] + dtc_ref[gg])
        if n_pad:
            dt_c = jnp.where(pad_col, 0.0, dt_c)
            dt_r = jnp.where(pad_row, 0.0, dt_r)
        a_c = dt_c * (-jnp.exp(alr_ref[gg]))
        a_r = dt_r * (-jnp.exp(alc_ref[gg]))
        b_c = jnp.dot(tril, a_c, preferred_element_type=F32, precision=lax.Precision.HIGHEST)
        b_r = jnp.dot(a_r, triu, preferred_element_type=F32, precision=lax.Precision.HIGHEST)
        eb_c = jnp.exp(b_c)
        b_last = b_c[L - 1:L, :]
        w_c = jnp.exp(b_last - b_c)
        e_last = jnp.exp(b_last)

        for p in range(B_HPG // 2):
            h0 = 2 * p
            pair = gg * (B_HPG // 2) + p
            x_pair = xs[:, pair * 2 * B_P:(pair + 1) * 2 * B_P]
            xdt = x_pair * two(dt_c, h0)
            y = None
            for hh, keep in ((h0, lane_lo), (h0 + 1, ~lane_lo)):
                decay = jnp.exp(jnp.where(causal, b_c[:, hh:hh + 1] - b_r[hh:hh + 1, :], NEG_BIG))
                part = _dot_nn((cb * decay).astype(BF16), jnp.where(keep, xdt, 0.0).astype(BF16))
                y = part if y is None else y + part
            s_prev = s_s[pair][...]
            y = y + two(eb_c, h0) * lax.dot_general(cmb, s_prev.astype(BF16), (((1,), (1,)), ((), ())),
                                                    preferred_element_type=F32)
            upd = lax.dot_general((xdt * two(w_c, h0)).astype(BF16), bmb, (((0,), (0,)), ((), ())),
                                  preferred_element_type=F32)
            s_new = jnp.where(row_lo, e_last[:, h0:h0 + 1], e_last[:, h0 + 1:h0 + 2]) * s_prev + upd
            s_s[pair][...] = s_new
            s_out_ref[0, pair] = s_new
            ybuf[:, pair * 2 * B_P:(pair + 1) * 2 * B_P] = y

    y = (ybuf[...] + dsk_ref[...] * xs) * _silu(z_ref[...])
    for gg in range(gps):
        yg = y[:, gg * B_GW:(gg + 1) * B_GW]
        yn = yg * lax.rsqrt(jnp.mean(yg * yg, axis=-1, keepdims=True) + 1e-6) * ng_ref[:, gg * B_GW:(gg + 1) * B_GW]
        y_ref[:, gg * B_GW:(gg + 1) * B_GW] = yn.astype(y_ref.dtype)


def _ssd(proj, gates_t, gates_c, dt_bias, a_log, d_skip, conv_w, conv_b, norm_g, s0, conv0, *,
         n_seq, n_chunks, L, row0, n_pad, name):
    rb0 = row0 // L

    def rblk(b, c):
        return rb0 + b * n_chunks + c

    out_dtype = BF16 if L % 16 == 0 else F32
    gps = B_GROUPS if n_chunks == 1 else 1
    xw, nw = gps * B_GW, gps * B_N
    n_pairs = gps * B_HPG // 2
    cw = conv_w
    cbias = conv_b.reshape(1, B_CONV_DIM)
    xoff = COL_XBC // xw
    boff = (COL_XBC + B_INNER) // nw
    coff = (COL_XBC + B_INNER + B_GROUPS * B_N) // nw
    cvb = B_INNER // nw
    cvc = (B_INNER + B_GROUPS * B_N) // nw
    dt_r = dt_bias.reshape(B_GROUPS, 1, B_HPG)
    dt_c = dt_bias.reshape(B_GROUPS, B_HPG, 1)
    al_r = a_log.reshape(B_GROUPS, 1, B_HPG)
    al_c = a_log.reshape(B_GROUPS, B_HPG, 1)
    d_row = jnp.repeat(d_skip, B_P).reshape(1, B_INNER)
    s0p = s0.reshape(n_seq, B_HEADS // 2, 2 * B_P, B_N)
    prow = pl.BlockSpec((gps, 1, B_HPG), lambda b, g, c: (g, 0, 0))
    pcol = pl.BlockSpec((gps, B_HPG, 1), lambda b, g, c: (g, 0, 0))
    kern = functools.partial(_ssd_kernel, L=L, n_pad=n_pad, gps=gps)
    y, s_new = pl.pallas_call(
        kern,
        grid=(n_seq, B_GROUPS // gps, n_chunks),
        in_specs=[
            prow, pcol, prow, pcol,
            pl.BlockSpec((L, xw), lambda b, g, c: (rblk(b, c), xoff + g)),
            pl.BlockSpec((L, nw), lambda b, g, c: (rblk(b, c), boff + g)),
            pl.BlockSpec((L, nw), lambda b, g, c: (rblk(b, c), coff + g)),
            pl.BlockSpec((L, xw), lambda b, g, c: (rblk(b, c), COL_Z // xw + g)),
            pl.BlockSpec((1, N_SMALL, L), lambda b, g, c: (b, 0, c)),
            pl.BlockSpec((1, gps, L, B_HPG), lambda b, g, c: (b, g, c, 0)),
            pl.BlockSpec((1, n_pairs, 2 * B_P, B_N), lambda b, g, c: (b, g, 0, 0)),
            pl.BlockSpec((1, 8, xw), lambda b, g, c: (b, 0, g)),
            pl.BlockSpec((1, 8, nw), lambda b, g, c: (b, 0, cvb + g)),
            pl.BlockSpec((1, 8, nw), lambda b, g, c: (b, 0, cvc + g)),
            pl.BlockSpec((CONV_W, xw), lambda b, g, c: (0, g)),
            pl.BlockSpec((CONV_W, nw), lambda b, g, c: (0, cvb + g)),
            pl.BlockSpec((CONV_W, nw), lambda b, g, c: (0, cvc + g)),
            pl.BlockSpec((1, xw), lambda b, g, c: (0, g)),
            pl.BlockSpec((1, nw), lambda b, g, c: (0, cvb + g)),
            pl.BlockSpec((1, nw), lambda b, g, c: (0, cvc + g)),
            pl.BlockSpec((1, xw), lambda b, g, c: (0, g)),
            pl.BlockSpec((1, xw), lambda b, g, c: (0, g)),
        ],
        out_specs=[
            pl.BlockSpec((L, xw), lambda b, g, c: (b * n_chunks + c, g)),
            pl.BlockSpec((1, n_pairs, 2 * B_P, B_N), lambda b, g, c: (b, g, 0, 0)),
        ],
        out_shape=[
            jax.ShapeDtypeStruct((n_seq * n_chunks * L, B_INNER), out_dtype),
            jax.ShapeDtypeStruct((n_seq, B_HEADS // 2, 2 * B_P, B_N), F32),
        ],
        scratch_shapes=[
            pltpu.VMEM((8, xw), F32), pltpu.VMEM((8, nw), F32), pltpu.VMEM((8, nw), F32),
            pltpu.VMEM((L + 8, xw), F32), pltpu.VMEM((L + 8, nw), F32), pltpu.VMEM((L + 8, nw), F32),
            pltpu.VMEM((L, xw), F32),
        ] + [pltpu.VMEM((2 * B_P, B_N), F32)] * n_pairs,
        compiler_params=_cparams(3),
        name=name,
    )(dt_r, dt_c, al_r, al_c, proj, proj, proj, proj, gates_t, gates_c, s0p, conv0, conv0, conv0,
      cw, cw, cw, cbias, cbias, cbias, norm_g.reshape(1, B_INNER), d_row)
    return y, s_new.reshape(n_seq, B_HEADS, B_P, B_N)


def _gelu_tanh(x):
    return 0.5 * x * (1.0 + jnp.tanh(math.sqrt(2.0 / math.pi) * (x + 0.044715 * (x * x * x))))


def _expm1_nonpos(z):
    e = jnp.exp(z)
    safe = (e < 1.0) & (z > -1.0)
    return jnp.where(safe, (e - 1.0) * z / jnp.log(jnp.where(safe, e, 0.5)), jnp.where(e == 1.0, z, e - 1.0))


def _lru_gates(xf, wg_ref, bg_ref, lam_ref):
    xfb = xf.astype(BF16)
    r = _sigmoid(jnp.dot(xfb, wg_ref[0, 0], preferred_element_type=F32) + bg_ref[0:1, :])
    i = _sigmoid(jnp.dot(xfb, wg_ref[1, 0], preferred_element_type=F32) + bg_ref[1:2, :])
    log_a = (-LRU_C) * r * _softplus(-lam_ref[...])
    a = jnp.exp(log_a)
    u = jnp.sqrt(-_expm1_nonpos(2.0 * log_a)) * (i * xf)
    return a, u


def _scan_rows8(a, u, t8):
    for k in (1, 2, 4):
        keep = t8 >= k
        a_sh = jnp.where(keep, pltpu.roll(a, k, 0), 1.0)
        u_sh = jnp.where(keep, pltpu.roll(u, k, 0), 0.0)
        u = a * u_sh + u
        a = a * a_sh
    return a, u


def _lru_kernel(gi_ref, xb_ref, h0_ref, cv0_ref, cw_ref, cb_ref, wg_ref, bg_ref, lam_ref,
                o_ref, h_out_ref, h_s, cv_s, buf, *, L, n_pad):
    c = pl.program_id(2)

    @pl.when(c == 0)
    def _():
        h_s[...] = h0_ref[0]
        cv_s[...] = cv0_ref[0]

    row_id = lax.broadcasted_iota(jnp.int32, (L, 1), 0)
    x_raw = xb_ref[...]
    if n_pad:
        pad_col = (row_id < n_pad) & (c == 0)
        x_raw = jnp.where(pad_col, 0.0, x_raw)
    xf = _causal_conv(x_raw, cv_s, buf, cw_ref, cb_ref, L)
    a, u = _lru_gates(xf, wg_ref, bg_ref, lam_ref)
    if n_pad:
        a = jnp.where(pad_col, 1.0, a)
        u = jnp.where(pad_col, 0.0, u)
    a, u = _scan_rows8(a, u, row_id % 8)
    h_in = h_s[...]
    pieces = []
    for t in range(L // 8):
        piece = a[t * 8:(t + 1) * 8, :] * h_in + u[t * 8:(t + 1) * 8, :]
        pieces.append(piece)
        h_in = piece[7:8, :]
    hs = jnp.concatenate(pieces, axis=0)
    h_s[...] = h_in
    h_out_ref[0] = h_in
    o_ref[...] = (hs * _gelu_tanh(gi_ref[...])).astype(o_ref.dtype)


def _lru_rows8_kernel(gi_ref, xb_ref, h0_ref, hist_ref, cw_ref, cb_ref, wg_ref, bg_ref, lam_ref,
                      o_ref, hs_ref):
    R = xb_ref.shape[0]
    t8 = lax.broadcasted_iota(jnp.int32, (R, 1), 0) % 8
    x = xb_ref[...]
    hist = hist_ref[...]
    acc = cb_ref[...] + cw_ref[CONV_W - 1:CONV_W, :] * x
    for j in range(CONV_W - 1):
        sh = CONV_W - 1 - j
        tap = jnp.where(t8 >= sh, pltpu.roll(x, sh, 0), pltpu.roll(hist, (sh - 8) % R, 0))
        acc = acc + cw_ref[j:j + 1, :] * tap
    a, u = _lru_gates(acc, wg_ref, bg_ref, lam_ref)
    a, u = _scan_rows8(a, u, t8)
    hs = a * h0_ref[...] + u
    hs_ref[...] = hs
    o_ref[...] = (hs * _gelu_tanh(gi_ref[...])).astype(o_ref.dtype)


def _lru_rows8(proj, conv_w, conv_b, wg_sb, b_gate, lam, h0, conv0, *, n_seq, row0, name):
    R = 128
    n_rows = n_seq * 8
    rb0 = row0 // R
    h0x = jnp.repeat(h0.reshape(n_seq, D_RNN), 8, axis=0)
    hist = conv0.reshape(n_rows, D_RNN)
    blk = lambda j, i: (i, j)
    return pl.pallas_call(
        _lru_rows8_kernel,
        grid=(N_LRU_SB, n_rows // R),
        in_specs=[
            pl.BlockSpec((R, LRU_SB), lambda j, i: (rb0 + i, j)),
            pl.BlockSpec((R, LRU_SB), lambda j, i: (rb0 + i, N_LRU_SB + j)),
            pl.BlockSpec((R, LRU_SB), blk),
            pl.BlockSpec((R, LRU_SB), blk),
            pl.BlockSpec((CONV_W, LRU_SB), lambda j, i: (0, j)),
            pl.BlockSpec((1, LRU_SB), lambda j, i: (0, j)),
            pl.BlockSpec((2, 1, LRU_SB, LRU_SB), lambda j, i: (0, j, 0, 0)),
            pl.BlockSpec((2, LRU_SB), lambda j, i: (0, j)),
            pl.BlockSpec((1, LRU_SB), lambda j, i: (0, j)),
        ],
        out_specs=[pl.BlockSpec((R, LRU_SB), blk), pl.BlockSpec((R, LRU_SB), blk)],
        out_shape=[jax.ShapeDtypeStruct((n_rows, D_RNN), BF16), jax.ShapeDtypeStruct((n_rows, D_RNN), F32)],
        compiler_params=_cparams(2),
        name=name,
    )(proj, proj, h0x, hist, conv_w, conv_b.reshape(1, D_RNN), wg_sb, b_gate, lam.reshape(1, D_RNN))


def _lru(proj, conv_w, conv_b, wg_sb, b_gate, lam, h0, conv0, *, n_seq, n_chunks, L, row0, n_pad, name):
    rb0 = row0 // L

    def rblk(b, c):
        return rb0 + b * n_chunks + c

    out_dtype = BF16 if L % 16 == 0 else F32
    kern = functools.partial(_lru_kernel, L=L, n_pad=n_pad)
    return pl.pallas_call(
        kern,
        grid=(n_seq, N_LRU_SB, n_chunks),
        in_specs=[
            pl.BlockSpec((L, LRU_SB), lambda b, j, c: (rblk(b, c), j)),
            pl.BlockSpec((L, LRU_SB), lambda b, j, c: (rblk(b, c), N_LRU_SB + j)),
            pl.BlockSpec((1, 1, LRU_SB), lambda b, j, c: (b, 0, j)),
            pl.BlockSpec((1, 8, LRU_SB), lambda b, j, c: (b, 0, j)),
            pl.BlockSpec((CONV_W, LRU_SB), lambda b, j, c: (0, j)),
            pl.BlockSpec((1, LRU_SB), lambda b, j, c: (0, j)),
            pl.BlockSpec((2, 1, LRU_SB, LRU_SB), lambda b, j, c: (0, j, 0, 0)),
            pl.BlockSpec((2, LRU_SB), lambda b, j, c: (0, j)),
            pl.BlockSpec((1, LRU_SB), lambda b, j, c: (0, j)),
        ],
        out_specs=[
            pl.BlockSpec((L, LRU_SB), lambda b, j, c: (b * n_chunks + c, j)),
            pl.BlockSpec((1, 1, LRU_SB), lambda b, j, c: (b, 0, j)),
        ],
        out_shape=[
            jax.ShapeDtypeStruct((n_seq * n_chunks * L, D_RNN), out_dtype),
            jax.ShapeDtypeStruct((n_seq, 1, D_RNN), F32),
        ],
        scratch_shapes=[pltpu.VMEM((1, LRU_SB), F32), pltpu.VMEM((8, LRU_SB), F32),
                        pltpu.VMEM((L + 8, LRU_SB), F32)],
        compiler_params=_cparams(3),
        name=name,
    )(proj, proj, h0, conv0, conv_w, conv_b.reshape(1, D_RNN), wg_sb, b_gate, lam.reshape(1, D_RNN))


def _router_kernel(x_ref, w_ref, b_ref, gate_ref, id_ref):
    logits = jnp.dot(x_ref[...], w_ref[...], preferred_element_type=F32) + b_ref[...]
    tm = logits.shape[0]
    lane = lax.broadcasted_iota(jnp.int32, (tm, 128), 1)
    lane_f = lane.astype(F32)
    is_group = (lane >= N_EXPERTS) & (lane < N_EXPERTS + N_EXP_GROUPS)
    gl = jnp.where(is_group, logits, -jnp.inf)
    g_max = jnp.max(gl, axis=1, keepdims=True)
    g_lane = jnp.min(jnp.where(gl == g_max, lane_f, 1e9), axis=1, keepdims=True)
    g_w = 1.0 / jnp.sum(jnp.exp(gl - g_max), axis=1, keepdims=True)
    g_idx = g_lane - float(N_EXPERTS)
    lo = g_idx * float(EXP_PER_GROUP)
    in_group = (lane_f >= lo) & (lane_f < lo + float(EXP_PER_GROUP))
    el = jnp.where(in_group, logits, -jnp.inf)
    e_max = jnp.max(el, axis=1, keepdims=True)
    i1 = jnp.min(jnp.where(el == e_max, lane_f, 1e9), axis=1, keepdims=True)
    e_sum = jnp.sum(jnp.exp(el - e_max), axis=1, keepdims=True)
    el2 = jnp.where(lane_f == i1, -jnp.inf, el)
    e2_max = jnp.max(el2, axis=1, keepdims=True)
    i2 = jnp.min(jnp.where(el2 == e2_max, lane_f, 1e9), axis=1, keepdims=True)
    p1 = 1.0 / e_sum
    p2 = jnp.exp(e2_max - e_max) / e_sum
    tot = p1 + p2
    gate1 = g_w * (p1 / tot)
    gate2 = g_w * (p2 / tot)
    gate_ref[...] = jnp.where(lane == 0, gate1, jnp.where(lane == 1, gate2, 0.0))
    id_ref[...] = jnp.where(lane == 0, i1, jnp.where(lane == 1, i2, 0.0)).astype(jnp.int32)


def _router(x, w_r, b_r, name):
    M, D = x.shape
    tm = 256
    return pl.pallas_call(
        _router_kernel,
        grid=(M // tm,),
        in_specs=[pl.BlockSpec((tm, D), lambda i: (i, 0)),
                  pl.BlockSpec((D, 128), lambda i: (0, 0)),
                  pl.BlockSpec((1, 128), lambda i: (0, 0))],
        out_specs=[pl.BlockSpec((tm, 128), lambda i: (i, 0)), pl.BlockSpec((tm, 128), lambda i: (i, 0))],
        out_shape=[jax.ShapeDtypeStruct((M, 128), F32), jax.ShapeDtypeStruct((M, 128), jnp.int32)],
        compiler_params=_cparams(1),
        name=name,
    )(x, w_r, b_r)


def _ffn_kernel(te_ref, nv_ref, idx_ref, x_hbm, w1_ref, w3_ref, w2_ref, o_ref,
                w1b, w3b, w2b, xb_s, xbuf, sems):
    i = pl.program_id(0)
    tm = TILE_ROWS
    nv = nv_ref[0]
    n_slots = FFN_AHEAD + 1
    slot = i % n_slots

    def fetch(tile):
        sl = tile % n_slots
        _issue_gather(idx_ref, tile * tm, x_hbm, xbuf, sl * tm, sems.at[sl], tm)

    @pl.when(i == 0)
    def _():
        for t in range(FFN_AHEAD):
            @pl.when(t < nv)
            def _():
                fetch(t)

    @pl.when(i + FFN_AHEAD < nv)
    def _():
        fetch(i + FFN_AHEAD)

    prev = te_ref[jnp.maximum(i - 1, 0)]

    @pl.when((i == 0) | (te_ref[i] != prev))
    def _():
        w1b[...] = w1_ref[0, 0].astype(BF16)
        w3b[...] = w3_ref[0, 0].astype(BF16)
        w2b[...] = w2_ref[0, 0].astype(BF16)

    @pl.when(i < nv)
    def _():
        _wait_gather(x_hbm, xbuf, sems.at[slot], tm)
        base = slot * (tm * SLAB)
        for j in range(SLAB):
            xb_s[:, j * 128:(j + 1) * 128] = xbuf[pl.ds(base + j, tm, stride=SLAB), :].astype(BF16)
        xb = xb_s[...]
        h1 = jnp.dot(xb, w1b[...], preferred_element_type=F32)
        h3 = jnp.dot(xb, w3b[...], preferred_element_type=F32)
        hid = (_silu(h1) * h3).astype(BF16)
        _slab_store(o_ref, jnp.dot(hid, w2b[...], preferred_element_type=F32))

    @pl.when(i >= nv)
    def _():
        o_ref[...] = jnp.zeros_like(o_ref)


def _expert_ffn(x_slab, row_src, tile_expert, n_valid, w1, w3, w2, layer, name):
    R = row_src.shape[0]
    D = D_MODEL
    tm = TILE_ROWS
    wmap = lambda i, te, nv, idx: (layer, te[i], 0, 0)
    return pl.pallas_call(
        _ffn_kernel,
        grid_spec=pltpu.PrefetchScalarGridSpec(
            num_scalar_prefetch=3,
            grid=(R // tm,),
            in_specs=[
                pl.BlockSpec(memory_space=pl.ANY),
                pl.BlockSpec((1, 1, D, D_EXPERT), wmap),
                pl.BlockSpec((1, 1, D, D_EXPERT), wmap),
                pl.BlockSpec((1, 1, D_EXPERT, D), wmap),
            ],
            out_specs=pl.BlockSpec((tm * SLAB, 128), lambda i, te, nv, idx: (i, 0)),
            scratch_shapes=[pltpu.VMEM((D, D_EXPERT), BF16), pltpu.VMEM((D, D_EXPERT), BF16),
                            pltpu.VMEM((D_EXPERT, D), BF16), pltpu.VMEM((tm, D), BF16),
                            pltpu.VMEM(((FFN_AHEAD + 1) * tm * SLAB, 128), F32),
                            pltpu.SemaphoreType.DMA((FFN_AHEAD + 1,))],
        ),
        out_shape=jax.ShapeDtypeStruct((R * SLAB, 128), F32),
        compiler_params=_cparams(1),
        name=name,
    )(tile_expert, n_valid, row_src, x_slab, w1, w3, w2)


def _route_plan(ids, n_tok):
    tm = TILE_ROWS
    n_pairs = TOP_K * n_tok
    n_rows = n_pairs + N_EXPERTS * tm
    experts = jnp.arange(N_EXPERTS, dtype=jnp.int32)
    e_flat = ids.T.reshape(-1)
    order = jnp.argsort(e_flat, stable=True).astype(jnp.int32)
    rank_sorted = jnp.argsort(order).astype(jnp.int32)
    sizes = jnp.sum((e_flat[:, None] == experts[None, :]).astype(jnp.int32), axis=0)
    start = jnp.cumsum(sizes) - sizes
    psz = ((sizes + tm - 1) // tm) * tm
    pend = jnp.cumsum(psz)
    pstart = pend - psz
    pos = pstart[e_flat] + rank_sorted - start[e_flat]
    rows = jnp.arange(n_rows, dtype=jnp.int32)
    tile_start = rows[::tm]
    e_tile = jnp.minimum(jnp.sum((tile_start[:, None] >= pend[None, :]).astype(jnp.int32), axis=1), N_EXPERTS - 1)
    e_row = jnp.repeat(e_tile, tm)
    rank = rows - pstart[e_row]
    valid = (rank < sizes[e_row]) & (rows < pend[-1])
    pair = order[jnp.clip(start[e_row] + rank, 0, n_pairs - 1)]
    row_src = jnp.where(valid, pair % n_tok, 0)
    last_e = jnp.max(jnp.where(sizes > 0, experts, 0))
    te = jnp.where(rows[::tm] < pend[-1], e_row[::tm], last_e)
    n_valid = (pend[-1] // tm).astype(jnp.int32)
    return row_src, pos, te, n_valid.reshape(1)


def _hier_moe_ln(x_slab, xb, w_r, b_r, w1, w3, w2, layer, ln_g, ln_b, tag, split=None):
    n_tok = xb.shape[0]
    gates, ids = _router(xb, w_r.astype(BF16), b_r, name=f"router_{tag}")
    row_src, pos, te, n_valid = _route_plan(ids[:, :TOP_K], n_tok)
    ys = _expert_ffn(x_slab, row_src, te, n_valid, w1, w3, w2, layer, name=f"experts_{tag}")
    return _combine_ln(x_slab, ys, pos, gates, ln_g, ln_b, name=f"combine_ln_{tag}", split=split)


def _router_weights(w_group, b_group, w_expert, b_expert):
    w = jnp.concatenate([w_expert, w_group], axis=1)
    w = jnp.pad(w, ((0, 0), (0, 128 - w.shape[1])))
    b = jnp.pad(jnp.concatenate([b_expert, b_group]), (0, 128 - N_EXPERTS - N_EXP_GROUPS))
    return w, b.reshape(1, 128)


def _pad_conv_state(conv):
    return jnp.pad(conv, ((0, 0), (8 - (CONV_W - 1), 0), (0, 0)))


def kernel(x_prompt, x_sample, state_mlstm_C, state_mlstm_n, state_mlstm_m, state_ssd, state_ssd_conv,
           state_lru_h, state_lru_conv, meta_tokens, w_in_even, mlstm_gate_b, ssd_dt_bias, ssd_A_log, ssd_D,
           ssd_conv_w, ssd_conv_b, mlstm_norm_g, ssd_norm_g, w_out_even, w_in_odd, lru_conv_w, lru_conv_b,
           lru_w_gate, lru_b_gate, lru_lambda, w_out_odd, ln_g, ln_b, moe_w_group, moe_b_group,
           moe_w_expert, moe_b_expert, moe_w1, moe_w3, moe_w2):
    Bp, Tp, D = x_prompt.shape
    Bs, Ts, _ = x_sample.shape
    n_chunks_p = (N_META + Tp + CHUNK - 1) // CHUNK
    Tpp = n_chunks_p * CHUNK
    n_pad = Tpp - N_META - Tp
    n_p = Bp * Tpp
    n_s = Bs * Ts
    n_tok = n_p + n_s

    pieces = []
    for b in range(Bp):
        pieces += [jnp.zeros((n_pad, D), F32), meta_tokens.astype(F32), x_prompt[b]]
    x0 = jnp.concatenate(pieces + [x_sample.reshape(n_s, D)], axis=0)

    grp = [dict(n_seq=Bp, n_chunks=n_chunks_p, L=CHUNK, row0=0, n_pad=n_pad),
           dict(n_seq=Bs, n_chunks=1, L=Ts, row0=n_p, n_pad=0)]

    def seq_view(a, gi):
        if gi == 0:
            return a[:n_p].reshape(Bp, Tpp, a.shape[1])
        return a[n_p:].reshape(Bs, Ts, a.shape[1])

    def tail_rows(a, gi, col0, ncol):
        nb, T, base = (Bp, Tpp, 0) if gi == 0 else (Bs, Ts, n_p)
        idx = (base + np.arange(nb)[:, None] * T + np.arange(T - (CONV_W - 1), T)[None, :]).reshape(-1)
        rows = jnp.take(a, jnp.asarray(idx, jnp.int32), axis=0)
        return rows[:, col0:col0 + ncol].reshape(nb, CONV_W - 1, ncol)

    e = 0
    w = w_in_even[e]
    src_small = 2 * A_HEADS * A_DK + 2 * A_INNER
    src_z = src_small + 2 * A_HEADS
    src_dt = src_z + B_INNER + B_CONV_DIM
    w_all = jnp.concatenate([w[:, :src_small], w[:, src_z:src_dt], w[:, src_small:src_z], w[:, src_dt:],
                             jnp.zeros((D, IN0_PAD - COL_SMALL - N_SMALL), F32)], axis=1).astype(BF16)
    proj = _mm(x0, w_all, 512, 1280, name="in_proj_even")

    small = proj[:, COL_SMALL:COL_SMALL + N_SMALL]
    h_parts, y_parts, st = [], [], []
    for gi, g in enumerate(grp):
        sv = seq_view(small, gi)
        gates_t = jnp.swapaxes(sv, 1, 2)
        nb, T = sv.shape[0], sv.shape[1]
        gates_c = sv[:, :, 2 * A_HEADS:].reshape(nb, T, B_GROUPS, B_HPG).transpose(0, 2, 1, 3)
        if gi == 0:
            c0 = jnp.zeros((Bp, A_HEADS, A_DK, A_DV), F32)
            n0 = jnp.zeros((Bp, A_HEADS, 1, A_DK), F32)
            m0 = jnp.zeros((Bp, A_HEADS, 1, 128), F32)
            s0 = jnp.zeros((Bp, B_HEADS, B_P, B_N), F32)
            cv0 = jnp.zeros((Bp, 8, B_CONV_DIM), F32)
        else:
            c0 = state_mlstm_C[e]
            n0 = state_mlstm_n[e][:, :, None, :]
            m0 = jnp.broadcast_to(state_mlstm_m[e][:, :, None, None], (Bs, A_HEADS, 1, 128))
            s0 = state_ssd[e]
            cv0 = _pad_conv_state(state_ssd_conv[e])
        h_g, c_g, n_g, m_g = _mlstm(proj, gates_t, mlstm_gate_b[e], mlstm_norm_g[e], c0, n0, m0,
                                    name=f"mlstm_{gi}", **g)
        y_g, s_g = _ssd(proj, gates_t, gates_c, ssd_dt_bias[e], ssd_A_log[e], ssd_D[e], ssd_conv_w[e],
                        ssd_conv_b[e], ssd_norm_g[e], s0, cv0, name=f"ssd_{gi}", **g)
        h_parts.append(h_g.astype(BF16))
        y_parts.append(y_g.astype(BF16))
        sconv = tail_rows(proj, gi, COL_XBC, B_CONV_DIM)
        st.append((c_g[None], n_g[:, :, 0, :][None], m_g[:, :, 0, 0][None], s_g[None], sconv[None]))
    mix = _mm_parts([h_parts[0], y_parts[0]], [h_parts[1], y_parts[1]], w_out_even[e].astype(BF16), 512, 1024,
                    name="out_proj_even")
    x1s, x1b = _add_ln(x0, mix, ln_g[0, 0], ln_b[0, 0], name="add_ln_0")
    w_r, b_r = _router_weights(moe_w_group[0], moe_b_group[0], moe_w_expert[0], moe_b_expert[0])
    x2, x2b = _hier_moe_ln(x1s, x1b, w_r, b_r, moe_w1, moe_w3, moe_w2, 0, ln_g[0, 1], ln_b[0, 1], "0")

    o = 0
    proj1 = _mm(x2b, w_in_odd[o].astype(BF16), 512, 1280, name="in_proj_odd")
    wg = lru_w_gate[o].reshape(2, N_LRU_SB, LRU_SB // LRU_BW, LRU_BW, LRU_BW)
    eye4 = jnp.eye(LRU_SB // LRU_BW, dtype=F32)
    wg_sb = jnp.einsum('gjaik,ab->gjaibk', wg, eye4).reshape(2, N_LRU_SB, LRU_SB, LRU_SB).astype(BF16)
    o_parts, st1 = [], []
    for gi, g in enumerate(grp):
        if gi == 0:
            h0 = jnp.zeros((Bp, 1, D_RNN), F32)
            cv0 = jnp.zeros((Bp, 8, D_RNN), F32)
        else:
            h0 = state_lru_h[o][:, None, :]
            cv0 = _pad_conv_state(state_lru_conv[o])
        if g["L"] == 8 and g["n_chunks"] == 1:
            o_g, hs = _lru_rows8(proj1, lru_conv_w[o], lru_conv_b[o], wg_sb, lru_b_gate[o], lru_lambda[o], h0, cv0,
                                 n_seq=g["n_seq"], row0=g["row0"], name=f"lru_{gi}")
            hN = hs.reshape(g["n_seq"], 8, D_RNN)[:, 7, :]
        else:
            o_g, hN = _lru(proj1, lru_conv_w[o], lru_conv_b[o], wg_sb, lru_b_gate[o], lru_lambda[o], h0, cv0,
                           name=f"lru_{gi}", **g)
            hN = hN[:, 0, :]
        o_parts.append(o_g.astype(BF16))
        hconv = tail_rows(proj1, gi, D_RNN, D_RNN)
        st1.append((hN[None], hconv[None]))
    mix1 = _mm_parts([o_parts[0]], [o_parts[1]], w_out_odd[o].astype(BF16), 512, 1024, name="out_proj_odd")
    x3s, x3b = _add_ln(x2, mix1, ln_g[1, 0], ln_b[1, 0], name="add_ln_1")
    w_r, b_r = _router_weights(moe_w_group[1], moe_b_group[1], moe_w_expert[1], moe_b_expert[1])
    assert Tpp - Tp == CHUNK
    yp, ys = _hier_moe_ln(x3s, x3b, w_r, b_r, moe_w1, moe_w3, moe_w2, 1, ln_g[1, 1], ln_b[1, 1], "1",
                          split=(n_p, n_chunks_p))
    y_prompt = yp.reshape(Bp, Tp, D)
    y_sample = ys.reshape(Bs, Ts, D)
    (pC, pn, pm, pS, pSc), (sC, sn, sm, sS, sSc) = st
    (pH, pHc), (sH, sHc) = st1
    return (y_prompt, y_sample, pC, pn, pm, pS, pSc, pH, pHc, sC, sn, sm, sS, sSc, sH, sHc)
```

```python
import functools
import math

import jax
import jax.numpy as jnp
import numpy as np
from jax import lax
from jax.experimental import pallas as pl
from jax.experimental.pallas import tpu as pltpu

F32 = jnp.float32
BF16 = jnp.bfloat16

D_MODEL = 2048
N_META = 16
CHUNK = 128
CONV_W = 4
A_HEADS = 8
A_DK = 128
A_DV = 256
A_INNER = A_HEADS * A_DV
B_HEADS = 32
B_P = 64
B_N = 128
B_GROUPS = 4
B_HPG = B_HEADS // B_GROUPS
B_INNER = B_HEADS * B_P
B_GW = B_INNER // B_GROUPS
B_CONV_DIM = B_INNER + 2 * B_GROUPS * B_N
D_RNN = 2560
LRU_BLOCKS = 16
LRU_BW = D_RNN // LRU_BLOCKS
LRU_C = 8.0
LRU_SB = 640
N_LRU_SB = D_RNN // LRU_SB
N_EXP_GROUPS = 4
EXP_PER_GROUP = 8
N_EXPERTS = N_EXP_GROUPS * EXP_PER_GROUP
TOP_K = 2
D_EXPERT = 512
DEPTH = 2
ALPHA = (2.0 * DEPTH) ** 0.25

COL_Q = 0
COL_K = A_HEADS * A_DK
COL_V = 2 * A_HEADS * A_DK
COL_O = COL_V + A_INNER
COL_Z = COL_O + A_INNER
COL_XBC = COL_Z + B_INNER
COL_SMALL = COL_XBC + B_CONV_DIM
N_SMALL = 2 * A_HEADS + B_HEADS
IN0_PAD = 11520

NEG_BIG = -1e30
VMEM_LIMIT_BYTES = 48 * 1024 * 1024
TILE_ROWS = 256
FFN_AHEAD = 2
LN_ROWS = 256
SLAB = D_MODEL // 128


def _cparams(n_axes):
    return pltpu.CompilerParams(dimension_semantics=("arbitrary",) * n_axes,
                                vmem_limit_bytes=VMEM_LIMIT_BYTES)


def _smem_spec():
    return pl.BlockSpec(memory_space=pltpu.SMEM)


def _mm_kernel(x_ref, w_ref, o_ref):
    o_ref[...] = jnp.dot(x_ref[...].astype(BF16), w_ref[...], preferred_element_type=F32).astype(o_ref.dtype)


def _mm(x, w, bm, bn, name):
    M, K = x.shape
    N = w.shape[1]
    return pl.pallas_call(
        _mm_kernel,
        grid=(N // bn, M // bm),
        in_specs=[pl.BlockSpec((bm, K), lambda j, i: (i, 0)),
                  pl.BlockSpec((K, bn), lambda j, i: (0, j))],
        out_specs=pl.BlockSpec((bm, bn), lambda j, i: (i, j)),
        out_shape=jax.ShapeDtypeStruct((M, N), F32),
        compiler_params=_cparams(2),
        name=name,
    )(x, w)


def _layer_norm_rows(y, g, b):
    mu = jnp.mean(y, axis=-1, keepdims=True)
    yc = y - mu
    var = jnp.mean(yc * yc, axis=-1, keepdims=True)
    return yc * lax.rsqrt(var + 1e-5) * g + b


def _mm_parts_ln_kernel(*refs, k_sizes, n_blk_a):
    n = len(k_sizes)
    a_refs, b_refs = refs[:n], refs[n:2 * n]
    w_ref, x_ref, g_ref, b_ref, os_ref, ob_ref = refs[2 * n:]
    i = pl.program_id(0)

    def run(lhs_refs):
        acc, k0 = None, 0
        for r, ks in zip(lhs_refs, k_sizes):
            part = jnp.dot(r[...], w_ref[k0:k0 + ks, :], preferred_element_type=F32)
            acc = part if acc is None else acc + part
            k0 += ks
        y = _layer_norm_rows(ALPHA * x_ref[...] + acc, g_ref[...], b_ref[...])
        _slab_store(os_ref, y)
        ob_ref[...] = y.astype(BF16)

    @pl.when(i < n_blk_a)
    def _():
        run(a_refs)

    @pl.when(i >= n_blk_a)
    def _():
        run(b_refs)


def _mm_parts_ln(parts_a, parts_b, w, x, g, b, name):
    bm = LN_ROWS
    k_sizes = tuple(p.shape[1] for p in parts_a)
    n_blk_a = parts_a[0].shape[0] // bm
    n_blk_b = parts_b[0].shape[0] // bm
    K, D = w.shape
    M = (n_blk_a + n_blk_b) * bm
    a_specs = [pl.BlockSpec((bm, ks), lambda i: (jnp.minimum(i, n_blk_a - 1), 0)) for ks in k_sizes]
    b_specs = [pl.BlockSpec((bm, ks), lambda i: (jnp.maximum(i - n_blk_a, 0), 0)) for ks in k_sizes]
    row = pl.BlockSpec((bm, D), lambda i: (i, 0))
    vec = pl.BlockSpec((1, D), lambda i: (0, 0))
    w_spec = pl.BlockSpec((K, D), lambda i: (0, 0), pipeline_mode=pl.Buffered(1))
    return pl.pallas_call(
        functools.partial(_mm_parts_ln_kernel, k_sizes=k_sizes, n_blk_a=n_blk_a),
        grid=(n_blk_a + n_blk_b,),
        in_specs=a_specs + b_specs + [w_spec, row, vec, vec],
        out_specs=[pl.BlockSpec((bm * SLAB, 128), lambda i: (i, 0)), row],
        out_shape=[jax.ShapeDtypeStruct((M * SLAB, 128), F32), jax.ShapeDtypeStruct((M, D), BF16)],
        compiler_params=_cparams(1),
        name=name,
    )(*parts_a, *parts_b, w, x, g.reshape(1, D), b.reshape(1, D))


def _slab_store(ref, val):
    tm = val.shape[0]
    for j in range(SLAB):
        ref[pl.ds(j, tm, stride=SLAB), :] = val[:, j * 128:(j + 1) * 128]


def _slab_piece(ref, j, tm):
    return ref[pl.ds(j, tm, stride=SLAB), :]


def _slab_copy(src_hbm, dst_vmem, sem, src_tok, dst_tok):
    return pltpu.make_async_copy(src_hbm.at[pl.ds(pl.multiple_of(src_tok * SLAB, SLAB), SLAB), :],
                                 dst_vmem.at[pl.ds(pl.multiple_of(dst_tok * SLAB, SLAB), SLAB), :], sem)


def _issue_gather(idx_ref, idx0, src_hbm, dst_vmem, dst0, sem, n):
    def body(r, carry):
        _slab_copy(src_hbm, dst_vmem, sem, idx_ref[idx0 + r], dst0 + r).start()
        return carry
    lax.fori_loop(0, n, body, 0, unroll=8)


def _wait_gather(src_hbm, dst_vmem, sem, n):
    def body(r, carry):
        _slab_copy(src_hbm, dst_vmem, sem, 0, 0).wait()
        return carry
    lax.fori_loop(0, n, body, 0, unroll=8)


def _combine_rows(pos_ref, x_ref, y_hbm, gate_ref, g_ref, b_ref, v_s, ybuf, sems, n_tok, tm):
    i = pl.program_id(0)
    n = pl.num_programs(0)
    slot = i % 2

    def fetch(tile, sl):
        for kk in range(TOP_K):
            _issue_gather(pos_ref, kk * n_tok + tile * tm, y_hbm, ybuf, (sl * TOP_K + kk) * tm, sems.at[sl], tm)

    @pl.when(i == 0)
    def _():
        fetch(0, 0)

    @pl.when(i + 1 < n)
    def _():
        fetch(i + 1, 1 - slot)

    _wait_gather(y_hbm, ybuf, sems.at[slot], TOP_K * tm)
    gate = gate_ref[...]
    g0 = gate[:, 0:1]
    g1 = gate[:, 1:2]
    base0 = slot * (TOP_K * tm * SLAB)
    base1 = base0 + tm * SLAB
    for j in range(SLAB):
        v_s[:, j * 128:(j + 1) * 128] = (ALPHA * _slab_piece(x_ref, j, tm)
                                         + g0 * ybuf[pl.ds(base0 + j, tm, stride=SLAB), :]
                                         + g1 * ybuf[pl.ds(base1 + j, tm, stride=SLAB), :])
    return _layer_norm_rows(v_s[...], g_ref[...], b_ref[...])


def _combine_ln_kernel(pos_ref, x_ref, y_hbm, gate_ref, g_ref, b_ref, o_ref, ob_ref, v_s, ybuf, sems, *, n_tok):
    y = _combine_rows(pos_ref, x_ref, y_hbm, gate_ref, g_ref, b_ref, v_s, ybuf, sems, n_tok, LN_ROWS)
    o_ref[...] = y
    ob_ref[...] = y.astype(BF16)


def _combine_out_kernel(pos_ref, x_ref, y_hbm, gate_ref, g_ref, b_ref, yp_ref, ys_ref, v_s, ybuf, sems, *,
                        n_tok, n_blk_p, blk_per_seq):
    y = _combine_rows(pos_ref, x_ref, y_hbm, gate_ref, g_ref, b_ref, v_s, ybuf, sems, n_tok, CHUNK)
    i = pl.program_id(0)

    @pl.when((i < n_blk_p) & (i % blk_per_seq > 0))
    def _():
        yp_ref[...] = y

    @pl.when(i >= n_blk_p)
    def _():
        ys_ref[...] = y


def _combine_ln(x_slab, y_slab, pos, gates, g, b, name, split=None):
    M = gates.shape[0]
    D = D_MODEL
    tm = LN_ROWS if split is None else CHUNK
    vec = pl.BlockSpec((1, D), lambda i, pos: (0, 0))
    if split is None:
        kern = functools.partial(_combine_ln_kernel, n_tok=M)
        row = pl.BlockSpec((tm, D), lambda i, pos: (i, 0))
        out_specs = [row, row]
        out_shape = [jax.ShapeDtypeStruct((M, D), F32), jax.ShapeDtypeStruct((M, D), BF16)]
    else:
        n_p, bps = split
        n_blk_p = n_p // tm
        kern = functools.partial(_combine_out_kernel, n_tok=M, n_blk_p=n_blk_p, blk_per_seq=bps)

        def yp_map(i, pos):
            ip = jnp.minimum(i, n_blk_p - 1)
            return ((ip // bps) * (bps - 1) + jnp.maximum(ip % bps - 1, 0), 0)

        out_specs = [pl.BlockSpec((tm, D), yp_map),
                     pl.BlockSpec((tm, D), lambda i, pos: (jnp.maximum(i - n_blk_p, 0), 0))]
        out_shape = [jax.ShapeDtypeStruct((n_blk_p // bps * (bps - 1) * tm, D), F32),
                     jax.ShapeDtypeStruct((M - n_p, D), F32)]
    return pl.pallas_call(
        kern,
        grid_spec=pltpu.PrefetchScalarGridSpec(
            num_scalar_prefetch=1,
            grid=(M // tm,),
            in_specs=[pl.BlockSpec((tm * SLAB, 128), lambda i, pos: (i, 0)),
                      pl.BlockSpec(memory_space=pl.ANY),
                      pl.BlockSpec((tm, 128), lambda i, pos: (i, 0)), vec, vec],
            out_specs=out_specs,
            scratch_shapes=[pltpu.VMEM((tm, D), F32), pltpu.VMEM((2 * TOP_K * tm * SLAB, 128), F32),
                            pltpu.SemaphoreType.DMA((2,))],
        ),
        out_shape=out_shape,
        compiler_params=_cparams(1),
        name=name,
    )(pos, x_slab, y_slab, gates, g.reshape(1, D), b.reshape(1, D))


def _softplus(x):
    return jnp.maximum(x, 0.0) + jnp.log1p(jnp.exp(-jnp.abs(x)))


def _sigmoid(x):
    return 1.0 / (1.0 + jnp.exp(-x))


def _silu(x):
    return x * _sigmoid(x)


def _dot_nn(m, x):
    return jnp.dot(m, x, preferred_element_type=F32)


def _dot_nt(a, b):
    return lax.dot_general(a, b, (((1,), (1,)), ((), ())), preferred_element_type=F32)


def _row_to_col(row, eye):
    return jnp.sum(jnp.where(eye, row, 0.0), axis=1, keepdims=True)


def _causal_conv(x, carry_ref, buf_ref, w_ref, b_ref, L):
    buf_ref[0:8, :] = carry_ref[...]
    buf_ref[8:8 + L, :] = x
    acc = b_ref[...] + w_ref[0:1, :] * buf_ref[5:5 + L, :]
    for j in range(1, CONV_W):
        acc = acc + w_ref[j:j + 1, :] * buf_ref[5 + j:5 + j + L, :]
    carry_ref[...] = buf_ref[L:L + 8, :]
    return acc


def _mlstm_kernel(gb_ref, q_ref, k_ref, v_ref, o_ref, gt_ref, c0_ref, n0_ref, m0_ref, ng_ref,
                  h_ref, c_out_ref, n_out_ref, m_out_ref, *state, L, n_pad):
    c_s, n_s, m_s = state[:A_HEADS], state[A_HEADS:2 * A_HEADS], state[2 * A_HEADS:]
    c = pl.program_id(1)

    @pl.when(c == 0)
    def _():
        for hd in range(A_HEADS):
            c_s[hd][...] = c0_ref[0, hd]
            n_s[hd][...] = n0_ref[0, hd]
            m_s[hd][...] = m0_ref[0, hd]

    rows = lax.broadcasted_iota(jnp.int32, (L, L), 0)
    cols = lax.broadcasted_iota(jnp.int32, (L, L), 1)
    eye = rows == cols
    causal = cols <= rows

    if n_pad:
        pad = (lax.broadcasted_iota(jnp.int32, (1, L), 1) < n_pad) & (c == 0)

    for hd in range(A_HEADS):
        li = gt_ref[0, hd:hd + 1, :] + gb_ref[0, hd]
        fr = gt_ref[0, A_HEADS + hd:A_HEADS + hd + 1, :] + gb_ref[1, hd]
        lf = jnp.minimum(fr, 0.0) - jnp.log1p(jnp.exp(-jnp.abs(fr)))
        if n_pad:
            li = jnp.where(pad, NEG_BIG, li)
            lf = jnp.where(pad, 0.0, lf)

        lf_col = _row_to_col(lf, eye)
        b_col = jnp.sum(jnp.where(causal, lf, 0.0), axis=1, keepdims=True)
        b_row = jnp.sum(jnp.where(rows <= cols, lf_col, 0.0), axis=0, keepdims=True)
        m_prev = m_s[hd][:, 0:1]

        d = jnp.where(causal, b_col - b_row + li, NEG_BIG)
        inter = b_col + m_prev
        m_t = jnp.maximum(inter, jnp.max(d, axis=1, keepdims=True))
        w_intra = jnp.exp(d - m_t)
        w_inter = jnp.exp(inter - m_t)

        q = q_ref[:, hd * A_DK:(hd + 1) * A_DK] * (A_DK ** -0.5)
        k = k_ref[:, hd * A_DK:(hd + 1) * A_DK]
        v = v_ref[:, hd * A_DV:(hd + 1) * A_DV]
        qb = q.astype(BF16)
        kb = k.astype(BF16)
        vb = v.astype(BF16)
        s = _dot_nt(qb, kb) * w_intra
        c_prev = c_s[hd][...]
        n_prev = n_s[hd][...]
        num = _dot_nn(s.astype(BF16), vb)
        num = num + w_inter * jnp.dot(qb, c_prev.astype(BF16), preferred_element_type=F32)
        qn = jnp.sum(qb.astype(F32) * n_prev.astype(BF16).astype(F32), axis=1, keepdims=True)
        nq = jnp.sum(s, axis=1, keepdims=True) + w_inter * qn
        hh = num / jnp.maximum(jnp.abs(nq), jnp.exp(-m_t))

        hn = hh * lax.rsqrt(jnp.mean(hh * hh, axis=-1, keepdims=True) + 1e-6)
        hn = hn * ng_ref[:, hd * A_DV:(hd + 1) * A_DV]
        h_ref[:, hd * A_DV:(hd + 1) * A_DV] = (
            hn * _sigmoid(o_ref[:, hd * A_DV:(hd + 1) * A_DV])).astype(h_ref.dtype)

        b_last = b_row[:, L - 1:L]
        ws_log = b_last - b_row + li
        m_new = jnp.maximum(b_last + m_prev, jnp.max(ws_log, axis=1, keepdims=True))
        ws_col = _row_to_col(jnp.exp(ws_log - m_new), eye)
        wc = jnp.exp(b_last + m_prev - m_new)
        kw = k * ws_col
        c_new = wc * c_prev + lax.dot_general(kw.astype(BF16), vb, (((0,), (0,)), ((), ())),
                                              preferred_element_type=F32)
        n_new = wc * n_prev + jnp.sum(ws_col.astype(BF16).astype(F32) * kb.astype(F32), axis=0, keepdims=True)
        m_new_b = jnp.broadcast_to(m_new, (1, 128))
        c_s[hd][...] = c_new
        n_s[hd][...] = n_new
        m_s[hd][...] = m_new_b
        c_out_ref[0, hd] = c_new
        n_out_ref[0, hd] = n_new
        m_out_ref[0, hd] = m_new_b


def _mlstm(proj, gates_t, gate_b, norm_g, c0, n0, m0, *, n_seq, n_chunks, L, row0, n_pad, name):
    rb0 = row0 // L

    def rblk(b, c):
        return rb0 + b * n_chunks + c

    out_dtype = BF16 if L % 16 == 0 else F32
    qk_w = A_HEADS * A_DK
    state_map = lambda b, c: (b, 0, 0, 0)
    m0b = jnp.broadcast_to(m0[:, :, None, None], (n_seq, A_HEADS, 1, 128))
    kern = functools.partial(_mlstm_kernel, L=L, n_pad=n_pad)
    h, c_new, n_new, m_new = pl.pallas_call(
        kern,
        grid=(n_seq, n_chunks),
        in_specs=[
            _smem_spec(),
            pl.BlockSpec((L, qk_w), lambda b, c: (rblk(b, c), COL_Q // qk_w)),
            pl.BlockSpec((L, qk_w), lambda b, c: (rblk(b, c), COL_K // qk_w)),
            pl.BlockSpec((L, A_INNER), lambda b, c: (rblk(b, c), COL_V // A_INNER)),
            pl.BlockSpec((L, A_INNER), lambda b, c: (rblk(b, c), COL_O // A_INNER)),
            pl.BlockSpec((1, N_SMALL, L), lambda b, c: (b, 0, c)),
            pl.BlockSpec((1, A_HEADS, A_DK, A_DV), state_map),
            pl.BlockSpec((1, A_HEADS, 1, A_DK), state_map),
            pl.BlockSpec((1, A_HEADS, 1, 128), state_map),
            pl.BlockSpec((1, A_INNER), lambda b, c: (0, 0)),
        ],
        out_specs=[
            pl.BlockSpec((L, A_INNER), lambda b, c: (b * n_chunks + c, 0)),
            pl.BlockSpec((1, A_HEADS, A_DK, A_DV), state_map),
            pl.BlockSpec((1, A_HEADS, 1, A_DK), state_map),
            pl.BlockSpec((1, A_HEADS, 1, 128), state_map),
        ],
        out_shape=[
            jax.ShapeDtypeStruct((n_seq * n_chunks * L, A_INNER), out_dtype),
            jax.ShapeDtypeStruct((n_seq, A_HEADS, A_DK, A_DV), F32),
            jax.ShapeDtypeStruct((n_seq, A_HEADS, 1, A_DK), F32),
            jax.ShapeDtypeStruct((n_seq, A_HEADS, 1, 128), F32),
        ],
        scratch_shapes=([pltpu.VMEM((A_DK, A_DV), F32)] * A_HEADS + [pltpu.VMEM((1, A_DK), F32)] * A_HEADS
                        + [pltpu.VMEM((1, 128), F32)] * A_HEADS),
        compiler_params=_cparams(2),
        name=name,
    )(gate_b, proj, proj, proj, proj, gates_t, c0, n0, m0b, norm_g.reshape(1, A_INNER))
    return h, c_new, n_new, m_new[:, :, 0, 0]


def _ssd_kernel(dtr_ref, dtc_ref, alr_ref, alc_ref, xs_ref, bm_ref, cm_ref, z_ref, gt_ref, gc_ref, s0_ref,
                cx0_ref, cb0_ref, cc0_ref, wx_ref, wb_ref, wc_ref, bx_ref, bb_ref, bc_ref, ng_ref, dsk_ref,
                y_ref, s_out_ref,
                cx_s, cb_s, cc_s, bufx, bufb, bufc, ybuf, *s_s, L, n_pad, gps):
    grp0 = pl.program_id(1) * gps
    c = pl.program_id(2)

    @pl.when(c == 0)
    def _():
        for pair in range(len(s_s)):
            s_s[pair][...] = s0_ref[0, pair]
        cx_s[...] = cx0_ref[0]
        cb_s[...] = cb0_ref[0]
        cc_s[...] = cc0_ref[0]

    rows = lax.broadcasted_iota(jnp.int32, (L, L), 0)
    cols = lax.broadcasted_iota(jnp.int32, (L, L), 1)
    causal = cols <= rows
    tril = causal.astype(F32)
    triu = (rows <= cols).astype(F32)
    lane_lo = lax.broadcasted_iota(jnp.int32, (L, 2 * B_P), 1) < B_P
    row_lo = lax.broadcasted_iota(jnp.int32, (2 * B_P, 1), 0) < B_P
    if n_pad:
        pad_row = (lax.broadcasted_iota(jnp.int32, (1, L), 1) < n_pad) & (c == 0)
        pad_col = (lax.broadcasted_iota(jnp.int32, (L, 1), 0) < n_pad) & (c == 0)

    xs_raw = xs_ref[...]
    bm_raw = bm_ref[...]
    cm_raw = cm_ref[...]
    if n_pad:
        xs_raw = jnp.where(pad_col, 0.0, xs_raw)
        bm_raw = jnp.where(pad_col, 0.0, bm_raw)
        cm_raw = jnp.where(pad_col, 0.0, cm_raw)
    xs = _silu(_causal_conv(xs_raw, cx_s, bufx, wx_ref, bx_ref, L))
    bm = _silu(_causal_conv(bm_raw, cb_s, bufb, wb_ref, bb_ref, L))
    cm = _silu(_causal_conv(cm_raw, cc_s, bufc, wc_ref, bc_ref, L))

    def two(col, h0):
        return jnp.where(lane_lo, col[:, h0:h0 + 1], col[:, h0 + 1:h0 + 2])

    for gg in range(gps):
        bmb = bm[:, gg * B_N:(gg + 1) * B_N].astype(BF16)
        cmb = cm[:, gg * B_N:(gg + 1) * B_N].astype(BF16)
        cb = lax.dot_general(cmb, bmb, (((1,), (1,)), ((), ())), preferred_element_type=F32)

        row0 = pl.multiple_of(2 * A_HEADS + (grp0 + gg) * B_HPG, B_HPG)
        dt_c = _softplus(gc_ref[0, gg] + dtr_ref[gg])
        dt_r = _softplus(gt_ref[0, pl.ds(row0, B_HPG), :] + dtc_ref[gg])
        if n_pad:
            dt_c = jnp.where(pad_col, 0.0, dt_c)
            dt_r = jnp.where(pad_row, 0.0, dt_r)
        a_c = dt_c * (-jnp.exp(alr_ref[gg]))
        a_r = dt_r * (-jnp.exp(alc_ref[gg]))
        b_c = jnp.dot(tril, a_c, preferred_element_type=F32, precision=lax.Precision.HIGHEST)
        b_r = jnp.dot(a_r, triu, preferred_element_type=F32, precision=lax.Precision.HIGHEST)
        eb_c = jnp.exp(b_c)
        b_last = b_c[L - 1:L, :]
        w_c = jnp.exp(b_last - b_c)
        e_last = jnp.exp(b_last)

        for p in range(B_HPG // 2):
            h0 = 2 * p
            pair = gg * (B_HPG // 2) + p
            x_pair = xs[:, pair * 2 * B_P:(pair + 1) * 2 * B_P]
            xdt = x_pair * two(dt_c, h0)
            y = None
            for hh, keep in ((h0, lane_lo), (h0 + 1, ~lane_lo)):
                decay = jnp.exp(jnp.where(causal, b_c[:, hh:hh + 1] - b_r[hh:hh + 1, :], NEG_BIG))
                part = _dot_nn((cb * decay).astype(BF16), jnp.where(keep, xdt, 0.0).astype(BF16))
                y = part if y is None else y + part
            s_prev = s_s[pair][...]
            y = y + two(eb_c, h0) * lax.dot_general(cmb, s_prev.astype(BF16), (((1,), (1,)), ((), ())),
                                                    preferred_element_type=F32)
            upd = lax.dot_general((xdt * two(w_c, h0)).astype(BF16), bmb, (((0,), (0,)), ((), ())),
                                  preferred_element_type=F32)
            s_new = jnp.where(row_lo, e_last[:, h0:h0 + 1], e_last[:, h0 + 1:h0 + 2]) * s_prev + upd
            s_s[pair][...] = s_new
            s_out_ref[0, pair] = s_new
            ybuf[:, pair * 2 * B_P:(pair + 1) * 2 * B_P] = y

    y = (ybuf[...] + dsk_ref[...] * xs) * _silu(z_ref[...])
    for gg in range(gps):
        yg = y[:, gg * B_GW:(gg + 1) * B_GW]
        yn = yg * lax.rsqrt(jnp.mean(yg * yg, axis=-1, keepdims=True) + 1e-6) * ng_ref[:, gg * B_GW:(gg + 1) * B_GW]
        y_ref[:, gg * B_GW:(gg + 1) * B_GW] = yn.astype(y_ref.dtype)


def _ssd(proj, gates_t, gates_c, dt_bias, a_log, d_skip, conv_w, conv_b, norm_g, s0, conv0, *,
         n_seq, n_chunks, L, row0, n_pad, name):
    rb0 = row0 // L

    def rblk(b, c):
        return rb0 + b * n_chunks + c

    out_dtype = BF16 if L % 16 == 0 else F32
    gps = B_GROUPS if n_chunks == 1 else 1
    xw, nw = gps * B_GW, gps * B_N
    n_pairs = gps * B_HPG // 2
    cw = conv_w
    cbias = conv_b.reshape(1, B_CONV_DIM)
    xoff = COL_XBC // xw
    boff = (COL_XBC + B_INNER) // nw
    coff = (COL_XBC + B_INNER + B_GROUPS * B_N) // nw
    cvb = B_INNER // nw
    cvc = (B_INNER + B_GROUPS * B_N) // nw
    dt_r = dt_bias.reshape(B_GROUPS, 1, B_HPG)
    dt_c = dt_bias.reshape(B_GROUPS, B_HPG, 1)
    al_r = a_log.reshape(B_GROUPS, 1, B_HPG)
    al_c = a_log.reshape(B_GROUPS, B_HPG, 1)
    d_row = jnp.repeat(d_skip, B_P).reshape(1, B_INNER)
    s0p = s0.reshape(n_seq, B_HEADS // 2, 2 * B_P, B_N)
    prow = pl.BlockSpec((gps, 1, B_HPG), lambda b, g, c: (g, 0, 0))
    pcol = pl.BlockSpec((gps, B_HPG, 1), lambda b, g, c: (g, 0, 0))
    kern = functools.partial(_ssd_kernel, L=L, n_pad=n_pad, gps=gps)
    y, s_new = pl.pallas_call(
        kern,
        grid=(n_seq, B_GROUPS // gps, n_chunks),
        in_specs=[
            prow, pcol, prow, pcol,
            pl.BlockSpec((L, xw), lambda b, g, c: (rblk(b, c), xoff + g)),
            pl.BlockSpec((L, nw), lambda b, g, c: (rblk(b, c), boff + g)),
            pl.BlockSpec((L, nw), lambda b, g, c: (rblk(b, c), coff + g)),
            pl.BlockSpec((L, xw), lambda b, g, c: (rblk(b, c), COL_Z // xw + g)),
            pl.BlockSpec((1, N_SMALL, L), lambda b, g, c: (b, 0, c)),
            pl.BlockSpec((1, gps, L, B_HPG), lambda b, g, c: (b, g, c, 0)),
            pl.BlockSpec((1, n_pairs, 2 * B_P, B_N), lambda b, g, c: (b, g, 0, 0)),
            pl.BlockSpec((1, 8, xw), lambda b, g, c: (b, 0, g)),
            pl.BlockSpec((1, 8, nw), lambda b, g, c: (b, 0, cvb + g)),
            pl.BlockSpec((1, 8, nw), lambda b, g, c: (b, 0, cvc + g)),
            pl.BlockSpec((CONV_W, xw), lambda b, g, c: (0, g)),
            pl.BlockSpec((CONV_W, nw), lambda b, g, c: (0, cvb + g)),
            pl.BlockSpec((CONV_W, nw), lambda b, g, c: (0, cvc + g)),
            pl.BlockSpec((1, xw), lambda b, g, c: (0, g)),
            pl.BlockSpec((1, nw), lambda b, g, c: (0, cvb + g)),
            pl.BlockSpec((1, nw), lambda b, g, c: (0, cvc + g)),
            pl.BlockSpec((1, xw), lambda b, g, c: (0, g)),
            pl.BlockSpec((1, xw), lambda b, g, c: (0, g)),
        ],
        out_specs=[
            pl.BlockSpec((L, xw), lambda b, g, c: (b * n_chunks + c, g)),
            pl.BlockSpec((1, n_pairs, 2 * B_P, B_N), lambda b, g, c: (b, g, 0, 0)),
        ],
        out_shape=[
            jax.ShapeDtypeStruct((n_seq * n_chunks * L, B_INNER), out_dtype),
            jax.ShapeDtypeStruct((n_seq, B_HEADS // 2, 2 * B_P, B_N), F32),
        ],
        scratch_shapes=[
            pltpu.VMEM((8, xw), F32), pltpu.VMEM((8, nw), F32), pltpu.VMEM((8, nw), F32),
            pltpu.VMEM((L + 8, xw), F32), pltpu.VMEM((L + 8, nw), F32), pltpu.VMEM((L + 8, nw), F32),
            pltpu.VMEM((L, xw), F32),
        ] + [pltpu.VMEM((2 * B_P, B_N), F32)] * n_pairs,
        compiler_params=_cparams(3),
        name=name,
    )(dt_r, dt_c, al_r, al_c, proj, proj, proj, proj, gates_t, gates_c, s0p, conv0, conv0, conv0,
      cw, cw, cw, cbias, cbias, cbias, norm_g.reshape(1, B_INNER), d_row)
    return y, s_new.reshape(n_seq, B_HEADS, B_P, B_N)


def _gelu_tanh(x):
    return 0.5 * x * (1.0 + jnp.tanh(math.sqrt(2.0 / math.pi) * (x + 0.044715 * (x * x * x))))


def _expm1_nonpos(z):
    e = jnp.exp(z)
    safe = (e < 1.0) & (z > -1.0)
    return jnp.where(safe, (e - 1.0) * z / jnp.log(jnp.where(safe, e, 0.5)), jnp.where(e == 1.0, z, e - 1.0))


def _lru_gates(xf, wg_ref, bg_ref, lam_ref):
    xfb = xf.astype(BF16)
    r = _sigmoid(jnp.dot(xfb, wg_ref[0, 0], preferred_element_type=F32) + bg_ref[0:1, :])
    i = _sigmoid(jnp.dot(xfb, wg_ref[1, 0], preferred_element_type=F32) + bg_ref[1:2, :])
    log_a = (-LRU_C) * r * _softplus(-lam_ref[...])
    a = jnp.exp(log_a)
    u = jnp.sqrt(-_expm1_nonpos(2.0 * log_a)) * (i * xf)
    return a, u


def _scan_rows8(a, u, t8):
    for k in (1, 2, 4):
        keep = t8 >= k
        a_sh = jnp.where(keep, pltpu.roll(a, k, 0), 1.0)
        u_sh = jnp.where(keep, pltpu.roll(u, k, 0), 0.0)
        u = a * u_sh + u
        a = a * a_sh
    return a, u


def _lru_kernel(gi_ref, xb_ref, h0_ref, cv0_ref, cw_ref, cb_ref, wg_ref, bg_ref, lam_ref,
                o_ref, h_out_ref, h_s, cv_s, buf, *, L, n_pad):
    c = pl.program_id(2)

    @pl.when(c == 0)
    def _():
        h_s[...] = h0_ref[0]
        cv_s[...] = cv0_ref[0]

    row_id = lax.broadcasted_iota(jnp.int32, (L, 1), 0)
    x_raw = xb_ref[...]
    if n_pad:
        pad_col = (row_id < n_pad) & (c == 0)
        x_raw = jnp.where(pad_col, 0.0, x_raw)
    xf = _causal_conv(x_raw, cv_s, buf, cw_ref, cb_ref, L)
    a, u = _lru_gates(xf, wg_ref, bg_ref, lam_ref)
    if n_pad:
        a = jnp.where(pad_col, 1.0, a)
        u = jnp.where(pad_col, 0.0, u)
    a, u = _scan_rows8(a, u, row_id % 8)
    h_in = h_s[...]
    pieces = []
    for t in range(L // 8):
        piece = a[t * 8:(t + 1) * 8, :] * h_in + u[t * 8:(t + 1) * 8, :]
        pieces.append(piece)
        h_in = piece[7:8, :]
    hs = jnp.concatenate(pieces, axis=0)
    h_s[...] = h_in
    h_out_ref[0] = h_in
    o_ref[...] = (hs * _gelu_tanh(gi_ref[...])).astype(o_ref.dtype)


def _lru_rows8_kernel(gi_ref, xb_ref, h0_ref, hist_ref, cw_ref, cb_ref, wg_ref, bg_ref, lam_ref,
                      o_ref, hs_ref):
    R = xb_ref.shape[0]
    t8 = lax.broadcasted_iota(jnp.int32, (R, 1), 0) % 8
    x = xb_ref[...]
    hist = hist_ref[...]
    acc = cb_ref[...] + cw_ref[CONV_W - 1:CONV_W, :] * x
    for j in range(CONV_W - 1):
        sh = CONV_W - 1 - j
        tap = jnp.where(t8 >= sh, pltpu.roll(x, sh, 0), pltpu.roll(hist, (sh - 8) % R, 0))
        acc = acc + cw_ref[j:j + 1, :] * tap
    a, u = _lru_gates(acc, wg_ref, bg_ref, lam_ref)
    a, u = _scan_rows8(a, u, t8)
    hs = a * h0_ref[...] + u
    hs_ref[...] = hs
    o_ref[...] = (hs * _gelu_tanh(gi_ref[...])).astype(o_ref.dtype)


def _lru_rows8(proj, conv_w, conv_b, wg_sb, b_gate, lam, h0, conv0, *, n_seq, row0, name):
    R = 128
    n_rows = n_seq * 8
    rb0 = row0 // R
    h0x = jnp.repeat(h0.reshape(n_seq, D_RNN), 8, axis=0)
    hist = conv0.reshape(n_rows, D_RNN)
    blk = lambda j, i: (i, j)
    return pl.pallas_call(
        _lru_rows8_kernel,
        grid=(N_LRU_SB, n_rows // R),
        in_specs=[
            pl.BlockSpec((R, LRU_SB), lambda j, i: (rb0 + i, j)),
            pl.BlockSpec((R, LRU_SB), lambda j, i: (rb0 + i, N_LRU_SB + j)),
            pl.BlockSpec((R, LRU_SB), blk),
            pl.BlockSpec((R, LRU_SB), blk),
            pl.BlockSpec((CONV_W, LRU_SB), lambda j, i: (0, j)),
            pl.BlockSpec((1, LRU_SB), lambda j, i: (0, j)),
            pl.BlockSpec((2, 1, LRU_SB, LRU_SB), lambda j, i: (0, j, 0, 0)),
            pl.BlockSpec((2, LRU_SB), lambda j, i: (0, j)),
            pl.BlockSpec((1, LRU_SB), lambda j, i: (0, j)),
        ],
        out_specs=[pl.BlockSpec((R, LRU_SB), blk), pl.BlockSpec((R, LRU_SB), blk)],
        out_shape=[jax.ShapeDtypeStruct((n_rows, D_RNN), BF16), jax.ShapeDtypeStruct((n_rows, D_RNN), F32)],
        compiler_params=_cparams(2),
        name=name,
    )(proj, proj, h0x, hist, conv_w, conv_b.reshape(1, D_RNN), wg_sb, b_gate, lam.reshape(1, D_RNN))


def _lru(proj, conv_w, conv_b, wg_sb, b_gate, lam, h0, conv0, *, n_seq, n_chunks, L, row0, n_pad, name):
    rb0 = row0 // L

    def rblk(b, c):
        return rb0 + b * n_chunks + c

    out_dtype = BF16 if L % 16 == 0 else F32
    kern = functools.partial(_lru_kernel, L=L, n_pad=n_pad)
    return pl.pallas_call(
        kern,
        grid=(n_seq, N_LRU_SB, n_chunks),
        in_specs=[
            pl.BlockSpec((L, LRU_SB), lambda b, j, c: (rblk(b, c), j)),
            pl.BlockSpec((L, LRU_SB), lambda b, j, c: (rblk(b, c), N_LRU_SB + j)),
            pl.BlockSpec((1, 1, LRU_SB), lambda b, j, c: (b, 0, j)),
            pl.BlockSpec((1, 8, LRU_SB), lambda b, j, c: (b, 0, j)),
            pl.BlockSpec((CONV_W, LRU_SB), lambda b, j, c: (0, j)),
            pl.BlockSpec((1, LRU_SB), lambda b, j, c: (0, j)),
            pl.BlockSpec((2, 1, LRU_SB, LRU_SB), lambda b, j, c: (0, j, 0, 0)),
            pl.BlockSpec((2, LRU_SB), lambda b, j, c: (0, j)),
            pl.BlockSpec((1, LRU_SB), lambda b, j, c: (0, j)),
        ],
        out_specs=[
            pl.BlockSpec((L, LRU_SB), lambda b, j, c: (b * n_chunks + c, j)),
            pl.BlockSpec((1, 1, LRU_SB), lambda b, j, c: (b, 0, j)),
        ],
        out_shape=[
            jax.ShapeDtypeStruct((n_seq * n_chunks * L, D_RNN), out_dtype),
            jax.ShapeDtypeStruct((n_seq, 1, D_RNN), F32),
        ],
        scratch_shapes=[pltpu.VMEM((1, LRU_SB), F32), pltpu.VMEM((8, LRU_SB), F32),
                        pltpu.VMEM((L + 8, LRU_SB), F32)],
        compiler_params=_cparams(3),
        name=name,
    )(proj, proj, h0, conv0, conv_w, conv_b.reshape(1, D_RNN), wg_sb, b_gate, lam.reshape(1, D_RNN))


def _router_kernel(x_ref, w_ref, b_ref, gate_ref, id_ref):
    logits = jnp.dot(x_ref[...], w_ref[...], preferred_element_type=F32) + b_ref[...]
    tm = logits.shape[0]
    lane = lax.broadcasted_iota(jnp.int32, (tm, 128), 1)
    lane_f = lane.astype(F32)
    is_group = (lane >= N_EXPERTS) & (lane < N_EXPERTS + N_EXP_GROUPS)
    gl = jnp.where(is_group, logits, -jnp.inf)
    g_max = jnp.max(gl, axis=1, keepdims=True)
    g_lane = jnp.min(jnp.where(gl == g_max, lane_f, 1e9), axis=1, keepdims=True)
    g_w = 1.0 / jnp.sum(jnp.exp(gl - g_max), axis=1, keepdims=True)
    g_idx = g_lane - float(N_EXPERTS)
    lo = g_idx * float(EXP_PER_GROUP)
    in_group = (lane_f >= lo) & (lane_f < lo + float(EXP_PER_GROUP))
    el = jnp.where(in_group, logits, -jnp.inf)
    e_max = jnp.max(el, axis=1, keepdims=True)
    i1 = jnp.min(jnp.where(el == e_max, lane_f, 1e9), axis=1, keepdims=True)
    e_sum = jnp.sum(jnp.exp(el - e_max), axis=1, keepdims=True)
    el2 = jnp.where(lane_f == i1, -jnp.inf, el)
    e2_max = jnp.max(el2, axis=1, keepdims=True)
    i2 = jnp.min(jnp.where(el2 == e2_max, lane_f, 1e9), axis=1, keepdims=True)
    p1 = 1.0 / e_sum
    p2 = jnp.exp(e2_max - e_max) / e_sum
    tot = p1 + p2
    gate1 = g_w * (p1 / tot)
    gate2 = g_w * (p2 / tot)
    gate_ref[...] = jnp.where(lane == 0, gate1, jnp.where(lane == 1, gate2, 0.0))
    id_ref[...] = jnp.where(lane == 0, i1, jnp.where(lane == 1, i2, 0.0)).astype(jnp.int32)


def _router(x, w_r, b_r, name):
    M, D = x.shape
    tm = 256
    return pl.pallas_call(
        _router_kernel,
        grid=(M // tm,),
        in_specs=[pl.BlockSpec((tm, D), lambda i: (i, 0)),
                  pl.BlockSpec((D, 128), lambda i: (0, 0)),
                  pl.BlockSpec((1, 128), lambda i: (0, 0))],
        out_specs=[pl.BlockSpec((tm, 128), lambda i: (i, 0)), pl.BlockSpec((tm, 128), lambda i: (i, 0))],
        out_shape=[jax.ShapeDtypeStruct((M, 128), F32), jax.ShapeDtypeStruct((M, 128), jnp.int32)],
        compiler_params=_cparams(1),
        name=name,
    )(x, w_r, b_r)


def _ffn_kernel(te_ref, nv_ref, idx_ref, x_hbm, w1_ref, w3_ref, w2_ref, o_ref,
                w1b, w3b, w2b, xb_s, xbuf, sems):
    i = pl.program_id(0)
    tm = TILE_ROWS
    nv = nv_ref[0]
    n_slots = FFN_AHEAD + 1
    slot = i % n_slots

    def fetch(tile):
        sl = tile % n_slots
        _issue_gather(idx_ref, tile * tm, x_hbm, xbuf, sl * tm, sems.at[sl], tm)

    @pl.when(i == 0)
    def _():
        for t in range(FFN_AHEAD):
            @pl.when(t < nv)
            def _():
                fetch(t)

    @pl.when(i + FFN_AHEAD < nv)
    def _():
        fetch(i + FFN_AHEAD)

    prev = te_ref[jnp.maximum(i - 1, 0)]

    @pl.when((i == 0) | (te_ref[i] != prev))
    def _():
        w1b[...] = w1_ref[0, 0].astype(BF16)
        w3b[...] = w3_ref[0, 0].astype(BF16)
        w2b[...] = w2_ref[0, 0].astype(BF16)

    @pl.when(i < nv)
    def _():
        _wait_gather(x_hbm, xbuf, sems.at[slot], tm)
        base = slot * (tm * SLAB)
        for j in range(SLAB):
            xb_s[:, j * 128:(j + 1) * 128] = xbuf[pl.ds(base + j, tm, stride=SLAB), :].astype(BF16)
        xb = xb_s[...]
        h1 = jnp.dot(xb, w1b[...], preferred_element_type=F32)
        h3 = jnp.dot(xb, w3b[...], preferred_element_type=F32)
        hid = (_silu(h1) * h3).astype(BF16)
        _slab_store(o_ref, jnp.dot(hid, w2b[...], preferred_element_type=F32))

    @pl.when(i >= nv)
    def _():
        o_ref[...] = jnp.zeros_like(o_ref)


def _expert_ffn(x_slab, row_src, tile_expert, n_valid, w1, w3, w2, layer, name):
    R = row_src.shape[0]
    D = D_MODEL
    tm = TILE_ROWS
    wmap = lambda i, te, nv, idx: (layer, te[i], 0, 0)
    return pl.pallas_call(
        _ffn_kernel,
        grid_spec=pltpu.PrefetchScalarGridSpec(
            num_scalar_prefetch=3,
            grid=(R // tm,),
            in_specs=[
                pl.BlockSpec(memory_space=pl.ANY),
                pl.BlockSpec((1, 1, D, D_EXPERT), wmap),
                pl.BlockSpec((1, 1, D, D_EXPERT), wmap),
                pl.BlockSpec((1, 1, D_EXPERT, D), wmap),
            ],
            out_specs=pl.BlockSpec((tm * SLAB, 128), lambda i, te, nv, idx: (i, 0)),
            scratch_shapes=[pltpu.VMEM((D, D_EXPERT), BF16), pltpu.VMEM((D, D_EXPERT), BF16),
                            pltpu.VMEM((D_EXPERT, D), BF16), pltpu.VMEM((tm, D), BF16),
                            pltpu.VMEM(((FFN_AHEAD + 1) * tm * SLAB, 128), F32),
                            pltpu.SemaphoreType.DMA((FFN_AHEAD + 1,))],
        ),
        out_shape=jax.ShapeDtypeStruct((R * SLAB, 128), F32),
        compiler_params=_cparams(1),
        name=name,
    )(tile_expert, n_valid, row_src, x_slab, w1, w3, w2)


def _route_plan(ids, n_tok):
    tm = TILE_ROWS
    n_pairs = TOP_K * n_tok
    n_rows = n_pairs + N_EXPERTS * tm
    experts = jnp.arange(N_EXPERTS, dtype=jnp.int32)
    e_flat = ids.T.reshape(-1)
    order = jnp.argsort(e_flat, stable=True).astype(jnp.int32)
    rank_sorted = jnp.argsort(order).astype(jnp.int32)
    sizes = jnp.sum((e_flat[:, None] == experts[None, :]).astype(jnp.int32), axis=0)
    start = jnp.cumsum(sizes) - sizes
    psz = ((sizes + tm - 1) // tm) * tm
    pend = jnp.cumsum(psz)
    pstart = pend - psz
    shift = pstart - start
    pos = shift[e_flat] + rank_sorted
    rows = jnp.arange(n_rows, dtype=jnp.int32)
    tile_start = rows[::tm]
    e_tile = jnp.minimum(jnp.sum((tile_start[:, None] >= pend[None, :]).astype(jnp.int32), axis=1), N_EXPERTS - 1)
    pair_idx = rows - jnp.repeat(shift[e_tile], tm)
    valid = rows < jnp.repeat((pstart + sizes)[e_tile], tm)
    pair = order[jnp.clip(pair_idx, 0, n_pairs - 1)]
    row_src = jnp.where(valid, pair % n_tok, 0)
    last_e = jnp.max(jnp.where(sizes > 0, experts, 0))
    te = jnp.where(tile_start < pend[-1], e_tile, last_e)
    n_valid = (pend[-1] // tm).astype(jnp.int32)
    return row_src, pos, te, n_valid.reshape(1)


def _hier_moe_ln(x_slab, xb, w_r, b_r, w1, w3, w2, layer, ln_g, ln_b, tag, split=None):
    n_tok = xb.shape[0]
    gates, ids = _router(xb, w_r.astype(BF16), b_r, name=f"router_{tag}")
    row_src, pos, te, n_valid = _route_plan(ids[:, :TOP_K], n_tok)
    ys = _expert_ffn(x_slab, row_src, te, n_valid, w1, w3, w2, layer, name=f"experts_{tag}")
    return _combine_ln(x_slab, ys, pos, gates, ln_g, ln_b, name=f"combine_ln_{tag}", split=split)


def _router_weights(w_group, b_group, w_expert, b_expert):
    w = jnp.concatenate([w_expert, w_group], axis=1)
    w = jnp.pad(w, ((0, 0), (0, 128 - w.shape[1])))
    b = jnp.pad(jnp.concatenate([b_expert, b_group]), (0, 128 - N_EXPERTS - N_EXP_GROUPS))
    return w, b.reshape(1, 128)


def _pad_conv_state(conv):
    return jnp.pad(conv, ((0, 0), (8 - (CONV_W - 1), 0), (0, 0)))


def kernel(x_prompt, x_sample, state_mlstm_C, state_mlstm_n, state_mlstm_m, state_ssd, state_ssd_conv,
           state_lru_h, state_lru_conv, meta_tokens, w_in_even, mlstm_gate_b, ssd_dt_bias, ssd_A_log, ssd_D,
           ssd_conv_w, ssd_conv_b, mlstm_norm_g, ssd_norm_g, w_out_even, w_in_odd, lru_conv_w, lru_conv_b,
           lru_w_gate, lru_b_gate, lru_lambda, w_out_odd, ln_g, ln_b, moe_w_group, moe_b_group,
           moe_w_expert, moe_b_expert, moe_w1, moe_w3, moe_w2):
    Bp, Tp, D = x_prompt.shape
    Bs, Ts, _ = x_sample.shape
    n_chunks_p = (N_META + Tp + CHUNK - 1) // CHUNK
    Tpp = n_chunks_p * CHUNK
    n_pad = Tpp - N_META - Tp
    n_p = Bp * Tpp
    n_s = Bs * Ts
    n_tok = n_p + n_s

    pieces = []
    for b in range(Bp):
        pieces += [jnp.zeros((n_pad, D), F32), meta_tokens.astype(F32), x_prompt[b]]
    x0 = jnp.concatenate(pieces + [x_sample.reshape(n_s, D)], axis=0)

    grp = [dict(n_seq=Bp, n_chunks=n_chunks_p, L=CHUNK, row0=0, n_pad=n_pad),
           dict(n_seq=Bs, n_chunks=1, L=Ts, row0=n_p, n_pad=0)]

    def seq_view(a, gi):
        if gi == 0:
            return a[:n_p].reshape(Bp, Tpp, a.shape[1])
        return a[n_p:].reshape(Bs, Ts, a.shape[1])

    def tail_rows(a, gi, col0, ncol):
        nb, T, base = (Bp, Tpp, 0) if gi == 0 else (Bs, Ts, n_p)
        idx = (base + np.arange(nb)[:, None] * T + np.arange(T - (CONV_W - 1), T)[None, :]).reshape(-1)
        rows = jnp.take(a, jnp.asarray(idx, jnp.int32), axis=0)
        return rows[:, col0:col0 + ncol].reshape(nb, CONV_W - 1, ncol)

    e = 0
    w = w_in_even[e]
    src_small = 2 * A_HEADS * A_DK + 2 * A_INNER
    src_z = src_small + 2 * A_HEADS
    src_dt = src_z + B_INNER + B_CONV_DIM
    w_all = jnp.concatenate([w[:, :src_small], w[:, src_z:src_dt], w[:, src_small:src_z], w[:, src_dt:],
                             jnp.zeros((D, IN0_PAD - COL_SMALL - N_SMALL), F32)], axis=1).astype(BF16)
    proj = _mm(x0, w_all, 512, IN0_PAD // 5, name="in_proj_even")

    small = proj[:, COL_SMALL:COL_SMALL + N_SMALL]
    h_parts, y_parts, st = [], [], []
    for gi, g in enumerate(grp):
        sv = seq_view(small, gi)
        gates_t = jnp.swapaxes(sv, 1, 2)
        nb, T = sv.shape[0], sv.shape[1]
        gates_c = sv[:, :, 2 * A_HEADS:].reshape(nb, T, B_GROUPS, B_HPG).transpose(0, 2, 1, 3)
        if gi == 0:
            c0 = jnp.zeros((Bp, A_HEADS, A_DK, A_DV), F32)
            n0 = jnp.zeros((Bp, A_HEADS, 1, A_DK), F32)
            m0 = jnp.zeros((Bp, A_HEADS), F32)
            s0 = jnp.zeros((Bp, B_HEADS, B_P, B_N), F32)
            cv0 = jnp.zeros((Bp, 8, B_CONV_DIM), F32)
        else:
            c0 = state_mlstm_C[e]
            n0 = state_mlstm_n[e][:, :, None, :]
            m0 = state_mlstm_m[e]
            s0 = state_ssd[e]
            cv0 = _pad_conv_state(state_ssd_conv[e])
        h_g, c_g, n_g, m_g = _mlstm(proj, gates_t, mlstm_gate_b[e], mlstm_norm_g[e], c0, n0, m0,
                                    name=f"mlstm_{gi}", **g)
        y_g, s_g = _ssd(proj, gates_t, gates_c, ssd_dt_bias[e], ssd_A_log[e], ssd_D[e], ssd_conv_w[e],
                        ssd_conv_b[e], ssd_norm_g[e], s0, cv0, name=f"ssd_{gi}", **g)
        h_parts.append(h_g.astype(BF16))
        y_parts.append(y_g.astype(BF16))
        sconv = tail_rows(proj, gi, COL_XBC, B_CONV_DIM)
        st.append((c_g[None], n_g[:, :, 0, :][None], m_g[None], s_g[None], sconv[None]))
    x1s, x1b = _mm_parts_ln([h_parts[0], y_parts[0]], [h_parts[1], y_parts[1]], w_out_even[e].astype(BF16),
                            x0, ln_g[0, 0], ln_b[0, 0], name="out_proj_ln_even")
    w_r, b_r = _router_weights(moe_w_group[0], moe_b_group[0], moe_w_expert[0], moe_b_expert[0])
    x2, x2b = _hier_moe_ln(x1s, x1b, w_r, b_r, moe_w1, moe_w3, moe_w2, 0, ln_g[0, 1], ln_b[0, 1], "0")

    o = 0
    proj1 = _mm(x2b, w_in_odd[o].astype(BF16), 512, D_RNN, name="in_proj_odd")
    wg = lru_w_gate[o].reshape(2, N_LRU_SB, LRU_SB // LRU_BW, LRU_BW, LRU_BW)
    eye4 = jnp.eye(LRU_SB // LRU_BW, dtype=F32)
    wg_sb = jnp.einsum('gjaik,ab->gjaibk', wg, eye4).reshape(2, N_LRU_SB, LRU_SB, LRU_SB).astype(BF16)
    o_parts, st1 = [], []
    for gi, g in enumerate(grp):
        if gi == 0:
            h0 = jnp.zeros((Bp, 1, D_RNN), F32)
            cv0 = jnp.zeros((Bp, 8, D_RNN), F32)
        else:
            h0 = state_lru_h[o][:, None, :]
            cv0 = _pad_conv_state(state_lru_conv[o])
        if g["L"] == 8 and g["n_chunks"] == 1:
            o_g, hs = _lru_rows8(proj1, lru_conv_w[o], lru_conv_b[o], wg_sb, lru_b_gate[o], lru_lambda[o], h0, cv0,
                                 n_seq=g["n_seq"], row0=g["row0"], name=f"lru_{gi}")
            hN = hs.reshape(g["n_seq"], 8, D_RNN)[:, 7, :]
        else:
            o_g, hN = _lru(proj1, lru_conv_w[o], lru_conv_b[o], wg_sb, lru_b_gate[o], lru_lambda[o], h0, cv0,
                           name=f"lru_{gi}", **g)
            hN = hN[:, 0, :]
        o_parts.append(o_g.astype(BF16))
        hconv = tail_rows(proj1, gi, D_RNN, D_RNN)
        st1.append((hN[None], hconv[None]))
    x3s, x3b = _mm_parts_ln([o_parts[0]], [o_parts[1]], w_out_odd[o].astype(BF16),
                            x2, ln_g[1, 0], ln_b[1, 0], name="out_proj_ln_odd")
    w_r, b_r = _router_weights(moe_w_group[1], moe_b_group[1], moe_w_expert[1], moe_b_expert[1])
    assert Tpp - Tp == CHUNK
    yp, ys = _hier_moe_ln(x3s, x3b, w_r, b_r, moe_w1, moe_w3, moe_w2, 1, ln_g[1, 1], ln_b[1, 1], "1",
                          split=(n_p, n_chunks_p))
    y_prompt = yp.reshape(Bp, Tp, D)
    y_sample = ys.reshape(Bs, Ts, D)
    (pC, pn, pm, pS, pSc), (sC, sn, sm, sS, sSc) = st
    (pH, pHc), (sH, sHc) = st1
    return (y_prompt, y_sample, pC, pn, pm, pS, pSc, pH, pHc, sC, sn, sm, sS, sSc, sH, sHc)
```

```python
import functools
import math

import jax
import jax.numpy as jnp
import numpy as np
from jax import lax
from jax.experimental import pallas as pl
from jax.experimental.pallas import tpu as pltpu

F32 = jnp.float32
BF16 = jnp.bfloat16

D_MODEL = 2048
N_META = 16
CHUNK = 128
CONV_W = 4
A_HEADS = 8
A_DK = 128
A_DV = 256
A_INNER = A_HEADS * A_DV
B_HEADS = 32
B_P = 64
B_N = 128
B_GROUPS = 4
B_HPG = B_HEADS // B_GROUPS
B_INNER = B_HEADS * B_P
B_GW = B_INNER // B_GROUPS
B_CONV_DIM = B_INNER + 2 * B_GROUPS * B_N
D_RNN = 2560
LRU_BLOCKS = 16
LRU_BW = D_RNN // LRU_BLOCKS
LRU_C = 8.0
LRU_SB = 640
N_LRU_SB = D_RNN // LRU_SB
N_EXP_GROUPS = 4
EXP_PER_GROUP = 8
N_EXPERTS = N_EXP_GROUPS * EXP_PER_GROUP
TOP_K = 2
D_EXPERT = 512
DEPTH = 2
ALPHA = (2.0 * DEPTH) ** 0.25

COL_Q = 0
COL_K = A_HEADS * A_DK
COL_V = 2 * A_HEADS * A_DK
COL_O = COL_V + A_INNER
COL_Z = COL_O + A_INNER
COL_XBC = COL_Z + B_INNER
COL_SMALL = COL_XBC + B_CONV_DIM
N_SMALL = 2 * A_HEADS + B_HEADS
IN0_PAD = 11520

NEG_BIG = -1e30
VMEM_LIMIT_BYTES = 48 * 1024 * 1024
TILE_ROWS = 256
FFN_AHEAD = 2
LN_ROWS = 256
SLAB = D_MODEL // 128


def _cparams(n_axes):
    return pltpu.CompilerParams(dimension_semantics=("arbitrary",) * n_axes,
                                vmem_limit_bytes=VMEM_LIMIT_BYTES)


def _smem_spec():
    return pl.BlockSpec(memory_space=pltpu.SMEM)


def _mm_kernel(x_ref, w_ref, o_ref):
    o_ref[...] = jnp.dot(x_ref[...].astype(BF16), w_ref[...], preferred_element_type=F32).astype(o_ref.dtype)


def _mm(x, w, bm, bn, name, n_cols=None):
    M, K = x.shape
    N = w.shape[1] if n_cols is None else n_cols
    return pl.pallas_call(
        _mm_kernel,
        grid=(N // bn, M // bm),
        in_specs=[pl.BlockSpec((bm, K), lambda j, i: (i, 0)),
                  pl.BlockSpec((K, bn), lambda j, i: (0, j))],
        out_specs=pl.BlockSpec((bm, bn), lambda j, i: (i, j)),
        out_shape=jax.ShapeDtypeStruct((M, N), F32),
        compiler_params=_cparams(2),
        name=name,
    )(x, w)


RELAY_W = 1024


def _w_relayout_kernel(a_ref, b_ref, g_ref, o_ref, *, n_same, n_shift, shift):
    j = pl.program_id(0)
    lane = lax.broadcasted_iota(jnp.int32, (1, 128), 1)
    n_t = RELAY_W // 128

    @pl.when(j < n_same)
    def _():
        o_ref[...] = a_ref[...].astype(BF16)

    @pl.when((j >= n_same) & (j < n_same + n_shift))
    def _():
        cur = pltpu.roll(a_ref[:, 0:128], 128 - shift, 1)
        for t in range(n_t):
            nxt_src = a_ref[:, (t + 1) * 128:(t + 2) * 128] if t + 1 < n_t else b_ref[...]
            nxt = pltpu.roll(nxt_src, 128 - shift, 1)
            o_ref[:, t * 128:(t + 1) * 128] = jnp.where(lane < 128 - shift, cur, nxt).astype(BF16)
            cur = nxt

    @pl.when(j == n_same + n_shift)
    def _():
        head = jnp.where(lane < 2 * A_HEADS, g_ref[...], jnp.where(lane < N_SMALL, a_ref[:, 0:128], 0.0))
        o_ref[:, 0:128] = head.astype(BF16)
        o_ref[:, 128:] = jnp.zeros((o_ref.shape[0], RELAY_W - 128), BF16)


def _w_relayout(w):
    K = w.shape[0]
    src_small = 2 * A_HEADS * A_DK + 2 * A_INNER
    n_same = src_small // RELAY_W
    n_shift = (B_INNER + B_CONV_DIM) // RELAY_W
    n_blk = n_same + n_shift + 1
    assert src_small % RELAY_W == 0 and (B_INNER + B_CONV_DIM) % RELAY_W == 0 and n_blk * RELAY_W >= IN0_PAD
    assert COL_SMALL == (n_blk - 1) * RELAY_W and w.shape[1] == COL_SMALL + N_SMALL
    per = RELAY_W // 128
    return pl.pallas_call(
        functools.partial(_w_relayout_kernel, n_same=n_same, n_shift=n_shift, shift=2 * A_HEADS),
        grid=(n_blk,),
        in_specs=[pl.BlockSpec((K, RELAY_W), lambda j: (0, j)),
                  pl.BlockSpec((K, 128), lambda j: (0, jnp.minimum(j + 1, n_blk - 1) * per)),
                  pl.BlockSpec((K, 128), lambda j: (0, src_small // 128))],
        out_specs=pl.BlockSpec((K, RELAY_W), lambda j: (0, j)),
        out_shape=jax.ShapeDtypeStruct((K, n_blk * RELAY_W), BF16),
        compiler_params=_cparams(1),
        name="w_in_even_relayout",
    )(w, w, w)


def _layer_norm_rows(y, g, b):
    mu = jnp.mean(y, axis=-1, keepdims=True)
    yc = y - mu
    var = jnp.mean(yc * yc, axis=-1, keepdims=True)
    return yc * lax.rsqrt(var + 1e-5) * g + b


def _mm_parts_ln_kernel(*refs, k_sizes, n_blk_a):
    n = len(k_sizes)
    a_refs, b_refs = refs[:n], refs[n:2 * n]
    w_ref, x_ref, g_ref, b_ref, wr_ref, br_ref, os_ref, gate_ref, id_ref = refs[2 * n:]
    i = pl.program_id(0)

    def run(lhs_refs):
        acc, k0 = None, 0
        for r, ks in zip(lhs_refs, k_sizes):
            part = jnp.dot(r[...], w_ref[k0:k0 + ks, :], preferred_element_type=F32)
            acc = part if acc is None else acc + part
            k0 += ks
        y = _layer_norm_rows(ALPHA * x_ref[...] + acc, g_ref[...], b_ref[...])
        _slab_store(os_ref, y)
        logits = jnp.dot(y.astype(BF16), wr_ref[...], preferred_element_type=F32) + br_ref[...]
        gate_ref[...], id_ref[...] = _route(logits)

    @pl.when(i < n_blk_a)
    def _():
        run(a_refs)

    @pl.when(i >= n_blk_a)
    def _():
        run(b_refs)


def _mm_parts_ln(parts_a, parts_b, w, x, g, b, w_r, b_r, name):
    bm = LN_ROWS
    k_sizes = tuple(p.shape[1] for p in parts_a)
    n_blk_a = parts_a[0].shape[0] // bm
    n_blk_b = parts_b[0].shape[0] // bm
    K, D = w.shape
    M = (n_blk_a + n_blk_b) * bm
    a_specs = [pl.BlockSpec((bm, ks), lambda i: (jnp.minimum(i, n_blk_a - 1), 0)) for ks in k_sizes]
    b_specs = [pl.BlockSpec((bm, ks), lambda i: (jnp.maximum(i - n_blk_a, 0), 0)) for ks in k_sizes]
    row = pl.BlockSpec((bm, D), lambda i: (i, 0))
    vec = pl.BlockSpec((1, D), lambda i: (0, 0))
    tile = pl.BlockSpec((bm, 128), lambda i: (i, 0))
    w_spec = pl.BlockSpec((K, D), lambda i: (0, 0), pipeline_mode=pl.Buffered(1))
    return pl.pallas_call(
        functools.partial(_mm_parts_ln_kernel, k_sizes=k_sizes, n_blk_a=n_blk_a),
        grid=(n_blk_a + n_blk_b,),
        in_specs=a_specs + b_specs + [w_spec, row, vec, vec, pl.BlockSpec((D, 128), lambda i: (0, 0)),
                                      pl.BlockSpec((1, 128), lambda i: (0, 0))],
        out_specs=[pl.BlockSpec((bm * SLAB, 128), lambda i: (i, 0)), tile, tile],
        out_shape=[jax.ShapeDtypeStruct((M * SLAB, 128), F32), jax.ShapeDtypeStruct((M, 128), F32),
                   jax.ShapeDtypeStruct((M, 128), jnp.int32)],
        compiler_params=_cparams(1),
        name=name,
    )(*parts_a, *parts_b, w, x, g.reshape(1, D), b.reshape(1, D), w_r, b_r)


def _slab_store(ref, val):
    tm = val.shape[0]
    for j in range(SLAB):
        ref[pl.ds(j, tm, stride=SLAB), :] = val[:, j * 128:(j + 1) * 128]


def _slab_piece(ref, j, tm):
    return ref[pl.ds(j, tm, stride=SLAB), :]


def _slab_copy(src_hbm, dst_vmem, sem, src_tok, dst_tok):
    return pltpu.make_async_copy(src_hbm.at[pl.ds(pl.multiple_of(src_tok * SLAB, SLAB), SLAB), :],
                                 dst_vmem.at[pl.ds(pl.multiple_of(dst_tok * SLAB, SLAB), SLAB), :], sem)


def _issue_gather(idx_ref, idx0, src_hbm, dst_vmem, dst0, sem, n):
    def body(r, carry):
        _slab_copy(src_hbm, dst_vmem, sem, idx_ref[idx0 + r], dst0 + r).start()
        return carry
    lax.fori_loop(0, n, body, 0, unroll=8)


def _wait_gather(src_hbm, dst_vmem, sem, n):
    def body(r, carry):
        _slab_copy(src_hbm, dst_vmem, sem, 0, 0).wait()
        return carry
    lax.fori_loop(0, n, body, 0, unroll=8)


def _combine_rows(pos_ref, x_ref, y_hbm, gate_ref, g_ref, b_ref, v_s, ybuf, sems, n_tok, tm):
    i = pl.program_id(0)
    n = pl.num_programs(0)
    slot = i % 2

    def fetch(tile, sl):
        for kk in range(TOP_K):
            _issue_gather(pos_ref, kk * n_tok + tile * tm, y_hbm, ybuf, (sl * TOP_K + kk) * tm, sems.at[sl], tm)

    @pl.when(i == 0)
    def _():
        fetch(0, 0)

    @pl.when(i + 1 < n)
    def _():
        fetch(i + 1, 1 - slot)

    _wait_gather(y_hbm, ybuf, sems.at[slot], TOP_K * tm)
    gate = gate_ref[...]
    g0 = gate[:, 0:1]
    g1 = gate[:, 1:2]
    base0 = slot * (TOP_K * tm * SLAB)
    base1 = base0 + tm * SLAB
    for j in range(SLAB):
        v_s[:, j * 128:(j + 1) * 128] = (ALPHA * _slab_piece(x_ref, j, tm)
                                         + g0 * ybuf[pl.ds(base0 + j, tm, stride=SLAB), :]
                                         + g1 * ybuf[pl.ds(base1 + j, tm, stride=SLAB), :])
    return _layer_norm_rows(v_s[...], g_ref[...], b_ref[...])


def _combine_ln_kernel(pos_ref, x_ref, y_hbm, gate_ref, g_ref, b_ref, o_ref, ob_ref, v_s, ybuf, sems, *, n_tok):
    y = _combine_rows(pos_ref, x_ref, y_hbm, gate_ref, g_ref, b_ref, v_s, ybuf, sems, n_tok, LN_ROWS)
    o_ref[...] = y
    ob_ref[...] = y.astype(BF16)


def _combine_out_kernel(pos_ref, x_ref, y_hbm, gate_ref, g_ref, b_ref, yp_ref, ys_ref, v_s, ybuf, sems, *,
                        n_tok, n_blk_p, blk_per_seq):
    y = _combine_rows(pos_ref, x_ref, y_hbm, gate_ref, g_ref, b_ref, v_s, ybuf, sems, n_tok, CHUNK)
    i = pl.program_id(0)

    @pl.when((i < n_blk_p) & (i % blk_per_seq > 0))
    def _():
        yp_ref[...] = y

    @pl.when(i >= n_blk_p)
    def _():
        ys_ref[...] = y


def _combine_ln(x_slab, y_slab, pos, gates, g, b, name, split=None):
    M = gates.shape[0]
    D = D_MODEL
    tm = LN_ROWS if split is None else CHUNK
    vec = pl.BlockSpec((1, D), lambda i, pos: (0, 0))
    if split is None:
        kern = functools.partial(_combine_ln_kernel, n_tok=M)
        row = pl.BlockSpec((tm, D), lambda i, pos: (i, 0))
        out_specs = [row, row]
        out_shape = [jax.ShapeDtypeStruct((M, D), F32), jax.ShapeDtypeStruct((M, D), BF16)]
    else:
        n_p, bps = split
        n_blk_p = n_p // tm
        kern = functools.partial(_combine_out_kernel, n_tok=M, n_blk_p=n_blk_p, blk_per_seq=bps)

        def yp_map(i, pos):
            ip = jnp.minimum(i, n_blk_p - 1)
            return ((ip // bps) * (bps - 1) + jnp.maximum(ip % bps - 1, 0), 0)

        out_specs = [pl.BlockSpec((tm, D), yp_map),
                     pl.BlockSpec((tm, D), lambda i, pos: (jnp.maximum(i - n_blk_p, 0), 0))]
        out_shape = [jax.ShapeDtypeStruct((n_blk_p // bps * (bps - 1) * tm, D), F32),
                     jax.ShapeDtypeStruct((M - n_p, D), F32)]
    return pl.pallas_call(
        kern,
        grid_spec=pltpu.PrefetchScalarGridSpec(
            num_scalar_prefetch=1,
            grid=(M // tm,),
            in_specs=[pl.BlockSpec((tm * SLAB, 128), lambda i, pos: (i, 0)),
                      pl.BlockSpec(memory_space=pl.ANY),
                      pl.BlockSpec((tm, 128), lambda i, pos: (i, 0)), vec, vec],
            out_specs=out_specs,
            scratch_shapes=[pltpu.VMEM((tm, D), F32), pltpu.VMEM((2 * TOP_K * tm * SLAB, 128), F32),
                            pltpu.SemaphoreType.DMA((2,))],
        ),
        out_shape=out_shape,
        compiler_params=_cparams(1),
        name=name,
    )(pos, x_slab, y_slab, gates, g.reshape(1, D), b.reshape(1, D))


def _softplus(x):
    return jnp.maximum(x, 0.0) + jnp.log1p(jnp.exp(-jnp.abs(x)))


def _sigmoid(x):
    return 1.0 / (1.0 + jnp.exp(-x))


def _silu(x):
    return x * _sigmoid(x)


def _dot_nn(m, x):
    return jnp.dot(m, x, preferred_element_type=F32)


def _dot_nt(a, b):
    return lax.dot_general(a, b, (((1,), (1,)), ((), ())), preferred_element_type=F32)


def _row_to_col(row, eye):
    return jnp.sum(jnp.where(eye, row, 0.0), axis=1, keepdims=True)


def _causal_conv(x, carry_ref, buf_ref, w_ref, b_ref, L):
    buf_ref[0:8, :] = carry_ref[...]
    buf_ref[8:8 + L, :] = x
    acc = b_ref[...] + w_ref[0:1, :] * buf_ref[5:5 + L, :]
    for j in range(1, CONV_W):
        acc = acc + w_ref[j:j + 1, :] * buf_ref[5 + j:5 + j + L, :]
    carry_ref[...] = buf_ref[L:L + 8, :]
    return acc


def _mlstm_kernel(gb_ref, q_ref, k_ref, v_ref, o_ref, gt_ref, c0_ref, n0_ref, m0_ref, ng_ref,
                  h_ref, c_out_ref, n_out_ref, m_out_ref, *state, L, n_pad):
    c_s, n_s, m_s = state[:A_HEADS], state[A_HEADS:2 * A_HEADS], state[2 * A_HEADS:]
    c = pl.program_id(1)

    @pl.when(c == 0)
    def _():
        for hd in range(A_HEADS):
            c_s[hd][...] = c0_ref[0, hd]
            n_s[hd][...] = n0_ref[0, hd]
            m_s[hd][...] = m0_ref[0, hd]

    rows = lax.broadcasted_iota(jnp.int32, (L, L), 0)
    cols = lax.broadcasted_iota(jnp.int32, (L, L), 1)
    eye = rows == cols
    causal = cols <= rows

    if n_pad:
        pad = (lax.broadcasted_iota(jnp.int32, (1, L), 1) < n_pad) & (c == 0)

    for hd in range(A_HEADS):
        li = gt_ref[0, hd:hd + 1, :] + gb_ref[0, hd]
        fr = gt_ref[0, A_HEADS + hd:A_HEADS + hd + 1, :] + gb_ref[1, hd]
        lf = jnp.minimum(fr, 0.0) - jnp.log1p(jnp.exp(-jnp.abs(fr)))
        if n_pad:
            li = jnp.where(pad, NEG_BIG, li)
            lf = jnp.where(pad, 0.0, lf)

        lf_col = _row_to_col(lf, eye)
        b_col = jnp.sum(jnp.where(causal, lf, 0.0), axis=1, keepdims=True)
        b_row = jnp.sum(jnp.where(rows <= cols, lf_col, 0.0), axis=0, keepdims=True)
        m_prev = m_s[hd][:, 0:1]

        d = jnp.where(causal, b_col - b_row + li, NEG_BIG)
        inter = b_col + m_prev
        m_t = jnp.maximum(inter, jnp.max(d, axis=1, keepdims=True))
        w_intra = jnp.exp(d - m_t)
        w_inter = jnp.exp(inter - m_t)

        q = q_ref[:, hd * A_DK:(hd + 1) * A_DK] * (A_DK ** -0.5)
        k = k_ref[:, hd * A_DK:(hd + 1) * A_DK]
        v = v_ref[:, hd * A_DV:(hd + 1) * A_DV]
        qb = q.astype(BF16)
        kb = k.astype(BF16)
        vb = v.astype(BF16)
        s = _dot_nt(qb, kb) * w_intra
        c_prev = c_s[hd][...]
        n_prev = n_s[hd][...]
        num = _dot_nn(s.astype(BF16), vb)
        num = num + w_inter * jnp.dot(qb, c_prev.astype(BF16), preferred_element_type=F32)
        qn = jnp.sum(qb.astype(F32) * n_prev.astype(BF16).astype(F32), axis=1, keepdims=True)
        nq = jnp.sum(s, axis=1, keepdims=True) + w_inter * qn
        hh = num / jnp.maximum(jnp.abs(nq), jnp.exp(-m_t))

        hn = hh * lax.rsqrt(jnp.mean(hh * hh, axis=-1, keepdims=True) + 1e-6)
        hn = hn * ng_ref[:, hd * A_DV:(hd + 1) * A_DV]
        h_ref[:, hd * A_DV:(hd + 1) * A_DV] = (
            hn * _sigmoid(o_ref[:, hd * A_DV:(hd + 1) * A_DV])).astype(h_ref.dtype)

        b_last = b_row[:, L - 1:L]
        ws_log = b_last - b_row + li
        m_new = jnp.maximum(b_last + m_prev, jnp.max(ws_log, axis=1, keepdims=True))
        ws_col = _row_to_col(jnp.exp(ws_log - m_new), eye)
        wc = jnp.exp(b_last + m_prev - m_new)
        kw = k * ws_col
        c_new = wc * c_prev + lax.dot_general(kw.astype(BF16), vb, (((0,), (0,)), ((), ())),
                                              preferred_element_type=F32)
        n_new = wc * n_prev + jnp.sum(ws_col.astype(BF16).astype(F32) * kb.astype(F32), axis=0, keepdims=True)
        m_new_b = jnp.broadcast_to(m_new, (1, 128))
        c_s[hd][...] = c_new
        n_s[hd][...] = n_new
        m_s[hd][...] = m_new_b
        c_out_ref[0, hd] = c_new
        n_out_ref[0, hd] = n_new
        m_out_ref[0, hd] = m_new_b


def _mlstm(proj, gates_t, gate_b, norm_g, c0, n0, m0, *, n_seq, n_chunks, L, row0, n_pad, name):
    rb0 = row0 // L

    def rblk(b, c):
        return rb0 + b * n_chunks + c

    out_dtype = BF16 if L % 16 == 0 else F32
    qk_w = A_HEADS * A_DK
    state_map = lambda b, c: (b, 0, 0, 0)
    m0b = jnp.broadcast_to(m0[:, :, None, None], (n_seq, A_HEADS, 1, 128))
    kern = functools.partial(_mlstm_kernel, L=L, n_pad=n_pad)
    h, c_new, n_new, m_new = pl.pallas_call(
        kern,
        grid=(n_seq, n_chunks),
        in_specs=[
            _smem_spec(),
            pl.BlockSpec((L, qk_w), lambda b, c: (rblk(b, c), COL_Q // qk_w)),
            pl.BlockSpec((L, qk_w), lambda b, c: (rblk(b, c), COL_K // qk_w)),
            pl.BlockSpec((L, A_INNER), lambda b, c: (rblk(b, c), COL_V // A_INNER)),
            pl.BlockSpec((L, A_INNER), lambda b, c: (rblk(b, c), COL_O // A_INNER)),
            pl.BlockSpec((1, N_SMALL, L), lambda b, c: (b, 0, c)),
            pl.BlockSpec((1, A_HEADS, A_DK, A_DV), state_map),
            pl.BlockSpec((1, A_HEADS, 1, A_DK), state_map),
            pl.BlockSpec((1, A_HEADS, 1, 128), state_map),
            pl.BlockSpec((1, A_INNER), lambda b, c: (0, 0)),
        ],
        out_specs=[
            pl.BlockSpec((L, A_INNER), lambda b, c: (b * n_chunks + c, 0)),
            pl.BlockSpec((1, A_HEADS, A_DK, A_DV), state_map),
            pl.BlockSpec((1, A_HEADS, 1, A_DK), state_map),
            pl.BlockSpec((1, A_HEADS, 1, 128), state_map),
        ],
        out_shape=[
            jax.ShapeDtypeStruct((n_seq * n_chunks * L, A_INNER), out_dtype),
            jax.ShapeDtypeStruct((n_seq, A_HEADS, A_DK, A_DV), F32),
            jax.ShapeDtypeStruct((n_seq, A_HEADS, 1, A_DK), F32),
            jax.ShapeDtypeStruct((n_seq, A_HEADS, 1, 128), F32),
        ],
        scratch_shapes=([pltpu.VMEM((A_DK, A_DV), F32)] * A_HEADS + [pltpu.VMEM((1, A_DK), F32)] * A_HEADS
                        + [pltpu.VMEM((1, 128), F32)] * A_HEADS),
        compiler_params=_cparams(2),
        name=name,
    )(gate_b, proj, proj, proj, proj, gates_t, c0, n0, m0b, norm_g.reshape(1, A_INNER))
    return h, c_new, n_new, m_new[:, :, 0, 0]


def _ssd_kernel(dtr_ref, dtc_ref, alr_ref, alc_ref, xs_ref, bm_ref, cm_ref, z_ref, gt_ref, gc_ref, s0_ref,
                cx0_ref, cb0_ref, cc0_ref, wx_ref, wb_ref, wc_ref, bx_ref, bb_ref, bc_ref, ng_ref, dsk_ref,
                y_ref, s_out_ref,
                cx_s, cb_s, cc_s, bufx, bufb, bufc, ybuf, *s_s, L, n_pad, gps):
    grp0 = pl.program_id(1) * gps
    c = pl.program_id(2)

    @pl.when(c == 0)
    def _():
        for pair in range(len(s_s)):
            s_s[pair][...] = s0_ref[0, pair]
        cx_s[...] = cx0_ref[0]
        cb_s[...] = cb0_ref[0]
        cc_s[...] = cc0_ref[0]

    rows = lax.broadcasted_iota(jnp.int32, (L, L), 0)
    cols = lax.broadcasted_iota(jnp.int32, (L, L), 1)
    causal = cols <= rows
    tril = causal.astype(F32)
    triu = (rows <= cols).astype(F32)
    lane_lo = lax.broadcasted_iota(jnp.int32, (L, 2 * B_P), 1) < B_P
    row_lo = lax.broadcasted_iota(jnp.int32, (2 * B_P, 1), 0) < B_P
    if n_pad:
        pad_row = (lax.broadcasted_iota(jnp.int32, (1, L), 1) < n_pad) & (c == 0)
        pad_col = (lax.broadcasted_iota(jnp.int32, (L, 1), 0) < n_pad) & (c == 0)

    xs_raw = xs_ref[...]
    bm_raw = bm_ref[...]
    cm_raw = cm_ref[...]
    if n_pad:
        xs_raw = jnp.where(pad_col, 0.0, xs_raw)
        bm_raw = jnp.where(pad_col, 0.0, bm_raw)
        cm_raw = jnp.where(pad_col, 0.0, cm_raw)
    xs = _silu(_causal_conv(xs_raw, cx_s, bufx, wx_ref, bx_ref, L))
    bm = _silu(_causal_conv(bm_raw, cb_s, bufb, wb_ref, bb_ref, L))
    cm = _silu(_causal_conv(cm_raw, cc_s, bufc, wc_ref, bc_ref, L))

    def two(col, h0):
        return jnp.where(lane_lo, col[:, h0:h0 + 1], col[:, h0 + 1:h0 + 2])

    for gg in range(gps):
        bmb = bm[:, gg * B_N:(gg + 1) * B_N].astype(BF16)
        cmb = cm[:, gg * B_N:(gg + 1) * B_N].astype(BF16)
        cb = lax.dot_general(cmb, bmb, (((1,), (1,)), ((), ())), preferred_element_type=F32)

        row0 = pl.multiple_of(2 * A_HEADS + (grp0 + gg) * B_HPG, B_HPG)
        dt_c = _softplus(gc_ref[0, gg] + dtr_ref[gg])
        dt_r = _softplus(gt_ref[0, pl.ds(row0, B_HPG), :] + dtc_ref[gg])
        if n_pad:
            dt_c = jnp.where(pad_col, 0.0, dt_c)
            dt_r = jnp.where(pad_row, 0.0, dt_r)
        a_c = dt_c * (-jnp.exp(alr_ref[gg]))
        a_r = dt_r * (-jnp.exp(alc_ref[gg]))
        b_c = jnp.dot(tril, a_c, preferred_element_type=F32, precision=lax.Precision.HIGHEST)
        b_r = jnp.dot(a_r, triu, preferred_element_type=F32, precision=lax.Precision.HIGHEST)
        eb_c = jnp.exp(b_c)
        b_last = b_c[L - 1:L, :]
        w_c = jnp.exp(b_last - b_c)
        e_last = jnp.exp(b_last)

        for p in range(B_HPG // 2):
            h0 = 2 * p
            pair = gg * (B_HPG // 2) + p
            x_pair = xs[:, pair * 2 * B_P:(pair + 1) * 2 * B_P]
            xdt = x_pair * two(dt_c, h0)
            y = None
            for hh, keep in ((h0, lane_lo), (h0 + 1, ~lane_lo)):
                decay = jnp.exp(jnp.where(causal, b_c[:, hh:hh + 1] - b_r[hh:hh + 1, :], NEG_BIG))
                part = _dot_nn((cb * decay).astype(BF16), jnp.where(keep, xdt, 0.0).astype(BF16))
                y = part if y is None else y + part
            s_prev = s_s[pair][...]
            y = y + two(eb_c, h0) * lax.dot_general(cmb, s_prev.astype(BF16), (((1,), (1,)), ((), ())),
                                                    preferred_element_type=F32)
            upd = lax.dot_general((xdt * two(w_c, h0)).astype(BF16), bmb, (((0,), (0,)), ((), ())),
                                  preferred_element_type=F32)
            s_new = jnp.where(row_lo, e_last[:, h0:h0 + 1], e_last[:, h0 + 1:h0 + 2]) * s_prev + upd
            s_s[pair][...] = s_new
            s_out_ref[0, pair] = s_new
            ybuf[:, pair * 2 * B_P:(pair + 1) * 2 * B_P] = y

    y = (ybuf[...] + dsk_ref[...] * xs) * _silu(z_ref[...])
    for gg in range(gps):
        yg = y[:, gg * B_GW:(gg + 1) * B_GW]
        yn = yg * lax.rsqrt(jnp.mean(yg * yg, axis=-1, keepdims=True) + 1e-6) * ng_ref[:, gg * B_GW:(gg + 1) * B_GW]
        y_ref[:, gg * B_GW:(gg + 1) * B_GW] = yn.astype(y_ref.dtype)


def _ssd(proj, gates_t, gates_c, dt_bias, a_log, d_skip, conv_w, conv_b, norm_g, s0, conv0, *,
         n_seq, n_chunks, L, row0, n_pad, name):
    rb0 = row0 // L

    def rblk(b, c):
        return rb0 + b * n_chunks + c

    out_dtype = BF16 if L % 16 == 0 else F32
    gps = B_GROUPS if n_chunks == 1 else 1
    xw, nw = gps * B_GW, gps * B_N
    n_pairs = gps * B_HPG // 2
    cw = conv_w
    cbias = conv_b.reshape(1, B_CONV_DIM)
    xoff = COL_XBC // xw
    boff = (COL_XBC + B_INNER) // nw
    coff = (COL_XBC + B_INNER + B_GROUPS * B_N) // nw
    cvb = B_INNER // nw
    cvc = (B_INNER + B_GROUPS * B_N) // nw
    dt_r = dt_bias.reshape(B_GROUPS, 1, B_HPG)
    dt_c = dt_bias.reshape(B_GROUPS, B_HPG, 1)
    al_r = a_log.reshape(B_GROUPS, 1, B_HPG)
    al_c = a_log.reshape(B_GROUPS, B_HPG, 1)
    d_row = jnp.repeat(d_skip, B_P).reshape(1, B_INNER)
    s0p = s0.reshape(n_seq, B_HEADS // 2, 2 * B_P, B_N)
    prow = pl.BlockSpec((gps, 1, B_HPG), lambda b, g, c: (g, 0, 0))
    pcol = pl.BlockSpec((gps, B_HPG, 1), lambda b, g, c: (g, 0, 0))
    kern = functools.partial(_ssd_kernel, L=L, n_pad=n_pad, gps=gps)
    y, s_new = pl.pallas_call(
        kern,
        grid=(n_seq, B_GROUPS // gps, n_chunks),
        in_specs=[
            prow, pcol, prow, pcol,
            pl.BlockSpec((L, xw), lambda b, g, c: (rblk(b, c), xoff + g)),
            pl.BlockSpec((L, nw), lambda b, g, c: (rblk(b, c), boff + g)),
            pl.BlockSpec((L, nw), lambda b, g, c: (rblk(b, c), coff + g)),
            pl.BlockSpec((L, xw), lambda b, g, c: (rblk(b, c), COL_Z // xw + g)),
            pl.BlockSpec((1, N_SMALL, L), lambda b, g, c: (b, 0, c)),
            pl.BlockSpec((1, gps, L, B_HPG), lambda b, g, c: (b, g, c, 0)),
            pl.BlockSpec((1, n_pairs, 2 * B_P, B_N), lambda b, g, c: (b, g, 0, 0)),
            pl.BlockSpec((1, 8, xw), lambda b, g, c: (b, 0, g)),
            pl.BlockSpec((1, 8, nw), lambda b, g, c: (b, 0, cvb + g)),
            pl.BlockSpec((1, 8, nw), lambda b, g, c: (b, 0, cvc + g)),
            pl.BlockSpec((CONV_W, xw), lambda b, g, c: (0, g)),
            pl.BlockSpec((CONV_W, nw), lambda b, g, c: (0, cvb + g)),
            pl.BlockSpec((CONV_W, nw), lambda b, g, c: (0, cvc + g)),
            pl.BlockSpec((1, xw), lambda b, g, c: (0, g)),
            pl.BlockSpec((1, nw), lambda b, g, c: (0, cvb + g)),
            pl.BlockSpec((1, nw), lambda b, g, c: (0, cvc + g)),
            pl.BlockSpec((1, xw), lambda b, g, c: (0, g)),
            pl.BlockSpec((1, xw), lambda b, g, c: (0, g)),
        ],
        out_specs=[
            pl.BlockSpec((L, xw), lambda b, g, c: (b * n_chunks + c, g)),
            pl.BlockSpec((1, n_pairs, 2 * B_P, B_N), lambda b, g, c: (b, g, 0, 0)),
        ],
        out_shape=[
            jax.ShapeDtypeStruct((n_seq * n_chunks * L, B_INNER), out_dtype),
            jax.ShapeDtypeStruct((n_seq, B_HEADS // 2, 2 * B_P, B_N), F32),
        ],
        scratch_shapes=[
            pltpu.VMEM((8, xw), F32), pltpu.VMEM((8, nw), F32), pltpu.VMEM((8, nw), F32),
            pltpu.VMEM((L + 8, xw), F32), pltpu.VMEM((L + 8, nw), F32), pltpu.VMEM((L + 8, nw), F32),
            pltpu.VMEM((L, xw), F32),
        ] + [pltpu.VMEM((2 * B_P, B_N), F32)] * n_pairs,
        compiler_params=_cparams(3),
        name=name,
    )(dt_r, dt_c, al_r, al_c, proj, proj, proj, proj, gates_t, gates_c, s0p, conv0, conv0, conv0,
      cw, cw, cw, cbias, cbias, cbias, norm_g.reshape(1, B_INNER), d_row)
    return y, s_new.reshape(n_seq, B_HEADS, B_P, B_N)


def _gelu_tanh(x):
    return 0.5 * x * (1.0 + jnp.tanh(math.sqrt(2.0 / math.pi) * (x + 0.044715 * (x * x * x))))


def _expm1_nonpos(z):
    e = jnp.exp(z)
    safe = (e < 1.0) & (z > -1.0)
    return jnp.where(safe, (e - 1.0) * z / jnp.log(jnp.where(safe, e, 0.5)), jnp.where(e == 1.0, z, e - 1.0))


def _lru_gates(xf, wg_ref, bg_ref, lam_ref):
    xfb = xf.astype(BF16)
    r = _sigmoid(jnp.dot(xfb, wg_ref[0, 0], preferred_element_type=F32) + bg_ref[0:1, :])
    i = _sigmoid(jnp.dot(xfb, wg_ref[1, 0], preferred_element_type=F32) + bg_ref[1:2, :])
    log_a = (-LRU_C) * r * _softplus(-lam_ref[...])
    a = jnp.exp(log_a)
    u = jnp.sqrt(-_expm1_nonpos(2.0 * log_a)) * (i * xf)
    return a, u


def _scan_rows8(a, u, t8):
    for k in (1, 2, 4):
        keep = t8 >= k
        a_sh = jnp.where(keep, pltpu.roll(a, k, 0), 1.0)
        u_sh = jnp.where(keep, pltpu.roll(u, k, 0), 0.0)
        u = a * u_sh + u
        a = a * a_sh
    return a, u


def _lru_kernel(gi_ref, xb_ref, h0_ref, cv0_ref, cw_ref, cb_ref, wg_ref, bg_ref, lam_ref,
                o_ref, h_out_ref, h_s, cv_s, buf, *, L, n_pad):
    c = pl.program_id(2)

    @pl.when(c == 0)
    def _():
        h_s[...] = h0_ref[0]
        cv_s[...] = cv0_ref[0]

    row_id = lax.broadcasted_iota(jnp.int32, (L, 1), 0)
    x_raw = xb_ref[...]
    if n_pad:
        pad_col = (row_id < n_pad) & (c == 0)
        x_raw = jnp.where(pad_col, 0.0, x_raw)
    xf = _causal_conv(x_raw, cv_s, buf, cw_ref, cb_ref, L)
    a, u = _lru_gates(xf, wg_ref, bg_ref, lam_ref)
    if n_pad:
        a = jnp.where(pad_col, 1.0, a)
        u = jnp.where(pad_col, 0.0, u)
    a, u = _scan_rows8(a, u, row_id % 8)
    h_in = h_s[...]
    pieces = []
    for t in range(L // 8):
        piece = a[t * 8:(t + 1) * 8, :] * h_in + u[t * 8:(t + 1) * 8, :]
        pieces.append(piece)
        h_in = piece[7:8, :]
    hs = jnp.concatenate(pieces, axis=0)
    h_s[...] = h_in
    h_out_ref[0] = h_in
    o_ref[...] = (hs * _gelu_tanh(gi_ref[...])).astype(o_ref.dtype)


def _lru_rows8_kernel(gi_ref, xb_ref, h0_ref, hist_ref, cw_ref, cb_ref, wg_ref, bg_ref, lam_ref,
                      o_ref, hs_ref):
    R = xb_ref.shape[0]
    t8 = lax.broadcasted_iota(jnp.int32, (R, 1), 0) % 8
    x = xb_ref[...]
    hist = hist_ref[...]
    acc = cb_ref[...] + cw_ref[CONV_W - 1:CONV_W, :] * x
    for j in range(CONV_W - 1):
        sh = CONV_W - 1 - j
        tap = jnp.where(t8 >= sh, pltpu.roll(x, sh, 0), pltpu.roll(hist, (sh - 8) % R, 0))
        acc = acc + cw_ref[j:j + 1, :] * tap
    a, u = _lru_gates(acc, wg_ref, bg_ref, lam_ref)
    a, u = _scan_rows8(a, u, t8)
    hs = a * h0_ref[...] + u
    hs_ref[...] = hs
    o_ref[...] = (hs * _gelu_tanh(gi_ref[...])).astype(o_ref.dtype)


def _lru_rows8(proj, conv_w, conv_b, wg_sb, b_gate, lam, h0, conv0, *, n_seq, row0, name):
    R = 128
    n_rows = n_seq * 8
    rb0 = row0 // R
    h0x = jnp.repeat(h0.reshape(n_seq, D_RNN), 8, axis=0)
    hist = conv0.reshape(n_rows, D_RNN)
    blk = lambda j, i: (i, j)
    return pl.pallas_call(
        _lru_rows8_kernel,
        grid=(N_LRU_SB, n_rows // R),
        in_specs=[
            pl.BlockSpec((R, LRU_SB), lambda j, i: (rb0 + i, j)),
            pl.BlockSpec((R, LRU_SB), lambda j, i: (rb0 + i, N_LRU_SB + j)),
            pl.BlockSpec((R, LRU_SB), blk),
            pl.BlockSpec((R, LRU_SB), blk),
            pl.BlockSpec((CONV_W, LRU_SB), lambda j, i: (0, j)),
            pl.BlockSpec((1, LRU_SB), lambda j, i: (0, j)),
            pl.BlockSpec((2, 1, LRU_SB, LRU_SB), lambda j, i: (0, j, 0, 0)),
            pl.BlockSpec((2, LRU_SB), lambda j, i: (0, j)),
            pl.BlockSpec((1, LRU_SB), lambda j, i: (0, j)),
        ],
        out_specs=[pl.BlockSpec((R, LRU_SB), blk), pl.BlockSpec((R, LRU_SB), blk)],
        out_shape=[jax.ShapeDtypeStruct((n_rows, D_RNN), BF16), jax.ShapeDtypeStruct((n_rows, D_RNN), F32)],
        compiler_params=_cparams(2),
        name=name,
    )(proj, proj, h0x, hist, conv_w, conv_b.reshape(1, D_RNN), wg_sb, b_gate, lam.reshape(1, D_RNN))


def _lru(proj, conv_w, conv_b, wg_sb, b_gate, lam, h0, conv0, *, n_seq, n_chunks, L, row0, n_pad, name):
    rb0 = row0 // L

    def rblk(b, c):
        return rb0 + b * n_chunks + c

    out_dtype = BF16 if L % 16 == 0 else F32
    kern = functools.partial(_lru_kernel, L=L, n_pad=n_pad)
    return pl.pallas_call(
        kern,
        grid=(n_seq, N_LRU_SB, n_chunks),
        in_specs=[
            pl.BlockSpec((L, LRU_SB), lambda b, j, c: (rblk(b, c), j)),
            pl.BlockSpec((L, LRU_SB), lambda b, j, c: (rblk(b, c), N_LRU_SB + j)),
            pl.BlockSpec((1, 1, LRU_SB), lambda b, j, c: (b, 0, j)),
            pl.BlockSpec((1, 8, LRU_SB), lambda b, j, c: (b, 0, j)),
            pl.BlockSpec((CONV_W, LRU_SB), lambda b, j, c: (0, j)),
            pl.BlockSpec((1, LRU_SB), lambda b, j, c: (0, j)),
            pl.BlockSpec((2, 1, LRU_SB, LRU_SB), lambda b, j, c: (0, j, 0, 0)),
            pl.BlockSpec((2, LRU_SB), lambda b, j, c: (0, j)),
            pl.BlockSpec((1, LRU_SB), lambda b, j, c: (0, j)),
        ],
        out_specs=[
            pl.BlockSpec((L, LRU_SB), lambda b, j, c: (b * n_chunks + c, j)),
            pl.BlockSpec((1, 1, LRU_SB), lambda b, j, c: (b, 0, j)),
        ],
        out_shape=[
            jax.ShapeDtypeStruct((n_seq * n_chunks * L, D_RNN), out_dtype),
            jax.ShapeDtypeStruct((n_seq, 1, D_RNN), F32),
        ],
        scratch_shapes=[pltpu.VMEM((1, LRU_SB), F32), pltpu.VMEM((8, LRU_SB), F32),
                        pltpu.VMEM((L + 8, LRU_SB), F32)],
        compiler_params=_cparams(3),
        name=name,
    )(proj, proj, h0, conv0, conv_w, conv_b.reshape(1, D_RNN), wg_sb, b_gate, lam.reshape(1, D_RNN))


def _route(logits):
    tm = logits.shape[0]
    lane = lax.broadcasted_iota(jnp.int32, (tm, 128), 1)
    lane_f = lane.astype(F32)
    is_group = (lane >= N_EXPERTS) & (lane < N_EXPERTS + N_EXP_GROUPS)
    gl = jnp.where(is_group, logits, -jnp.inf)
    g_max = jnp.max(gl, axis=1, keepdims=True)
    g_lane = jnp.min(jnp.where(gl == g_max, lane_f, 1e9), axis=1, keepdims=True)
    g_w = 1.0 / jnp.sum(jnp.exp(gl - g_max), axis=1, keepdims=True)
    g_idx = g_lane - float(N_EXPERTS)
    lo = g_idx * float(EXP_PER_GROUP)
    in_group = (lane_f >= lo) & (lane_f < lo + float(EXP_PER_GROUP))
    el = jnp.where(in_group, logits, -jnp.inf)
    e_max = jnp.max(el, axis=1, keepdims=True)
    i1 = jnp.min(jnp.where(el == e_max, lane_f, 1e9), axis=1, keepdims=True)
    e_sum = jnp.sum(jnp.exp(el - e_max), axis=1, keepdims=True)
    el2 = jnp.where(lane_f == i1, -jnp.inf, el)
    e2_max = jnp.max(el2, axis=1, keepdims=True)
    i2 = jnp.min(jnp.where(el2 == e2_max, lane_f, 1e9), axis=1, keepdims=True)
    p1 = 1.0 / e_sum
    p2 = jnp.exp(e2_max - e_max) / e_sum
    tot = p1 + p2
    gate1 = g_w * (p1 / tot)
    gate2 = g_w * (p2 / tot)
    gates = jnp.where(lane == 0, gate1, jnp.where(lane == 1, gate2, 0.0))
    ids = jnp.where(lane == 0, i1, jnp.where(lane == 1, i2, 0.0)).astype(jnp.int32)
    return gates, ids


def _ffn_kernel(te_ref, nv_ref, idx_ref, x_hbm, w1_ref, w3_ref, w2_ref, o_ref,
                w1b, w3b, w2b, xb_s, xbuf, sems):
    i = pl.program_id(0)
    tm = TILE_ROWS
    nv = nv_ref[0]
    n_slots = FFN_AHEAD + 1
    slot = i % n_slots

    def fetch(tile):
        sl = tile % n_slots
        _issue_gather(idx_ref, tile * tm, x_hbm, xbuf, sl * tm, sems.at[sl], tm)

    @pl.when(i == 0)
    def _():
        for t in range(FFN_AHEAD):
            @pl.when(t < nv)
            def _():
                fetch(t)

    @pl.when(i + FFN_AHEAD < nv)
    def _():
        fetch(i + FFN_AHEAD)

    prev = te_ref[jnp.maximum(i - 1, 0)]

    @pl.when((i == 0) | (te_ref[i] != prev))
    def _():
        w1b[...] = w1_ref[0, 0].astype(BF16)
        w3b[...] = w3_ref[0, 0].astype(BF16)
        w2b[...] = w2_ref[0, 0].astype(BF16)

    @pl.when(i < nv)
    def _():
        _wait_gather(x_hbm, xbuf, sems.at[slot], tm)
        base = slot * (tm * SLAB)
        for j in range(SLAB):
            xb_s[:, j * 128:(j + 1) * 128] = xbuf[pl.ds(base + j, tm, stride=SLAB), :].astype(BF16)
        xb = xb_s[...]
        h1 = jnp.dot(xb, w1b[...], preferred_element_type=F32)
        h3 = jnp.dot(xb, w3b[...], preferred_element_type=F32)
        hid = (_silu(h1) * h3).astype(BF16)
        _slab_store(o_ref, jnp.dot(hid, w2b[...], preferred_element_type=F32))

    @pl.when(i >= nv)
    def _():
        o_ref[...] = jnp.zeros_like(o_ref)


def _expert_ffn(x_slab, row_src, tile_expert, n_valid, w1, w3, w2, layer, name):
    R = row_src.shape[0]
    D = D_MODEL
    tm = TILE_ROWS
    wmap = lambda i, te, nv, idx: (layer, te[i], 0, 0)
    return pl.pallas_call(
        _ffn_kernel,
        grid_spec=pltpu.PrefetchScalarGridSpec(
            num_scalar_prefetch=3,
            grid=(R // tm,),
            in_specs=[
                pl.BlockSpec(memory_space=pl.ANY),
                pl.BlockSpec((1, 1, D, D_EXPERT), wmap),
                pl.BlockSpec((1, 1, D, D_EXPERT), wmap),
                pl.BlockSpec((1, 1, D_EXPERT, D), wmap),
            ],
            out_specs=pl.BlockSpec((tm * SLAB, 128), lambda i, te, nv, idx: (i, 0)),
            scratch_shapes=[pltpu.VMEM((D, D_EXPERT), BF16), pltpu.VMEM((D, D_EXPERT), BF16),
                            pltpu.VMEM((D_EXPERT, D), BF16), pltpu.VMEM((tm, D), BF16),
                            pltpu.VMEM(((FFN_AHEAD + 1) * tm * SLAB, 128), F32),
                            pltpu.SemaphoreType.DMA((FFN_AHEAD + 1,))],
        ),
        out_shape=jax.ShapeDtypeStruct((R * SLAB, 128), F32),
        compiler_params=_cparams(1),
        name=name,
    )(tile_expert, n_valid, row_src, x_slab, w1, w3, w2)


def _route_plan(ids, n_tok):
    tm = TILE_ROWS
    n_pairs = TOP_K * n_tok
    n_rows = n_pairs + N_EXPERTS * tm
    experts = jnp.arange(N_EXPERTS, dtype=jnp.int32)
    e_flat = ids.T.reshape(-1)
    order = jnp.argsort(e_flat, stable=True).astype(jnp.int32)
    rank_sorted = jnp.argsort(order).astype(jnp.int32)
    sizes = jnp.sum((e_flat[:, None] == experts[None, :]).astype(jnp.int32), axis=0)
    start = jnp.cumsum(sizes) - sizes
    psz = ((sizes + tm - 1) // tm) * tm
    pend = jnp.cumsum(psz)
    pstart = pend - psz
    shift = pstart - start
    pos = shift[e_flat] + rank_sorted
    rows = jnp.arange(n_rows, dtype=jnp.int32)
    tile_start = rows[::tm]
    e_tile = jnp.minimum(jnp.sum((tile_start[:, None] >= pend[None, :]).astype(jnp.int32), axis=1), N_EXPERTS - 1)
    pair_idx = rows - jnp.repeat(shift[e_tile], tm)
    valid = rows < jnp.repeat((pstart + sizes)[e_tile], tm)
    pair = order[jnp.clip(pair_idx, 0, n_pairs - 1)]
    row_src = jnp.where(valid, pair % n_tok, 0)
    last_e = jnp.max(jnp.where(sizes > 0, experts, 0))
    te = jnp.where(tile_start < pend[-1], e_tile, last_e)
    n_valid = (pend[-1] // tm).astype(jnp.int32)
    return row_src, pos, te, n_valid.reshape(1)


def _hier_moe_ln(x_slab, gates, ids, w1, w3, w2, layer, ln_g, ln_b, tag, split=None):
    n_tok = gates.shape[0]
    row_src, pos, te, n_valid = _route_plan(ids[:, :TOP_K], n_tok)
    ys = _expert_ffn(x_slab, row_src, te, n_valid, w1, w3, w2, layer, name=f"experts_{tag}")
    return _combine_ln(x_slab, ys, pos, gates, ln_g, ln_b, name=f"combine_ln_{tag}", split=split)


def _router_weights(w_group, b_group, w_expert, b_expert):
    w = jnp.concatenate([w_expert, w_group], axis=1)
    w = jnp.pad(w, ((0, 0), (0, 128 - w.shape[1])))
    b = jnp.pad(jnp.concatenate([b_expert, b_group]), (0, 128 - N_EXPERTS - N_EXP_GROUPS))
    return w.astype(BF16), b.reshape(1, 128)


def _pad_conv_state(conv):
    return jnp.pad(conv, ((0, 0), (8 - (CONV_W - 1), 0), (0, 0)))


def kernel(x_prompt, x_sample, state_mlstm_C, state_mlstm_n, state_mlstm_m, state_ssd, state_ssd_conv,
           state_lru_h, state_lru_conv, meta_tokens, w_in_even, mlstm_gate_b, ssd_dt_bias, ssd_A_log, ssd_D,
           ssd_conv_w, ssd_conv_b, mlstm_norm_g, ssd_norm_g, w_out_even, w_in_odd, lru_conv_w, lru_conv_b,
           lru_w_gate, lru_b_gate, lru_lambda, w_out_odd, ln_g, ln_b, moe_w_group, moe_b_group,
           moe_w_expert, moe_b_expert, moe_w1, moe_w3, moe_w2):
    Bp, Tp, D = x_prompt.shape
    Bs, Ts, _ = x_sample.shape
    n_chunks_p = (N_META + Tp + CHUNK - 1) // CHUNK
    Tpp = n_chunks_p * CHUNK
    n_pad = Tpp - N_META - Tp
    n_p = Bp * Tpp
    n_s = Bs * Ts
    n_tok = n_p + n_s

    pieces = []
    for b in range(Bp):
        pieces += [jnp.zeros((n_pad, D), F32), meta_tokens.astype(F32), x_prompt[b]]
    x0 = jnp.concatenate(pieces + [x_sample.reshape(n_s, D)], axis=0)

    grp = [dict(n_seq=Bp, n_chunks=n_chunks_p, L=CHUNK, row0=0, n_pad=n_pad),
           dict(n_seq=Bs, n_chunks=1, L=Ts, row0=n_p, n_pad=0)]

    def seq_view(a, gi):
        if gi == 0:
            return a[:n_p].reshape(Bp, Tpp, a.shape[1])
        return a[n_p:].reshape(Bs, Ts, a.shape[1])

    def tail_rows(a, gi, col0, ncol):
        nb, T, base = (Bp, Tpp, 0) if gi == 0 else (Bs, Ts, n_p)
        seqs = a[base:base + nb * T].reshape(nb, T, a.shape[1])
        return seqs[:, T - (CONV_W - 1):, col0:col0 + ncol]

    e = 0
    w_all = _w_relayout(w_in_even[e])
    proj = _mm(x0, w_all, 512, IN0_PAD // 5, name="in_proj_even", n_cols=IN0_PAD)

    small = proj[:, COL_SMALL:COL_SMALL + N_SMALL]
    h_parts, y_parts, st = [], [], []
    for gi, g in enumerate(grp):
        sv = seq_view(small, gi)
        gates_t = jnp.swapaxes(sv, 1, 2)
        nb, T = sv.shape[0], sv.shape[1]
        gates_c = sv[:, :, 2 * A_HEADS:].reshape(nb, T, B_GROUPS, B_HPG).transpose(0, 2, 1, 3)
        if gi == 0:
            c0 = jnp.zeros((Bp, A_HEADS, A_DK, A_DV), F32)
            n0 = jnp.zeros((Bp, A_HEADS, 1, A_DK), F32)
            m0 = jnp.zeros((Bp, A_HEADS), F32)
            s0 = jnp.zeros((Bp, B_HEADS, B_P, B_N), F32)
            cv0 = jnp.zeros((Bp, 8, B_CONV_DIM), F32)
        else:
            c0 = state_mlstm_C[e]
            n0 = state_mlstm_n[e][:, :, None, :]
            m0 = state_mlstm_m[e]
            s0 = state_ssd[e]
            cv0 = _pad_conv_state(state_ssd_conv[e])
        h_g, c_g, n_g, m_g = _mlstm(proj, gates_t, mlstm_gate_b[e], mlstm_norm_g[e], c0, n0, m0,
                                    name=f"mlstm_{gi}", **g)
        y_g, s_g = _ssd(proj, gates_t, gates_c, ssd_dt_bias[e], ssd_A_log[e], ssd_D[e], ssd_conv_w[e],
                        ssd_conv_b[e], ssd_norm_g[e], s0, cv0, name=f"ssd_{gi}", **g)
        h_parts.append(h_g.astype(BF16))
        y_parts.append(y_g.astype(BF16))
        sconv = tail_rows(proj, gi, COL_XBC, B_CONV_DIM)
        st.append((c_g[None], n_g[:, :, 0, :][None], m_g[None], s_g[None], sconv[None]))
    w_r, b_r = _router_weights(moe_w_group[0], moe_b_group[0], moe_w_expert[0], moe_b_expert[0])
    x1s, gates, ids = _mm_parts_ln([h_parts[0], y_parts[0]], [h_parts[1], y_parts[1]],
                                   w_out_even[e].astype(BF16), x0, ln_g[0, 0], ln_b[0, 0], w_r, b_r,
                                   name="out_proj_ln_even")
    x2, x2b = _hier_moe_ln(x1s, gates, ids, moe_w1, moe_w3, moe_w2, 0, ln_g[0, 1], ln_b[0, 1], "0")

    o = 0
    proj1 = _mm(x2b, w_in_odd[o].astype(BF16), 512, D_RNN, name="in_proj_odd")
    wg = lru_w_gate[o].reshape(2, N_LRU_SB, LRU_SB // LRU_BW, LRU_BW, LRU_BW)
    eye4 = jnp.eye(LRU_SB // LRU_BW, dtype=F32)
    wg_sb = jnp.einsum('gjaik,ab->gjaibk', wg, eye4).reshape(2, N_LRU_SB, LRU_SB, LRU_SB).astype(BF16)
    o_parts, st1 = [], []
    for gi, g in enumerate(grp):
        if gi == 0:
            h0 = jnp.zeros((Bp, 1, D_RNN), F32)
            cv0 = jnp.zeros((Bp, 8, D_RNN), F32)
        else:
            h0 = state_lru_h[o][:, None, :]
            cv0 = _pad_conv_state(state_lru_conv[o])
        if g["L"] == 8 and g["n_chunks"] == 1:
            o_g, hs = _lru_rows8(proj1, lru_conv_w[o], lru_conv_b[o], wg_sb, lru_b_gate[o], lru_lambda[o], h0, cv0,
                                 n_seq=g["n_seq"], row0=g["row0"], name=f"lru_{gi}")
            hN = hs.reshape(g["n_seq"], 8, D_RNN)[:, 7, :]
        else:
            o_g, hN = _lru(proj1, lru_conv_w[o], lru_conv_b[o], wg_sb, lru_b_gate[o], lru_lambda[o], h0, cv0,
                           name=f"lru_{gi}", **g)
            hN = hN[:, 0, :]
        o_parts.append(o_g.astype(BF16))
        hconv = tail_rows(proj1, gi, D_RNN, D_RNN)
        st1.append((hN[None], hconv[None]))
    w_r, b_r = _router_weights(moe_w_group[1], moe_b_group[1], moe_w_expert[1], moe_b_expert[1])
    x3s, gates, ids = _mm_parts_ln([o_parts[0]], [o_parts[1]], w_out_odd[o].astype(BF16),
                                   x2, ln_g[1, 0], ln_b[1, 0], w_r, b_r, name="out_proj_ln_odd")
    assert Tpp - Tp == CHUNK
    yp, ys = _hier_moe_ln(x3s, gates, ids, moe_w1, moe_w3, moe_w2, 1, ln_g[1, 1], ln_b[1, 1], "1",
                          split=(n_p, n_chunks_p))
    y_prompt = yp.reshape(Bp, Tp, D)
    y_sample = ys.reshape(Bs, Ts, D)
    (pC, pn, pm, pS, pSc), (sC, sn, sm, sS, sSc) = st
    (pH, pHc), (sH, sHc) = st1
    return (y_prompt, y_sample, pC, pn, pm, pS, pSc, pH, pHc, sC, sn, sm, sS, sSc, sH, sHc)
```

```python
import functools
import math

import jax
import jax.numpy as jnp
import numpy as np
from jax import lax
from jax.experimental import pallas as pl
from jax.experimental.pallas import tpu as pltpu

F32 = jnp.float32
BF16 = jnp.bfloat16

D_MODEL = 2048
N_META = 16
CHUNK = 128
CONV_W = 4
A_HEADS = 8
A_DK = 128
A_DV = 256
A_INNER = A_HEADS * A_DV
B_HEADS = 32
B_P = 64
B_N = 128
B_GROUPS = 4
B_HPG = B_HEADS // B_GROUPS
B_INNER = B_HEADS * B_P
B_GW = B_INNER // B_GROUPS
B_CONV_DIM = B_INNER + 2 * B_GROUPS * B_N
D_RNN = 2560
LRU_BLOCKS = 16
LRU_BW = D_RNN // LRU_BLOCKS
LRU_C = 8.0
LRU_SB = 640
N_LRU_SB = D_RNN // LRU_SB
N_EXP_GROUPS = 4
EXP_PER_GROUP = 8
N_EXPERTS = N_EXP_GROUPS * EXP_PER_GROUP
TOP_K = 2
D_EXPERT = 512
DEPTH = 2
ALPHA = (2.0 * DEPTH) ** 0.25

COL_Q = 0
COL_K = A_HEADS * A_DK
COL_V = 2 * A_HEADS * A_DK
COL_O = COL_V + A_INNER
COL_Z = COL_O + A_INNER
COL_XBC = COL_Z + B_INNER
COL_SMALL = COL_XBC + B_CONV_DIM
N_SMALL = 2 * A_HEADS + B_HEADS
IN0_PAD = 11520

NEG_BIG = -1e30
VMEM_LIMIT_BYTES = 48 * 1024 * 1024
TILE_ROWS = 256
FFN_AHEAD = 2
LN_ROWS = 256
SLAB = D_MODEL // 128


def _cparams(n_axes):
    return pltpu.CompilerParams(dimension_semantics=("arbitrary",) * n_axes,
                                vmem_limit_bytes=VMEM_LIMIT_BYTES)


def _smem_spec():
    return pl.BlockSpec(memory_space=pltpu.SMEM)


def _mm_kernel(x_ref, w_ref, o_ref):
    o_ref[...] = jnp.dot(x_ref[...].astype(BF16), w_ref[...], preferred_element_type=F32).astype(o_ref.dtype)


def _mm(x, w, bm, bn, name, n_cols=None):
    M, K = x.shape
    N = w.shape[1] if n_cols is None else n_cols
    return pl.pallas_call(
        _mm_kernel,
        grid=(N // bn, M // bm),
        in_specs=[pl.BlockSpec((bm, K), lambda j, i: (i, 0)),
                  pl.BlockSpec((K, bn), lambda j, i: (0, j))],
        out_specs=pl.BlockSpec((bm, bn), lambda j, i: (i, j)),
        out_shape=jax.ShapeDtypeStruct((M, N), F32),
        compiler_params=_cparams(2),
        name=name,
    )(x, w)


RELAY_W = 1024


def _w_relayout_kernel(a_ref, b_ref, g_ref, o_ref, *, n_same, n_shift, shift):
    j = pl.program_id(0)
    lane = lax.broadcasted_iota(jnp.int32, (1, 128), 1)
    n_t = RELAY_W // 128

    @pl.when(j < n_same)
    def _():
        o_ref[...] = a_ref[...].astype(BF16)

    @pl.when((j >= n_same) & (j < n_same + n_shift))
    def _():
        cur = pltpu.roll(a_ref[:, 0:128], 128 - shift, 1)
        for t in range(n_t):
            nxt_src = a_ref[:, (t + 1) * 128:(t + 2) * 128] if t + 1 < n_t else b_ref[...]
            nxt = pltpu.roll(nxt_src, 128 - shift, 1)
            o_ref[:, t * 128:(t + 1) * 128] = jnp.where(lane < 128 - shift, cur, nxt).astype(BF16)
            cur = nxt

    @pl.when(j == n_same + n_shift)
    def _():
        head = jnp.where(lane < 2 * A_HEADS, g_ref[...], jnp.where(lane < N_SMALL, a_ref[:, 0:128], 0.0))
        o_ref[:, 0:128] = head.astype(BF16)
        o_ref[:, 128:] = jnp.zeros((o_ref.shape[0], RELAY_W - 128), BF16)


def _w_relayout(w):
    K = w.shape[0]
    src_small = 2 * A_HEADS * A_DK + 2 * A_INNER
    n_same = src_small // RELAY_W
    n_shift = (B_INNER + B_CONV_DIM) // RELAY_W
    n_blk = n_same + n_shift + 1
    assert src_small % RELAY_W == 0 and (B_INNER + B_CONV_DIM) % RELAY_W == 0 and n_blk * RELAY_W >= IN0_PAD
    assert COL_SMALL == (n_blk - 1) * RELAY_W and w.shape[1] == COL_SMALL + N_SMALL
    per = RELAY_W // 128
    return pl.pallas_call(
        functools.partial(_w_relayout_kernel, n_same=n_same, n_shift=n_shift, shift=2 * A_HEADS),
        grid=(n_blk,),
        in_specs=[pl.BlockSpec((K, RELAY_W), lambda j: (0, j)),
                  pl.BlockSpec((K, 128), lambda j: (0, jnp.minimum(j + 1, n_blk - 1) * per)),
                  pl.BlockSpec((K, 128), lambda j: (0, src_small // 128))],
        out_specs=pl.BlockSpec((K, RELAY_W), lambda j: (0, j)),
        out_shape=jax.ShapeDtypeStruct((K, n_blk * RELAY_W), BF16),
        compiler_params=_cparams(1),
        name="w_in_even_relayout",
    )(w, w, w)


def _layer_norm_rows(y, g, b):
    mu = jnp.mean(y, axis=-1, keepdims=True)
    yc = y - mu
    var = jnp.mean(yc * yc, axis=-1, keepdims=True)
    return yc * lax.rsqrt(var + 1e-5) * g + b


def _mm_parts_ln_kernel(*refs, k_sizes, n_blk_a):
    n = len(k_sizes)
    a_refs, b_refs = refs[:n], refs[n:2 * n]
    w_ref, x_ref, g_ref, b_ref, wr_ref, br_ref, os_ref, gate_ref, id_ref = refs[2 * n:]
    i = pl.program_id(0)

    def run(lhs_refs):
        acc, k0 = None, 0
        for r, ks in zip(lhs_refs, k_sizes):
            part = jnp.dot(r[...], w_ref[k0:k0 + ks, :], preferred_element_type=F32)
            acc = part if acc is None else acc + part
            k0 += ks
        y = _layer_norm_rows(ALPHA * x_ref[...] + acc, g_ref[...], b_ref[...])
        _slab_store(os_ref, y)
        logits = jnp.dot(y.astype(BF16), wr_ref[...], preferred_element_type=F32) + br_ref[...]
        gate_ref[...], id_ref[...] = _route(logits)

    @pl.when(i < n_blk_a)
    def _():
        run(a_refs)

    @pl.when(i >= n_blk_a)
    def _():
        run(b_refs)


def _mm_parts_ln(parts_a, parts_b, w, x, g, b, w_r, b_r, name):
    bm = LN_ROWS
    k_sizes = tuple(p.shape[1] for p in parts_a)
    n_blk_a = parts_a[0].shape[0] // bm
    n_blk_b = parts_b[0].shape[0] // bm
    K, D = w.shape
    M = (n_blk_a + n_blk_b) * bm
    a_specs = [pl.BlockSpec((bm, ks), lambda i: (jnp.minimum(i, n_blk_a - 1), 0)) for ks in k_sizes]
    b_specs = [pl.BlockSpec((bm, ks), lambda i: (jnp.maximum(i - n_blk_a, 0), 0)) for ks in k_sizes]
    row = pl.BlockSpec((bm, D), lambda i: (i, 0))
    vec = pl.BlockSpec((1, D), lambda i: (0, 0))
    tile = pl.BlockSpec((bm, 128), lambda i: (i, 0))
    w_spec = pl.BlockSpec((K, D), lambda i: (0, 0), pipeline_mode=pl.Buffered(1))
    return pl.pallas_call(
        functools.partial(_mm_parts_ln_kernel, k_sizes=k_sizes, n_blk_a=n_blk_a),
        grid=(n_blk_a + n_blk_b,),
        in_specs=a_specs + b_specs + [w_spec, row, vec, vec, pl.BlockSpec((D, 128), lambda i: (0, 0)),
                                      pl.BlockSpec((1, 128), lambda i: (0, 0))],
        out_specs=[pl.BlockSpec((bm * SLAB, 128), lambda i: (i, 0)), tile, tile],
        out_shape=[jax.ShapeDtypeStruct((M * SLAB, 128), F32), jax.ShapeDtypeStruct((M, 128), F32),
                   jax.ShapeDtypeStruct((M, 128), jnp.int32)],
        compiler_params=_cparams(1),
        name=name,
    )(*parts_a, *parts_b, w, x, g.reshape(1, D), b.reshape(1, D), w_r, b_r)


def _slab_store(ref, val):
    tm = val.shape[0]
    for j in range(SLAB):
        ref[pl.ds(j, tm, stride=SLAB), :] = val[:, j * 128:(j + 1) * 128]


def _slab_piece(ref, j, tm):
    return ref[pl.ds(j, tm, stride=SLAB), :]


def _slab_copy(src_hbm, dst_vmem, sem, src_tok, dst_tok):
    return pltpu.make_async_copy(src_hbm.at[pl.ds(pl.multiple_of(src_tok * SLAB, SLAB), SLAB), :],
                                 dst_vmem.at[pl.ds(pl.multiple_of(dst_tok * SLAB, SLAB), SLAB), :], sem)


def _issue_gather(idx_ref, idx0, src_hbm, dst_vmem, dst0, sem, n):
    def body(r, carry):
        _slab_copy(src_hbm, dst_vmem, sem, idx_ref[idx0 + r], dst0 + r).start()
        return carry
    lax.fori_loop(0, n, body, 0, unroll=8)


def _wait_gather(src_hbm, dst_vmem, sem, n):
    def body(r, carry):
        _slab_copy(src_hbm, dst_vmem, sem, 0, 0).wait()
        return carry
    lax.fori_loop(0, n, body, 0, unroll=8)


def _combine_rows(pos_ref, x_ref, y_hbm, gate_ref, g_ref, b_ref, v_s, ybuf, sems, n_tok, tm):
    i = pl.program_id(0)
    n = pl.num_programs(0)
    slot = i % 2

    def fetch(tile, sl):
        for kk in range(TOP_K):
            _issue_gather(pos_ref, kk * n_tok + tile * tm, y_hbm, ybuf, (sl * TOP_K + kk) * tm, sems.at[sl], tm)

    @pl.when(i == 0)
    def _():
        fetch(0, 0)

    @pl.when(i + 1 < n)
    def _():
        fetch(i + 1, 1 - slot)

    _wait_gather(y_hbm, ybuf, sems.at[slot], TOP_K * tm)
    gate = gate_ref[...]
    g0 = gate[:, 0:1]
    g1 = gate[:, 1:2]
    base0 = slot * (TOP_K * tm * SLAB)
    base1 = base0 + tm * SLAB
    for j in range(SLAB):
        v_s[:, j * 128:(j + 1) * 128] = (ALPHA * _slab_piece(x_ref, j, tm)
                                         + g0 * ybuf[pl.ds(base0 + j, tm, stride=SLAB), :]
                                         + g1 * ybuf[pl.ds(base1 + j, tm, stride=SLAB), :])
    return _layer_norm_rows(v_s[...], g_ref[...], b_ref[...])


def _combine_ln_kernel(pos_ref, x_ref, y_hbm, gate_ref, g_ref, b_ref, o_ref, ob_ref, v_s, ybuf, sems, *, n_tok):
    y = _combine_rows(pos_ref, x_ref, y_hbm, gate_ref, g_ref, b_ref, v_s, ybuf, sems, n_tok, LN_ROWS)
    o_ref[...] = y
    ob_ref[...] = y.astype(BF16)


def _combine_out_kernel(pos_ref, x_ref, y_hbm, gate_ref, g_ref, b_ref, yp_ref, ys_ref, v_s, ybuf, sems, *,
                        n_tok, n_blk_p, blk_per_seq):
    y = _combine_rows(pos_ref, x_ref, y_hbm, gate_ref, g_ref, b_ref, v_s, ybuf, sems, n_tok, CHUNK)
    i = pl.program_id(0)

    @pl.when((i < n_blk_p) & (i % blk_per_seq > 0))
    def _():
        yp_ref[...] = y

    @pl.when(i >= n_blk_p)
    def _():
        ys_ref[...] = y


def _combine_ln(x_slab, y_slab, pos, gates, g, b, name, split=None):
    M = gates.shape[0]
    D = D_MODEL
    tm = LN_ROWS if split is None else CHUNK
    vec = pl.BlockSpec((1, D), lambda i, pos: (0, 0))
    if split is None:
        kern = functools.partial(_combine_ln_kernel, n_tok=M)
        row = pl.BlockSpec((tm, D), lambda i, pos: (i, 0))
        out_specs = [row, row]
        out_shape = [jax.ShapeDtypeStruct((M, D), F32), jax.ShapeDtypeStruct((M, D), BF16)]
    else:
        n_p, bps = split
        n_blk_p = n_p // tm
        kern = functools.partial(_combine_out_kernel, n_tok=M, n_blk_p=n_blk_p, blk_per_seq=bps)

        def yp_map(i, pos):
            ip = jnp.minimum(i, n_blk_p - 1)
            return ((ip // bps) * (bps - 1) + jnp.maximum(ip % bps - 1, 0), 0)

        out_specs = [pl.BlockSpec((tm, D), yp_map),
                     pl.BlockSpec((tm, D), lambda i, pos: (jnp.maximum(i - n_blk_p, 0), 0))]
        out_shape = [jax.ShapeDtypeStruct((n_blk_p // bps * (bps - 1) * tm, D), F32),
                     jax.ShapeDtypeStruct((M - n_p, D), F32)]
    return pl.pallas_call(
        kern,
        grid_spec=pltpu.PrefetchScalarGridSpec(
            num_scalar_prefetch=1,
            grid=(M // tm,),
            in_specs=[pl.BlockSpec((tm * SLAB, 128), lambda i, pos: (i, 0)),
                      pl.BlockSpec(memory_space=pl.ANY),
                      pl.BlockSpec((tm, 128), lambda i, pos: (i, 0)), vec, vec],
            out_specs=out_specs,
            scratch_shapes=[pltpu.VMEM((tm, D), F32), pltpu.VMEM((2 * TOP_K * tm * SLAB, 128), F32),
                            pltpu.SemaphoreType.DMA((2,))],
        ),
        out_shape=out_shape,
        compiler_params=_cparams(1),
        name=name,
    )(pos, x_slab, y_slab, gates, g.reshape(1, D), b.reshape(1, D))


def _softplus(x):
    return jnp.maximum(x, 0.0) + jnp.log1p(jnp.exp(-jnp.abs(x)))


def _sigmoid(x):
    return 1.0 / (1.0 + jnp.exp(-x))


def _silu(x):
    return x * _sigmoid(x)


def _dot_nn(m, x):
    return jnp.dot(m, x, preferred_element_type=F32)


def _dot_nt(a, b):
    return lax.dot_general(a, b, (((1,), (1,)), ((), ())), preferred_element_type=F32)


def _row_to_col(row, eye):
    return jnp.sum(jnp.where(eye, row, 0.0), axis=1, keepdims=True)


def _causal_conv(x, carry_ref, buf_ref, w_ref, b_ref, L):
    buf_ref[0:8, :] = carry_ref[...]
    buf_ref[8:8 + L, :] = x
    acc = b_ref[...] + w_ref[0:1, :] * buf_ref[5:5 + L, :]
    for j in range(1, CONV_W):
        acc = acc + w_ref[j:j + 1, :] * buf_ref[5 + j:5 + j + L, :]
    carry_ref[...] = buf_ref[L:L + 8, :]
    return acc


def _mlstm_kernel(gb_ref, q_ref, k_ref, v_ref, o_ref, gt_ref, c0_ref, n0_ref, m0_ref, ng_ref,
                  h_ref, c_out_ref, n_out_ref, m_out_ref, *state, L, n_pad):
    c_s, n_s, m_s = state[:A_HEADS], state[A_HEADS:2 * A_HEADS], state[2 * A_HEADS:]
    c = pl.program_id(1)

    @pl.when(c == 0)
    def _():
        for hd in range(A_HEADS):
            c_s[hd][...] = c0_ref[0, hd]
            n_s[hd][...] = n0_ref[0, hd]
            m_s[hd][...] = m0_ref[0, hd]

    rows = lax.broadcasted_iota(jnp.int32, (L, L), 0)
    cols = lax.broadcasted_iota(jnp.int32, (L, L), 1)
    eye = rows == cols
    causal = cols <= rows

    if n_pad:
        pad = (lax.broadcasted_iota(jnp.int32, (1, L), 1) < n_pad) & (c == 0)

    for hd in range(A_HEADS):
        li = gt_ref[0, hd:hd + 1, :] + gb_ref[0, hd]
        fr = gt_ref[0, A_HEADS + hd:A_HEADS + hd + 1, :] + gb_ref[1, hd]
        lf = jnp.minimum(fr, 0.0) - jnp.log1p(jnp.exp(-jnp.abs(fr)))
        if n_pad:
            li = jnp.where(pad, NEG_BIG, li)
            lf = jnp.where(pad, 0.0, lf)

        lf_col = _row_to_col(lf, eye)
        b_col = jnp.sum(jnp.where(causal, lf, 0.0), axis=1, keepdims=True)
        b_row = jnp.sum(jnp.where(rows <= cols, lf_col, 0.0), axis=0, keepdims=True)
        m_prev = m_s[hd][:, 0:1]

        d = jnp.where(causal, b_col - b_row + li, NEG_BIG)
        inter = b_col + m_prev
        m_t = jnp.maximum(inter, jnp.max(d, axis=1, keepdims=True))
        w_intra = jnp.exp(d - m_t)
        w_inter = jnp.exp(inter - m_t)

        q = q_ref[:, hd * A_DK:(hd + 1) * A_DK] * (A_DK ** -0.5)
        k = k_ref[:, hd * A_DK:(hd + 1) * A_DK]
        v = v_ref[:, hd * A_DV:(hd + 1) * A_DV]
        qb = q.astype(BF16)
        kb = k.astype(BF16)
        vb = v.astype(BF16)
        s = _dot_nt(qb, kb) * w_intra
        c_prev = c_s[hd][...]
        n_prev = n_s[hd][...]
        num = _dot_nn(s.astype(BF16), vb)
        num = num + w_inter * jnp.dot(qb, c_prev.astype(BF16), preferred_element_type=F32)
        qn = jnp.sum(qb.astype(F32) * n_prev.astype(BF16).astype(F32), axis=1, keepdims=True)
        nq = jnp.sum(s, axis=1, keepdims=True) + w_inter * qn
        hh = num / jnp.maximum(jnp.abs(nq), jnp.exp(-m_t))

        hn = hh * lax.rsqrt(jnp.mean(hh * hh, axis=-1, keepdims=True) + 1e-6)
        hn = hn * ng_ref[:, hd * A_DV:(hd + 1) * A_DV]
        h_ref[:, hd * A_DV:(hd + 1) * A_DV] = (
            hn * _sigmoid(o_ref[:, hd * A_DV:(hd + 1) * A_DV])).astype(h_ref.dtype)

        b_last = b_row[:, L - 1:L]
        ws_log = b_last - b_row + li
        m_new = jnp.maximum(b_last + m_prev, jnp.max(ws_log, axis=1, keepdims=True))
        ws_col = _row_to_col(jnp.exp(ws_log - m_new), eye)
        wc = jnp.exp(b_last + m_prev - m_new)
        kw = k * ws_col
        c_new = wc * c_prev + lax.dot_general(kw.astype(BF16), vb, (((0,), (0,)), ((), ())),
                                              preferred_element_type=F32)
        n_new = wc * n_prev + jnp.sum(ws_col.astype(BF16).astype(F32) * kb.astype(F32), axis=0, keepdims=True)
        m_new_b = jnp.broadcast_to(m_new, (1, 128))
        c_s[hd][...] = c_new
        n_s[hd][...] = n_new
        m_s[hd][...] = m_new_b
        c_out_ref[0, hd] = c_new
        n_out_ref[0, hd] = n_new
        m_out_ref[0, hd] = m_new_b


def _mlstm(proj, gates_t, gate_b, norm_g, c0, n0, m0, *, n_seq, n_chunks, L, row0, n_pad, name):
    rb0 = row0 // L

    def rblk(b, c):
        return rb0 + b * n_chunks + c

    out_dtype = BF16 if L % 16 == 0 else F32
    qk_w = A_HEADS * A_DK
    state_map = lambda b, c: (b, 0, 0, 0)
    m0b = jnp.broadcast_to(m0[:, :, None, None], (n_seq, A_HEADS, 1, 128))
    kern = functools.partial(_mlstm_kernel, L=L, n_pad=n_pad)
    h, c_new, n_new, m_new = pl.pallas_call(
        kern,
        grid=(n_seq, n_chunks),
        in_specs=[
            _smem_spec(),
            pl.BlockSpec((L, qk_w), lambda b, c: (rblk(b, c), COL_Q // qk_w)),
            pl.BlockSpec((L, qk_w), lambda b, c: (rblk(b, c), COL_K // qk_w)),
            pl.BlockSpec((L, A_INNER), lambda b, c: (rblk(b, c), COL_V // A_INNER)),
            pl.BlockSpec((L, A_INNER), lambda b, c: (rblk(b, c), COL_O // A_INNER)),
            pl.BlockSpec((1, N_SMALL, L), lambda b, c: (b, 0, c)),
            pl.BlockSpec((1, A_HEADS, A_DK, A_DV), state_map),
            pl.BlockSpec((1, A_HEADS, 1, A_DK), state_map),
            pl.BlockSpec((1, A_HEADS, 1, 128), state_map),
            pl.BlockSpec((1, A_INNER), lambda b, c: (0, 0)),
        ],
        out_specs=[
            pl.BlockSpec((L, A_INNER), lambda b, c: (b * n_chunks + c, 0)),
            pl.BlockSpec((1, A_HEADS, A_DK, A_DV), state_map),
            pl.BlockSpec((1, A_HEADS, 1, A_DK), state_map),
            pl.BlockSpec((1, A_HEADS, 1, 128), state_map),
        ],
        out_shape=[
            jax.ShapeDtypeStruct((n_seq * n_chunks * L, A_INNER), out_dtype),
            jax.ShapeDtypeStruct((n_seq, A_HEADS, A_DK, A_DV), F32),
            jax.ShapeDtypeStruct((n_seq, A_HEADS, 1, A_DK), F32),
            jax.ShapeDtypeStruct((n_seq, A_HEADS, 1, 128), F32),
        ],
        scratch_shapes=([pltpu.VMEM((A_DK, A_DV), F32)] * A_HEADS + [pltpu.VMEM((1, A_DK), F32)] * A_HEADS
                        + [pltpu.VMEM((1, 128), F32)] * A_HEADS),
        compiler_params=_cparams(2),
        name=name,
    )(gate_b, proj, proj, proj, proj, gates_t, c0, n0, m0b, norm_g.reshape(1, A_INNER))
    return h, c_new, n_new, m_new[:, :, 0, 0]


def _ssd_kernel(dtr_ref, dtc_ref, alr_ref, alc_ref, xs_ref, bm_ref, cm_ref, z_ref, gt_ref, gc_ref, s0_ref,
                cx0_ref, cb0_ref, cc0_ref, wx_ref, wb_ref, wc_ref, bx_ref, bb_ref, bc_ref, ng_ref, dsk_ref,
                y_ref, s_out_ref,
                cx_s, cb_s, cc_s, bufx, bufb, bufc, ybuf, *s_s, L, n_pad, gps):
    grp0 = pl.program_id(1) * gps
    c = pl.program_id(2)

    @pl.when(c == 0)
    def _():
        for pair in range(len(s_s)):
            s_s[pair][...] = s0_ref[0, pair]
        cx_s[...] = cx0_ref[0]
        cb_s[...] = cb0_ref[0]
        cc_s[...] = cc0_ref[0]

    rows = lax.broadcasted_iota(jnp.int32, (L, L), 0)
    cols = lax.broadcasted_iota(jnp.int32, (L, L), 1)
    causal = cols <= rows
    tril = causal.astype(F32)
    triu = (rows <= cols).astype(F32)
    lane_lo = lax.broadcasted_iota(jnp.int32, (L, 2 * B_P), 1) < B_P
    row_lo = lax.broadcasted_iota(jnp.int32, (2 * B_P, 1), 0) < B_P
    if n_pad:
        pad_row = (lax.broadcasted_iota(jnp.int32, (1, L), 1) < n_pad) & (c == 0)
        pad_col = (lax.broadcasted_iota(jnp.int32, (L, 1), 0) < n_pad) & (c == 0)

    xs_raw = xs_ref[...]
    bm_raw = bm_ref[...]
    cm_raw = cm_ref[...]
    if n_pad:
        xs_raw = jnp.where(pad_col, 0.0, xs_raw)
        bm_raw = jnp.where(pad_col, 0.0, bm_raw)
        cm_raw = jnp.where(pad_col, 0.0, cm_raw)
    xs = _silu(_causal_conv(xs_raw, cx_s, bufx, wx_ref, bx_ref, L))
    bm = _silu(_causal_conv(bm_raw, cb_s, bufb, wb_ref, bb_ref, L))
    cm = _silu(_causal_conv(cm_raw, cc_s, bufc, wc_ref, bc_ref, L))

    def two(col, h0):
        return jnp.where(lane_lo, col[:, h0:h0 + 1], col[:, h0 + 1:h0 + 2])

    for gg in range(gps):
        bmb = bm[:, gg * B_N:(gg + 1) * B_N].astype(BF16)
        cmb = cm[:, gg * B_N:(gg + 1) * B_N].astype(BF16)
        cb = lax.dot_general(cmb, bmb, (((1,), (1,)), ((), ())), preferred_element_type=F32)

        row0 = pl.multiple_of(2 * A_HEADS + (grp0 + gg) * B_HPG, B_HPG)
        dt_c = _softplus(gc_ref[0, gg] + dtr_ref[gg])
        dt_r = _softplus(gt_ref[0, pl.ds(row0, B_HPG), :] + dtc_ref[gg])
        if n_pad:
            dt_c = jnp.where(pad_col, 0.0, dt_c)
            dt_r = jnp.where(pad_row, 0.0, dt_r)
        a_c = dt_c * (-jnp.exp(alr_ref[gg]))
        a_r = dt_r * (-jnp.exp(alc_ref[gg]))
        b_c = jnp.dot(tril, a_c, preferred_element_type=F32, precision=lax.Precision.HIGHEST)
        b_r = jnp.dot(a_r, triu, preferred_element_type=F32, precision=lax.Precision.HIGHEST)
        eb_c = jnp.exp(b_c)
        b_last = b_c[L - 1:L, :]
        w_c = jnp.exp(b_last - b_c)
        e_last = jnp.exp(b_last)

        for p in range(B_HPG // 2):
            h0 = 2 * p
            pair = gg * (B_HPG // 2) + p
            x_pair = xs[:, pair * 2 * B_P:(pair + 1) * 2 * B_P]
            xdt = x_pair * two(dt_c, h0)
            y = None
            for hh, keep in ((h0, lane_lo), (h0 + 1, ~lane_lo)):
                decay = jnp.exp(jnp.where(causal, b_c[:, hh:hh + 1] - b_r[hh:hh + 1, :], NEG_BIG))
                part = _dot_nn((cb * decay).astype(BF16), jnp.where(keep, xdt, 0.0).astype(BF16))
                y = part if y is None else y + part
            s_prev = s_s[pair][...]
            y = y + two(eb_c, h0) * lax.dot_general(cmb, s_prev.astype(BF16), (((1,), (1,)), ((), ())),
                                                    preferred_element_type=F32)
            upd = lax.dot_general((xdt * two(w_c, h0)).astype(BF16), bmb, (((0,), (0,)), ((), ())),
                                  preferred_element_type=F32)
            s_new = jnp.where(row_lo, e_last[:, h0:h0 + 1], e_last[:, h0 + 1:h0 + 2]) * s_prev + upd
            s_s[pair][...] = s_new
            s_out_ref[0, pair] = s_new
            ybuf[:, pair * 2 * B_P:(pair + 1) * 2 * B_P] = y

    y = (ybuf[...] + dsk_ref[...] * xs) * _silu(z_ref[...])
    for gg in range(gps):
        yg = y[:, gg * B_GW:(gg + 1) * B_GW]
        yn = yg * lax.rsqrt(jnp.mean(yg * yg, axis=-1, keepdims=True) + 1e-6) * ng_ref[:, gg * B_GW:(gg + 1) * B_GW]
        y_ref[:, gg * B_GW:(gg + 1) * B_GW] = yn.astype(y_ref.dtype)


def _ssd(proj, gates_t, gates_c, dt_bias, a_log, d_skip, conv_w, conv_b, norm_g, s0, conv0, *,
         n_seq, n_chunks, L, row0, n_pad, name):
    rb0 = row0 // L

    def rblk(b, c):
        return rb0 + b * n_chunks + c

    out_dtype = BF16 if L % 16 == 0 else F32
    gps = B_GROUPS if n_chunks == 1 else 1
    xw, nw = gps * B_GW, gps * B_N
    n_pairs = gps * B_HPG // 2
    cw = conv_w
    cbias = conv_b.reshape(1, B_CONV_DIM)
    xoff = COL_XBC // xw
    boff = (COL_XBC + B_INNER) // nw
    coff = (COL_XBC + B_INNER + B_GROUPS * B_N) // nw
    cvb = B_INNER // nw
    cvc = (B_INNER + B_GROUPS * B_N) // nw
    dt_r = dt_bias.reshape(B_GROUPS, 1, B_HPG)
    dt_c = dt_bias.reshape(B_GROUPS, B_HPG, 1)
    al_r = a_log.reshape(B_GROUPS, 1, B_HPG)
    al_c = a_log.reshape(B_GROUPS, B_HPG, 1)
    d_row = jnp.repeat(d_skip, B_P).reshape(1, B_INNER)
    s0p = s0.reshape(n_seq, B_HEADS // 2, 2 * B_P, B_N)
    prow = pl.BlockSpec((gps, 1, B_HPG), lambda b, g, c: (g, 0, 0))
    pcol = pl.BlockSpec((gps, B_HPG, 1), lambda b, g, c: (g, 0, 0))
    kern = functools.partial(_ssd_kernel, L=L, n_pad=n_pad, gps=gps)
    y, s_new = pl.pallas_call(
        kern,
        grid=(n_seq, B_GROUPS // gps, n_chunks),
        in_specs=[
            prow, pcol, prow, pcol,
            pl.BlockSpec((L, xw), lambda b, g, c: (rblk(b, c), xoff + g)),
            pl.BlockSpec((L, nw), lambda b, g, c: (rblk(b, c), boff + g)),
            pl.BlockSpec((L, nw), lambda b, g, c: (rblk(b, c), coff + g)),
            pl.BlockSpec((L, xw), lambda b, g, c: (rblk(b, c), COL_Z // xw + g)),
            pl.BlockSpec((1, N_SMALL, L), lambda b, g, c: (b, 0, c)),
            pl.BlockSpec((1, gps, L, B_HPG), lambda b, g, c: (b, g, c, 0)),
            pl.BlockSpec((1, n_pairs, 2 * B_P, B_N), lambda b, g, c: (b, g, 0, 0)),
            pl.BlockSpec((1, 8, xw), lambda b, g, c: (b, 0, g)),
            pl.BlockSpec((1, 8, nw), lambda b, g, c: (b, 0, cvb + g)),
            pl.BlockSpec((1, 8, nw), lambda b, g, c: (b, 0, cvc + g)),
            pl.BlockSpec((CONV_W, xw), lambda b, g, c: (0, g)),
            pl.BlockSpec((CONV_W, nw), lambda b, g, c: (0, cvb + g)),
            pl.BlockSpec((CONV_W, nw), lambda b, g, c: (0, cvc + g)),
            pl.BlockSpec((1, xw), lambda b, g, c: (0, g)),
            pl.BlockSpec((1, nw), lambda b, g, c: (0, cvb + g)),
            pl.BlockSpec((1, nw), lambda b, g, c: (0, cvc + g)),
            pl.BlockSpec((1, xw), lambda b, g, c: (0, g)),
            pl.BlockSpec((1, xw), lambda b, g, c: (0, g)),
        ],
        out_specs=[
            pl.BlockSpec((L, xw), lambda b, g, c: (b * n_chunks + c, g)),
            pl.BlockSpec((1, n_pairs, 2 * B_P, B_N), lambda b, g, c: (b, g, 0, 0)),
        ],
        out_shape=[
            jax.ShapeDtypeStruct((n_seq * n_chunks * L, B_INNER), out_dtype),
            jax.ShapeDtypeStruct((n_seq, B_HEADS // 2, 2 * B_P, B_N), F32),
        ],
        scratch_shapes=[
            pltpu.VMEM((8, xw), F32), pltpu.VMEM((8, nw), F32), pltpu.VMEM((8, nw), F32),
            pltpu.VMEM((L + 8, xw), F32), pltpu.VMEM((L + 8, nw), F32), pltpu.VMEM((L + 8, nw), F32),
            pltpu.VMEM((L, xw), F32),
        ] + [pltpu.VMEM((2 * B_P, B_N), F32)] * n_pairs,
        compiler_params=_cparams(3),
        name=name,
    )(dt_r, dt_c, al_r, al_c, proj, proj, proj, proj, gates_t, gates_c, s0p, conv0, conv0, conv0,
      cw, cw, cw, cbias, cbias, cbias, norm_g.reshape(1, B_INNER), d_row)
    return y, s_new.reshape(n_seq, B_HEADS, B_P, B_N)


def _gelu_tanh(x):
    return 0.5 * x * (1.0 + jnp.tanh(math.sqrt(2.0 / math.pi) * (x + 0.044715 * (x * x * x))))


def _expm1_nonpos(z):
    e = jnp.exp(z)
    safe = (e < 1.0) & (z > -1.0)
    return jnp.where(safe, (e - 1.0) * z / jnp.log(jnp.where(safe, e, 0.5)), jnp.where(e == 1.0, z, e - 1.0))


def _lru_gates(xf, wg_ref, bg_ref, lam_ref):
    xfb = xf.astype(BF16)
    r = _sigmoid(jnp.dot(xfb, wg_ref[0, 0], preferred_element_type=F32) + bg_ref[0:1, :])
    i = _sigmoid(jnp.dot(xfb, wg_ref[1, 0], preferred_element_type=F32) + bg_ref[1:2, :])
    log_a = (-LRU_C) * r * _softplus(-lam_ref[...])
    a = jnp.exp(log_a)
    u = jnp.sqrt(-_expm1_nonpos(2.0 * log_a)) * (i * xf)
    return a, u


def _scan_rows8(a, u, t8):
    for k in (1, 2, 4):
        keep = t8 >= k
        a_sh = jnp.where(keep, pltpu.roll(a, k, 0), 1.0)
        u_sh = jnp.where(keep, pltpu.roll(u, k, 0), 0.0)
        u = a * u_sh + u
        a = a * a_sh
    return a, u


def _lru_kernel(gi_ref, xb_ref, h0_ref, cv0_ref, cw_ref, cb_ref, wg_ref, bg_ref, lam_ref,
                o_ref, h_out_ref, h_s, cv_s, buf, *, L, n_pad):
    c = pl.program_id(2)

    @pl.when(c == 0)
    def _():
        h_s[...] = h0_ref[0]
        cv_s[...] = cv0_ref[0]

    row_id = lax.broadcasted_iota(jnp.int32, (L, 1), 0)
    x_raw = xb_ref[...]
    if n_pad:
        pad_col = (row_id < n_pad) & (c == 0)
        x_raw = jnp.where(pad_col, 0.0, x_raw)
    xf = _causal_conv(x_raw, cv_s, buf, cw_ref, cb_ref, L)
    a, u = _lru_gates(xf, wg_ref, bg_ref, lam_ref)
    if n_pad:
        a = jnp.where(pad_col, 1.0, a)
        u = jnp.where(pad_col, 0.0, u)
    a, u = _scan_rows8(a, u, row_id % 8)
    h_in = h_s[...]
    pieces = []
    for t in range(L // 8):
        piece = a[t * 8:(t + 1) * 8, :] * h_in + u[t * 8:(t + 1) * 8, :]
        pieces.append(piece)
        h_in = piece[7:8, :]
    hs = jnp.concatenate(pieces, axis=0)
    h_s[...] = h_in
    h_out_ref[0] = h_in
    o_ref[...] = (hs * _gelu_tanh(gi_ref[...])).astype(o_ref.dtype)


def _lru_rows8_kernel(gi_ref, xb_ref, h0_ref, hist_ref, cw_ref, cb_ref, wg_ref, bg_ref, lam_ref,
                      o_ref, hs_ref):
    R = xb_ref.shape[0]
    t8 = lax.broadcasted_iota(jnp.int32, (R, 1), 0) % 8
    x = xb_ref[...]
    hist = hist_ref[...]
    acc = cb_ref[...] + cw_ref[CONV_W - 1:CONV_W, :] * x
    for j in range(CONV_W - 1):
        sh = CONV_W - 1 - j
        tap = jnp.where(t8 >= sh, pltpu.roll(x, sh, 0), pltpu.roll(hist, (sh - 8) % R, 0))
        acc = acc + cw_ref[j:j + 1, :] * tap
    a, u = _lru_gates(acc, wg_ref, bg_ref, lam_ref)
    a, u = _scan_rows8(a, u, t8)
    hs = a * h0_ref[...] + u
    hs_ref[...] = hs
    o_ref[...] = (hs * _gelu_tanh(gi_ref[...])).astype(o_ref.dtype)


def _lru_rows8(proj, conv_w, conv_b, wg_sb, b_gate, lam, h0, conv0, *, n_seq, row0, name):
    R = 128
    n_rows = n_seq * 8
    rb0 = row0 // R
    h0x = jnp.repeat(h0.reshape(n_seq, D_RNN), 8, axis=0)
    hist = conv0.reshape(n_rows, D_RNN)
    blk = lambda j, i: (i, j)
    return pl.pallas_call(
        _lru_rows8_kernel,
        grid=(N_LRU_SB, n_rows // R),
        in_specs=[
            pl.BlockSpec((R, LRU_SB), lambda j, i: (rb0 + i, j)),
            pl.BlockSpec((R, LRU_SB), lambda j, i: (rb0 + i, N_LRU_SB + j)),
            pl.BlockSpec((R, LRU_SB), blk),
            pl.BlockSpec((R, LRU_SB), blk),
            pl.BlockSpec((CONV_W, LRU_SB), lambda j, i: (0, j)),
            pl.BlockSpec((1, LRU_SB), lambda j, i: (0, j)),
            pl.BlockSpec((2, 1, LRU_SB, LRU_SB), lambda j, i: (0, j, 0, 0)),
            pl.BlockSpec((2, LRU_SB), lambda j, i: (0, j)),
            pl.BlockSpec((1, LRU_SB), lambda j, i: (0, j)),
        ],
        out_specs=[pl.BlockSpec((R, LRU_SB), blk), pl.BlockSpec((R, LRU_SB), blk)],
        out_shape=[jax.ShapeDtypeStruct((n_rows, D_RNN), BF16), jax.ShapeDtypeStruct((n_rows, D_RNN), F32)],
        compiler_params=_cparams(2),
        name=name,
    )(proj, proj, h0x, hist, conv_w, conv_b.reshape(1, D_RNN), wg_sb, b_gate, lam.reshape(1, D_RNN))


def _lru(proj, conv_w, conv_b, wg_sb, b_gate, lam, h0, conv0, *, n_seq, n_chunks, L, row0, n_pad, name):
    rb0 = row0 // L

    def rblk(b, c):
        return rb0 + b * n_chunks + c

    out_dtype = BF16 if L % 16 == 0 else F32
    kern = functools.partial(_lru_kernel, L=L, n_pad=n_pad)
    return pl.pallas_call(
        kern,
        grid=(n_seq, N_LRU_SB, n_chunks),
        in_specs=[
            pl.BlockSpec((L, LRU_SB), lambda b, j, c: (rblk(b, c), j)),
            pl.BlockSpec((L, LRU_SB), lambda b, j, c: (rblk(b, c), N_LRU_SB + j)),
            pl.BlockSpec((1, 1, LRU_SB), lambda b, j, c: (b, 0, j)),
            pl.BlockSpec((1, 8, LRU_SB), lambda b, j, c: (b, 0, j)),
            pl.BlockSpec((CONV_W, LRU_SB), lambda b, j, c: (0, j)),
            pl.BlockSpec((1, LRU_SB), lambda b, j, c: (0, j)),
            pl.BlockSpec((2, 1, LRU_SB, LRU_SB), lambda b, j, c: (0, j, 0, 0)),
            pl.BlockSpec((2, LRU_SB), lambda b, j, c: (0, j)),
            pl.BlockSpec((1, LRU_SB), lambda b, j, c: (0, j)),
        ],
        out_specs=[
            pl.BlockSpec((L, LRU_SB), lambda b, j, c: (b * n_chunks + c, j)),
            pl.BlockSpec((1, 1, LRU_SB), lambda b, j, c: (b, 0, j)),
        ],
        out_shape=[
            jax.ShapeDtypeStruct((n_seq * n_chunks * L, D_RNN), out_dtype),
            jax.ShapeDtypeStruct((n_seq, 1, D_RNN), F32),
        ],
        scratch_shapes=[pltpu.VMEM((1, LRU_SB), F32), pltpu.VMEM((8, LRU_SB), F32),
                        pltpu.VMEM((L + 8, LRU_SB), F32)],
        compiler_params=_cparams(3),
        name=name,
    )(proj, proj, h0, conv0, conv_w, conv_b.reshape(1, D_RNN), wg_sb, b_gate, lam.reshape(1, D_RNN))


def _route(logits):
    tm = logits.shape[0]
    lane = lax.broadcasted_iota(jnp.int32, (tm, 128), 1)
    lane_f = lane.astype(F32)
    is_group = (lane >= N_EXPERTS) & (lane < N_EXPERTS + N_EXP_GROUPS)
    gl = jnp.where(is_group, logits, -jnp.inf)
    g_max = jnp.max(gl, axis=1, keepdims=True)
    g_lane = jnp.min(jnp.where(gl == g_max, lane_f, 1e9), axis=1, keepdims=True)
    g_w = 1.0 / jnp.sum(jnp.exp(gl - g_max), axis=1, keepdims=True)
    g_idx = g_lane - float(N_EXPERTS)
    lo = g_idx * float(EXP_PER_GROUP)
    in_group = (lane_f >= lo) & (lane_f < lo + float(EXP_PER_GROUP))
    el = jnp.where(in_group, logits, -jnp.inf)
    e_max = jnp.max(el, axis=1, keepdims=True)
    i1 = jnp.min(jnp.where(el == e_max, lane_f, 1e9), axis=1, keepdims=True)
    e_sum = jnp.sum(jnp.exp(el - e_max), axis=1, keepdims=True)
    el2 = jnp.where(lane_f == i1, -jnp.inf, el)
    e2_max = jnp.max(el2, axis=1, keepdims=True)
    i2 = jnp.min(jnp.where(el2 == e2_max, lane_f, 1e9), axis=1, keepdims=True)
    p1 = 1.0 / e_sum
    p2 = jnp.exp(e2_max - e_max) / e_sum
    tot = p1 + p2
    gate1 = g_w * (p1 / tot)
    gate2 = g_w * (p2 / tot)
    gates = jnp.where(lane == 0, gate1, jnp.where(lane == 1, gate2, 0.0))
    ids = jnp.where(lane == 0, i1, jnp.where(lane == 1, i2, 0.0)).astype(jnp.int32)
    return gates, ids


def _ffn_kernel(te_ref, nv_ref, idx_ref, x_hbm, w1_ref, w3_ref, w2_ref, o_ref,
                w1b, w3b, w2b, xb_s, xbuf, sems):
    i = pl.program_id(0)
    tm = TILE_ROWS
    nv = nv_ref[0]
    n_slots = FFN_AHEAD + 1
    slot = i % n_slots

    def fetch(tile):
        sl = tile % n_slots
        _issue_gather(idx_ref, tile * tm, x_hbm, xbuf, sl * tm, sems.at[sl], tm)

    @pl.when(i == 0)
    def _():
        for t in range(FFN_AHEAD):
            @pl.when(t < nv)
            def _():
                fetch(t)

    @pl.when(i + FFN_AHEAD < nv)
    def _():
        fetch(i + FFN_AHEAD)

    prev = te_ref[jnp.maximum(i - 1, 0)]

    @pl.when((i == 0) | (te_ref[i] != prev))
    def _():
        w1b[...] = w1_ref[0, 0].astype(BF16)
        w3b[...] = w3_ref[0, 0].astype(BF16)
        w2b[...] = w2_ref[0, 0].astype(BF16)

    @pl.when(i < nv)
    def _():
        _wait_gather(x_hbm, xbuf, sems.at[slot], tm)
        base = slot * (tm * SLAB)
        for j in range(SLAB):
            xb_s[:, j * 128:(j + 1) * 128] = xbuf[pl.ds(base + j, tm, stride=SLAB), :].astype(BF16)
        xb = xb_s[...]
        h1 = jnp.dot(xb, w1b[...], preferred_element_type=F32)
        h3 = jnp.dot(xb, w3b[...], preferred_element_type=F32)
        hid = (_silu(h1) * h3).astype(BF16)
        _slab_store(o_ref, jnp.dot(hid, w2b[...], preferred_element_type=F32))

    @pl.when(i >= nv)
    def _():
        o_ref[...] = jnp.zeros_like(o_ref)


def _expert_ffn(x_slab, row_src, tile_expert, n_valid, w1, w3, w2, layer, name):
    R = row_src.shape[0]
    D = D_MODEL
    tm = TILE_ROWS
    wmap = lambda i, te, nv, idx: (layer, te[i], 0, 0)
    return pl.pallas_call(
        _ffn_kernel,
        grid_spec=pltpu.PrefetchScalarGridSpec(
            num_scalar_prefetch=3,
            grid=(R // tm,),
            in_specs=[
                pl.BlockSpec(memory_space=pl.ANY),
                pl.BlockSpec((1, 1, D, D_EXPERT), wmap),
                pl.BlockSpec((1, 1, D, D_EXPERT), wmap),
                pl.BlockSpec((1, 1, D_EXPERT, D), wmap),
            ],
            out_specs=pl.BlockSpec((tm * SLAB, 128), lambda i, te, nv, idx: (i, 0)),
            scratch_shapes=[pltpu.VMEM((D, D_EXPERT), BF16), pltpu.VMEM((D, D_EXPERT), BF16),
                            pltpu.VMEM((D_EXPERT, D), BF16), pltpu.VMEM((tm, D), BF16),
                            pltpu.VMEM(((FFN_AHEAD + 1) * tm * SLAB, 128), F32),
                            pltpu.SemaphoreType.DMA((FFN_AHEAD + 1,))],
        ),
        out_shape=jax.ShapeDtypeStruct((R * SLAB, 128), F32),
        compiler_params=_cparams(1),
        name=name,
    )(tile_expert, n_valid, row_src, x_slab, w1, w3, w2)


def _route_plan(ids, n_tok):
    tm = TILE_ROWS
    n_pairs = TOP_K * n_tok
    n_rows = n_pairs + N_EXPERTS * tm
    experts = jnp.arange(N_EXPERTS, dtype=jnp.int32)
    e_flat = ids.T.reshape(-1)
    order = jnp.argsort(e_flat, stable=True).astype(jnp.int32)
    rank_sorted = jnp.argsort(order).astype(jnp.int32)
    sizes = jnp.sum((e_flat[:, None] == experts[None, :]).astype(jnp.int32), axis=0)
    start = jnp.cumsum(sizes) - sizes
    psz = ((sizes + tm - 1) // tm) * tm
    pend = jnp.cumsum(psz)
    pstart = pend - psz
    shift = pstart - start
    pos = shift[e_flat] + rank_sorted
    rows = jnp.arange(n_rows, dtype=jnp.int32)
    tile_start = rows[::tm]
    e_tile = jnp.minimum(jnp.sum((tile_start[:, None] >= pend[None, :]).astype(jnp.int32), axis=1), N_EXPERTS - 1)
    pair_idx = rows - jnp.repeat(shift[e_tile], tm)
    valid = rows < jnp.repeat((pstart + sizes)[e_tile], tm)
    pair = order[jnp.clip(pair_idx, 0, n_pairs - 1)]
    row_src = jnp.where(valid, pair % n_tok, 0)
    last_e = jnp.max(jnp.where(sizes > 0, experts, 0))
    te = jnp.where(tile_start < pend[-1], e_tile, last_e)
    n_valid = (pend[-1] // tm).astype(jnp.int32)
    return row_src, pos, te, n_valid.reshape(1)


def _hier_moe_ln(x_slab, gates, ids, w1, w3, w2, layer, ln_g, ln_b, tag, split=None):
    n_tok = gates.shape[0]
    row_src, pos, te, n_valid = _route_plan(ids[:, :TOP_K], n_tok)
    ys = _expert_ffn(x_slab, row_src, te, n_valid, w1, w3, w2, layer, name=f"experts_{tag}")
    return _combine_ln(x_slab, ys, pos, gates, ln_g, ln_b, name=f"combine_ln_{tag}", split=split)


def _router_weights(w_group, b_group, w_expert, b_expert):
    w = jnp.concatenate([w_expert, w_group], axis=1)
    w = jnp.pad(w, ((0, 0), (0, 128 - w.shape[1])))
    b = jnp.pad(jnp.concatenate([b_expert, b_group]), (0, 128 - N_EXPERTS - N_EXP_GROUPS))
    return w.astype(BF16), b.reshape(1, 128)


def _pad_conv_state(conv):
    return jnp.pad(conv, ((0, 0), (8 - (CONV_W - 1), 0), (0, 0)))


def kernel(x_prompt, x_sample, state_mlstm_C, state_mlstm_n, state_mlstm_m, state_ssd, state_ssd_conv,
           state_lru_h, state_lru_conv, meta_tokens, w_in_even, mlstm_gate_b, ssd_dt_bias, ssd_A_log, ssd_D,
           ssd_conv_w, ssd_conv_b, mlstm_norm_g, ssd_norm_g, w_out_even, w_in_odd, lru_conv_w, lru_conv_b,
           lru_w_gate, lru_b_gate, lru_lambda, w_out_odd, ln_g, ln_b, moe_w_group, moe_b_group,
           moe_w_expert, moe_b_expert, moe_w1, moe_w3, moe_w2):
    Bp, Tp, D = x_prompt.shape
    Bs, Ts, _ = x_sample.shape
    n_chunks_p = (N_META + Tp + CHUNK - 1) // CHUNK
    Tpp = n_chunks_p * CHUNK
    n_pad = Tpp - N_META - Tp
    n_p = Bp * Tpp
    n_s = Bs * Ts
    n_tok = n_p + n_s

    pieces = []
    for b in range(Bp):
        pieces += [jnp.zeros((n_pad, D), F32), meta_tokens.astype(F32), x_prompt[b]]
    x0 = jnp.concatenate(pieces + [x_sample.reshape(n_s, D)], axis=0)

    grp = [dict(n_seq=Bp, n_chunks=n_chunks_p, L=CHUNK, row0=0, n_pad=n_pad),
           dict(n_seq=Bs, n_chunks=1, L=Ts, row0=n_p, n_pad=0)]

    def seq_view(a, gi):
        if gi == 0:
            return a[:n_p].reshape(Bp, Tpp, a.shape[1])
        return a[n_p:].reshape(Bs, Ts, a.shape[1])

    def tail_rows(a, gi, col0, ncol):
        nb, T, base = (Bp, Tpp, 0) if gi == 0 else (Bs, Ts, n_p)
        idx = (base + np.arange(nb)[:, None] * T + np.arange(T - (CONV_W - 1), T)[None, :]).reshape(-1)
        starts = np.stack([idx, np.full_like(idx, col0)], axis=1)
        dnums = lax.GatherDimensionNumbers(offset_dims=(1,), collapsed_slice_dims=(0,), start_index_map=(0, 1))
        rows = lax.gather(a, jnp.asarray(starts, jnp.int32), dnums, slice_sizes=(1, ncol))
        return rows.reshape(nb, CONV_W - 1, ncol)

    e = 0
    w_all = _w_relayout(w_in_even[e])
    proj = _mm(x0, w_all, 512, IN0_PAD // 5, name="in_proj_even", n_cols=IN0_PAD)

    small = proj[:, COL_SMALL:COL_SMALL + N_SMALL]
    h_parts, y_parts, st = [], [], []
    for gi, g in enumerate(grp):
        sv = seq_view(small, gi)
        gates_t = jnp.swapaxes(sv, 1, 2)
        nb, T = sv.shape[0], sv.shape[1]
        gates_c = sv[:, :, 2 * A_HEADS:].reshape(nb, T, B_GROUPS, B_HPG).transpose(0, 2, 1, 3)
        if gi == 0:
            c0 = jnp.zeros((Bp, A_HEADS, A_DK, A_DV), F32)
            n0 = jnp.zeros((Bp, A_HEADS, 1, A_DK), F32)
            m0 = jnp.zeros((Bp, A_HEADS), F32)
            s0 = jnp.zeros((Bp, B_HEADS, B_P, B_N), F32)
            cv0 = jnp.zeros((Bp, 8, B_CONV_DIM), F32)
        else:
            c0 = state_mlstm_C[e]
            n0 = state_mlstm_n[e][:, :, None, :]
            m0 = state_mlstm_m[e]
            s0 = state_ssd[e]
            cv0 = _pad_conv_state(state_ssd_conv[e])
        h_g, c_g, n_g, m_g = _mlstm(proj, gates_t, mlstm_gate_b[e], mlstm_norm_g[e], c0, n0, m0,
                                    name=f"mlstm_{gi}", **g)
        y_g, s_g = _ssd(proj, gates_t, gates_c, ssd_dt_bias[e], ssd_A_log[e], ssd_D[e], ssd_conv_w[e],
                        ssd_conv_b[e], ssd_norm_g[e], s0, cv0, name=f"ssd_{gi}", **g)
        h_parts.append(h_g.astype(BF16))
        y_parts.append(y_g.astype(BF16))
        sconv = tail_rows(proj, gi, COL_XBC, B_CONV_DIM)
        st.append((c_g[None], n_g[:, :, 0, :][None], m_g[None], s_g[None], sconv[None]))
    w_r, b_r = _router_weights(moe_w_group[0], moe_b_group[0], moe_w_expert[0], moe_b_expert[0])
    x1s, gates, ids = _mm_parts_ln([h_parts[0], y_parts[0]], [h_parts[1], y_parts[1]],
                                   w_out_even[e].astype(BF16), x0, ln_g[0, 0], ln_b[0, 0], w_r, b_r,
                                   name="out_proj_ln_even")
    x2, x2b = _hier_moe_ln(x1s, gates, ids, moe_w1, moe_w3, moe_w2, 0, ln_g[0, 1], ln_b[0, 1], "0")

    o = 0
    proj1 = _mm(x2b, w_in_odd[o].astype(BF16), 512, D_RNN, name="in_proj_odd")
    wg = lru_w_gate[o].reshape(2, N_LRU_SB, LRU_SB // LRU_BW, LRU_BW, LRU_BW)
    eye4 = jnp.eye(LRU_SB // LRU_BW, dtype=F32)
    wg_sb = jnp.einsum('gjaik,ab->gjaibk', wg, eye4).reshape(2, N_LRU_SB, LRU_SB, LRU_SB).astype(BF16)
    o_parts, st1 = [], []
    for gi, g in enumerate(grp):
        if gi == 0:
            h0 = jnp.zeros((Bp, 1, D_RNN), F32)
            cv0 = jnp.zeros((Bp, 8, D_RNN), F32)
        else:
            h0 = state_lru_h[o][:, None, :]
            cv0 = _pad_conv_state(state_lru_conv[o])
        if g["L"] == 8 and g["n_chunks"] == 1:
            o_g, hs = _lru_rows8(proj1, lru_conv_w[o], lru_conv_b[o], wg_sb, lru_b_gate[o], lru_lambda[o], h0, cv0,
                                 n_seq=g["n_seq"], row0=g["row0"], name=f"lru_{gi}")
            hN = hs.reshape(g["n_seq"], 8, D_RNN)[:, 7, :]
        else:
            o_g, hN = _lru(proj1, lru_conv_w[o], lru_conv_b[o], wg_sb, lru_b_gate[o], lru_lambda[o], h0, cv0,
                           name=f"lru_{gi}", **g)
            hN = hN[:, 0, :]
        o_parts.append(o_g.astype(BF16))
        hconv = tail_rows(proj1, gi, D_RNN, D_RNN)
        st1.append((hN[None], hconv[None]))
    w_r, b_r = _router_weights(moe_w_group[1], moe_b_group[1], moe_w_expert[1], moe_b_expert[1])
    x3s, gates, ids = _mm_parts_ln([o_parts[0]], [o_parts[1]], w_out_odd[o].astype(BF16),
                                   x2, ln_g[1, 0], ln_b[1, 0], w_r, b_r, name="out_proj_ln_odd")
    assert Tpp - Tp == CHUNK
    yp, ys = _hier_moe_ln(x3s, gates, ids, moe_w1, moe_w3, moe_w2, 1, ln_g[1, 1], ln_b[1, 1], "1",
                          split=(n_p, n_chunks_p))
    y_prompt = yp.reshape(Bp, Tp, D)
    y_sample = ys.reshape(Bs, Ts, D)
    (pC, pn, pm, pS, pSc), (sC, sn, sm, sS, sSc) = st
    (pH, pHc), (sH, sHc) = st1
    return (y_prompt, y_sample, pC, pn, pm, pS, pSc, pH, pHc, sC, sn, sm, sS, sSc, sH, sHc)
```

```python
import functools
import math

import jax
import jax.numpy as jnp
import numpy as np
from jax import lax
from jax.experimental import pallas as pl
from jax.experimental.pallas import tpu as pltpu

F32 = jnp.float32
BF16 = jnp.bfloat16

D_MODEL = 2048
N_META = 16
CHUNK = 128
CONV_W = 4
A_HEADS = 8
A_DK = 128
A_DV = 256
A_INNER = A_HEADS * A_DV
B_HEADS = 32
B_P = 64
B_N = 128
B_GROUPS = 4
B_HPG = B_HEADS // B_GROUPS
B_INNER = B_HEADS * B_P
B_GW = B_INNER // B_GROUPS
B_CONV_DIM = B_INNER + 2 * B_GROUPS * B_N
D_RNN = 2560
LRU_BLOCKS = 16
LRU_BW = D_RNN // LRU_BLOCKS
LRU_C = 8.0
LRU_SB = 640
N_LRU_SB = D_RNN // LRU_SB
N_EXP_GROUPS = 4
EXP_PER_GROUP = 8
N_EXPERTS = N_EXP_GROUPS * EXP_PER_GROUP
TOP_K = 2
D_EXPERT = 512
DEPTH = 2
ALPHA = (2.0 * DEPTH) ** 0.25

COL_Q = 0
COL_K = A_HEADS * A_DK
COL_V = 2 * A_HEADS * A_DK
COL_O = COL_V + A_INNER
COL_Z = COL_O + A_INNER
COL_XBC = COL_Z + B_INNER
COL_SMALL = COL_XBC + B_CONV_DIM
N_SMALL = 2 * A_HEADS + B_HEADS
IN0_PAD = 11520

NEG_BIG = -1e30
VMEM_LIMIT_BYTES = 48 * 1024 * 1024
TILE_ROWS = 256
FFN_AHEAD = 2
LN_ROWS = 256
SLAB = D_MODEL // 128


def _cparams(n_axes):
    return pltpu.CompilerParams(dimension_semantics=("arbitrary",) * n_axes,
                                vmem_limit_bytes=VMEM_LIMIT_BYTES)


def _smem_spec():
    return pl.BlockSpec(memory_space=pltpu.SMEM)


def _mm_kernel(x_ref, w_ref, o_ref):
    o_ref[...] = jnp.dot(x_ref[...].astype(BF16), w_ref[...], preferred_element_type=F32).astype(o_ref.dtype)


def _mm(x, w, bm, bn, name, n_cols=None):
    M, K = x.shape
    N = w.shape[1] if n_cols is None else n_cols
    return pl.pallas_call(
        _mm_kernel,
        grid=(N // bn, M // bm),
        in_specs=[pl.BlockSpec((bm, K), lambda j, i: (i, 0)),
                  pl.BlockSpec((K, bn), lambda j, i: (0, j))],
        out_specs=pl.BlockSpec((bm, bn), lambda j, i: (i, j)),
        out_shape=jax.ShapeDtypeStruct((M, N), F32),
        compiler_params=_cparams(2),
        name=name,
    )(x, w)


RELAY_W = 1024


def _w_relayout_kernel(a_ref, b_ref, g_ref, o_ref, *, n_same, n_shift, shift):
    j = pl.program_id(0)
    lane = lax.broadcasted_iota(jnp.int32, (1, 128), 1)
    n_t = RELAY_W // 128

    @pl.when(j < n_same)
    def _():
        o_ref[...] = a_ref[...].astype(BF16)

    @pl.when((j >= n_same) & (j < n_same + n_shift))
    def _():
        cur = pltpu.roll(a_ref[:, 0:128], 128 - shift, 1)
        for t in range(n_t):
            nxt_src = a_ref[:, (t + 1) * 128:(t + 2) * 128] if t + 1 < n_t else b_ref[...]
            nxt = pltpu.roll(nxt_src, 128 - shift, 1)
            o_ref[:, t * 128:(t + 1) * 128] = jnp.where(lane < 128 - shift, cur, nxt).astype(BF16)
            cur = nxt

    @pl.when(j == n_same + n_shift)
    def _():
        head = jnp.where(lane < 2 * A_HEADS, g_ref[...], jnp.where(lane < N_SMALL, a_ref[:, 0:128], 0.0))
        o_ref[:, 0:128] = head.astype(BF16)
        o_ref[:, 128:] = jnp.zeros((o_ref.shape[0], RELAY_W - 128), BF16)


def _w_relayout(w):
    K = w.shape[0]
    src_small = 2 * A_HEADS * A_DK + 2 * A_INNER
    n_same = src_small // RELAY_W
    n_shift = (B_INNER + B_CONV_DIM) // RELAY_W
    n_blk = n_same + n_shift + 1
    assert src_small % RELAY_W == 0 and (B_INNER + B_CONV_DIM) % RELAY_W == 0 and n_blk * RELAY_W >= IN0_PAD
    assert COL_SMALL == (n_blk - 1) * RELAY_W and w.shape[1] == COL_SMALL + N_SMALL
    per = RELAY_W // 128
    return pl.pallas_call(
        functools.partial(_w_relayout_kernel, n_same=n_same, n_shift=n_shift, shift=2 * A_HEADS),
        grid=(n_blk,),
        in_specs=[pl.BlockSpec((K, RELAY_W), lambda j: (0, j)),
                  pl.BlockSpec((K, 128), lambda j: (0, jnp.minimum(j + 1, n_blk - 1) * per)),
                  pl.BlockSpec((K, 128), lambda j: (0, src_small // 128))],
        out_specs=pl.BlockSpec((K, RELAY_W), lambda j: (0, j)),
        out_shape=jax.ShapeDtypeStruct((K, n_blk * RELAY_W), BF16),
        compiler_params=_cparams(1),
        name="w_in_even_relayout",
    )(w, w, w)


def _layer_norm_rows(y, g, b):
    mu = jnp.mean(y, axis=-1, keepdims=True)
    yc = y - mu
    var = jnp.mean(yc * yc, axis=-1, keepdims=True)
    return yc * lax.rsqrt(var + 1e-5) * g + b


def _mm_parts_ln_kernel(*refs, k_sizes, n_blk_a):
    n = len(k_sizes)
    a_refs, b_refs = refs[:n], refs[n:2 * n]
    w_ref, x_ref, g_ref, b_ref, wr_ref, br_ref, os_ref, gate_ref, id_ref = refs[2 * n:]
    i = pl.program_id(0)

    def run(lhs_refs):
        acc, k0 = None, 0
        for r, ks in zip(lhs_refs, k_sizes):
            part = jnp.dot(r[...], w_ref[k0:k0 + ks, :], preferred_element_type=F32)
            acc = part if acc is None else acc + part
            k0 += ks
        y = _layer_norm_rows(ALPHA * x_ref[...] + acc, g_ref[...], b_ref[...])
        _slab_store(os_ref, y)
        logits = jnp.dot(y.astype(BF16), wr_ref[...], preferred_element_type=F32) + br_ref[...]
        gate_ref[...], id_ref[...] = _route(logits)

    @pl.when(i < n_blk_a)
    def _():
        run(a_refs)

    @pl.when(i >= n_blk_a)
    def _():
        run(b_refs)


def _mm_parts_ln(parts_a, parts_b, w, x, g, b, w_r, b_r, name):
    bm = LN_ROWS
    k_sizes = tuple(p.shape[1] for p in parts_a)
    n_blk_a = parts_a[0].shape[0] // bm
    n_blk_b = parts_b[0].shape[0] // bm
    K, D = w.shape
    M = (n_blk_a + n_blk_b) * bm
    a_specs = [pl.BlockSpec((bm, ks), lambda i: (jnp.minimum(i, n_blk_a - 1), 0)) for ks in k_sizes]
    b_specs = [pl.BlockSpec((bm, ks), lambda i: (jnp.maximum(i - n_blk_a, 0), 0)) for ks in k_sizes]
    row = pl.BlockSpec((bm, D), lambda i: (i, 0))
    vec = pl.BlockSpec((1, D), lambda i: (0, 0))
    tile = pl.BlockSpec((bm, 128), lambda i: (i, 0))
    w_spec = pl.BlockSpec((K, D), lambda i: (0, 0), pipeline_mode=pl.Buffered(1))
    return pl.pallas_call(
        functools.partial(_mm_parts_ln_kernel, k_sizes=k_sizes, n_blk_a=n_blk_a),
        grid=(n_blk_a + n_blk_b,),
        in_specs=a_specs + b_specs + [w_spec, row, vec, vec, pl.BlockSpec((D, 128), lambda i: (0, 0)),
                                      pl.BlockSpec((1, 128), lambda i: (0, 0))],
        out_specs=[pl.BlockSpec((bm * SLAB, 128), lambda i: (i, 0)), tile, tile],
        out_shape=[jax.ShapeDtypeStruct((M * SLAB, 128), F32), jax.ShapeDtypeStruct((M, 128), F32),
                   jax.ShapeDtypeStruct((M, 128), jnp.int32)],
        compiler_params=_cparams(1),
        name=name,
    )(*parts_a, *parts_b, w, x, g.reshape(1, D), b.reshape(1, D), w_r, b_r)


def _slab_store(ref, val):
    tm = val.shape[0]
    for j in range(SLAB):
        ref[pl.ds(j, tm, stride=SLAB), :] = val[:, j * 128:(j + 1) * 128]


def _slab_piece(ref, j, tm):
    return ref[pl.ds(j, tm, stride=SLAB), :]


def _slab_copy(src_hbm, dst_vmem, sem, src_tok, dst_tok):
    return pltpu.make_async_copy(src_hbm.at[pl.ds(pl.multiple_of(src_tok * SLAB, SLAB), SLAB), :],
                                 dst_vmem.at[pl.ds(pl.multiple_of(dst_tok * SLAB, SLAB), SLAB), :], sem)


def _issue_gather(idx_ref, idx0, src_hbm, dst_vmem, dst0, sem, n):
    def body(r, carry):
        _slab_copy(src_hbm, dst_vmem, sem, idx_ref[idx0 + r], dst0 + r).start()
        return carry
    lax.fori_loop(0, n, body, 0, unroll=8)


def _wait_gather(src_hbm, dst_vmem, sem, n):
    def body(r, carry):
        _slab_copy(src_hbm, dst_vmem, sem, 0, 0).wait()
        return carry
    lax.fori_loop(0, n, body, 0, unroll=8)


def _combine_rows(pos_ref, x_ref, y_hbm, gate_ref, g_ref, b_ref, v_s, ybuf, sems, n_tok, tm):
    i = pl.program_id(0)
    n = pl.num_programs(0)
    slot = i % 2

    def fetch(tile, sl):
        for kk in range(TOP_K):
            _issue_gather(pos_ref, kk * n_tok + tile * tm, y_hbm, ybuf, (sl * TOP_K + kk) * tm, sems.at[sl], tm)

    @pl.when(i == 0)
    def _():
        fetch(0, 0)

    @pl.when(i + 1 < n)
    def _():
        fetch(i + 1, 1 - slot)

    _wait_gather(y_hbm, ybuf, sems.at[slot], TOP_K * tm)
    gate = gate_ref[...]
    g0 = gate[:, 0:1]
    g1 = gate[:, 1:2]
    base0 = slot * (TOP_K * tm * SLAB)
    base1 = base0 + tm * SLAB
    for j in range(SLAB):
        v_s[:, j * 128:(j + 1) * 128] = (ALPHA * _slab_piece(x_ref, j, tm)
                                         + g0 * ybuf[pl.ds(base0 + j, tm, stride=SLAB), :]
                                         + g1 * ybuf[pl.ds(base1 + j, tm, stride=SLAB), :])
    return _layer_norm_rows(v_s[...], g_ref[...], b_ref[...])


def _combine_ln_kernel(pos_ref, x_ref, y_hbm, gate_ref, g_ref, b_ref, o_ref, ob_ref, v_s, ybuf, sems, *, n_tok):
    y = _combine_rows(pos_ref, x_ref, y_hbm, gate_ref, g_ref, b_ref, v_s, ybuf, sems, n_tok, LN_ROWS)
    o_ref[...] = y
    ob_ref[...] = y.astype(BF16)


def _combine_out_kernel(pos_ref, x_ref, y_hbm, gate_ref, g_ref, b_ref, yp_ref, ys_ref, v_s, ybuf, sems, *,
                        n_tok, n_blk_p, blk_per_seq):
    y = _combine_rows(pos_ref, x_ref, y_hbm, gate_ref, g_ref, b_ref, v_s, ybuf, sems, n_tok, CHUNK)
    i = pl.program_id(0)

    @pl.when((i < n_blk_p) & (i % blk_per_seq > 0))
    def _():
        yp_ref[...] = y

    @pl.when(i >= n_blk_p)
    def _():
        ys_ref[...] = y


def _combine_ln(x_slab, y_slab, pos, gates, g, b, name, split=None):
    M = gates.shape[0]
    D = D_MODEL
    tm = LN_ROWS if split is None else CHUNK
    vec = pl.BlockSpec((1, D), lambda i, pos: (0, 0))
    if split is None:
        kern = functools.partial(_combine_ln_kernel, n_tok=M)
        row = pl.BlockSpec((tm, D), lambda i, pos: (i, 0))
        out_specs = [row, row]
        out_shape = [jax.ShapeDtypeStruct((M, D), F32), jax.ShapeDtypeStruct((M, D), BF16)]
    else:
        n_p, bps = split
        n_blk_p = n_p // tm
        kern = functools.partial(_combine_out_kernel, n_tok=M, n_blk_p=n_blk_p, blk_per_seq=bps)

        def yp_map(i, pos):
            ip = jnp.minimum(i, n_blk_p - 1)
            return ((ip // bps) * (bps - 1) + jnp.maximum(ip % bps - 1, 0), 0)

        out_specs = [pl.BlockSpec((tm, D), yp_map),
                     pl.BlockSpec((tm, D), lambda i, pos: (jnp.maximum(i - n_blk_p, 0), 0))]
        out_shape = [jax.ShapeDtypeStruct((n_blk_p // bps * (bps - 1) * tm, D), F32),
                     jax.ShapeDtypeStruct((M - n_p, D), F32)]
    return pl.pallas_call(
        kern,
        grid_spec=pltpu.PrefetchScalarGridSpec(
            num_scalar_prefetch=1,
            grid=(M // tm,),
            in_specs=[pl.BlockSpec((tm * SLAB, 128), lambda i, pos: (i, 0)),
                      pl.BlockSpec(memory_space=pl.ANY),
                      pl.BlockSpec((tm, 128), lambda i, pos: (i, 0)), vec, vec],
            out_specs=out_specs,
            scratch_shapes=[pltpu.VMEM((tm, D), F32), pltpu.VMEM((2 * TOP_K * tm * SLAB, 128), F32),
                            pltpu.SemaphoreType.DMA((2,))],
        ),
        out_shape=out_shape,
        compiler_params=_cparams(1),
        name=name,
    )(pos, x_slab, y_slab, gates, g.reshape(1, D), b.reshape(1, D))


def _softplus(x):
    return jnp.maximum(x, 0.0) + jnp.log1p(jnp.exp(-jnp.abs(x)))


def _sigmoid(x):
    return 1.0 / (1.0 + jnp.exp(-x))


def _silu(x):
    return x * _sigmoid(x)


def _dot_nn(m, x):
    return jnp.dot(m, x, preferred_element_type=F32)


def _dot_nt(a, b):
    return lax.dot_general(a, b, (((1,), (1,)), ((), ())), preferred_element_type=F32)


def _row_to_col(row, eye):
    return jnp.sum(jnp.where(eye, row, 0.0), axis=1, keepdims=True)


def _causal_conv(x, carry_ref, buf_ref, w_ref, b_ref, L):
    buf_ref[0:8, :] = carry_ref[...]
    buf_ref[8:8 + L, :] = x
    acc = b_ref[...] + w_ref[0:1, :] * buf_ref[5:5 + L, :]
    for j in range(1, CONV_W):
        acc = acc + w_ref[j:j + 1, :] * buf_ref[5 + j:5 + j + L, :]
    carry_ref[...] = buf_ref[L:L + 8, :]
    return acc


def _mlstm_kernel(gb_ref, q_ref, k_ref, v_ref, o_ref, gt_ref, c0_ref, n0_ref, m0_ref, ng_ref,
                  h_ref, c_out_ref, n_out_ref, m_out_ref, *state, L, n_pad):
    c_s, n_s, m_s = state[:A_HEADS], state[A_HEADS:2 * A_HEADS], state[2 * A_HEADS:]
    c = pl.program_id(1)

    @pl.when(c == 0)
    def _():
        for hd in range(A_HEADS):
            c_s[hd][...] = c0_ref[0, hd]
            n_s[hd][...] = n0_ref[0, hd]
            m_s[hd][...] = m0_ref[0, hd]

    rows = lax.broadcasted_iota(jnp.int32, (L, L), 0)
    cols = lax.broadcasted_iota(jnp.int32, (L, L), 1)
    eye = rows == cols
    causal = cols <= rows

    if n_pad:
        pad = (lax.broadcasted_iota(jnp.int32, (1, L), 1) < n_pad) & (c == 0)

    for hd in range(A_HEADS):
        li = gt_ref[0, hd:hd + 1, :] + gb_ref[0, hd]
        fr = gt_ref[0, A_HEADS + hd:A_HEADS + hd + 1, :] + gb_ref[1, hd]
        lf = jnp.minimum(fr, 0.0) - jnp.log1p(jnp.exp(-jnp.abs(fr)))
        if n_pad:
            li = jnp.where(pad, NEG_BIG, li)
            lf = jnp.where(pad, 0.0, lf)

        lf_col = _row_to_col(lf, eye)
        b_col = jnp.sum(jnp.where(causal, lf, 0.0), axis=1, keepdims=True)
        b_row = jnp.sum(jnp.where(rows <= cols, lf_col, 0.0), axis=0, keepdims=True)
        m_prev = m_s[hd][:, 0:1]

        d = jnp.where(causal, b_col - b_row + li, NEG_BIG)
        inter = b_col + m_prev
        m_t = jnp.maximum(inter, jnp.max(d, axis=1, keepdims=True))
        w_intra = jnp.exp(d - m_t)
        w_inter = jnp.exp(inter - m_t)

        q = q_ref[:, hd * A_DK:(hd + 1) * A_DK] * (A_DK ** -0.5)
        k = k_ref[:, hd * A_DK:(hd + 1) * A_DK]
        v = v_ref[:, hd * A_DV:(hd + 1) * A_DV]
        qb = q.astype(BF16)
        kb = k.astype(BF16)
        vb = v.astype(BF16)
        s = _dot_nt(qb, kb) * w_intra
        c_prev = c_s[hd][...]
        n_prev = n_s[hd][...]
        num = _dot_nn(s.astype(BF16), vb)
        num = num + w_inter * jnp.dot(qb, c_prev.astype(BF16), preferred_element_type=F32)
        qn = jnp.sum(qb.astype(F32) * n_prev.astype(BF16).astype(F32), axis=1, keepdims=True)
        nq = jnp.sum(s, axis=1, keepdims=True) + w_inter * qn
        hh = num / jnp.maximum(jnp.abs(nq), jnp.exp(-m_t))

        hn = hh * lax.rsqrt(jnp.mean(hh * hh, axis=-1, keepdims=True) + 1e-6)
        hn = hn * ng_ref[:, hd * A_DV:(hd + 1) * A_DV]
        h_ref[:, hd * A_DV:(hd + 1) * A_DV] = (
            hn * _sigmoid(o_ref[:, hd * A_DV:(hd + 1) * A_DV])).astype(h_ref.dtype)

        b_last = b_row[:, L - 1:L]
        ws_log = b_last - b_row + li
        m_new = jnp.maximum(b_last + m_prev, jnp.max(ws_log, axis=1, keepdims=True))
        ws_col = _row_to_col(jnp.exp(ws_log - m_new), eye)
        wc = jnp.exp(b_last + m_prev - m_new)
        kw = k * ws_col
        c_new = wc * c_prev + lax.dot_general(kw.astype(BF16), vb, (((0,), (0,)), ((), ())),
                                              preferred_element_type=F32)
        n_new = wc * n_prev + jnp.sum(ws_col.astype(BF16).astype(F32) * kb.astype(F32), axis=0, keepdims=True)
        m_new_b = jnp.broadcast_to(m_new, (1, 128))
        c_s[hd][...] = c_new
        n_s[hd][...] = n_new
        m_s[hd][...] = m_new_b
        c_out_ref[0, hd] = c_new
        n_out_ref[0, hd] = n_new
        m_out_ref[0, hd] = m_new_b


def _mlstm(proj, gates_t, gate_b, norm_g, c0, n0, m0, *, n_seq, n_chunks, L, row0, n_pad, name):
    rb0 = row0 // L

    def rblk(b, c):
        return rb0 + b * n_chunks + c

    out_dtype = BF16 if L % 16 == 0 else F32
    qk_w = A_HEADS * A_DK
    state_map = lambda b, c: (b, 0, 0, 0)
    m0b = jnp.broadcast_to(m0[:, :, None, None], (n_seq, A_HEADS, 1, 128))
    kern = functools.partial(_mlstm_kernel, L=L, n_pad=n_pad)
    h, c_new, n_new, m_new = pl.pallas_call(
        kern,
        grid=(n_seq, n_chunks),
        in_specs=[
            _smem_spec(),
            pl.BlockSpec((L, qk_w), lambda b, c: (rblk(b, c), COL_Q // qk_w)),
            pl.BlockSpec((L, qk_w), lambda b, c: (rblk(b, c), COL_K // qk_w)),
            pl.BlockSpec((L, A_INNER), lambda b, c: (rblk(b, c), COL_V // A_INNER)),
            pl.BlockSpec((L, A_INNER), lambda b, c: (rblk(b, c), COL_O // A_INNER)),
            pl.BlockSpec((1, N_SMALL, L), lambda b, c: (b, 0, c)),
            pl.BlockSpec((1, A_HEADS, A_DK, A_DV), state_map),
            pl.BlockSpec((1, A_HEADS, 1, A_DK), state_map),
            pl.BlockSpec((1, A_HEADS, 1, 128), state_map),
            pl.BlockSpec((1, A_INNER), lambda b, c: (0, 0)),
        ],
        out_specs=[
            pl.BlockSpec((L, A_INNER), lambda b, c: (b * n_chunks + c, 0)),
            pl.BlockSpec((1, A_HEADS, A_DK, A_DV), state_map),
            pl.BlockSpec((1, A_HEADS, 1, A_DK), state_map),
            pl.BlockSpec((1, A_HEADS, 1, 128), state_map),
        ],
        out_shape=[
            jax.ShapeDtypeStruct((n_seq * n_chunks * L, A_INNER), out_dtype),
            jax.ShapeDtypeStruct((n_seq, A_HEADS, A_DK, A_DV), F32),
            jax.ShapeDtypeStruct((n_seq, A_HEADS, 1, A_DK), F32),
            jax.ShapeDtypeStruct((n_seq, A_HEADS, 1, 128), F32),
        ],
        scratch_shapes=([pltpu.VMEM((A_DK, A_DV), F32)] * A_HEADS + [pltpu.VMEM((1, A_DK), F32)] * A_HEADS
                        + [pltpu.VMEM((1, 128), F32)] * A_HEADS),
        compiler_params=_cparams(2),
        name=name,
    )(gate_b, proj, proj, proj, proj, gates_t, c0, n0, m0b, norm_g.reshape(1, A_INNER))
    return h, c_new, n_new, m_new[:, :, 0, 0]


def _ssd_kernel(dtr_ref, dtc_ref, alr_ref, alc_ref, xs_ref, bm_ref, cm_ref, z_ref, gt_ref, gc_ref, s0_ref,
                cx0_ref, cb0_ref, cc0_ref, wx_ref, wb_ref, wc_ref, bx_ref, bb_ref, bc_ref, ng_ref, dsk_ref,
                y_ref, s_out_ref,
                cx_s, cb_s, cc_s, bufx, bufb, bufc, ybuf, *s_s, L, n_pad, gps):
    grp0 = pl.program_id(1) * gps
    c = pl.program_id(2)

    @pl.when(c == 0)
    def _():
        for pair in range(len(s_s)):
            s_s[pair][...] = s0_ref[0, pair]
        cx_s[...] = cx0_ref[0]
        cb_s[...] = cb0_ref[0]
        cc_s[...] = cc0_ref[0]

    rows = lax.broadcasted_iota(jnp.int32, (L, L), 0)
    cols = lax.broadcasted_iota(jnp.int32, (L, L), 1)
    causal = cols <= rows
    tril = causal.astype(F32)
    triu = (rows <= cols).astype(F32)
    lane_lo = lax.broadcasted_iota(jnp.int32, (L, 2 * B_P), 1) < B_P
    row_lo = lax.broadcasted_iota(jnp.int32, (2 * B_P, 1), 0) < B_P
    if n_pad:
        pad_row = (lax.broadcasted_iota(jnp.int32, (1, L), 1) < n_pad) & (c == 0)
        pad_col = (lax.broadcasted_iota(jnp.int32, (L, 1), 0) < n_pad) & (c == 0)

    xs_raw = xs_ref[...]
    bm_raw = bm_ref[...]
    cm_raw = cm_ref[...]
    if n_pad:
        xs_raw = jnp.where(pad_col, 0.0, xs_raw)
        bm_raw = jnp.where(pad_col, 0.0, bm_raw)
        cm_raw = jnp.where(pad_col, 0.0, cm_raw)
    xs = _silu(_causal_conv(xs_raw, cx_s, bufx, wx_ref, bx_ref, L))
    bm = _silu(_causal_conv(bm_raw, cb_s, bufb, wb_ref, bb_ref, L))
    cm = _silu(_causal_conv(cm_raw, cc_s, bufc, wc_ref, bc_ref, L))

    def two(col, h0):
        return jnp.where(lane_lo, col[:, h0:h0 + 1], col[:, h0 + 1:h0 + 2])

    for gg in range(gps):
        bmb = bm[:, gg * B_N:(gg + 1) * B_N].astype(BF16)
        cmb = cm[:, gg * B_N:(gg + 1) * B_N].astype(BF16)
        cb = lax.dot_general(cmb, bmb, (((1,), (1,)), ((), ())), preferred_element_type=F32)

        row0 = pl.multiple_of(2 * A_HEADS + (grp0 + gg) * B_HPG, B_HPG)
        dt_c = _softplus(gc_ref[0, gg] + dtr_ref[gg])
        dt_r = _softplus(gt_ref[0, pl.ds(row0, B_HPG), :] + dtc_ref[gg])
        if n_pad:
            dt_c = jnp.where(pad_col, 0.0, dt_c)
            dt_r = jnp.where(pad_row, 0.0, dt_r)
        a_c = dt_c * (-jnp.exp(alr_ref[gg]))
        a_r = dt_r * (-jnp.exp(alc_ref[gg]))
        b_c = jnp.dot(tril, a_c, preferred_element_type=F32, precision=lax.Precision.HIGHEST)
        b_r = jnp.dot(a_r, triu, preferred_element_type=F32, precision=lax.Precision.HIGHEST)
        eb_c = jnp.exp(b_c)
        b_last = b_c[L - 1:L, :]
        w_c = jnp.exp(b_last - b_c)
        e_last = jnp.exp(b_last)

        for p in range(B_HPG // 2):
            h0 = 2 * p
            pair = gg * (B_HPG // 2) + p
            x_pair = xs[:, pair * 2 * B_P:(pair + 1) * 2 * B_P]
            xdt = x_pair * two(dt_c, h0)
            y = None
            for hh, keep in ((h0, lane_lo), (h0 + 1, ~lane_lo)):
                decay = jnp.exp(jnp.where(causal, b_c[:, hh:hh + 1] - b_r[hh:hh + 1, :], NEG_BIG))
                part = _dot_nn((cb * decay).astype(BF16), jnp.where(keep, xdt, 0.0).astype(BF16))
                y = part if y is None else y + part
            s_prev = s_s[pair][...]
            y = y + two(eb_c, h0) * lax.dot_general(cmb, s_prev.astype(BF16), (((1,), (1,)), ((), ())),
                                                    preferred_element_type=F32)
            upd = lax.dot_general((xdt * two(w_c, h0)).astype(BF16), bmb, (((0,), (0,)), ((), ())),
                                  preferred_element_type=F32)
            s_new = jnp.where(row_lo, e_last[:, h0:h0 + 1], e_last[:, h0 + 1:h0 + 2]) * s_prev + upd
            s_s[pair][...] = s_new
            s_out_ref[0, pair] = s_new
            ybuf[:, pair * 2 * B_P:(pair + 1) * 2 * B_P] = y

    y = (ybuf[...] + dsk_ref[...] * xs) * _silu(z_ref[...])
    for gg in range(gps):
        yg = y[:, gg * B_GW:(gg + 1) * B_GW]
        yn = yg * lax.rsqrt(jnp.mean(yg * yg, axis=-1, keepdims=True) + 1e-6) * ng_ref[:, gg * B_GW:(gg + 1) * B_GW]
        y_ref[:, gg * B_GW:(gg + 1) * B_GW] = yn.astype(y_ref.dtype)


def _ssd(proj, gates_t, gates_c, dt_bias, a_log, d_skip, conv_w, conv_b, norm_g, s0, conv0, *,
         n_seq, n_chunks, L, row0, n_pad, name):
    rb0 = row0 // L

    def rblk(b, c):
        return rb0 + b * n_chunks + c

    out_dtype = BF16 if L % 16 == 0 else F32
    gps = B_GROUPS if n_chunks == 1 else 1
    xw, nw = gps * B_GW, gps * B_N
    n_pairs = gps * B_HPG // 2
    cw = conv_w
    cbias = conv_b.reshape(1, B_CONV_DIM)
    xoff = COL_XBC // xw
    boff = (COL_XBC + B_INNER) // nw
    coff = (COL_XBC + B_INNER + B_GROUPS * B_N) // nw
    cvb = B_INNER // nw
    cvc = (B_INNER + B_GROUPS * B_N) // nw
    dt_r = dt_bias.reshape(B_GROUPS, 1, B_HPG)
    dt_c = dt_bias.reshape(B_GROUPS, B_HPG, 1)
    al_r = a_log.reshape(B_GROUPS, 1, B_HPG)
    al_c = a_log.reshape(B_GROUPS, B_HPG, 1)
    d_row = jnp.repeat(d_skip, B_P).reshape(1, B_INNER)
    s0p = s0.reshape(n_seq, B_HEADS // 2, 2 * B_P, B_N)
    prow = pl.BlockSpec((gps, 1, B_HPG), lambda b, g, c: (g, 0, 0))
    pcol = pl.BlockSpec((gps, B_HPG, 1), lambda b, g, c: (g, 0, 0))
    kern = functools.partial(_ssd_kernel, L=L, n_pad=n_pad, gps=gps)
    y, s_new = pl.pallas_call(
        kern,
        grid=(n_seq, B_GROUPS // gps, n_chunks),
        in_specs=[
            prow, pcol, prow, pcol,
            pl.BlockSpec((L, xw), lambda b, g, c: (rblk(b, c), xoff + g)),
            pl.BlockSpec((L, nw), lambda b, g, c: (rblk(b, c), boff + g)),
            pl.BlockSpec((L, nw), lambda b, g, c: (rblk(b, c), coff + g)),
            pl.BlockSpec((L, xw), lambda b, g, c: (rblk(b, c), COL_Z // xw + g)),
            pl.BlockSpec((1, N_SMALL, L), lambda b, g, c: (b, 0, c)),
            pl.BlockSpec((1, gps, L, B_HPG), lambda b, g, c: (b, g, c, 0)),
            pl.BlockSpec((1, n_pairs, 2 * B_P, B_N), lambda b, g, c: (b, g, 0, 0)),
            pl.BlockSpec((1, 8, xw), lambda b, g, c: (b, 0, g)),
            pl.BlockSpec((1, 8, nw), lambda b, g, c: (b, 0, cvb + g)),
            pl.BlockSpec((1, 8, nw), lambda b, g, c: (b, 0, cvc + g)),
            pl.BlockSpec((CONV_W, xw), lambda b, g, c: (0, g)),
            pl.BlockSpec((CONV_W, nw), lambda b, g, c: (0, cvb + g)),
            pl.BlockSpec((CONV_W, nw), lambda b, g, c: (0, cvc + g)),
            pl.BlockSpec((1, xw), lambda b, g, c: (0, g)),
            pl.BlockSpec((1, nw), lambda b, g, c: (0, cvb + g)),
            pl.BlockSpec((1, nw), lambda b, g, c: (0, cvc + g)),
            pl.BlockSpec((1, xw), lambda b, g, c: (0, g)),
            pl.BlockSpec((1, xw), lambda b, g, c: (0, g)),
        ],
        out_specs=[
            pl.BlockSpec((L, xw), lambda b, g, c: (b * n_chunks + c, g)),
            pl.BlockSpec((1, n_pairs, 2 * B_P, B_N), lambda b, g, c: (b, g, 0, 0)),
        ],
        out_shape=[
            jax.ShapeDtypeStruct((n_seq * n_chunks * L, B_INNER), out_dtype),
            jax.ShapeDtypeStruct((n_seq, B_HEADS // 2, 2 * B_P, B_N), F32),
        ],
        scratch_shapes=[
            pltpu.VMEM((8, xw), F32), pltpu.VMEM((8, nw), F32), pltpu.VMEM((8, nw), F32),
            pltpu.VMEM((L + 8, xw), F32), pltpu.VMEM((L + 8, nw), F32), pltpu.VMEM((L + 8, nw), F32),
            pltpu.VMEM((L, xw), F32),
        ] + [pltpu.VMEM((2 * B_P, B_N), F32)] * n_pairs,
        compiler_params=_cparams(3),
        name=name,
    )(dt_r, dt_c, al_r, al_c, proj, proj, proj, proj, gates_t, gates_c, s0p, conv0, conv0, conv0,
      cw, cw, cw, cbias, cbias, cbias, norm_g.reshape(1, B_INNER), d_row)
    return y, s_new.reshape(n_seq, B_HEADS, B_P, B_N)


def _gelu_tanh(x):
    return 0.5 * x * (1.0 + jnp.tanh(math.sqrt(2.0 / math.pi) * (x + 0.044715 * (x * x * x))))


def _expm1_nonpos(z):
    e = jnp.exp(z)
    safe = (e < 1.0) & (z > -1.0)
    return jnp.where(safe, (e - 1.0) * z / jnp.log(jnp.where(safe, e, 0.5)), jnp.where(e == 1.0, z, e - 1.0))


def _lru_gates(xf, wg_ref, bg_ref, lam_ref):
    xfb = xf.astype(BF16)
    r = _sigmoid(jnp.dot(xfb, wg_ref[0, 0], preferred_element_type=F32) + bg_ref[0:1, :])
    i = _sigmoid(jnp.dot(xfb, wg_ref[1, 0], preferred_element_type=F32) + bg_ref[1:2, :])
    log_a = (-LRU_C) * r * _softplus(-lam_ref[...])
    a = jnp.exp(log_a)
    u = jnp.sqrt(-_expm1_nonpos(2.0 * log_a)) * (i * xf)
    return a, u


def _scan_rows8(a, u, t8):
    for k in (1, 2, 4):
        keep = t8 >= k
        a_sh = jnp.where(keep, pltpu.roll(a, k, 0), 1.0)
        u_sh = jnp.where(keep, pltpu.roll(u, k, 0), 0.0)
        u = a * u_sh + u
        a = a * a_sh
    return a, u


def _lru_kernel(gi_ref, xb_ref, h0_ref, cv0_ref, cw_ref, cb_ref, wg_ref, bg_ref, lam_ref,
                o_ref, h_out_ref, h_s, cv_s, buf, *, L, n_pad):
    c = pl.program_id(2)

    @pl.when(c == 0)
    def _():
        h_s[...] = h0_ref[0]
        cv_s[...] = cv0_ref[0]

    row_id = lax.broadcasted_iota(jnp.int32, (L, 1), 0)
    x_raw = xb_ref[...]
    if n_pad:
        pad_col = (row_id < n_pad) & (c == 0)
        x_raw = jnp.where(pad_col, 0.0, x_raw)
    xf = _causal_conv(x_raw, cv_s, buf, cw_ref, cb_ref, L)
    a, u = _lru_gates(xf, wg_ref, bg_ref, lam_ref)
    if n_pad:
        a = jnp.where(pad_col, 1.0, a)
        u = jnp.where(pad_col, 0.0, u)
    a, u = _scan_rows8(a, u, row_id % 8)
    h_in = h_s[...]
    pieces = []
    for t in range(L // 8):
        piece = a[t * 8:(t + 1) * 8, :] * h_in + u[t * 8:(t + 1) * 8, :]
        pieces.append(piece)
        h_in = piece[7:8, :]
    hs = jnp.concatenate(pieces, axis=0)
    h_s[...] = h_in
    h_out_ref[0] = h_in
    o_ref[...] = (hs * _gelu_tanh(gi_ref[...])).astype(o_ref.dtype)


def _lru_rows8_kernel(gi_ref, xb_ref, h0_ref, hist_ref, cw_ref, cb_ref, wg_ref, bg_ref, lam_ref,
                      o_ref, hs_ref):
    R = xb_ref.shape[0]
    t8 = lax.broadcasted_iota(jnp.int32, (R, 1), 0) % 8
    x = xb_ref[...]
    hist = hist_ref[...]
    acc = cb_ref[...] + cw_ref[CONV_W - 1:CONV_W, :] * x
    for j in range(CONV_W - 1):
        sh = CONV_W - 1 - j
        tap = jnp.where(t8 >= sh, pltpu.roll(x, sh, 0), pltpu.roll(hist, (sh - 8) % R, 0))
        acc = acc + cw_ref[j:j + 1, :] * tap
    a, u = _lru_gates(acc, wg_ref, bg_ref, lam_ref)
    a, u = _scan_rows8(a, u, t8)
    hs = a * h0_ref[...] + u
    hs_ref[...] = hs
    o_ref[...] = (hs * _gelu_tanh(gi_ref[...])).astype(o_ref.dtype)


def _lru_rows8(proj, conv_w, conv_b, wg_sb, b_gate, lam, h0, conv0, *, n_seq, row0, name):
    R = 128
    n_rows = n_seq * 8
    rb0 = row0 // R
    h0x = jnp.repeat(h0.reshape(n_seq, D_RNN), 8, axis=0)
    hist = conv0.reshape(n_rows, D_RNN)
    blk = lambda j, i: (i, j)
    return pl.pallas_call(
        _lru_rows8_kernel,
        grid=(N_LRU_SB, n_rows // R),
        in_specs=[
            pl.BlockSpec((R, LRU_SB), lambda j, i: (rb0 + i, j)),
            pl.BlockSpec((R, LRU_SB), lambda j, i: (rb0 + i, N_LRU_SB + j)),
            pl.BlockSpec((R, LRU_SB), blk),
            pl.BlockSpec((R, LRU_SB), blk),
            pl.BlockSpec((CONV_W, LRU_SB), lambda j, i: (0, j)),
            pl.BlockSpec((1, LRU_SB), lambda j, i: (0, j)),
            pl.BlockSpec((2, 1, LRU_SB, LRU_SB), lambda j, i: (0, j, 0, 0)),
            pl.BlockSpec((2, LRU_SB), lambda j, i: (0, j)),
            pl.BlockSpec((1, LRU_SB), lambda j, i: (0, j)),
        ],
        out_specs=[pl.BlockSpec((R, LRU_SB), blk), pl.BlockSpec((R, LRU_SB), blk)],
        out_shape=[jax.ShapeDtypeStruct((n_rows, D_RNN), BF16), jax.ShapeDtypeStruct((n_rows, D_RNN), F32)],
        compiler_params=_cparams(2),
        name=name,
    )(proj, proj, h0x, hist, conv_w, conv_b.reshape(1, D_RNN), wg_sb, b_gate, lam.reshape(1, D_RNN))


def _lru(proj, conv_w, conv_b, wg_sb, b_gate, lam, h0, conv0, *, n_seq, n_chunks, L, row0, n_pad, name):
    rb0 = row0 // L

    def rblk(b, c):
        return rb0 + b * n_chunks + c

    out_dtype = BF16 if L % 16 == 0 else F32
    kern = functools.partial(_lru_kernel, L=L, n_pad=n_pad)
    return pl.pallas_call(
        kern,
        grid=(n_seq, N_LRU_SB, n_chunks),
        in_specs=[
            pl.BlockSpec((L, LRU_SB), lambda b, j, c: (rblk(b, c), j)),
            pl.BlockSpec((L, LRU_SB), lambda b, j, c: (rblk(b, c), N_LRU_SB + j)),
            pl.BlockSpec((1, 1, LRU_SB), lambda b, j, c: (b, 0, j)),
            pl.BlockSpec((1, 8, LRU_SB), lambda b, j, c: (b, 0, j)),
            pl.BlockSpec((CONV_W, LRU_SB), lambda b, j, c: (0, j)),
            pl.BlockSpec((1, LRU_SB), lambda b, j, c: (0, j)),
            pl.BlockSpec((2, 1, LRU_SB, LRU_SB), lambda b, j, c: (0, j, 0, 0)),
            pl.BlockSpec((2, LRU_SB), lambda b, j, c: (0, j)),
            pl.BlockSpec((1, LRU_SB), lambda b, j, c: (0, j)),
        ],
        out_specs=[
            pl.BlockSpec((L, LRU_SB), lambda b, j, c: (b * n_chunks + c, j)),
            pl.BlockSpec((1, 1, LRU_SB), lambda b, j, c: (b, 0, j)),
        ],
        out_shape=[
            jax.ShapeDtypeStruct((n_seq * n_chunks * L, D_RNN), out_dtype),
            jax.ShapeDtypeStruct((n_seq, 1, D_RNN), F32),
        ],
        scratch_shapes=[pltpu.VMEM((1, LRU_SB), F32), pltpu.VMEM((8, LRU_SB), F32),
                        pltpu.VMEM((L + 8, LRU_SB), F32)],
        compiler_params=_cparams(3),
        name=name,
    )(proj, proj, h0, conv0, conv_w, conv_b.reshape(1, D_RNN), wg_sb, b_gate, lam.reshape(1, D_RNN))


def _route(logits):
    tm = logits.shape[0]
    lane = lax.broadcasted_iota(jnp.int32, (tm, 128), 1)
    lane_f = lane.astype(F32)
    is_group = (lane >= N_EXPERTS) & (lane < N_EXPERTS + N_EXP_GROUPS)
    gl = jnp.where(is_group, logits, -jnp.inf)
    g_max = jnp.max(gl, axis=1, keepdims=True)
    g_lane = jnp.min(jnp.where(gl == g_max, lane_f, 1e9), axis=1, keepdims=True)
    g_w = 1.0 / jnp.sum(jnp.exp(gl - g_max), axis=1, keepdims=True)
    g_idx = g_lane - float(N_EXPERTS)
    lo = g_idx * float(EXP_PER_GROUP)
    in_group = (lane_f >= lo) & (lane_f < lo + float(EXP_PER_GROUP))
    el = jnp.where(in_group, logits, -jnp.inf)
    e_max = jnp.max(el, axis=1, keepdims=True)
    i1 = jnp.min(jnp.where(el == e_max, lane_f, 1e9), axis=1, keepdims=True)
    e_sum = jnp.sum(jnp.exp(el - e_max), axis=1, keepdims=True)
    el2 = jnp.where(lane_f == i1, -jnp.inf, el)
    e2_max = jnp.max(el2, axis=1, keepdims=True)
    i2 = jnp.min(jnp.where(el2 == e2_max, lane_f, 1e9), axis=1, keepdims=True)
    p1 = 1.0 / e_sum
    p2 = jnp.exp(e2_max - e_max) / e_sum
    tot = p1 + p2
    gate1 = g_w * (p1 / tot)
    gate2 = g_w * (p2 / tot)
    gates = jnp.where(lane == 0, gate1, jnp.where(lane == 1, gate2, 0.0))
    ids = jnp.where(lane == 0, i1, jnp.where(lane == 1, i2, 0.0)).astype(jnp.int32)
    return gates, ids


def _ffn_kernel(te_ref, nv_ref, idx_ref, x_hbm, w1_ref, w3_ref, w2_ref, o_ref,
                w1b, w3b, w2b, xb_s, xbuf, sems):
    i = pl.program_id(0)
    tm = TILE_ROWS
    nv = nv_ref[0]
    n_slots = FFN_AHEAD + 1
    slot = i % n_slots

    def fetch(tile):
        sl = tile % n_slots
        _issue_gather(idx_ref, tile * tm, x_hbm, xbuf, sl * tm, sems.at[sl], tm)

    @pl.when(i == 0)
    def _():
        for t in range(FFN_AHEAD):
            @pl.when(t < nv)
            def _():
                fetch(t)

    @pl.when(i + FFN_AHEAD < nv)
    def _():
        fetch(i + FFN_AHEAD)

    prev = te_ref[jnp.maximum(i - 1, 0)]

    @pl.when((i == 0) | (te_ref[i] != prev))
    def _():
        w1b[...] = w1_ref[0, 0].astype(BF16)
        w3b[...] = w3_ref[0, 0].astype(BF16)
        w2b[...] = w2_ref[0, 0].astype(BF16)

    @pl.when(i < nv)
    def _():
        _wait_gather(x_hbm, xbuf, sems.at[slot], tm)
        base = slot * (tm * SLAB)
        for j in range(SLAB):
            xb_s[:, j * 128:(j + 1) * 128] = xbuf[pl.ds(base + j, tm, stride=SLAB), :].astype(BF16)
        xb = xb_s[...]
        h1 = jnp.dot(xb, w1b[...], preferred_element_type=F32)
        h3 = jnp.dot(xb, w3b[...], preferred_element_type=F32)
        hid = (_silu(h1) * h3).astype(BF16)
        _slab_store(o_ref, jnp.dot(hid, w2b[...], preferred_element_type=F32))

    @pl.when(i >= nv)
    def _():
        o_ref[...] = jnp.zeros_like(o_ref)


def _expert_ffn(x_slab, row_src, tile_expert, n_valid, w1, w3, w2, layer, name):
    R = row_src.shape[0]
    D = D_MODEL
    tm = TILE_ROWS
    wmap = lambda i, te, nv, idx: (layer, te[i], 0, 0)
    return pl.pallas_call(
        _ffn_kernel,
        grid_spec=pltpu.PrefetchScalarGridSpec(
            num_scalar_prefetch=3,
            grid=(R // tm,),
            in_specs=[
                pl.BlockSpec(memory_space=pl.ANY),
                pl.BlockSpec((1, 1, D, D_EXPERT), wmap),
                pl.BlockSpec((1, 1, D, D_EXPERT), wmap),
                pl.BlockSpec((1, 1, D_EXPERT, D), wmap),
            ],
            out_specs=pl.BlockSpec((tm * SLAB, 128), lambda i, te, nv, idx: (i, 0)),
            scratch_shapes=[pltpu.VMEM((D, D_EXPERT), BF16), pltpu.VMEM((D, D_EXPERT), BF16),
                            pltpu.VMEM((D_EXPERT, D), BF16), pltpu.VMEM((tm, D), BF16),
                            pltpu.VMEM(((FFN_AHEAD + 1) * tm * SLAB, 128), F32),
                            pltpu.SemaphoreType.DMA((FFN_AHEAD + 1,))],
        ),
        out_shape=jax.ShapeDtypeStruct((R * SLAB, 128), F32),
        compiler_params=_cparams(1),
        name=name,
    )(tile_expert, n_valid, row_src, x_slab, w1, w3, w2)


def _route_plan(ids, n_tok):
    tm = TILE_ROWS
    n_pairs = TOP_K * n_tok
    n_rows = n_pairs + N_EXPERTS * tm
    experts = jnp.arange(N_EXPERTS, dtype=jnp.int32)
    e_flat = ids.T.reshape(-1)
    order = jnp.argsort(e_flat, stable=True).astype(jnp.int32)
    rank_sorted = jnp.argsort(order).astype(jnp.int32)
    sizes = jnp.sum((e_flat[:, None] == experts[None, :]).astype(jnp.int32), axis=0)
    start = jnp.cumsum(sizes) - sizes
    psz = ((sizes + tm - 1) // tm) * tm
    pend = jnp.cumsum(psz)
    pstart = pend - psz
    shift = pstart - start
    pos = shift[e_flat] + rank_sorted
    rows = jnp.arange(n_rows, dtype=jnp.int32)
    tile_start = rows[::tm]
    e_tile = jnp.minimum(jnp.sum((tile_start[:, None] >= pend[None, :]).astype(jnp.int32), axis=1), N_EXPERTS - 1)
    pair_idx = rows - jnp.repeat(shift[e_tile], tm)
    valid = rows < jnp.repeat((pstart + sizes)[e_tile], tm)
    pair = order[jnp.clip(pair_idx, 0, n_pairs - 1)]
    row_src = jnp.where(valid, pair % n_tok, 0)
    last_e = jnp.max(jnp.where(sizes > 0, experts, 0))
    te = jnp.where(tile_start < pend[-1], e_tile, last_e)
    n_valid = (pend[-1] // tm).astype(jnp.int32)
    return row_src, pos, te, n_valid.reshape(1)


def _hier_moe_ln(x_slab, gates, ids, w1, w3, w2, layer, ln_g, ln_b, tag, split=None):
    n_tok = gates.shape[0]
    row_src, pos, te, n_valid = _route_plan(ids[:, :TOP_K], n_tok)
    ys = _expert_ffn(x_slab, row_src, te, n_valid, w1, w3, w2, layer, name=f"experts_{tag}")
    return _combine_ln(x_slab, ys, pos, gates, ln_g, ln_b, name=f"combine_ln_{tag}", split=split)


def _router_weights(w_group, b_group, w_expert, b_expert):
    w = jnp.concatenate([w_expert, w_group], axis=1)
    w = jnp.pad(w, ((0, 0), (0, 128 - w.shape[1])))
    b = jnp.pad(jnp.concatenate([b_expert, b_group]), (0, 128 - N_EXPERTS - N_EXP_GROUPS))
    return w.astype(BF16), b.reshape(1, 128)


def _pad_conv_state(conv):
    return jnp.pad(conv, ((0, 0), (8 - (CONV_W - 1), 0), (0, 0)))


def kernel(x_prompt, x_sample, state_mlstm_C, state_mlstm_n, state_mlstm_m, state_ssd, state_ssd_conv,
           state_lru_h, state_lru_conv, meta_tokens, w_in_even, mlstm_gate_b, ssd_dt_bias, ssd_A_log, ssd_D,
           ssd_conv_w, ssd_conv_b, mlstm_norm_g, ssd_norm_g, w_out_even, w_in_odd, lru_conv_w, lru_conv_b,
           lru_w_gate, lru_b_gate, lru_lambda, w_out_odd, ln_g, ln_b, moe_w_group, moe_b_group,
           moe_w_expert, moe_b_expert, moe_w1, moe_w3, moe_w2):
    Bp, Tp, D = x_prompt.shape
    Bs, Ts, _ = x_sample.shape
    n_chunks_p = (N_META + Tp + CHUNK - 1) // CHUNK
    Tpp = n_chunks_p * CHUNK
    n_pad = Tpp - N_META - Tp
    n_p = Bp * Tpp
    n_s = Bs * Ts
    n_tok = n_p + n_s

    pieces = []
    for b in range(Bp):
        pieces += [jnp.zeros((n_pad, D), F32), meta_tokens.astype(F32), x_prompt[b]]
    x0 = jnp.concatenate(pieces + [x_sample.reshape(n_s, D)], axis=0)

    grp = [dict(n_seq=Bp, n_chunks=n_chunks_p, L=CHUNK, row0=0, n_pad=n_pad),
           dict(n_seq=Bs, n_chunks=1, L=Ts, row0=n_p, n_pad=0)]

    def seq_view(a, gi):
        if gi == 0:
            return a[:n_p].reshape(Bp, Tpp, a.shape[1])
        return a[n_p:].reshape(Bs, Ts, a.shape[1])

    def tail_rows(a, gi, col0, ncol):
        nb, T, base = (Bp, Tpp, 0) if gi == 0 else (Bs, Ts, n_p)
        idx = (base + np.arange(nb)[:, None] * T + np.arange(T - (CONV_W - 1), T)[None, :]).reshape(-1)
        rows = jnp.take(a, jnp.asarray(idx, jnp.int32), axis=0)
        return rows[:, col0:col0 + ncol].reshape(nb, CONV_W - 1, ncol)

    e = 0
    w_all = _w_relayout(w_in_even[e])
    proj = _mm(x0, w_all, 512, IN0_PAD // 5, name="in_proj_even", n_cols=IN0_PAD)

    small = proj[:, COL_SMALL:COL_SMALL + N_SMALL]
    h_parts, y_parts, st = [], [], []
    for gi, g in enumerate(grp):
        sv = seq_view(small, gi)
        gates_t = jnp.swapaxes(sv, 1, 2)
        nb, T = sv.shape[0], sv.shape[1]
        gates_c = sv[:, :, 2 * A_HEADS:].reshape(nb, T, B_GROUPS, B_HPG).transpose(0, 2, 1, 3)
        if gi == 0:
            c0 = jnp.zeros((Bp, A_HEADS, A_DK, A_DV), F32)
            n0 = jnp.zeros((Bp, A_HEADS, 1, A_DK), F32)
            m0 = jnp.zeros((Bp, A_HEADS), F32)
            s0 = jnp.zeros((Bp, B_HEADS, B_P, B_N), F32)
            cv0 = jnp.zeros((Bp, 8, B_CONV_DIM), F32)
        else:
            c0 = state_mlstm_C[e]
            n0 = state_mlstm_n[e][:, :, None, :]
            m0 = state_mlstm_m[e]
            s0 = state_ssd[e]
            cv0 = _pad_conv_state(state_ssd_conv[e])
        h_g, c_g, n_g, m_g = _mlstm(proj, gates_t, mlstm_gate_b[e], mlstm_norm_g[e], c0, n0, m0,
                                    name=f"mlstm_{gi}", **g)
        y_g, s_g = _ssd(proj, gates_t, gates_c, ssd_dt_bias[e], ssd_A_log[e], ssd_D[e], ssd_conv_w[e],
                        ssd_conv_b[e], ssd_norm_g[e], s0, cv0, name=f"ssd_{gi}", **g)
        h_parts.append(h_g.astype(BF16))
        y_parts.append(y_g.astype(BF16))
        sconv = tail_rows(proj, gi, COL_XBC, B_CONV_DIM)
        st.append((c_g[None], n_g[:, :, 0, :][None], m_g[None], s_g[None], sconv[None]))
    w_r, b_r = _router_weights(moe_w_group[0], moe_b_group[0], moe_w_expert[0], moe_b_expert[0])
    x1s, gates, ids = _mm_parts_ln([h_parts[0], y_parts[0]], [h_parts[1], y_parts[1]],
                                   w_out_even[e].astype(BF16), x0, ln_g[0, 0], ln_b[0, 0], w_r, b_r,
                                   name="out_proj_ln_even")
    x2, x2b = _hier_moe_ln(x1s, gates, ids, moe_w1, moe_w3, moe_w2, 0, ln_g[0, 1], ln_b[0, 1], "0")

    o = 0
    proj1 = _mm(x2b, w_in_odd[o].astype(BF16), 512, D_RNN, name="in_proj_odd")
    wg = lru_w_gate[o].reshape(2, N_LRU_SB, LRU_SB // LRU_BW, LRU_BW, LRU_BW)
    eye4 = jnp.eye(LRU_SB // LRU_BW, dtype=F32)
    wg_sb = jnp.einsum('gjaik,ab->gjaibk', wg, eye4).reshape(2, N_LRU_SB, LRU_SB, LRU_SB).astype(BF16)
    o_parts, st1 = [], []
    for gi, g in enumerate(grp):
        if gi == 0:
            h0 = jnp.zeros((Bp, 1, D_RNN), F32)
            cv0 = jnp.zeros((Bp, 8, D_RNN), F32)
        else:
            h0 = state_lru_h[o][:, None, :]
            cv0 = _pad_conv_state(state_lru_conv[o])
        if g["L"] == 8 and g["n_chunks"] == 1:
            o_g, hs = _lru_rows8(proj1, lru_conv_w[o], lru_conv_b[o], wg_sb, lru_b_gate[o], lru_lambda[o], h0, cv0,
                                 n_seq=g["n_seq"], row0=g["row0"], name=f"lru_{gi}")
            hN = hs.reshape(g["n_seq"], 8, D_RNN)[:, 7, :]
        else:
            o_g, hN = _lru(proj1, lru_conv_w[o], lru_conv_b[o], wg_sb, lru_b_gate[o], lru_lambda[o], h0, cv0,
                           name=f"lru_{gi}", **g)
            hN = hN[:, 0, :]
        o_parts.append(o_g.astype(BF16))
        hconv = tail_rows(proj1, gi, D_RNN, D_RNN)
        st1.append((hN[None], hconv[None]))
    w_r, b_r = _router_weights(moe_w_group[1], moe_b_group[1], moe_w_expert[1], moe_b_expert[1])
    x3s, gates, ids = _mm_parts_ln([o_parts[0]], [o_parts[1]], w_out_odd[o].astype(BF16),
                                   x2, ln_g[1, 0], ln_b[1, 0], w_r, b_r, name="out_proj_ln_odd")
    assert Tpp - Tp == CHUNK
    yp, ys = _hier_moe_ln(x3s, gates, ids, moe_w1, moe_w3, moe_w2, 1, ln_g[1, 1], ln_b[1, 1], "1",
                          split=(n_p, n_chunks_p))
    y_prompt = yp.reshape(Bp, Tp, D)
    y_sample = ys.reshape(Bs, Ts, D)
    (pC, pn, pm, pS, pSc), (sC, sn, sm, sS, sSc) = st
    (pH, pHc), (sH, sHc) = st1
    return (y_prompt, y_sample, pC, pn, pm, pS, pSc, pH, pHc, sC, sn, sm, sS, sSc, sH, sHc)
```

```python
import functools
import math

import jax
import jax.numpy as jnp
import numpy as np
from jax import lax
from jax.experimental import pallas as pl
from jax.experimental.pallas import tpu as pltpu

F32 = jnp.float32
BF16 = jnp.bfloat16

D_MODEL = 2048
N_META = 16
CHUNK = 128
CONV_W = 4
A_HEADS = 8
A_DK = 128
A_DV = 256
A_INNER = A_HEADS * A_DV
B_HEADS = 32
B_P = 64
B_N = 128
B_GROUPS = 4
B_HPG = B_HEADS // B_GROUPS
B_INNER = B_HEADS * B_P
B_GW = B_INNER // B_GROUPS
B_CONV_DIM = B_INNER + 2 * B_GROUPS * B_N
D_RNN = 2560
LRU_BLOCKS = 16
LRU_BW = D_RNN // LRU_BLOCKS
LRU_C = 8.0
LRU_SB = 640
N_LRU_SB = D_RNN // LRU_SB
N_EXP_GROUPS = 4
EXP_PER_GROUP = 8
N_EXPERTS = N_EXP_GROUPS * EXP_PER_GROUP
TOP_K = 2
D_EXPERT = 512
DEPTH = 2
ALPHA = (2.0 * DEPTH) ** 0.25

COL_Q = 0
COL_K = A_HEADS * A_DK
COL_V = 2 * A_HEADS * A_DK
COL_O = COL_V + A_INNER
COL_Z = COL_O + A_INNER
COL_XBC = COL_Z + B_INNER
COL_SMALL = COL_XBC + B_CONV_DIM
N_SMALL = 2 * A_HEADS + B_HEADS
IN0_PAD = 11520

NEG_BIG = -1e30
VMEM_LIMIT_BYTES = 48 * 1024 * 1024
TILE_ROWS = 256
FFN_AHEAD = 2
LN_ROWS = 256
SLAB = D_MODEL // 128


def _cparams(n_axes):
    return pltpu.CompilerParams(dimension_semantics=("arbitrary",) * n_axes,
                                vmem_limit_bytes=VMEM_LIMIT_BYTES)


def _smem_spec():
    return pl.BlockSpec(memory_space=pltpu.SMEM)


def _mm_kernel(x_ref, w_ref, o_ref):
    o_ref[...] = jnp.dot(x_ref[...].astype(BF16), w_ref[...], preferred_element_type=F32).astype(o_ref.dtype)


def _mm(x, w, bm, bn, name, n_cols=None):
    M, K = x.shape
    N = w.shape[1] if n_cols is None else n_cols
    return pl.pallas_call(
        _mm_kernel,
        grid=(N // bn, M // bm),
        in_specs=[pl.BlockSpec((bm, K), lambda j, i: (i, 0)),
                  pl.BlockSpec((K, bn), lambda j, i: (0, j))],
        out_specs=pl.BlockSpec((bm, bn), lambda j, i: (i, j)),
        out_shape=jax.ShapeDtypeStruct((M, N), F32),
        compiler_params=_cparams(2),
        name=name,
    )(x, w)


RELAY_W = 1024


def _w_relayout_kernel(a_ref, b_ref, g_ref, o_ref, *, n_same, n_shift, shift):
    j = pl.program_id(0)
    lane = lax.broadcasted_iota(jnp.int32, (1, 128), 1)
    n_t = RELAY_W // 128

    @pl.when(j < n_same)
    def _():
        o_ref[...] = a_ref[...].astype(BF16)

    @pl.when((j >= n_same) & (j < n_same + n_shift))
    def _():
        cur = pltpu.roll(a_ref[:, 0:128], 128 - shift, 1)
        for t in range(n_t):
            nxt_src = a_ref[:, (t + 1) * 128:(t + 2) * 128] if t + 1 < n_t else b_ref[...]
            nxt = pltpu.roll(nxt_src, 128 - shift, 1)
            o_ref[:, t * 128:(t + 1) * 128] = jnp.where(lane < 128 - shift, cur, nxt).astype(BF16)
            cur = nxt

    @pl.when(j == n_same + n_shift)
    def _():
        head = jnp.where(lane < 2 * A_HEADS, g_ref[...], jnp.where(lane < N_SMALL, a_ref[:, 0:128], 0.0))
        o_ref[:, 0:128] = head.astype(BF16)
        o_ref[:, 128:] = jnp.zeros((o_ref.shape[0], RELAY_W - 128), BF16)


def _w_relayout(w, layer):
    K = w.shape[1]
    src_small = 2 * A_HEADS * A_DK + 2 * A_INNER
    n_same = src_small // RELAY_W
    n_shift = (B_INNER + B_CONV_DIM) // RELAY_W
    n_blk = n_same + n_shift + 1
    assert src_small % RELAY_W == 0 and (B_INNER + B_CONV_DIM) % RELAY_W == 0 and n_blk * RELAY_W >= IN0_PAD
    assert COL_SMALL == (n_blk - 1) * RELAY_W and w.shape[2] == COL_SMALL + N_SMALL
    per = RELAY_W // 128
    return pl.pallas_call(
        functools.partial(_w_relayout_kernel, n_same=n_same, n_shift=n_shift, shift=2 * A_HEADS),
        grid=(n_blk,),
        in_specs=[pl.BlockSpec((None, K, RELAY_W), lambda j: (layer, 0, j)),
                  pl.BlockSpec((None, K, 128), lambda j: (layer, 0, jnp.minimum(j + 1, n_blk - 1) * per)),
                  pl.BlockSpec((None, K, 128), lambda j: (layer, 0, src_small // 128))],
        out_specs=pl.BlockSpec((K, RELAY_W), lambda j: (0, j)),
        out_shape=jax.ShapeDtypeStruct((K, n_blk * RELAY_W), BF16),
        compiler_params=_cparams(1),
        name="w_in_even_relayout",
    )(w, w, w)


def _layer_norm_rows(y, g, b):
    mu = jnp.mean(y, axis=-1, keepdims=True)
    yc = y - mu
    var = jnp.mean(yc * yc, axis=-1, keepdims=True)
    return yc * lax.rsqrt(var + 1e-5) * g + b


def _mm_parts_ln_kernel(*refs, k_sizes, n_blk_a):
    n = len(k_sizes)
    a_refs, b_refs = refs[:n], refs[n:2 * n]
    w_ref, x_ref, g_ref, b_ref, wr_ref, br_ref, os_ref, gate_ref, id_ref = refs[2 * n:]
    i = pl.program_id(0)

    def run(lhs_refs):
        acc, k0 = None, 0
        for r, ks in zip(lhs_refs, k_sizes):
            part = jnp.dot(r[...], w_ref[k0:k0 + ks, :], preferred_element_type=F32)
            acc = part if acc is None else acc + part
            k0 += ks
        y = _layer_norm_rows(ALPHA * x_ref[...] + acc, g_ref[...], b_ref[...])
        _slab_store(os_ref, y)
        logits = jnp.dot(y.astype(BF16), wr_ref[...], preferred_element_type=F32) + br_ref[...]
        gate_ref[...], id_ref[...] = _route(logits)

    @pl.when(i < n_blk_a)
    def _():
        run(a_refs)

    @pl.when(i >= n_blk_a)
    def _():
        run(b_refs)


def _mm_parts_ln(parts_a, parts_b, w, x, g, b, w_r, b_r, name):
    bm = LN_ROWS
    k_sizes = tuple(p.shape[1] for p in parts_a)
    n_blk_a = parts_a[0].shape[0] // bm
    n_blk_b = parts_b[0].shape[0] // bm
    K, D = w.shape
    M = (n_blk_a + n_blk_b) * bm
    a_specs = [pl.BlockSpec((bm, ks), lambda i: (jnp.minimum(i, n_blk_a - 1), 0)) for ks in k_sizes]
    b_specs = [pl.BlockSpec((bm, ks), lambda i: (jnp.maximum(i - n_blk_a, 0), 0)) for ks in k_sizes]
    row = pl.BlockSpec((bm, D), lambda i: (i, 0))
    vec = pl.BlockSpec((1, D), lambda i: (0, 0))
    tile = pl.BlockSpec((bm, 128), lambda i: (i, 0))
    w_spec = pl.BlockSpec((K, D), lambda i: (0, 0), pipeline_mode=pl.Buffered(1))
    return pl.pallas_call(
        functools.partial(_mm_parts_ln_kernel, k_sizes=k_sizes, n_blk_a=n_blk_a),
        grid=(n_blk_a + n_blk_b,),
        in_specs=a_specs + b_specs + [w_spec, row, vec, vec, pl.BlockSpec((D, 128), lambda i: (0, 0)),
                                      pl.BlockSpec((1, 128), lambda i: (0, 0))],
        out_specs=[pl.BlockSpec((bm * SLAB, 128), lambda i: (i, 0)), tile, tile],
        out_shape=[jax.ShapeDtypeStruct((M * SLAB, 128), F32), jax.ShapeDtypeStruct((M, 128), F32),
                   jax.ShapeDtypeStruct((M, 128), jnp.int32)],
        compiler_params=_cparams(1),
        name=name,
    )(*parts_a, *parts_b, w, x, g.reshape(1, D), b.reshape(1, D), w_r, b_r)


def _slab_store(ref, val):
    tm = val.shape[0]
    for j in range(SLAB):
        ref[pl.ds(j, tm, stride=SLAB), :] = val[:, j * 128:(j + 1) * 128]


def _slab_piece(ref, j, tm):
    return ref[pl.ds(j, tm, stride=SLAB), :]


def _slab_copy(src_hbm, dst_vmem, sem, src_tok, dst_tok):
    return pltpu.make_async_copy(src_hbm.at[pl.ds(pl.multiple_of(src_tok * SLAB, SLAB), SLAB), :],
                                 dst_vmem.at[pl.ds(pl.multiple_of(dst_tok * SLAB, SLAB), SLAB), :], sem)


def _issue_gather(idx_ref, idx0, src_hbm, dst_vmem, dst0, sem, n):
    def body(r, carry):
        _slab_copy(src_hbm, dst_vmem, sem, idx_ref[idx0 + r], dst0 + r).start()
        return carry
    lax.fori_loop(0, n, body, 0, unroll=8)


def _wait_gather(src_hbm, dst_vmem, sem, n):
    def body(r, carry):
        _slab_copy(src_hbm, dst_vmem, sem, 0, 0).wait()
        return carry
    lax.fori_loop(0, n, body, 0, unroll=8)


def _combine_rows(pos_ref, x_ref, y_hbm, gate_ref, g_ref, b_ref, v_s, ybuf, sems, n_tok, tm):
    i = pl.program_id(0)
    n = pl.num_programs(0)
    slot = i % 2

    def fetch(tile, sl):
        for kk in range(TOP_K):
            _issue_gather(pos_ref, kk * n_tok + tile * tm, y_hbm, ybuf, (sl * TOP_K + kk) * tm, sems.at[sl], tm)

    @pl.when(i == 0)
    def _():
        fetch(0, 0)

    @pl.when(i + 1 < n)
    def _():
        fetch(i + 1, 1 - slot)

    _wait_gather(y_hbm, ybuf, sems.at[slot], TOP_K * tm)
    gate = gate_ref[...]
    g0 = gate[:, 0:1]
    g1 = gate[:, 1:2]
    base0 = slot * (TOP_K * tm * SLAB)
    base1 = base0 + tm * SLAB
    for j in range(SLAB):
        v_s[:, j * 128:(j + 1) * 128] = (ALPHA * _slab_piece(x_ref, j, tm)
                                         + g0 * ybuf[pl.ds(base0 + j, tm, stride=SLAB), :]
                                         + g1 * ybuf[pl.ds(base1 + j, tm, stride=SLAB), :])
    return _layer_norm_rows(v_s[...], g_ref[...], b_ref[...])


def _combine_ln_kernel(pos_ref, x_ref, y_hbm, gate_ref, g_ref, b_ref, o_ref, ob_ref, v_s, ybuf, sems, *, n_tok):
    y = _combine_rows(pos_ref, x_ref, y_hbm, gate_ref, g_ref, b_ref, v_s, ybuf, sems, n_tok, LN_ROWS)
    o_ref[...] = y
    ob_ref[...] = y.astype(BF16)


def _combine_out_kernel(pos_ref, x_ref, y_hbm, gate_ref, g_ref, b_ref, yp_ref, ys_ref, v_s, ybuf, sems, *,
                        n_tok, n_blk_p, blk_per_seq):
    y = _combine_rows(pos_ref, x_ref, y_hbm, gate_ref, g_ref, b_ref, v_s, ybuf, sems, n_tok, CHUNK)
    i = pl.program_id(0)

    @pl.when((i < n_blk_p) & (i % blk_per_seq > 0))
    def _():
        yp_ref[...] = y

    @pl.when(i >= n_blk_p)
    def _():
        ys_ref[...] = y


def _combine_ln(x_slab, y_slab, pos, gates, g, b, name, split=None):
    M = gates.shape[0]
    D = D_MODEL
    tm = LN_ROWS if split is None else CHUNK
    vec = pl.BlockSpec((1, D), lambda i, pos: (0, 0))
    if split is None:
        kern = functools.partial(_combine_ln_kernel, n_tok=M)
        row = pl.BlockSpec((tm, D), lambda i, pos: (i, 0))
        out_specs = [row, row]
        out_shape = [jax.ShapeDtypeStruct((M, D), F32), jax.ShapeDtypeStruct((M, D), BF16)]
    else:
        n_p, bps = split
        n_blk_p = n_p // tm
        kern = functools.partial(_combine_out_kernel, n_tok=M, n_blk_p=n_blk_p, blk_per_seq=bps)

        def yp_map(i, pos):
            ip = jnp.minimum(i, n_blk_p - 1)
            return ((ip // bps) * (bps - 1) + jnp.maximum(ip % bps - 1, 0), 0)

        out_specs = [pl.BlockSpec((tm, D), yp_map),
                     pl.BlockSpec((tm, D), lambda i, pos: (jnp.maximum(i - n_blk_p, 0), 0))]
        out_shape = [jax.ShapeDtypeStruct((n_blk_p // bps * (bps - 1) * tm, D), F32),
                     jax.ShapeDtypeStruct((M - n_p, D), F32)]
    return pl.pallas_call(
        kern,
        grid_spec=pltpu.PrefetchScalarGridSpec(
            num_scalar_prefetch=1,
            grid=(M // tm,),
            in_specs=[pl.BlockSpec((tm * SLAB, 128), lambda i, pos: (i, 0)),
                      pl.BlockSpec(memory_space=pl.ANY),
                      pl.BlockSpec((tm, 128), lambda i, pos: (i, 0)), vec, vec],
            out_specs=out_specs,
            scratch_shapes=[pltpu.VMEM((tm, D), F32), pltpu.VMEM((2 * TOP_K * tm * SLAB, 128), F32),
                            pltpu.SemaphoreType.DMA((2,))],
        ),
        out_shape=out_shape,
        compiler_params=_cparams(1),
        name=name,
    )(pos, x_slab, y_slab, gates, g.reshape(1, D), b.reshape(1, D))


def _softplus(x):
    return jnp.maximum(x, 0.0) + jnp.log1p(jnp.exp(-jnp.abs(x)))


def _sigmoid(x):
    return 1.0 / (1.0 + jnp.exp(-x))


def _silu(x):
    return x * _sigmoid(x)


def _dot_nn(m, x):
    return jnp.dot(m, x, preferred_element_type=F32)


def _dot_nt(a, b):
    return lax.dot_general(a, b, (((1,), (1,)), ((), ())), preferred_element_type=F32)


def _row_to_col(row, eye):
    return jnp.sum(jnp.where(eye, row, 0.0), axis=1, keepdims=True)


def _causal_conv(x, carry_ref, buf_ref, w_ref, b_ref, L):
    buf_ref[0:8, :] = carry_ref[...]
    buf_ref[8:8 + L, :] = x
    acc = b_ref[...] + w_ref[0:1, :] * buf_ref[5:5 + L, :]
    for j in range(1, CONV_W):
        acc = acc + w_ref[j:j + 1, :] * buf_ref[5 + j:5 + j + L, :]
    carry_ref[...] = buf_ref[L:L + 8, :]
    return acc


def _mlstm_kernel(gb_ref, q_ref, k_ref, v_ref, o_ref, gt_ref, c0_ref, n0_ref, m0_ref, ng_ref,
                  h_ref, c_out_ref, n_out_ref, m_out_ref, *state, L, n_pad):
    c_s, n_s, m_s = state[:A_HEADS], state[A_HEADS:2 * A_HEADS], state[2 * A_HEADS:]
    c = pl.program_id(1)

    @pl.when(c == 0)
    def _():
        for hd in range(A_HEADS):
            c_s[hd][...] = c0_ref[0, hd]
            n_s[hd][...] = n0_ref[0, hd]
            m_s[hd][...] = m0_ref[0, hd]

    rows = lax.broadcasted_iota(jnp.int32, (L, L), 0)
    cols = lax.broadcasted_iota(jnp.int32, (L, L), 1)
    eye = rows == cols
    causal = cols <= rows

    if n_pad:
        pad = (lax.broadcasted_iota(jnp.int32, (1, L), 1) < n_pad) & (c == 0)

    for hd in range(A_HEADS):
        li = gt_ref[0, hd:hd + 1, :] + gb_ref[0, hd]
        fr = gt_ref[0, A_HEADS + hd:A_HEADS + hd + 1, :] + gb_ref[1, hd]
        lf = jnp.minimum(fr, 0.0) - jnp.log1p(jnp.exp(-jnp.abs(fr)))
        if n_pad:
            li = jnp.where(pad, NEG_BIG, li)
            lf = jnp.where(pad, 0.0, lf)

        lf_col = _row_to_col(lf, eye)
        b_col = jnp.sum(jnp.where(causal, lf, 0.0), axis=1, keepdims=True)
        b_row = jnp.sum(jnp.where(rows <= cols, lf_col, 0.0), axis=0, keepdims=True)
        m_prev = m_s[hd][:, 0:1]

        d = jnp.where(causal, b_col - b_row + li, NEG_BIG)
        inter = b_col + m_prev
        m_t = jnp.maximum(inter, jnp.max(d, axis=1, keepdims=True))
        w_intra = jnp.exp(d - m_t)
        w_inter = jnp.exp(inter - m_t)

        q = q_ref[:, hd * A_DK:(hd + 1) * A_DK] * (A_DK ** -0.5)
        k = k_ref[:, hd * A_DK:(hd + 1) * A_DK]
        v = v_ref[:, hd * A_DV:(hd + 1) * A_DV]
        qb = q.astype(BF16)
        kb = k.astype(BF16)
        vb = v.astype(BF16)
        s = _dot_nt(qb, kb) * w_intra
        c_prev = c_s[hd][...]
        n_prev = n_s[hd][...]
        num = _dot_nn(s.astype(BF16), vb)
        num = num + w_inter * jnp.dot(qb, c_prev.astype(BF16), preferred_element_type=F32)
        qn = jnp.sum(qb.astype(F32) * n_prev.astype(BF16).astype(F32), axis=1, keepdims=True)
        nq = jnp.sum(s, axis=1, keepdims=True) + w_inter * qn
        hh = num / jnp.maximum(jnp.abs(nq), jnp.exp(-m_t))

        hn = hh * lax.rsqrt(jnp.mean(hh * hh, axis=-1, keepdims=True) + 1e-6)
        hn = hn * ng_ref[:, hd * A_DV:(hd + 1) * A_DV]
        h_ref[:, hd * A_DV:(hd + 1) * A_DV] = (
            hn * _sigmoid(o_ref[:, hd * A_DV:(hd + 1) * A_DV])).astype(h_ref.dtype)

        b_last = b_row[:, L - 1:L]
        ws_log = b_last - b_row + li
        m_new = jnp.maximum(b_last + m_prev, jnp.max(ws_log, axis=1, keepdims=True))
        ws_col = _row_to_col(jnp.exp(ws_log - m_new), eye)
        wc = jnp.exp(b_last + m_prev - m_new)
        kw = k * ws_col
        c_new = wc * c_prev + lax.dot_general(kw.astype(BF16), vb, (((0,), (0,)), ((), ())),
                                              preferred_element_type=F32)
        n_new = wc * n_prev + jnp.sum(ws_col.astype(BF16).astype(F32) * kb.astype(F32), axis=0, keepdims=True)
        m_new_b = jnp.broadcast_to(m_new, (1, 128))
        c_s[hd][...] = c_new
        n_s[hd][...] = n_new
        m_s[hd][...] = m_new_b
        c_out_ref[0, hd] = c_new
        n_out_ref[0, hd] = n_new
        m_out_ref[0, hd] = m_new_b


def _mlstm(proj, gates_t, gate_b, norm_g, c0, n0, m0, *, n_seq, n_chunks, L, row0, n_pad, name):
    rb0 = row0 // L

    def rblk(b, c):
        return rb0 + b * n_chunks + c

    out_dtype = BF16 if L % 16 == 0 else F32
    qk_w = A_HEADS * A_DK
    state_map = lambda b, c: (b, 0, 0, 0)
    m0b = jnp.broadcast_to(m0[:, :, None, None], (n_seq, A_HEADS, 1, 128))
    kern = functools.partial(_mlstm_kernel, L=L, n_pad=n_pad)
    h, c_new, n_new, m_new = pl.pallas_call(
        kern,
        grid=(n_seq, n_chunks),
        in_specs=[
            _smem_spec(),
            pl.BlockSpec((L, qk_w), lambda b, c: (rblk(b, c), COL_Q // qk_w)),
            pl.BlockSpec((L, qk_w), lambda b, c: (rblk(b, c), COL_K // qk_w)),
            pl.BlockSpec((L, A_INNER), lambda b, c: (rblk(b, c), COL_V // A_INNER)),
            pl.BlockSpec((L, A_INNER), lambda b, c: (rblk(b, c), COL_O // A_INNER)),
            pl.BlockSpec((1, N_SMALL, L), lambda b, c: (b, 0, c)),
            pl.BlockSpec((1, A_HEADS, A_DK, A_DV), state_map),
            pl.BlockSpec((1, A_HEADS, 1, A_DK), state_map),
            pl.BlockSpec((1, A_HEADS, 1, 128), state_map),
            pl.BlockSpec((1, A_INNER), lambda b, c: (0, 0)),
        ],
        out_specs=[
            pl.BlockSpec((L, A_INNER), lambda b, c: (b * n_chunks + c, 0)),
            pl.BlockSpec((1, A_HEADS, A_DK, A_DV), state_map),
            pl.BlockSpec((1, A_HEADS, 1, A_DK), state_map),
            pl.BlockSpec((1, A_HEADS, 1, 128), state_map),
        ],
        out_shape=[
            jax.ShapeDtypeStruct((n_seq * n_chunks * L, A_INNER), out_dtype),
            jax.ShapeDtypeStruct((n_seq, A_HEADS, A_DK, A_DV), F32),
            jax.ShapeDtypeStruct((n_seq, A_HEADS, 1, A_DK), F32),
            jax.ShapeDtypeStruct((n_seq, A_HEADS, 1, 128), F32),
        ],
        scratch_shapes=([pltpu.VMEM((A_DK, A_DV), F32)] * A_HEADS + [pltpu.VMEM((1, A_DK), F32)] * A_HEADS
                        + [pltpu.VMEM((1, 128), F32)] * A_HEADS),
        compiler_params=_cparams(2),
        name=name,
    )(gate_b, proj, proj, proj, proj, gates_t, c0, n0, m0b, norm_g.reshape(1, A_INNER))
    return h, c_new, n_new, m_new[:, :, 0, 0]


def _ssd_kernel(dtr_ref, dtc_ref, alr_ref, alc_ref, xs_ref, bm_ref, cm_ref, z_ref, gt_ref, gc_ref, s0_ref,
                cx0_ref, cb0_ref, cc0_ref, wx_ref, wb_ref, wc_ref, bx_ref, bb_ref, bc_ref, ng_ref, dsk_ref,
                y_ref, s_out_ref,
                cx_s, cb_s, cc_s, bufx, bufb, bufc, ybuf, *s_s, L, n_pad, gps):
    grp0 = pl.program_id(1) * gps
    c = pl.program_id(2)

    @pl.when(c == 0)
    def _():
        for pair in range(len(s_s)):
            s_s[pair][...] = s0_ref[0, pair]
        cx_s[...] = cx0_ref[0]
        cb_s[...] = cb0_ref[0]
        cc_s[...] = cc0_ref[0]

    rows = lax.broadcasted_iota(jnp.int32, (L, L), 0)
    cols = lax.broadcasted_iota(jnp.int32, (L, L), 1)
    causal = cols <= rows
    tril = causal.astype(F32)
    triu = (rows <= cols).astype(F32)
    lane_lo = lax.broadcasted_iota(jnp.int32, (L, 2 * B_P), 1) < B_P
    row_lo = lax.broadcasted_iota(jnp.int32, (2 * B_P, 1), 0) < B_P
    if n_pad:
        pad_row = (lax.broadcasted_iota(jnp.int32, (1, L), 1) < n_pad) & (c == 0)
        pad_col = (lax.broadcasted_iota(jnp.int32, (L, 1), 0) < n_pad) & (c == 0)

    xs_raw = xs_ref[...]
    bm_raw = bm_ref[...]
    cm_raw = cm_ref[...]
    if n_pad:
        xs_raw = jnp.where(pad_col, 0.0, xs_raw)
        bm_raw = jnp.where(pad_col, 0.0, bm_raw)
        cm_raw = jnp.where(pad_col, 0.0, cm_raw)
    xs = _silu(_causal_conv(xs_raw, cx_s, bufx, wx_ref, bx_ref, L))
    bm = _silu(_causal_conv(bm_raw, cb_s, bufb, wb_ref, bb_ref, L))
    cm = _silu(_causal_conv(cm_raw, cc_s, bufc, wc_ref, bc_ref, L))

    def two(col, h0):
        return jnp.where(lane_lo, col[:, h0:h0 + 1], col[:, h0 + 1:h0 + 2])

    for gg in range(gps):
        bmb = bm[:, gg * B_N:(gg + 1) * B_N].astype(BF16)
        cmb = cm[:, gg * B_N:(gg + 1) * B_N].astype(BF16)
        cb = lax.dot_general(cmb, bmb, (((1,), (1,)), ((), ())), preferred_element_type=F32)

        row0 = pl.multiple_of(2 * A_HEADS + (grp0 + gg) * B_HPG, B_HPG)
        dt_c = _softplus(gc_ref[0, gg] + dtr_ref[gg])
        dt_r = _softplus(gt_ref[0, pl.ds(row0, B_HPG), :] + dtc_ref[gg])
        if n_pad:
            dt_c = jnp.where(pad_col, 0.0, dt_c)
            dt_r = jnp.where(pad_row, 0.0, dt_r)
        a_c = dt_c * (-jnp.exp(alr_ref[gg]))
        a_r = dt_r * (-jnp.exp(alc_ref[gg]))
        b_c = jnp.dot(tril, a_c, preferred_element_type=F32, precision=lax.Precision.HIGHEST)
        b_r = jnp.dot(a_r, triu, preferred_element_type=F32, precision=lax.Precision.HIGHEST)
        eb_c = jnp.exp(b_c)
        b_last = b_c[L - 1:L, :]
        w_c = jnp.exp(b_last - b_c)
        e_last = jnp.exp(b_last)

        for p in range(B_HPG // 2):
            h0 = 2 * p
            pair = gg * (B_HPG // 2) + p
            x_pair = xs[:, pair * 2 * B_P:(pair + 1) * 2 * B_P]
            xdt = x_pair * two(dt_c, h0)
            y = None
            for hh, keep in ((h0, lane_lo), (h0 + 1, ~lane_lo)):
                decay = jnp.exp(jnp.where(causal, b_c[:, hh:hh + 1] - b_r[hh:hh + 1, :], NEG_BIG))
                part = _dot_nn((cb * decay).astype(BF16), jnp.where(keep, xdt, 0.0).astype(BF16))
                y = part if y is None else y + part
            s_prev = s_s[pair][...]
            y = y + two(eb_c, h0) * lax.dot_general(cmb, s_prev.astype(BF16), (((1,), (1,)), ((), ())),
                                                    preferred_element_type=F32)
            upd = lax.dot_general((xdt * two(w_c, h0)).astype(BF16), bmb, (((0,), (0,)), ((), ())),
                                  preferred_element_type=F32)
            s_new = jnp.where(row_lo, e_last[:, h0:h0 + 1], e_last[:, h0 + 1:h0 + 2]) * s_prev + upd
            s_s[pair][...] = s_new
            s_out_ref[0, pair] = s_new
            ybuf[:, pair * 2 * B_P:(pair + 1) * 2 * B_P] = y

    y = (ybuf[...] + dsk_ref[...] * xs) * _silu(z_ref[...])
    for gg in range(gps):
        yg = y[:, gg * B_GW:(gg + 1) * B_GW]
        yn = yg * lax.rsqrt(jnp.mean(yg * yg, axis=-1, keepdims=True) + 1e-6) * ng_ref[:, gg * B_GW:(gg + 1) * B_GW]
        y_ref[:, gg * B_GW:(gg + 1) * B_GW] = yn.astype(y_ref.dtype)


def _ssd(proj, gates_t, gates_c, dt_bias, a_log, d_skip, conv_w, conv_b, norm_g, s0, conv0, *,
         n_seq, n_chunks, L, row0, n_pad, name):
    rb0 = row0 // L

    def rblk(b, c):
        return rb0 + b * n_chunks + c

    out_dtype = BF16 if L % 16 == 0 else F32
    gps = B_GROUPS if n_chunks == 1 else 1
    xw, nw = gps * B_GW, gps * B_N
    n_pairs = gps * B_HPG // 2
    cw = conv_w
    cbias = conv_b.reshape(1, B_CONV_DIM)
    xoff = COL_XBC // xw
    boff = (COL_XBC + B_INNER) // nw
    coff = (COL_XBC + B_INNER + B_GROUPS * B_N) // nw
    cvb = B_INNER // nw
    cvc = (B_INNER + B_GROUPS * B_N) // nw
    dt_r = dt_bias.reshape(B_GROUPS, 1, B_HPG)
    dt_c = dt_bias.reshape(B_GROUPS, B_HPG, 1)
    al_r = a_log.reshape(B_GROUPS, 1, B_HPG)
    al_c = a_log.reshape(B_GROUPS, B_HPG, 1)
    d_row = jnp.repeat(d_skip, B_P).reshape(1, B_INNER)
    s0p = s0.reshape(n_seq, B_HEADS // 2, 2 * B_P, B_N)
    prow = pl.BlockSpec((gps, 1, B_HPG), lambda b, g, c: (g, 0, 0))
    pcol = pl.BlockSpec((gps, B_HPG, 1), lambda b, g, c: (g, 0, 0))
    kern = functools.partial(_ssd_kernel, L=L, n_pad=n_pad, gps=gps)
    y, s_new = pl.pallas_call(
        kern,
        grid=(n_seq, B_GROUPS // gps, n_chunks),
        in_specs=[
            prow, pcol, prow, pcol,
            pl.BlockSpec((L, xw), lambda b, g, c: (rblk(b, c), xoff + g)),
            pl.BlockSpec((L, nw), lambda b, g, c: (rblk(b, c), boff + g)),
            pl.BlockSpec((L, nw), lambda b, g, c: (rblk(b, c), coff + g)),
            pl.BlockSpec((L, xw), lambda b, g, c: (rblk(b, c), COL_Z // xw + g)),
            pl.BlockSpec((1, N_SMALL, L), lambda b, g, c: (b, 0, c)),
            pl.BlockSpec((1, gps, L, B_HPG), lambda b, g, c: (b, g, c, 0)),
            pl.BlockSpec((1, n_pairs, 2 * B_P, B_N), lambda b, g, c: (b, g, 0, 0)),
            pl.BlockSpec((1, 8, xw), lambda b, g, c: (b, 0, g)),
            pl.BlockSpec((1, 8, nw), lambda b, g, c: (b, 0, cvb + g)),
            pl.BlockSpec((1, 8, nw), lambda b, g, c: (b, 0, cvc + g)),
            pl.BlockSpec((CONV_W, xw), lambda b, g, c: (0, g)),
            pl.BlockSpec((CONV_W, nw), lambda b, g, c: (0, cvb + g)),
            pl.BlockSpec((CONV_W, nw), lambda b, g, c: (0, cvc + g)),
            pl.BlockSpec((1, xw), lambda b, g, c: (0, g)),
            pl.BlockSpec((1, nw), lambda b, g, c: (0, cvb + g)),
            pl.BlockSpec((1, nw), lambda b, g, c: (0, cvc + g)),
            pl.BlockSpec((1, xw), lambda b, g, c: (0, g)),
            pl.BlockSpec((1, xw), lambda b, g, c: (0, g)),
        ],
        out_specs=[
            pl.BlockSpec((L, xw), lambda b, g, c: (b * n_chunks + c, g)),
            pl.BlockSpec((1, n_pairs, 2 * B_P, B_N), lambda b, g, c: (b, g, 0, 0)),
        ],
        out_shape=[
            jax.ShapeDtypeStruct((n_seq * n_chunks * L, B_INNER), out_dtype),
            jax.ShapeDtypeStruct((n_seq, B_HEADS // 2, 2 * B_P, B_N), F32),
        ],
        scratch_shapes=[
            pltpu.VMEM((8, xw), F32), pltpu.VMEM((8, nw), F32), pltpu.VMEM((8, nw), F32),
            pltpu.VMEM((L + 8, xw), F32), pltpu.VMEM((L + 8, nw), F32), pltpu.VMEM((L + 8, nw), F32),
            pltpu.VMEM((L, xw), F32),
        ] + [pltpu.VMEM((2 * B_P, B_N), F32)] * n_pairs,
        compiler_params=_cparams(3),
        name=name,
    )(dt_r, dt_c, al_r, al_c, proj, proj, proj, proj, gates_t, gates_c, s0p, conv0, conv0, conv0,
      cw, cw, cw, cbias, cbias, cbias, norm_g.reshape(1, B_INNER), d_row)
    return y, s_new.reshape(n_seq, B_HEADS, B_P, B_N)


def _gelu_tanh(x):
    return 0.5 * x * (1.0 + jnp.tanh(math.sqrt(2.0 / math.pi) * (x + 0.044715 * (x * x * x))))


def _expm1_nonpos(z):
    e = jnp.exp(z)
    safe = (e < 1.0) & (z > -1.0)
    return jnp.where(safe, (e - 1.0) * z / jnp.log(jnp.where(safe, e, 0.5)), jnp.where(e == 1.0, z, e - 1.0))


def _lru_gates(xf, wg_ref, bg_ref, lam_ref):
    xfb = xf.astype(BF16)
    r = _sigmoid(jnp.dot(xfb, wg_ref[0, 0], preferred_element_type=F32) + bg_ref[0:1, :])
    i = _sigmoid(jnp.dot(xfb, wg_ref[1, 0], preferred_element_type=F32) + bg_ref[1:2, :])
    log_a = (-LRU_C) * r * _softplus(-lam_ref[...])
    a = jnp.exp(log_a)
    u = jnp.sqrt(-_expm1_nonpos(2.0 * log_a)) * (i * xf)
    return a, u


def _scan_rows8(a, u, t8):
    for k in (1, 2, 4):
        keep = t8 >= k
        a_sh = jnp.where(keep, pltpu.roll(a, k, 0), 1.0)
        u_sh = jnp.where(keep, pltpu.roll(u, k, 0), 0.0)
        u = a * u_sh + u
        a = a * a_sh
    return a, u


def _lru_kernel(gi_ref, xb_ref, h0_ref, cv0_ref, cw_ref, cb_ref, wg_ref, bg_ref, lam_ref,
                o_ref, h_out_ref, h_s, cv_s, buf, *, L, n_pad):
    c = pl.program_id(2)

    @pl.when(c == 0)
    def _():
        h_s[...] = h0_ref[0]
        cv_s[...] = cv0_ref[0]

    row_id = lax.broadcasted_iota(jnp.int32, (L, 1), 0)
    x_raw = xb_ref[...]
    if n_pad:
        pad_col = (row_id < n_pad) & (c == 0)
        x_raw = jnp.where(pad_col, 0.0, x_raw)
    xf = _causal_conv(x_raw, cv_s, buf, cw_ref, cb_ref, L)
    a, u = _lru_gates(xf, wg_ref, bg_ref, lam_ref)
    if n_pad:
        a = jnp.where(pad_col, 1.0, a)
        u = jnp.where(pad_col, 0.0, u)
    a, u = _scan_rows8(a, u, row_id % 8)
    h_in = h_s[...]
    pieces = []
    for t in range(L // 8):
        piece = a[t * 8:(t + 1) * 8, :] * h_in + u[t * 8:(t + 1) * 8, :]
        pieces.append(piece)
        h_in = piece[7:8, :]
    hs = jnp.concatenate(pieces, axis=0)
    h_s[...] = h_in
    h_out_ref[0] = h_in
    o_ref[...] = (hs * _gelu_tanh(gi_ref[...])).astype(o_ref.dtype)


def _lru_rows8_kernel(gi_ref, xb_ref, h0_ref, hist_ref, cw_ref, cb_ref, wg_ref, bg_ref, lam_ref,
                      o_ref, hs_ref):
    R = xb_ref.shape[0]
    t8 = lax.broadcasted_iota(jnp.int32, (R, 1), 0) % 8
    x = xb_ref[...]
    hist = hist_ref[...]
    acc = cb_ref[...] + cw_ref[CONV_W - 1:CONV_W, :] * x
    for j in range(CONV_W - 1):
        sh = CONV_W - 1 - j
        tap = jnp.where(t8 >= sh, pltpu.roll(x, sh, 0), pltpu.roll(hist, (sh - 8) % R, 0))
        acc = acc + cw_ref[j:j + 1, :] * tap
    a, u = _lru_gates(acc, wg_ref, bg_ref, lam_ref)
    a, u = _scan_rows8(a, u, t8)
    hs = a * h0_ref[...] + u
    hs_ref[...] = hs
    o_ref[...] = (hs * _gelu_tanh(gi_ref[...])).astype(o_ref.dtype)


def _lru_rows8(proj, conv_w, conv_b, wg_sb, b_gate, lam, h0, conv0, *, n_seq, row0, name):
    R = 128
    n_rows = n_seq * 8
    rb0 = row0 // R
    h0x = jnp.repeat(h0.reshape(n_seq, D_RNN), 8, axis=0)
    hist = conv0.reshape(n_rows, D_RNN)
    blk = lambda j, i: (i, j)
    return pl.pallas_call(
        _lru_rows8_kernel,
        grid=(N_LRU_SB, n_rows // R),
        in_specs=[
            pl.BlockSpec((R, LRU_SB), lambda j, i: (rb0 + i, j)),
            pl.BlockSpec((R, LRU_SB), lambda j, i: (rb0 + i, N_LRU_SB + j)),
            pl.BlockSpec((R, LRU_SB), blk),
            pl.BlockSpec((R, LRU_SB), blk),
            pl.BlockSpec((CONV_W, LRU_SB), lambda j, i: (0, j)),
            pl.BlockSpec((1, LRU_SB), lambda j, i: (0, j)),
            pl.BlockSpec((2, 1, LRU_SB, LRU_SB), lambda j, i: (0, j, 0, 0)),
            pl.BlockSpec((2, LRU_SB), lambda j, i: (0, j)),
            pl.BlockSpec((1, LRU_SB), lambda j, i: (0, j)),
        ],
        out_specs=[pl.BlockSpec((R, LRU_SB), blk), pl.BlockSpec((R, LRU_SB), blk)],
        out_shape=[jax.ShapeDtypeStruct((n_rows, D_RNN), BF16), jax.ShapeDtypeStruct((n_rows, D_RNN), F32)],
        compiler_params=_cparams(2),
        name=name,
    )(proj, proj, h0x, hist, conv_w, conv_b.reshape(1, D_RNN), wg_sb, b_gate, lam.reshape(1, D_RNN))


def _lru(proj, conv_w, conv_b, wg_sb, b_gate, lam, h0, conv0, *, n_seq, n_chunks, L, row0, n_pad, name):
    rb0 = row0 // L

    def rblk(b, c):
        return rb0 + b * n_chunks + c

    out_dtype = BF16 if L % 16 == 0 else F32
    kern = functools.partial(_lru_kernel, L=L, n_pad=n_pad)
    return pl.pallas_call(
        kern,
        grid=(n_seq, N_LRU_SB, n_chunks),
        in_specs=[
            pl.BlockSpec((L, LRU_SB), lambda b, j, c: (rblk(b, c), j)),
            pl.BlockSpec((L, LRU_SB), lambda b, j, c: (rblk(b, c), N_LRU_SB + j)),
            pl.BlockSpec((1, 1, LRU_SB), lambda b, j, c: (b, 0, j)),
            pl.BlockSpec((1, 8, LRU_SB), lambda b, j, c: (b, 0, j)),
            pl.BlockSpec((CONV_W, LRU_SB), lambda b, j, c: (0, j)),
            pl.BlockSpec((1, LRU_SB), lambda b, j, c: (0, j)),
            pl.BlockSpec((2, 1, LRU_SB, LRU_SB), lambda b, j, c: (0, j, 0, 0)),
            pl.BlockSpec((2, LRU_SB), lambda b, j, c: (0, j)),
            pl.BlockSpec((1, LRU_SB), lambda b, j, c: (0, j)),
        ],
        out_specs=[
            pl.BlockSpec((L, LRU_SB), lambda b, j, c: (b * n_chunks + c, j)),
            pl.BlockSpec((1, 1, LRU_SB), lambda b, j, c: (b, 0, j)),
        ],
        out_shape=[
            jax.ShapeDtypeStruct((n_seq * n_chunks * L, D_RNN), out_dtype),
            jax.ShapeDtypeStruct((n_seq, 1, D_RNN), F32),
        ],
        scratch_shapes=[pltpu.VMEM((1, LRU_SB), F32), pltpu.VMEM((8, LRU_SB), F32),
                        pltpu.VMEM((L + 8, LRU_SB), F32)],
        compiler_params=_cparams(3),
        name=name,
    )(proj, proj, h0, conv0, conv_w, conv_b.reshape(1, D_RNN), wg_sb, b_gate, lam.reshape(1, D_RNN))


def _route(logits):
    tm = logits.shape[0]
    lane = lax.broadcasted_iota(jnp.int32, (tm, 128), 1)
    lane_f = lane.astype(F32)
    is_group = (lane >= N_EXPERTS) & (lane < N_EXPERTS + N_EXP_GROUPS)
    gl = jnp.where(is_group, logits, -jnp.inf)
    g_max = jnp.max(gl, axis=1, keepdims=True)
    g_lane = jnp.min(jnp.where(gl == g_max, lane_f, 1e9), axis=1, keepdims=True)
    g_w = 1.0 / jnp.sum(jnp.exp(gl - g_max), axis=1, keepdims=True)
    g_idx = g_lane - float(N_EXPERTS)
    lo = g_idx * float(EXP_PER_GROUP)
    in_group = (lane_f >= lo) & (lane_f < lo + float(EXP_PER_GROUP))
    el = jnp.where(in_group, logits, -jnp.inf)
    e_max = jnp.max(el, axis=1, keepdims=True)
    i1 = jnp.min(jnp.where(el == e_max, lane_f, 1e9), axis=1, keepdims=True)
    e_sum = jnp.sum(jnp.exp(el - e_max), axis=1, keepdims=True)
    el2 = jnp.where(lane_f == i1, -jnp.inf, el)
    e2_max = jnp.max(el2, axis=1, keepdims=True)
    i2 = jnp.min(jnp.where(el2 == e2_max, lane_f, 1e9), axis=1, keepdims=True)
    p1 = 1.0 / e_sum
    p2 = jnp.exp(e2_max - e_max) / e_sum
    tot = p1 + p2
    gate1 = g_w * (p1 / tot)
    gate2 = g_w * (p2 / tot)
    gates = jnp.where(lane == 0, gate1, jnp.where(lane == 1, gate2, 0.0))
    ids = jnp.where(lane == 0, i1, jnp.where(lane == 1, i2, 0.0)).astype(jnp.int32)
    return gates, ids


def _ffn_kernel(te_ref, nv_ref, idx_ref, x_hbm, w1_ref, w3_ref, w2_ref, o_ref,
                w1b, w3b, w2b, xb_s, xbuf, sems):
    i = pl.program_id(0)
    tm = TILE_ROWS
    nv = nv_ref[0]
    n_slots = FFN_AHEAD + 1
    slot = i % n_slots

    def fetch(tile):
        sl = tile % n_slots
        _issue_gather(idx_ref, tile * tm, x_hbm, xbuf, sl * tm, sems.at[sl], tm)

    @pl.when(i == 0)
    def _():
        for t in range(FFN_AHEAD):
            @pl.when(t < nv)
            def _():
                fetch(t)

    @pl.when(i + FFN_AHEAD < nv)
    def _():
        fetch(i + FFN_AHEAD)

    prev = te_ref[jnp.maximum(i - 1, 0)]

    @pl.when((i == 0) | (te_ref[i] != prev))
    def _():
        w1b[...] = w1_ref[0, 0].astype(BF16)
        w3b[...] = w3_ref[0, 0].astype(BF16)
        w2b[...] = w2_ref[0, 0].astype(BF16)

    @pl.when(i < nv)
    def _():
        _wait_gather(x_hbm, xbuf, sems.at[slot], tm)
        base = slot * (tm * SLAB)
        for j in range(SLAB):
            xb_s[:, j * 128:(j + 1) * 128] = xbuf[pl.ds(base + j, tm, stride=SLAB), :].astype(BF16)
        xb = xb_s[...]
        h1 = jnp.dot(xb, w1b[...], preferred_element_type=F32)
        h3 = jnp.dot(xb, w3b[...], preferred_element_type=F32)
        hid = (_silu(h1) * h3).astype(BF16)
        _slab_store(o_ref, jnp.dot(hid, w2b[...], preferred_element_type=F32))

    @pl.when(i >= nv)
    def _():
        o_ref[...] = jnp.zeros_like(o_ref)


def _expert_ffn(x_slab, row_src, tile_expert, n_valid, w1, w3, w2, layer, name):
    R = row_src.shape[0]
    D = D_MODEL
    tm = TILE_ROWS
    wmap = lambda i, te, nv, idx: (layer, te[i], 0, 0)
    return pl.pallas_call(
        _ffn_kernel,
        grid_spec=pltpu.PrefetchScalarGridSpec(
            num_scalar_prefetch=3,
            grid=(R // tm,),
            in_specs=[
                pl.BlockSpec(memory_space=pl.ANY),
                pl.BlockSpec((1, 1, D, D_EXPERT), wmap),
                pl.BlockSpec((1, 1, D, D_EXPERT), wmap),
                pl.BlockSpec((1, 1, D_EXPERT, D), wmap),
            ],
            out_specs=pl.BlockSpec((tm * SLAB, 128), lambda i, te, nv, idx: (i, 0)),
            scratch_shapes=[pltpu.VMEM((D, D_EXPERT), BF16), pltpu.VMEM((D, D_EXPERT), BF16),
                            pltpu.VMEM((D_EXPERT, D), BF16), pltpu.VMEM((tm, D), BF16),
                            pltpu.VMEM(((FFN_AHEAD + 1) * tm * SLAB, 128), F32),
                            pltpu.SemaphoreType.DMA((FFN_AHEAD + 1,))],
        ),
        out_shape=jax.ShapeDtypeStruct((R * SLAB, 128), F32),
        compiler_params=_cparams(1),
        name=name,
    )(tile_expert, n_valid, row_src, x_slab, w1, w3, w2)


def _route_plan(ids, n_tok):
    tm = TILE_ROWS
    n_pairs = TOP_K * n_tok
    n_rows = n_pairs + N_EXPERTS * tm
    experts = jnp.arange(N_EXPERTS, dtype=jnp.int32)
    e_flat = ids.T.reshape(-1)
    order = jnp.argsort(e_flat, stable=True).astype(jnp.int32)
    rank_sorted = jnp.argsort(order).astype(jnp.int32)
    sizes = jnp.sum((e_flat[:, None] == experts[None, :]).astype(jnp.int32), axis=0)
    start = jnp.cumsum(sizes) - sizes
    psz = ((sizes + tm - 1) // tm) * tm
    pend = jnp.cumsum(psz)
    pstart = pend - psz
    shift = pstart - start
    pos = shift[e_flat] + rank_sorted
    rows = jnp.arange(n_rows, dtype=jnp.int32)
    tile_start = rows[::tm]
    e_tile = jnp.minimum(jnp.sum((tile_start[:, None] >= pend[None, :]).astype(jnp.int32), axis=1), N_EXPERTS - 1)
    pair_idx = rows - jnp.repeat(shift[e_tile], tm)
    valid = rows < jnp.repeat((pstart + sizes)[e_tile], tm)
    pair = order[jnp.clip(pair_idx, 0, n_pairs - 1)]
    row_src = jnp.where(valid, pair % n_tok, 0)
    last_e = jnp.max(jnp.where(sizes > 0, experts, 0))
    te = jnp.where(tile_start < pend[-1], e_tile, last_e)
    n_valid = (pend[-1] // tm).astype(jnp.int32)
    return row_src, pos, te, n_valid.reshape(1)


def _hier_moe_ln(x_slab, gates, ids, w1, w3, w2, layer, ln_g, ln_b, tag, split=None):
    n_tok = gates.shape[0]
    row_src, pos, te, n_valid = _route_plan(ids[:, :TOP_K], n_tok)
    ys = _expert_ffn(x_slab, row_src, te, n_valid, w1, w3, w2, layer, name=f"experts_{tag}")
    return _combine_ln(x_slab, ys, pos, gates, ln_g, ln_b, name=f"combine_ln_{tag}", split=split)


def _router_weights(w_group, b_group, w_expert, b_expert):
    w = jnp.concatenate([w_expert, w_group], axis=1)
    w = jnp.pad(w, ((0, 0), (0, 128 - w.shape[1])))
    b = jnp.pad(jnp.concatenate([b_expert, b_group]), (0, 128 - N_EXPERTS - N_EXP_GROUPS))
    return w.astype(BF16), b.reshape(1, 128)


def _pad_conv_state(conv):
    return jnp.pad(conv, ((0, 0), (8 - (CONV_W - 1), 0), (0, 0)))


def kernel(x_prompt, x_sample, state_mlstm_C, state_mlstm_n, state_mlstm_m, state_ssd, state_ssd_conv,
           state_lru_h, state_lru_conv, meta_tokens, w_in_even, mlstm_gate_b, ssd_dt_bias, ssd_A_log, ssd_D,
           ssd_conv_w, ssd_conv_b, mlstm_norm_g, ssd_norm_g, w_out_even, w_in_odd, lru_conv_w, lru_conv_b,
           lru_w_gate, lru_b_gate, lru_lambda, w_out_odd, ln_g, ln_b, moe_w_group, moe_b_group,
           moe_w_expert, moe_b_expert, moe_w1, moe_w3, moe_w2):
    Bp, Tp, D = x_prompt.shape
    Bs, Ts, _ = x_sample.shape
    n_chunks_p = (N_META + Tp + CHUNK - 1) // CHUNK
    Tpp = n_chunks_p * CHUNK
    n_pad = Tpp - N_META - Tp
    n_p = Bp * Tpp
    n_s = Bs * Ts
    n_tok = n_p + n_s

    pieces = []
    for b in range(Bp):
        pieces += [jnp.zeros((n_pad, D), F32), meta_tokens.astype(F32), x_prompt[b]]
    x0 = jnp.concatenate(pieces + [x_sample.reshape(n_s, D)], axis=0)

    grp = [dict(n_seq=Bp, n_chunks=n_chunks_p, L=CHUNK, row0=0, n_pad=n_pad),
           dict(n_seq=Bs, n_chunks=1, L=Ts, row0=n_p, n_pad=0)]

    def seq_view(a, gi):
        if gi == 0:
            return a[:n_p].reshape(Bp, Tpp, a.shape[1])
        return a[n_p:].reshape(Bs, Ts, a.shape[1])

    def tail_rows(a, gi, col0, ncol):
        nb, T, base = (Bp, Tpp, 0) if gi == 0 else (Bs, Ts, n_p)
        idx = (base + np.arange(nb)[:, None] * T + np.arange(T - (CONV_W - 1), T)[None, :]).reshape(-1)
        rows = jnp.take(a, jnp.asarray(idx, jnp.int32), axis=0)
        return rows[:, col0:col0 + ncol].reshape(nb, CONV_W - 1, ncol)

    e = 0
    w_all = _w_relayout(w_in_even, e)
    proj = _mm(x0, w_all, 512, IN0_PAD // 5, name="in_proj_even", n_cols=IN0_PAD)

    small = proj[:, COL_SMALL:COL_SMALL + N_SMALL]
    h_parts, y_parts, st = [], [], []
    for gi, g in enumerate(grp):
        sv = seq_view(small, gi)
        gates_t = jnp.swapaxes(sv, 1, 2)
        nb, T = sv.shape[0], sv.shape[1]
        gates_c = sv[:, :, 2 * A_HEADS:].reshape(nb, T, B_GROUPS, B_HPG).transpose(0, 2, 1, 3)
        if gi == 0:
            c0 = jnp.zeros((Bp, A_HEADS, A_DK, A_DV), F32)
            n0 = jnp.zeros((Bp, A_HEADS, 1, A_DK), F32)
            m0 = jnp.zeros((Bp, A_HEADS), F32)
            s0 = jnp.zeros((Bp, B_HEADS, B_P, B_N), F32)
            cv0 = jnp.zeros((Bp, 8, B_CONV_DIM), F32)
        else:
            c0 = state_mlstm_C[e]
            n0 = state_mlstm_n[e][:, :, None, :]
            m0 = state_mlstm_m[e]
            s0 = state_ssd[e]
            cv0 = _pad_conv_state(state_ssd_conv[e])
        h_g, c_g, n_g, m_g = _mlstm(proj, gates_t, mlstm_gate_b[e], mlstm_norm_g[e], c0, n0, m0,
                                    name=f"mlstm_{gi}", **g)
        y_g, s_g = _ssd(proj, gates_t, gates_c, ssd_dt_bias[e], ssd_A_log[e], ssd_D[e], ssd_conv_w[e],
                        ssd_conv_b[e], ssd_norm_g[e], s0, cv0, name=f"ssd_{gi}", **g)
        h_parts.append(h_g.astype(BF16))
        y_parts.append(y_g.astype(BF16))
        sconv = tail_rows(proj, gi, COL_XBC, B_CONV_DIM)
        st.append((c_g[None], n_g[:, :, 0, :][None], m_g[None], s_g[None], sconv[None]))
    w_r, b_r = _router_weights(moe_w_group[0], moe_b_group[0], moe_w_expert[0], moe_b_expert[0])
    x1s, gates, ids = _mm_parts_ln([h_parts[0], y_parts[0]], [h_parts[1], y_parts[1]],
                                   w_out_even[e].astype(BF16), x0, ln_g[0, 0], ln_b[0, 0], w_r, b_r,
                                   name="out_proj_ln_even")
    x2, x2b = _hier_moe_ln(x1s, gates, ids, moe_w1, moe_w3, moe_w2, 0, ln_g[0, 1], ln_b[0, 1], "0")

    o = 0
    proj1 = _mm(x2b, w_in_odd[o].astype(BF16), 512, D_RNN, name="in_proj_odd")
    wg = lru_w_gate[o].reshape(2, N_LRU_SB, LRU_SB // LRU_BW, LRU_BW, LRU_BW)
    eye4 = jnp.eye(LRU_SB // LRU_BW, dtype=F32)
    wg_sb = jnp.einsum('gjaik,ab->gjaibk', wg, eye4).reshape(2, N_LRU_SB, LRU_SB, LRU_SB).astype(BF16)
    o_parts, st1 = [], []
    for gi, g in enumerate(grp):
        if gi == 0:
            h0 = jnp.zeros((Bp, 1, D_RNN), F32)
            cv0 = jnp.zeros((Bp, 8, D_RNN), F32)
        else:
            h0 = state_lru_h[o][:, None, :]
            cv0 = _pad_conv_state(state_lru_conv[o])
        if g["L"] == 8 and g["n_chunks"] == 1:
            o_g, hs = _lru_rows8(proj1, lru_conv_w[o], lru_conv_b[o], wg_sb, lru_b_gate[o], lru_lambda[o], h0, cv0,
                                 n_seq=g["n_seq"], row0=g["row0"], name=f"lru_{gi}")
            hN = hs.reshape(g["n_seq"], 8, D_RNN)[:, 7, :]
        else:
            o_g, hN = _lru(proj1, lru_conv_w[o], lru_conv_b[o], wg_sb, lru_b_gate[o], lru_lambda[o], h0, cv0,
                           name=f"lru_{gi}", **g)
            hN = hN[:, 0, :]
        o_parts.append(o_g.astype(BF16))
        hconv = tail_rows(proj1, gi, D_RNN, D_RNN)
        st1.append((hN[None], hconv[None]))
    w_r, b_r = _router_weights(moe_w_group[1], moe_b_group[1], moe_w_expert[1], moe_b_expert[1])
    x3s, gates, ids = _mm_parts_ln([o_parts[0]], [o_parts[1]], w_out_odd[o].astype(BF16),
                                   x2, ln_g[1, 0], ln_b[1, 0], w_r, b_r, name="out_proj_ln_odd")
    assert Tpp - Tp == CHUNK
    yp, ys = _hier_moe_ln(x3s, gates, ids, moe_w1, moe_w3, moe_w2, 1, ln_g[1, 1], ln_b[1, 1], "1",
                          split=(n_p, n_chunks_p))
    y_prompt = yp.reshape(Bp, Tp, D)
    y_sample = ys.reshape(Bs, Ts, D)
    (pC, pn, pm, pS, pSc), (sC, sn, sm, sS, sSc) = st
    (pH, pHc), (sH, sHc) = st1
    return (y_prompt, y_sample, pC, pn, pm, pS, pSc, pH, pHc, sC, sn, sm, sS, sSc, sH, sHc)
```

```python
import functools
import math

import jax
import jax.numpy as jnp
import numpy as np
from jax import lax
from jax.experimental import pallas as pl
from jax.experimental.pallas import tpu as pltpu

F32 = jnp.float32
BF16 = jnp.bfloat16

D_MODEL = 2048
N_META = 16
CHUNK = 128
CONV_W = 4
A_HEADS = 8
A_DK = 128
A_DV = 256
A_INNER = A_HEADS * A_DV
B_HEADS = 32
B_P = 64
B_N = 128
B_GROUPS = 4
B_HPG = B_HEADS // B_GROUPS
B_INNER = B_HEADS * B_P
B_GW = B_INNER // B_GROUPS
B_CONV_DIM = B_INNER + 2 * B_GROUPS * B_N
D_RNN = 2560
LRU_BLOCKS = 16
LRU_BW = D_RNN // LRU_BLOCKS
LRU_C = 8.0
LRU_SB = 640
N_LRU_SB = D_RNN // LRU_SB
N_EXP_GROUPS = 4
EXP_PER_GROUP = 8
N_EXPERTS = N_EXP_GROUPS * EXP_PER_GROUP
TOP_K = 2
D_EXPERT = 512
DEPTH = 2
ALPHA = (2.0 * DEPTH) ** 0.25

COL_Q = 0
COL_K = A_HEADS * A_DK
COL_V = 2 * A_HEADS * A_DK
COL_O = COL_V + A_INNER
COL_Z = COL_O + A_INNER
COL_XBC = COL_Z + B_INNER
COL_SMALL = COL_XBC + B_CONV_DIM
N_SMALL = 2 * A_HEADS + B_HEADS
IN0_PAD = 11520

NEG_BIG = -1e30
VMEM_LIMIT_BYTES = 48 * 1024 * 1024
TILE_ROWS = 256
FFN_AHEAD = 3
LN_ROWS = 256
SLAB = D_MODEL // 128


def _cparams(n_axes):
    return pltpu.CompilerParams(dimension_semantics=("arbitrary",) * n_axes,
                                vmem_limit_bytes=VMEM_LIMIT_BYTES)


def _smem_spec():
    return pl.BlockSpec(memory_space=pltpu.SMEM)


def _mm_kernel(x_ref, w_ref, o_ref):
    o_ref[...] = jnp.dot(x_ref[...].astype(BF16), w_ref[...], preferred_element_type=F32).astype(o_ref.dtype)


def _mm(x, w, bm, bn, name, n_cols=None):
    M, K = x.shape
    N = w.shape[1] if n_cols is None else n_cols
    return pl.pallas_call(
        _mm_kernel,
        grid=(N // bn, M // bm),
        in_specs=[pl.BlockSpec((bm, K), lambda j, i: (i, 0)),
                  pl.BlockSpec((K, bn), lambda j, i: (0, j))],
        out_specs=pl.BlockSpec((bm, bn), lambda j, i: (i, j)),
        out_shape=jax.ShapeDtypeStruct((M, N), F32),
        compiler_params=_cparams(2),
        name=name,
    )(x, w)


RELAY_W = 1024


def _w_relayout_kernel(a_ref, b_ref, g_ref, o_ref, *, n_same, n_shift, shift):
    j = pl.program_id(0)
    lane = lax.broadcasted_iota(jnp.int32, (1, 128), 1)
    n_t = RELAY_W // 128

    @pl.when(j < n_same)
    def _():
        o_ref[...] = a_ref[...].astype(BF16)

    @pl.when((j >= n_same) & (j < n_same + n_shift))
    def _():
        cur = pltpu.roll(a_ref[:, 0:128], 128 - shift, 1)
        for t in range(n_t):
            nxt_src = a_ref[:, (t + 1) * 128:(t + 2) * 128] if t + 1 < n_t else b_ref[...]
            nxt = pltpu.roll(nxt_src, 128 - shift, 1)
            o_ref[:, t * 128:(t + 1) * 128] = jnp.where(lane < 128 - shift, cur, nxt).astype(BF16)
            cur = nxt

    @pl.when(j == n_same + n_shift)
    def _():
        head = jnp.where(lane < 2 * A_HEADS, g_ref[...], jnp.where(lane < N_SMALL, a_ref[:, 0:128], 0.0))
        o_ref[:, 0:128] = head.astype(BF16)
        o_ref[:, 128:] = jnp.zeros((o_ref.shape[0], RELAY_W - 128), BF16)


def _w_relayout(w, layer):
    K = w.shape[1]
    src_small = 2 * A_HEADS * A_DK + 2 * A_INNER
    n_same = src_small // RELAY_W
    n_shift = (B_INNER + B_CONV_DIM) // RELAY_W
    n_blk = n_same + n_shift + 1
    assert src_small % RELAY_W == 0 and (B_INNER + B_CONV_DIM) % RELAY_W == 0 and n_blk * RELAY_W >= IN0_PAD
    assert COL_SMALL == (n_blk - 1) * RELAY_W and w.shape[2] == COL_SMALL + N_SMALL
    per = RELAY_W // 128
    return pl.pallas_call(
        functools.partial(_w_relayout_kernel, n_same=n_same, n_shift=n_shift, shift=2 * A_HEADS),
        grid=(n_blk,),
        in_specs=[pl.BlockSpec((None, K, RELAY_W), lambda j: (layer, 0, j)),
                  pl.BlockSpec((None, K, 128), lambda j: (layer, 0, jnp.minimum(j + 1, n_blk - 1) * per)),
                  pl.BlockSpec((None, K, 128), lambda j: (layer, 0, src_small // 128))],
        out_specs=pl.BlockSpec((K, RELAY_W), lambda j: (0, j)),
        out_shape=jax.ShapeDtypeStruct((K, n_blk * RELAY_W), BF16),
        compiler_params=_cparams(1),
        name="w_in_even_relayout",
    )(w, w, w)


def _layer_norm_rows(y, g, b):
    mu = jnp.mean(y, axis=-1, keepdims=True)
    yc = y - mu
    var = jnp.mean(yc * yc, axis=-1, keepdims=True)
    return yc * lax.rsqrt(var + 1e-5) * g + b


def _mm_parts_ln_kernel(*refs, k_sizes, n_blk_a):
    n = len(k_sizes)
    a_refs, b_refs = refs[:n], refs[n:2 * n]
    w_ref, x_ref, g_ref, b_ref, wr_ref, br_ref, os_ref, gate_ref, id_ref = refs[2 * n:]
    i = pl.program_id(0)

    def run(lhs_refs):
        acc, k0 = None, 0
        for r, ks in zip(lhs_refs, k_sizes):
            part = jnp.dot(r[...], w_ref[k0:k0 + ks, :], preferred_element_type=F32)
            acc = part if acc is None else acc + part
            k0 += ks
        y = _layer_norm_rows(ALPHA * x_ref[...] + acc, g_ref[...], b_ref[...])
        _slab_store(os_ref, y)
        logits = jnp.dot(y.astype(BF16), wr_ref[...], preferred_element_type=F32) + br_ref[...]
        gate_ref[...], id_ref[...] = _route(logits)

    @pl.when(i < n_blk_a)
    def _():
        run(a_refs)

    @pl.when(i >= n_blk_a)
    def _():
        run(b_refs)


def _mm_parts_ln(parts_a, parts_b, w, x, g, b, w_r, b_r, name):
    bm = LN_ROWS
    k_sizes = tuple(p.shape[1] for p in parts_a)
    n_blk_a = parts_a[0].shape[0] // bm
    n_blk_b = parts_b[0].shape[0] // bm
    K, D = w.shape
    M = (n_blk_a + n_blk_b) * bm
    a_specs = [pl.BlockSpec((bm, ks), lambda i: (jnp.minimum(i, n_blk_a - 1), 0)) for ks in k_sizes]
    b_specs = [pl.BlockSpec((bm, ks), lambda i: (jnp.maximum(i - n_blk_a, 0), 0)) for ks in k_sizes]
    row = pl.BlockSpec((bm, D), lambda i: (i, 0))
    vec = pl.BlockSpec((1, D), lambda i: (0, 0))
    tile = pl.BlockSpec((bm, 128), lambda i: (i, 0))
    w_spec = pl.BlockSpec((K, D), lambda i: (0, 0), pipeline_mode=pl.Buffered(1))
    return pl.pallas_call(
        functools.partial(_mm_parts_ln_kernel, k_sizes=k_sizes, n_blk_a=n_blk_a),
        grid=(n_blk_a + n_blk_b,),
        in_specs=a_specs + b_specs + [w_spec, row, vec, vec, pl.BlockSpec((D, 128), lambda i: (0, 0)),
                                      pl.BlockSpec((1, 128), lambda i: (0, 0))],
        out_specs=[pl.BlockSpec((bm * SLAB, 128), lambda i: (i, 0)), tile, tile],
        out_shape=[jax.ShapeDtypeStruct((M * SLAB, 128), F32), jax.ShapeDtypeStruct((M, 128), F32),
                   jax.ShapeDtypeStruct((M, 128), jnp.int32)],
        compiler_params=_cparams(1),
        name=name,
    )(*parts_a, *parts_b, w, x, g.reshape(1, D), b.reshape(1, D), w_r, b_r)


def _slab_store(ref, val):
    tm = val.shape[0]
    for j in range(SLAB):
        ref[pl.ds(j, tm, stride=SLAB), :] = val[:, j * 128:(j + 1) * 128]


def _slab_piece(ref, j, tm):
    return ref[pl.ds(j, tm, stride=SLAB), :]


def _slab_copy(src_hbm, dst_vmem, sem, src_tok, dst_tok):
    return pltpu.make_async_copy(src_hbm.at[pl.ds(pl.multiple_of(src_tok * SLAB, SLAB), SLAB), :],
                                 dst_vmem.at[pl.ds(pl.multiple_of(dst_tok * SLAB, SLAB), SLAB), :], sem)


def _issue_gather(idx_ref, idx0, src_hbm, dst_vmem, dst0, sem, n):
    def body(r, carry):
        _slab_copy(src_hbm, dst_vmem, sem, idx_ref[idx0 + r], dst0 + r).start()
        return carry
    lax.fori_loop(0, n, body, 0, unroll=8)


def _wait_gather(src_hbm, dst_vmem, sem, n):
    def body(r, carry):
        _slab_copy(src_hbm, dst_vmem, sem, 0, 0).wait()
        return carry
    lax.fori_loop(0, n, body, 0, unroll=8)


def _combine_rows(pos_ref, x_ref, y_hbm, gate_ref, g_ref, b_ref, v_s, ybuf, sems, n_tok, tm):
    i = pl.program_id(0)
    n = pl.num_programs(0)
    slot = i % 2

    def fetch(tile, sl):
        for kk in range(TOP_K):
            _issue_gather(pos_ref, kk * n_tok + tile * tm, y_hbm, ybuf, (sl * TOP_K + kk) * tm, sems.at[sl], tm)

    @pl.when(i == 0)
    def _():
        fetch(0, 0)

    @pl.when(i + 1 < n)
    def _():
        fetch(i + 1, 1 - slot)

    _wait_gather(y_hbm, ybuf, sems.at[slot], TOP_K * tm)
    gate = gate_ref[...]
    g0 = gate[:, 0:1]
    g1 = gate[:, 1:2]
    base0 = slot * (TOP_K * tm * SLAB)
    base1 = base0 + tm * SLAB
    for j in range(SLAB):
        v_s[:, j * 128:(j + 1) * 128] = (ALPHA * _slab_piece(x_ref, j, tm)
                                         + g0 * ybuf[pl.ds(base0 + j, tm, stride=SLAB), :]
                                         + g1 * ybuf[pl.ds(base1 + j, tm, stride=SLAB), :])
    return _layer_norm_rows(v_s[...], g_ref[...], b_ref[...])


def _combine_ln_kernel(pos_ref, x_ref, y_hbm, gate_ref, g_ref, b_ref, o_ref, ob_ref, v_s, ybuf, sems, *, n_tok):
    y = _combine_rows(pos_ref, x_ref, y_hbm, gate_ref, g_ref, b_ref, v_s, ybuf, sems, n_tok, LN_ROWS)
    o_ref[...] = y
    ob_ref[...] = y.astype(BF16)


def _combine_out_kernel(pos_ref, x_ref, y_hbm, gate_ref, g_ref, b_ref, yp_ref, ys_ref, v_s, ybuf, sems, *,
                        n_tok, n_blk_p, blk_per_seq):
    y = _combine_rows(pos_ref, x_ref, y_hbm, gate_ref, g_ref, b_ref, v_s, ybuf, sems, n_tok, CHUNK)
    i = pl.program_id(0)

    @pl.when((i < n_blk_p) & (i % blk_per_seq > 0))
    def _():
        yp_ref[...] = y

    @pl.when(i >= n_blk_p)
    def _():
        ys_ref[...] = y


def _combine_ln(x_slab, y_slab, pos, gates, g, b, name, split=None):
    M = gates.shape[0]
    D = D_MODEL
    tm = LN_ROWS if split is None else CHUNK
    vec = pl.BlockSpec((1, D), lambda i, pos: (0, 0))
    if split is None:
        kern = functools.partial(_combine_ln_kernel, n_tok=M)
        row = pl.BlockSpec((tm, D), lambda i, pos: (i, 0))
        out_specs = [row, row]
        out_shape = [jax.ShapeDtypeStruct((M, D), F32), jax.ShapeDtypeStruct((M, D), BF16)]
    else:
        n_p, bps = split
        n_blk_p = n_p // tm
        kern = functools.partial(_combine_out_kernel, n_tok=M, n_blk_p=n_blk_p, blk_per_seq=bps)

        def yp_map(i, pos):
            ip = jnp.minimum(i, n_blk_p - 1)
            return ((ip // bps) * (bps - 1) + jnp.maximum(ip % bps - 1, 0), 0)

        out_specs = [pl.BlockSpec((tm, D), yp_map),
                     pl.BlockSpec((tm, D), lambda i, pos: (jnp.maximum(i - n_blk_p, 0), 0))]
        out_shape = [jax.ShapeDtypeStruct((n_blk_p // bps * (bps - 1) * tm, D), F32),
                     jax.ShapeDtypeStruct((M - n_p, D), F32)]
    return pl.pallas_call(
        kern,
        grid_spec=pltpu.PrefetchScalarGridSpec(
            num_scalar_prefetch=1,
            grid=(M // tm,),
            in_specs=[pl.BlockSpec((tm * SLAB, 128), lambda i, pos: (i, 0)),
                      pl.BlockSpec(memory_space=pl.ANY),
                      pl.BlockSpec((tm, 128), lambda i, pos: (i, 0)), vec, vec],
            out_specs=out_specs,
            scratch_shapes=[pltpu.VMEM((tm, D), F32), pltpu.VMEM((2 * TOP_K * tm * SLAB, 128), F32),
                            pltpu.SemaphoreType.DMA((2,))],
        ),
        out_shape=out_shape,
        compiler_params=_cparams(1),
        name=name,
    )(pos, x_slab, y_slab, gates, g.reshape(1, D), b.reshape(1, D))


def _softplus(x):
    return jnp.maximum(x, 0.0) + jnp.log1p(jnp.exp(-jnp.abs(x)))


def _sigmoid(x):
    return 1.0 / (1.0 + jnp.exp(-x))


def _silu(x):
    return x * _sigmoid(x)


def _dot_nn(m, x):
    return jnp.dot(m, x, preferred_element_type=F32)


def _dot_nt(a, b):
    return lax.dot_general(a, b, (((1,), (1,)), ((), ())), preferred_element_type=F32)


def _row_to_col(row, eye):
    return jnp.sum(jnp.where(eye, row, 0.0), axis=1, keepdims=True)


def _causal_conv(x, carry_ref, buf_ref, w_ref, b_ref, L):
    buf_ref[0:8, :] = carry_ref[...]
    buf_ref[8:8 + L, :] = x
    acc = b_ref[...] + w_ref[0:1, :] * buf_ref[5:5 + L, :]
    for j in range(1, CONV_W):
        acc = acc + w_ref[j:j + 1, :] * buf_ref[5 + j:5 + j + L, :]
    carry_ref[...] = buf_ref[L:L + 8, :]
    return acc


def _mlstm_kernel(gb_ref, q_ref, k_ref, v_ref, o_ref, gt_ref, c0_ref, n0_ref, m0_ref, ng_ref,
                  h_ref, c_out_ref, n_out_ref, m_out_ref, *state, L, n_pad):
    c_s, n_s, m_s = state[:A_HEADS], state[A_HEADS:2 * A_HEADS], state[2 * A_HEADS:]
    c = pl.program_id(1)

    @pl.when(c == 0)
    def _():
        for hd in range(A_HEADS):
            c_s[hd][...] = c0_ref[0, hd]
            n_s[hd][...] = n0_ref[0, hd]
            m_s[hd][...] = m0_ref[0, hd]

    rows = lax.broadcasted_iota(jnp.int32, (L, L), 0)
    cols = lax.broadcasted_iota(jnp.int32, (L, L), 1)
    eye = rows == cols
    causal = cols <= rows

    if n_pad:
        pad = (lax.broadcasted_iota(jnp.int32, (1, L), 1) < n_pad) & (c == 0)

    for hd in range(A_HEADS):
        li = gt_ref[0, hd:hd + 1, :] + gb_ref[0, hd]
        fr = gt_ref[0, A_HEADS + hd:A_HEADS + hd + 1, :] + gb_ref[1, hd]
        lf = jnp.minimum(fr, 0.0) - jnp.log1p(jnp.exp(-jnp.abs(fr)))
        if n_pad:
            li = jnp.where(pad, NEG_BIG, li)
            lf = jnp.where(pad, 0.0, lf)

        lf_col = _row_to_col(lf, eye)
        b_col = jnp.sum(jnp.where(causal, lf, 0.0), axis=1, keepdims=True)
        b_row = jnp.sum(jnp.where(rows <= cols, lf_col, 0.0), axis=0, keepdims=True)
        m_prev = m_s[hd][:, 0:1]

        d = jnp.where(causal, b_col - b_row + li, NEG_BIG)
        inter = b_col + m_prev
        m_t = jnp.maximum(inter, jnp.max(d, axis=1, keepdims=True))
        w_intra = jnp.exp(d - m_t)
        w_inter = jnp.exp(inter - m_t)

        q = q_ref[:, hd * A_DK:(hd + 1) * A_DK] * (A_DK ** -0.5)
        k = k_ref[:, hd * A_DK:(hd + 1) * A_DK]
        v = v_ref[:, hd * A_DV:(hd + 1) * A_DV]
        qb = q.astype(BF16)
        kb = k.astype(BF16)
        vb = v.astype(BF16)
        s = _dot_nt(qb, kb) * w_intra
        c_prev = c_s[hd][...]
        n_prev = n_s[hd][...]
        num = _dot_nn(s.astype(BF16), vb)
        num = num + w_inter * jnp.dot(qb, c_prev.astype(BF16), preferred_element_type=F32)
        qn = jnp.sum(qb.astype(F32) * n_prev.astype(BF16).astype(F32), axis=1, keepdims=True)
        nq = jnp.sum(s, axis=1, keepdims=True) + w_inter * qn
        hh = num / jnp.maximum(jnp.abs(nq), jnp.exp(-m_t))

        hn = hh * lax.rsqrt(jnp.mean(hh * hh, axis=-1, keepdims=True) + 1e-6)
        hn = hn * ng_ref[:, hd * A_DV:(hd + 1) * A_DV]
        h_ref[:, hd * A_DV:(hd + 1) * A_DV] = (
            hn * _sigmoid(o_ref[:, hd * A_DV:(hd + 1) * A_DV])).astype(h_ref.dtype)

        b_last = b_row[:, L - 1:L]
        ws_log = b_last - b_row + li
        m_new = jnp.maximum(b_last + m_prev, jnp.max(ws_log, axis=1, keepdims=True))
        ws_col = _row_to_col(jnp.exp(ws_log - m_new), eye)
        wc = jnp.exp(b_last + m_prev - m_new)
        kw = k * ws_col
        c_new = wc * c_prev + lax.dot_general(kw.astype(BF16), vb, (((0,), (0,)), ((), ())),
                                              preferred_element_type=F32)
        n_new = wc * n_prev + jnp.sum(ws_col.astype(BF16).astype(F32) * kb.astype(F32), axis=0, keepdims=True)
        m_new_b = jnp.broadcast_to(m_new, (1, 128))
        c_s[hd][...] = c_new
        n_s[hd][...] = n_new
        m_s[hd][...] = m_new_b
        c_out_ref[0, hd] = c_new
        n_out_ref[0, hd] = n_new
        m_out_ref[0, hd] = m_new_b


def _mlstm(proj, gates_t, gate_b, norm_g, c0, n0, m0, *, n_seq, n_chunks, L, row0, n_pad, name):
    rb0 = row0 // L

    def rblk(b, c):
        return rb0 + b * n_chunks + c

    out_dtype = BF16 if L % 16 == 0 else F32
    qk_w = A_HEADS * A_DK
    state_map = lambda b, c: (b, 0, 0, 0)
    m0b = jnp.broadcast_to(m0[:, :, None, None], (n_seq, A_HEADS, 1, 128))
    kern = functools.partial(_mlstm_kernel, L=L, n_pad=n_pad)
    h, c_new, n_new, m_new = pl.pallas_call(
        kern,
        grid=(n_seq, n_chunks),
        in_specs=[
            _smem_spec(),
            pl.BlockSpec((L, qk_w), lambda b, c: (rblk(b, c), COL_Q // qk_w)),
            pl.BlockSpec((L, qk_w), lambda b, c: (rblk(b, c), COL_K // qk_w)),
            pl.BlockSpec((L, A_INNER), lambda b, c: (rblk(b, c), COL_V // A_INNER)),
            pl.BlockSpec((L, A_INNER), lambda b, c: (rblk(b, c), COL_O // A_INNER)),
            pl.BlockSpec((1, N_SMALL, L), lambda b, c: (b, 0, c)),
            pl.BlockSpec((1, A_HEADS, A_DK, A_DV), state_map),
            pl.BlockSpec((1, A_HEADS, 1, A_DK), state_map),
            pl.BlockSpec((1, A_HEADS, 1, 128), state_map),
            pl.BlockSpec((1, A_INNER), lambda b, c: (0, 0)),
        ],
        out_specs=[
            pl.BlockSpec((L, A_INNER), lambda b, c: (b * n_chunks + c, 0)),
            pl.BlockSpec((1, A_HEADS, A_DK, A_DV), state_map),
            pl.BlockSpec((1, A_HEADS, 1, A_DK), state_map),
            pl.BlockSpec((1, A_HEADS, 1, 128), state_map),
        ],
        out_shape=[
            jax.ShapeDtypeStruct((n_seq * n_chunks * L, A_INNER), out_dtype),
            jax.ShapeDtypeStruct((n_seq, A_HEADS, A_DK, A_DV), F32),
            jax.ShapeDtypeStruct((n_seq, A_HEADS, 1, A_DK), F32),
            jax.ShapeDtypeStruct((n_seq, A_HEADS, 1, 128), F32),
        ],
        scratch_shapes=([pltpu.VMEM((A_DK, A_DV), F32)] * A_HEADS + [pltpu.VMEM((1, A_DK), F32)] * A_HEADS
                        + [pltpu.VMEM((1, 128), F32)] * A_HEADS),
        compiler_params=_cparams(2),
        name=name,
    )(gate_b, proj, proj, proj, proj, gates_t, c0, n0, m0b, norm_g.reshape(1, A_INNER))
    return h, c_new, n_new, m_new[:, :, 0, 0]


def _ssd_kernel(dtr_ref, dtc_ref, alr_ref, alc_ref, xs_ref, bm_ref, cm_ref, z_ref, gt_ref, gc_ref, s0_ref,
                cx0_ref, cb0_ref, cc0_ref, wx_ref, wb_ref, wc_ref, bx_ref, bb_ref, bc_ref, ng_ref, dsk_ref,
                y_ref, s_out_ref,
                cx_s, cb_s, cc_s, bufx, bufb, bufc, ybuf, *s_s, L, n_pad, gps):
    grp0 = pl.program_id(1) * gps
    c = pl.program_id(2)

    @pl.when(c == 0)
    def _():
        for pair in range(len(s_s)):
            s_s[pair][...] = s0_ref[0, pair]
        cx_s[...] = cx0_ref[0]
        cb_s[...] = cb0_ref[0]
        cc_s[...] = cc0_ref[0]

    rows = lax.broadcasted_iota(jnp.int32, (L, L), 0)
    cols = lax.broadcasted_iota(jnp.int32, (L, L), 1)
    causal = cols <= rows
    tril = causal.astype(F32)
    triu = (rows <= cols).astype(F32)
    lane_lo = lax.broadcasted_iota(jnp.int32, (L, 2 * B_P), 1) < B_P
    row_lo = lax.broadcasted_iota(jnp.int32, (2 * B_P, 1), 0) < B_P
    if n_pad:
        pad_row = (lax.broadcasted_iota(jnp.int32, (1, L), 1) < n_pad) & (c == 0)
        pad_col = (lax.broadcasted_iota(jnp.int32, (L, 1), 0) < n_pad) & (c == 0)

    xs_raw = xs_ref[...]
    bm_raw = bm_ref[...]
    cm_raw = cm_ref[...]
    if n_pad:
        xs_raw = jnp.where(pad_col, 0.0, xs_raw)
        bm_raw = jnp.where(pad_col, 0.0, bm_raw)
        cm_raw = jnp.where(pad_col, 0.0, cm_raw)
    xs = _silu(_causal_conv(xs_raw, cx_s, bufx, wx_ref, bx_ref, L))
    bm = _silu(_causal_conv(bm_raw, cb_s, bufb, wb_ref, bb_ref, L))
    cm = _silu(_causal_conv(cm_raw, cc_s, bufc, wc_ref, bc_ref, L))

    def two(col, h0):
        return jnp.where(lane_lo, col[:, h0:h0 + 1], col[:, h0 + 1:h0 + 2])

    for gg in range(gps):
        bmb = bm[:, gg * B_N:(gg + 1) * B_N].astype(BF16)
        cmb = cm[:, gg * B_N:(gg + 1) * B_N].astype(BF16)
        cb = lax.dot_general(cmb, bmb, (((1,), (1,)), ((), ())), preferred_element_type=F32)

        row0 = pl.multiple_of(2 * A_HEADS + (grp0 + gg) * B_HPG, B_HPG)
        dt_c = _softplus(gc_ref[0, gg] + dtr_ref[gg])
        dt_r = _softplus(gt_ref[0, pl.ds(row0, B_HPG), :] + dtc_ref[gg])
        if n_pad:
            dt_c = jnp.where(pad_col, 0.0, dt_c)
            dt_r = jnp.where(pad_row, 0.0, dt_r)
        a_c = dt_c * (-jnp.exp(alr_ref[gg]))
        a_r = dt_r * (-jnp.exp(alc_ref[gg]))
        b_c = jnp.dot(tril, a_c, preferred_element_type=F32, precision=lax.Precision.HIGHEST)
        b_r = jnp.dot(a_r, triu, preferred_element_type=F32, precision=lax.Precision.HIGHEST)
        eb_c = jnp.exp(b_c)
        b_last = b_c[L - 1:L, :]
        w_c = jnp.exp(b_last - b_c)
        e_last = jnp.exp(b_last)

        for p in range(B_HPG // 2):
            h0 = 2 * p
            pair = gg * (B_HPG // 2) + p
            x_pair = xs[:, pair * 2 * B_P:(pair + 1) * 2 * B_P]
            xdt = x_pair * two(dt_c, h0)
            y = None
            for hh, keep in ((h0, lane_lo), (h0 + 1, ~lane_lo)):
                decay = jnp.exp(jnp.where(causal, b_c[:, hh:hh + 1] - b_r[hh:hh + 1, :], NEG_BIG))
                part = _dot_nn((cb * decay).astype(BF16), jnp.where(keep, xdt, 0.0).astype(BF16))
                y = part if y is None else y + part
            s_prev = s_s[pair][...]
            y = y + two(eb_c, h0) * lax.dot_general(cmb, s_prev.astype(BF16), (((1,), (1,)), ((), ())),
                                                    preferred_element_type=F32)
            upd = lax.dot_general((xdt * two(w_c, h0)).astype(BF16), bmb, (((0,), (0,)), ((), ())),
                                  preferred_element_type=F32)
            s_new = jnp.where(row_lo, e_last[:, h0:h0 + 1], e_last[:, h0 + 1:h0 + 2]) * s_prev + upd
            s_s[pair][...] = s_new
            s_out_ref[0, pair] = s_new
            ybuf[:, pair * 2 * B_P:(pair + 1) * 2 * B_P] = y

    y = (ybuf[...] + dsk_ref[...] * xs) * _silu(z_ref[...])
    for gg in range(gps):
        yg = y[:, gg * B_GW:(gg + 1) * B_GW]
        yn = yg * lax.rsqrt(jnp.mean(yg * yg, axis=-1, keepdims=True) + 1e-6) * ng_ref[:, gg * B_GW:(gg + 1) * B_GW]
        y_ref[:, gg * B_GW:(gg + 1) * B_GW] = yn.astype(y_ref.dtype)


def _ssd(proj, gates_t, gates_c, dt_bias, a_log, d_skip, conv_w, conv_b, norm_g, s0, conv0, *,
         n_seq, n_chunks, L, row0, n_pad, name):
    rb0 = row0 // L

    def rblk(b, c):
        return rb0 + b * n_chunks + c

    out_dtype = BF16 if L % 16 == 0 else F32
    gps = B_GROUPS if n_chunks == 1 else 1
    xw, nw = gps * B_GW, gps * B_N
    n_pairs = gps * B_HPG // 2
    cw = conv_w
    cbias = conv_b.reshape(1, B_CONV_DIM)
    xoff = COL_XBC // xw
    boff = (COL_XBC + B_INNER) // nw
    coff = (COL_XBC + B_INNER + B_GROUPS * B_N) // nw
    cvb = B_INNER // nw
    cvc = (B_INNER + B_GROUPS * B_N) // nw
    dt_r = dt_bias.reshape(B_GROUPS, 1, B_HPG)
    dt_c = dt_bias.reshape(B_GROUPS, B_HPG, 1)
    al_r = a_log.reshape(B_GROUPS, 1, B_HPG)
    al_c = a_log.reshape(B_GROUPS, B_HPG, 1)
    d_row = jnp.repeat(d_skip, B_P).reshape(1, B_INNER)
    s0p = s0.reshape(n_seq, B_HEADS // 2, 2 * B_P, B_N)
    prow = pl.BlockSpec((gps, 1, B_HPG), lambda b, g, c: (g, 0, 0))
    pcol = pl.BlockSpec((gps, B_HPG, 1), lambda b, g, c: (g, 0, 0))
    kern = functools.partial(_ssd_kernel, L=L, n_pad=n_pad, gps=gps)
    y, s_new = pl.pallas_call(
        kern,
        grid=(n_seq, B_GROUPS // gps, n_chunks),
        in_specs=[
            prow, pcol, prow, pcol,
            pl.BlockSpec((L, xw), lambda b, g, c: (rblk(b, c), xoff + g)),
            pl.BlockSpec((L, nw), lambda b, g, c: (rblk(b, c), boff + g)),
            pl.BlockSpec((L, nw), lambda b, g, c: (rblk(b, c), coff + g)),
            pl.BlockSpec((L, xw), lambda b, g, c: (rblk(b, c), COL_Z // xw + g)),
            pl.BlockSpec((1, N_SMALL, L), lambda b, g, c: (b, 0, c)),
            pl.BlockSpec((1, gps, L, B_HPG), lambda b, g, c: (b, g, c, 0)),
            pl.BlockSpec((1, n_pairs, 2 * B_P, B_N), lambda b, g, c: (b, g, 0, 0)),
            pl.BlockSpec((1, 8, xw), lambda b, g, c: (b, 0, g)),
            pl.BlockSpec((1, 8, nw), lambda b, g, c: (b, 0, cvb + g)),
            pl.BlockSpec((1, 8, nw), lambda b, g, c: (b, 0, cvc + g)),
            pl.BlockSpec((CONV_W, xw), lambda b, g, c: (0, g)),
            pl.BlockSpec((CONV_W, nw), lambda b, g, c: (0, cvb + g)),
            pl.BlockSpec((CONV_W, nw), lambda b, g, c: (0, cvc + g)),
            pl.BlockSpec((1, xw), lambda b, g, c: (0, g)),
            pl.BlockSpec((1, nw), lambda b, g, c: (0, cvb + g)),
            pl.BlockSpec((1, nw), lambda b, g, c: (0, cvc + g)),
            pl.BlockSpec((1, xw), lambda b, g, c: (0, g)),
            pl.BlockSpec((1, xw), lambda b, g, c: (0, g)),
        ],
        out_specs=[
            pl.BlockSpec((L, xw), lambda b, g, c: (b * n_chunks + c, g)),
            pl.BlockSpec((1, n_pairs, 2 * B_P, B_N), lambda b, g, c: (b, g, 0, 0)),
        ],
        out_shape=[
            jax.ShapeDtypeStruct((n_seq * n_chunks * L, B_INNER), out_dtype),
            jax.ShapeDtypeStruct((n_seq, B_HEADS // 2, 2 * B_P, B_N), F32),
        ],
        scratch_shapes=[
            pltpu.VMEM((8, xw), F32), pltpu.VMEM((8, nw), F32), pltpu.VMEM((8, nw), F32),
            pltpu.VMEM((L + 8, xw), F32), pltpu.VMEM((L + 8, nw), F32), pltpu.VMEM((L + 8, nw), F32),
            pltpu.VMEM((L, xw), F32),
        ] + [pltpu.VMEM((2 * B_P, B_N), F32)] * n_pairs,
        compiler_params=_cparams(3),
        name=name,
    )(dt_r, dt_c, al_r, al_c, proj, proj, proj, proj, gates_t, gates_c, s0p, conv0, conv0, conv0,
      cw, cw, cw, cbias, cbias, cbias, norm_g.reshape(1, B_INNER), d_row)
    return y, s_new.reshape(n_seq, B_HEADS, B_P, B_N)


def _gelu_tanh(x):
    return 0.5 * x * (1.0 + jnp.tanh(math.sqrt(2.0 / math.pi) * (x + 0.044715 * (x * x * x))))


def _expm1_nonpos(z):
    e = jnp.exp(z)
    safe = (e < 1.0) & (z > -1.0)
    return jnp.where(safe, (e - 1.0) * z / jnp.log(jnp.where(safe, e, 0.5)), jnp.where(e == 1.0, z, e - 1.0))


def _lru_gates(xf, wg_ref, bg_ref, lam_ref):
    xfb = xf.astype(BF16)
    r = _sigmoid(jnp.dot(xfb, wg_ref[0, 0], preferred_element_type=F32) + bg_ref[0:1, :])
    i = _sigmoid(jnp.dot(xfb, wg_ref[1, 0], preferred_element_type=F32) + bg_ref[1:2, :])
    log_a = (-LRU_C) * r * _softplus(-lam_ref[...])
    a = jnp.exp(log_a)
    u = jnp.sqrt(-_expm1_nonpos(2.0 * log_a)) * (i * xf)
    return a, u


def _scan_rows8(a, u, t8):
    for k in (1, 2, 4):
        keep = t8 >= k
        a_sh = jnp.where(keep, pltpu.roll(a, k, 0), 1.0)
        u_sh = jnp.where(keep, pltpu.roll(u, k, 0), 0.0)
        u = a * u_sh + u
        a = a * a_sh
    return a, u


def _lru_kernel(gi_ref, xb_ref, h0_ref, cv0_ref, cw_ref, cb_ref, wg_ref, bg_ref, lam_ref,
                o_ref, h_out_ref, h_s, cv_s, buf, *, L, n_pad):
    c = pl.program_id(2)

    @pl.when(c == 0)
    def _():
        h_s[...] = h0_ref[0]
        cv_s[...] = cv0_ref[0]

    row_id = lax.broadcasted_iota(jnp.int32, (L, 1), 0)
    x_raw = xb_ref[...]
    if n_pad:
        pad_col = (row_id < n_pad) & (c == 0)
        x_raw = jnp.where(pad_col, 0.0, x_raw)
    xf = _causal_conv(x_raw, cv_s, buf, cw_ref, cb_ref, L)
    a, u = _lru_gates(xf, wg_ref, bg_ref, lam_ref)
    if n_pad:
        a = jnp.where(pad_col, 1.0, a)
        u = jnp.where(pad_col, 0.0, u)
    a, u = _scan_rows8(a, u, row_id % 8)
    h_in = h_s[...]
    pieces = []
    for t in range(L // 8):
        piece = a[t * 8:(t + 1) * 8, :] * h_in + u[t * 8:(t + 1) * 8, :]
        pieces.append(piece)
        h_in = piece[7:8, :]
    hs = jnp.concatenate(pieces, axis=0)
    h_s[...] = h_in
    h_out_ref[0] = h_in
    o_ref[...] = (hs * _gelu_tanh(gi_ref[...])).astype(o_ref.dtype)


def _lru_rows8_kernel(gi_ref, xb_ref, h0_ref, hist_ref, cw_ref, cb_ref, wg_ref, bg_ref, lam_ref,
                      o_ref, hs_ref):
    R = xb_ref.shape[0]
    t8 = lax.broadcasted_iota(jnp.int32, (R, 1), 0) % 8
    x = xb_ref[...]
    hist = hist_ref[...]
    acc = cb_ref[...] + cw_ref[CONV_W - 1:CONV_W, :] * x
    for j in range(CONV_W - 1):
        sh = CONV_W - 1 - j
        tap = jnp.where(t8 >= sh, pltpu.roll(x, sh, 0), pltpu.roll(hist, (sh - 8) % R, 0))
        acc = acc + cw_ref[j:j + 1, :] * tap
    a, u = _lru_gates(acc, wg_ref, bg_ref, lam_ref)
    a, u = _scan_rows8(a, u, t8)
    hs = a * h0_ref[...] + u
    hs_ref[...] = hs
    o_ref[...] = (hs * _gelu_tanh(gi_ref[...])).astype(o_ref.dtype)


def _lru_rows8(proj, conv_w, conv_b, wg_sb, b_gate, lam, h0, conv0, *, n_seq, row0, name):
    R = 128
    n_rows = n_seq * 8
    rb0 = row0 // R
    h0x = jnp.repeat(h0.reshape(n_seq, D_RNN), 8, axis=0)
    hist = conv0.reshape(n_rows, D_RNN)
    blk = lambda j, i: (i, j)
    return pl.pallas_call(
        _lru_rows8_kernel,
        grid=(N_LRU_SB, n_rows // R),
        in_specs=[
            pl.BlockSpec((R, LRU_SB), lambda j, i: (rb0 + i, j)),
            pl.BlockSpec((R, LRU_SB), lambda j, i: (rb0 + i, N_LRU_SB + j)),
            pl.BlockSpec((R, LRU_SB), blk),
            pl.BlockSpec((R, LRU_SB), blk),
            pl.BlockSpec((CONV_W, LRU_SB), lambda j, i: (0, j)),
            pl.BlockSpec((1, LRU_SB), lambda j, i: (0, j)),
            pl.BlockSpec((2, 1, LRU_SB, LRU_SB), lambda j, i: (0, j, 0, 0)),
            pl.BlockSpec((2, LRU_SB), lambda j, i: (0, j)),
            pl.BlockSpec((1, LRU_SB), lambda j, i: (0, j)),
        ],
        out_specs=[pl.BlockSpec((R, LRU_SB), blk), pl.BlockSpec((R, LRU_SB), blk)],
        out_shape=[jax.ShapeDtypeStruct((n_rows, D_RNN), BF16), jax.ShapeDtypeStruct((n_rows, D_RNN), F32)],
        compiler_params=_cparams(2),
        name=name,
    )(proj, proj, h0x, hist, conv_w, conv_b.reshape(1, D_RNN), wg_sb, b_gate, lam.reshape(1, D_RNN))


def _lru(proj, conv_w, conv_b, wg_sb, b_gate, lam, h0, conv0, *, n_seq, n_chunks, L, row0, n_pad, name):
    rb0 = row0 // L

    def rblk(b, c):
        return rb0 + b * n_chunks + c

    out_dtype = BF16 if L % 16 == 0 else F32
    kern = functools.partial(_lru_kernel, L=L, n_pad=n_pad)
    return pl.pallas_call(
        kern,
        grid=(n_seq, N_LRU_SB, n_chunks),
        in_specs=[
            pl.BlockSpec((L, LRU_SB), lambda b, j, c: (rblk(b, c), j)),
            pl.BlockSpec((L, LRU_SB), lambda b, j, c: (rblk(b, c), N_LRU_SB + j)),
            pl.BlockSpec((1, 1, LRU_SB), lambda b, j, c: (b, 0, j)),
            pl.BlockSpec((1, 8, LRU_SB), lambda b, j, c: (b, 0, j)),
            pl.BlockSpec((CONV_W, LRU_SB), lambda b, j, c: (0, j)),
            pl.BlockSpec((1, LRU_SB), lambda b, j, c: (0, j)),
            pl.BlockSpec((2, 1, LRU_SB, LRU_SB), lambda b, j, c: (0, j, 0, 0)),
            pl.BlockSpec((2, LRU_SB), lambda b, j, c: (0, j)),
            pl.BlockSpec((1, LRU_SB), lambda b, j, c: (0, j)),
        ],
        out_specs=[
            pl.BlockSpec((L, LRU_SB), lambda b, j, c: (b * n_chunks + c, j)),
            pl.BlockSpec((1, 1, LRU_SB), lambda b, j, c: (b, 0, j)),
        ],
        out_shape=[
            jax.ShapeDtypeStruct((n_seq * n_chunks * L, D_RNN), out_dtype),
            jax.ShapeDtypeStruct((n_seq, 1, D_RNN), F32),
        ],
        scratch_shapes=[pltpu.VMEM((1, LRU_SB), F32), pltpu.VMEM((8, LRU_SB), F32),
                        pltpu.VMEM((L + 8, LRU_SB), F32)],
        compiler_params=_cparams(3),
        name=name,
    )(proj, proj, h0, conv0, conv_w, conv_b.reshape(1, D_RNN), wg_sb, b_gate, lam.reshape(1, D_RNN))


def _route(logits):
    tm = logits.shape[0]
    lane = lax.broadcasted_iota(jnp.int32, (tm, 128), 1)
    lane_f = lane.astype(F32)
    is_group = (lane >= N_EXPERTS) & (lane < N_EXPERTS + N_EXP_GROUPS)
    gl = jnp.where(is_group, logits, -jnp.inf)
    g_max = jnp.max(gl, axis=1, keepdims=True)
    g_lane = jnp.min(jnp.where(gl == g_max, lane_f, 1e9), axis=1, keepdims=True)
    g_w = 1.0 / jnp.sum(jnp.exp(gl - g_max), axis=1, keepdims=True)
    g_idx = g_lane - float(N_EXPERTS)
    lo = g_idx * float(EXP_PER_GROUP)
    in_group = (lane_f >= lo) & (lane_f < lo + float(EXP_PER_GROUP))
    el = jnp.where(in_group, logits, -jnp.inf)
    e_max = jnp.max(el, axis=1, keepdims=True)
    i1 = jnp.min(jnp.where(el == e_max, lane_f, 1e9), axis=1, keepdims=True)
    e_sum = jnp.sum(jnp.exp(el - e_max), axis=1, keepdims=True)
    el2 = jnp.where(lane_f == i1, -jnp.inf, el)
    e2_max = jnp.max(el2, axis=1, keepdims=True)
    i2 = jnp.min(jnp.where(el2 == e2_max, lane_f, 1e9), axis=1, keepdims=True)
    p1 = 1.0 / e_sum
    p2 = jnp.exp(e2_max - e_max) / e_sum
    tot = p1 + p2
    gate1 = g_w * (p1 / tot)
    gate2 = g_w * (p2 / tot)
    gates = jnp.where(lane == 0, gate1, jnp.where(lane == 1, gate2, 0.0))
    ids = jnp.where(lane == 0, i1, jnp.where(lane == 1, i2, 0.0)).astype(jnp.int32)
    return gates, ids


def _ffn_kernel(te_ref, nv_ref, idx_ref, x_hbm, w1_ref, w3_ref, w2_ref, o_ref,
                w1b, w3b, w2b, xb_s, xbuf, sems):
    i = pl.program_id(0)
    tm = TILE_ROWS
    nv = nv_ref[0]
    n_slots = FFN_AHEAD + 1
    slot = i % n_slots

    def fetch(tile):
        sl = tile % n_slots
        _issue_gather(idx_ref, tile * tm, x_hbm, xbuf, sl * tm, sems.at[sl], tm)

    @pl.when(i == 0)
    def _():
        for t in range(FFN_AHEAD):
            @pl.when(t < nv)
            def _():
                fetch(t)

    @pl.when(i + FFN_AHEAD < nv)
    def _():
        fetch(i + FFN_AHEAD)

    prev = te_ref[jnp.maximum(i - 1, 0)]

    @pl.when((i == 0) | (te_ref[i] != prev))
    def _():
        w1b[...] = w1_ref[0, 0].astype(BF16)
        w3b[...] = w3_ref[0, 0].astype(BF16)
        w2b[...] = w2_ref[0, 0].astype(BF16)

    @pl.when(i < nv)
    def _():
        _wait_gather(x_hbm, xbuf, sems.at[slot], tm)
        base = slot * (tm * SLAB)
        for j in range(SLAB):
            xb_s[:, j * 128:(j + 1) * 128] = xbuf[pl.ds(base + j, tm, stride=SLAB), :].astype(BF16)
        xb = xb_s[...]
        h1 = jnp.dot(xb, w1b[...], preferred_element_type=F32)
        h3 = jnp.dot(xb, w3b[...], preferred_element_type=F32)
        hid = (_silu(h1) * h3).astype(BF16)
        _slab_store(o_ref, jnp.dot(hid, w2b[...], preferred_element_type=F32))

    @pl.when(i >= nv)
    def _():
        o_ref[...] = jnp.zeros_like(o_ref)


def _expert_ffn(x_slab, row_src, tile_expert, n_valid, w1, w3, w2, layer, name):
    R = row_src.shape[0]
    D = D_MODEL
    tm = TILE_ROWS
    wmap = lambda i, te, nv, idx: (layer, te[i], 0, 0)
    return pl.pallas_call(
        _ffn_kernel,
        grid_spec=pltpu.PrefetchScalarGridSpec(
            num_scalar_prefetch=3,
            grid=(R // tm,),
            in_specs=[
                pl.BlockSpec(memory_space=pl.ANY),
                pl.BlockSpec((1, 1, D, D_EXPERT), wmap),
                pl.BlockSpec((1, 1, D, D_EXPERT), wmap),
                pl.BlockSpec((1, 1, D_EXPERT, D), wmap),
            ],
            out_specs=pl.BlockSpec((tm * SLAB, 128), lambda i, te, nv, idx: (i, 0)),
            scratch_shapes=[pltpu.VMEM((D, D_EXPERT), BF16), pltpu.VMEM((D, D_EXPERT), BF16),
                            pltpu.VMEM((D_EXPERT, D), BF16), pltpu.VMEM((tm, D), BF16),
                            pltpu.VMEM(((FFN_AHEAD + 1) * tm * SLAB, 128), F32),
                            pltpu.SemaphoreType.DMA((FFN_AHEAD + 1,))],
        ),
        out_shape=jax.ShapeDtypeStruct((R * SLAB, 128), F32),
        compiler_params=_cparams(1),
        name=name,
    )(tile_expert, n_valid, row_src, x_slab, w1, w3, w2)


def _route_plan(ids, n_tok):
    tm = TILE_ROWS
    n_pairs = TOP_K * n_tok
    n_rows = n_pairs + N_EXPERTS * tm
    experts = jnp.arange(N_EXPERTS, dtype=jnp.int32)
    e_flat = ids.T.reshape(-1)
    order = jnp.argsort(e_flat, stable=True).astype(jnp.int32)
    rank_sorted = jnp.argsort(order).astype(jnp.int32)
    sizes = jnp.sum((e_flat[:, None] == experts[None, :]).astype(jnp.int32), axis=0)
    start = jnp.cumsum(sizes) - sizes
    psz = ((sizes + tm - 1) // tm) * tm
    pend = jnp.cumsum(psz)
    pstart = pend - psz
    shift = pstart - start
    pos = shift[e_flat] + rank_sorted
    rows = jnp.arange(n_rows, dtype=jnp.int32)
    tile_start = rows[::tm]
    e_tile = jnp.minimum(jnp.sum((tile_start[:, None] >= pend[None, :]).astype(jnp.int32), axis=1), N_EXPERTS - 1)
    pair_idx = rows - jnp.repeat(shift[e_tile], tm)
    valid = rows < jnp.repeat((pstart + sizes)[e_tile], tm)
    pair = order[jnp.clip(pair_idx, 0, n_pairs - 1)]
    row_src = jnp.where(valid, pair % n_tok, 0)
    last_e = jnp.max(jnp.where(sizes > 0, experts, 0))
    te = jnp.where(tile_start < pend[-1], e_tile, last_e)
    n_valid = (pend[-1] // tm).astype(jnp.int32)
    return row_src, pos, te, n_valid.reshape(1)


def _hier_moe_ln(x_slab, gates, ids, w1, w3, w2, layer, ln_g, ln_b, tag, split=None):
    n_tok = gates.shape[0]
    row_src, pos, te, n_valid = _route_plan(ids[:, :TOP_K], n_tok)
    ys = _expert_ffn(x_slab, row_src, te, n_valid, w1, w3, w2, layer, name=f"experts_{tag}")
    return _combine_ln(x_slab, ys, pos, gates, ln_g, ln_b, name=f"combine_ln_{tag}", split=split)


def _router_weights(w_group, b_group, w_expert, b_expert):
    w = jnp.concatenate([w_expert, w_group], axis=1)
    w = jnp.pad(w, ((0, 0), (0, 128 - w.shape[1])))
    b = jnp.pad(jnp.concatenate([b_expert, b_group]), (0, 128 - N_EXPERTS - N_EXP_GROUPS))
    return w.astype(BF16), b.reshape(1, 128)


def _pad_conv_state(conv):
    return jnp.pad(conv, ((0, 0), (8 - (CONV_W - 1), 0), (0, 0)))


def kernel(x_prompt, x_sample, state_mlstm_C, state_mlstm_n, state_mlstm_m, state_ssd, state_ssd_conv,
           state_lru_h, state_lru_conv, meta_tokens, w_in_even, mlstm_gate_b, ssd_dt_bias, ssd_A_log, ssd_D,
           ssd_conv_w, ssd_conv_b, mlstm_norm_g, ssd_norm_g, w_out_even, w_in_odd, lru_conv_w, lru_conv_b,
           lru_w_gate, lru_b_gate, lru_lambda, w_out_odd, ln_g, ln_b, moe_w_group, moe_b_group,
           moe_w_expert, moe_b_expert, moe_w1, moe_w3, moe_w2):
    Bp, Tp, D = x_prompt.shape
    Bs, Ts, _ = x_sample.shape
    n_chunks_p = (N_META + Tp + CHUNK - 1) // CHUNK
    Tpp = n_chunks_p * CHUNK
    n_pad = Tpp - N_META - Tp
    n_p = Bp * Tpp
    n_s = Bs * Ts
    n_tok = n_p + n_s

    pieces = []
    for b in range(Bp):
        pieces += [jnp.zeros((n_pad, D), F32), meta_tokens.astype(F32), x_prompt[b]]
    x0 = jnp.concatenate(pieces + [x_sample.reshape(n_s, D)], axis=0)

    grp = [dict(n_seq=Bp, n_chunks=n_chunks_p, L=CHUNK, row0=0, n_pad=n_pad),
           dict(n_seq=Bs, n_chunks=1, L=Ts, row0=n_p, n_pad=0)]

    def seq_view(a, gi):
        if gi == 0:
            return a[:n_p].reshape(Bp, Tpp, a.shape[1])
        return a[n_p:].reshape(Bs, Ts, a.shape[1])

    def tail_rows(a, gi, col0, ncol):
        nb, T, base = (Bp, Tpp, 0) if gi == 0 else (Bs, Ts, n_p)
        idx = (base + np.arange(nb)[:, None] * T + np.arange(T - (CONV_W - 1), T)[None, :]).reshape(-1)
        rows = jnp.take(a, jnp.asarray(idx, jnp.int32), axis=0)
        return rows[:, col0:col0 + ncol].reshape(nb, CONV_W - 1, ncol)

    e = 0
    w_all = _w_relayout(w_in_even, e)
    proj = _mm(x0, w_all, 512, IN0_PAD // 5, name="in_proj_even", n_cols=IN0_PAD)

    small = proj[:, COL_SMALL:COL_SMALL + N_SMALL]
    h_parts, y_parts, st = [], [], []
    for gi, g in enumerate(grp):
        sv = seq_view(small, gi)
        gates_t = jnp.swapaxes(sv, 1, 2)
        nb, T = sv.shape[0], sv.shape[1]
        gates_c = sv[:, :, 2 * A_HEADS:].reshape(nb, T, B_GROUPS, B_HPG).transpose(0, 2, 1, 3)
        if gi == 0:
            c0 = jnp.zeros((Bp, A_HEADS, A_DK, A_DV), F32)
            n0 = jnp.zeros((Bp, A_HEADS, 1, A_DK), F32)
            m0 = jnp.zeros((Bp, A_HEADS), F32)
            s0 = jnp.zeros((Bp, B_HEADS, B_P, B_N), F32)
            cv0 = jnp.zeros((Bp, 8, B_CONV_DIM), F32)
        else:
            c0 = state_mlstm_C[e]
            n0 = state_mlstm_n[e][:, :, None, :]
            m0 = state_mlstm_m[e]
            s0 = state_ssd[e]
            cv0 = _pad_conv_state(state_ssd_conv[e])
        h_g, c_g, n_g, m_g = _mlstm(proj, gates_t, mlstm_gate_b[e], mlstm_norm_g[e], c0, n0, m0,
                                    name=f"mlstm_{gi}", **g)
        y_g, s_g = _ssd(proj, gates_t, gates_c, ssd_dt_bias[e], ssd_A_log[e], ssd_D[e], ssd_conv_w[e],
                        ssd_conv_b[e], ssd_norm_g[e], s0, cv0, name=f"ssd_{gi}", **g)
        h_parts.append(h_g.astype(BF16))
        y_parts.append(y_g.astype(BF16))
        sconv = tail_rows(proj, gi, COL_XBC, B_CONV_DIM)
        st.append((c_g[None], n_g[:, :, 0, :][None], m_g[None], s_g[None], sconv[None]))
    w_r, b_r = _router_weights(moe_w_group[0], moe_b_group[0], moe_w_expert[0], moe_b_expert[0])
    x1s, gates, ids = _mm_parts_ln([h_parts[0], y_parts[0]], [h_parts[1], y_parts[1]],
                                   w_out_even[e].astype(BF16), x0, ln_g[0, 0], ln_b[0, 0], w_r, b_r,
                                   name="out_proj_ln_even")
    x2, x2b = _hier_moe_ln(x1s, gates, ids, moe_w1, moe_w3, moe_w2, 0, ln_g[0, 1], ln_b[0, 1], "0")

    o = 0
    proj1 = _mm(x2b, w_in_odd[o].astype(BF16), 512, D_RNN, name="in_proj_odd")
    wg = lru_w_gate[o].reshape(2, N_LRU_SB, LRU_SB // LRU_BW, LRU_BW, LRU_BW)
    eye4 = jnp.eye(LRU_SB // LRU_BW, dtype=F32)
    wg_sb = jnp.einsum('gjaik,ab->gjaibk', wg, eye4).reshape(2, N_LRU_SB, LRU_SB, LRU_SB).astype(BF16)
    o_parts, st1 = [], []
    for gi, g in enumerate(grp):
        if gi == 0:
            h0 = jnp.zeros((Bp, 1, D_RNN), F32)
            cv0 = jnp.zeros((Bp, 8, D_RNN), F32)
        else:
            h0 = state_lru_h[o][:, None, :]
            cv0 = _pad_conv_state(state_lru_conv[o])
        if g["L"] == 8 and g["n_chunks"] == 1:
            o_g, hs = _lru_rows8(proj1, lru_conv_w[o], lru_conv_b[o], wg_sb, lru_b_gate[o], lru_lambda[o], h0, cv0,
                                 n_seq=g["n_seq"], row0=g["row0"], name=f"lru_{gi}")
            hN = hs.reshape(g["n_seq"], 8, D_RNN)[:, 7, :]
        else:
            o_g, hN = _lru(proj1, lru_conv_w[o], lru_conv_b[o], wg_sb, lru_b_gate[o], lru_lambda[o], h0, cv0,
                           name=f"lru_{gi}", **g)
            hN = hN[:, 0, :]
        o_parts.append(o_g.astype(BF16))
        hconv = tail_rows(proj1, gi, D_RNN, D_RNN)
        st1.append((hN[None], hconv[None]))
    w_r, b_r = _router_weights(moe_w_group[1], moe_b_group[1], moe_w_expert[1], moe_b_expert[1])
    x3s, gates, ids = _mm_parts_ln([o_parts[0]], [o_parts[1]], w_out_odd[o].astype(BF16),
                                   x2, ln_g[1, 0], ln_b[1, 0], w_r, b_r, name="out_proj_ln_odd")
    assert Tpp - Tp == CHUNK
    yp, ys = _hier_moe_ln(x3s, gates, ids, moe_w1, moe_w3, moe_w2, 1, ln_g[1, 1], ln_b[1, 1], "1",
                          split=(n_p, n_chunks_p))
    y_prompt = yp.reshape(Bp, Tp, D)
    y_sample = ys.reshape(Bs, Ts, D)
    (pC, pn, pm, pS, pSc), (sC, sn, sm, sS, sSc) = st
    (pH, pHc), (sH, sHc) = st1
    return (y_prompt, y_sample, pC, pn, pm, pS, pSc, pH, pHc, sC, sn, sm, sS, sSc, sH, sHc)
```
